```python
import functools
import jax, jax.numpy as jnp
from jax import lax
import numpy as np

D_MODEL = 1024
BATCH = 8
SEQ = 2048
DEPTH = 2
DEC_BATCH = 128
DEC_SEQ = 8
PAST_LEN = 16384
PAGE_SIZE = 128

N_POOL_LAYERS = (DEPTH + 1) // 2
N_GDN_LAYERS = DEPTH // 2

POOL_WINDOWS = (2, 4, 8, 16)
N_POOL_GROUPS = len(POOL_WINDOWS)
POOL_GROUP_DIM = D_MODEL // N_POOL_GROUPS
POOL_BUF = max(POOL_WINDOWS) - 1

GDN_K_HEADS = 8
GDN_V_HEADS = 16
GDN_HEAD_K = 128
GDN_HEAD_V = 128
GDN_QK_DIM = GDN_K_HEADS * GDN_HEAD_K
GDN_V_DIM = GDN_V_HEADS * GDN_HEAD_V
GDN_CONV_DIM = 2 * GDN_QK_DIM + GDN_V_DIM
GDN_IN_DIM = GDN_CONV_DIM + GDN_V_DIM + 2 * GDN_V_HEADS
CONV_WIDTH = 4
CHUNK = 64

D_FF = -(-8 * D_MODEL // (3 * 256)) * 256

EPS = 1e-6

kernel_name = 'hybrid_pool_gdn_decoder_step'


def rms_norm(x, gain):
    xf = x.astype(jnp.float32)
    y = xf * lax.rsqrt(jnp.mean(xf * xf, axis=-1, keepdims=True) + EPS)
    return (y * gain.astype(jnp.float32)).astype(x.dtype)


def l2norm(x):
    xf = x.astype(jnp.float32)
    return xf * lax.rsqrt(jnp.sum(xf * xf, axis=-1, keepdims=True) + EPS)


def swiglu(h, w_in, w_out):
    gu = h @ w_in
    return (jax.nn.silu(gu[..., :D_FF]) * gu[..., D_FF:]) @ w_out


def pool_mixer(h, buf, w_grp, scale, n_past):
    B, L, _ = h.shape
    hp = jnp.concatenate([buf.astype(h.dtype), h], axis=1)
    hf = hp.astype(jnp.float32)
    cs = jnp.cumsum(hf, axis=1)
    cs = jnp.concatenate([jnp.zeros_like(cs[:, :1]), cs], axis=1)
    cur = hf[:, POOL_BUF:]
    t = jnp.arange(L)
    diffs = []
    for gi, win in enumerate(POOL_WINDOWS):
        c0, c1 = gi * POOL_GROUP_DIM, (gi + 1) * POOL_GROUP_DIM
        total = (cs[:, POOL_BUF + 1:POOL_BUF + 1 + L, c0:c1]
                 - cs[:, POOL_BUF + 1 - win:POOL_BUF + 1 - win + L, c0:c1])
        count = jnp.minimum(win, t + 1 + n_past).astype(jnp.float32)[None, :, None]
        diffs.append(total / count - cur[..., c0:c1])
    d = jnp.stack(diffs, axis=2)
    y = jnp.einsum('blgc,gce->blge', d, w_grp.astype(jnp.float32)).reshape(B, L, D_MODEL)
    y = y * scale.astype(jnp.float32)
    return y.astype(h.dtype), hp[:, L:]


def short_conv(u, buf, w):
    L = u.shape[1]
    up = jnp.concatenate([buf.astype(u.dtype), u], axis=1)
    out = up[:, 0:L] * w[0]
    for tap in range(1, CONV_WIDTH):
        out = out + up[:, tap:tap + L] * w[tap]
    return jax.nn.silu(out), up[:, L:]


def gated_delta_chunked(q, k, v, g, beta, s0):
    B, L, H, _ = q.shape
    n = -(-L // CHUNK)
    pad = n * CHUNK - L

    def blocks(x):
        x = jnp.pad(x, [(0, 0), (0, pad)] + [(0, 0)] * (x.ndim - 2))
        x = x.reshape((B, n, CHUNK) + x.shape[2:])
        return jnp.moveaxis(x, 3, 1)

    q, k, v, g, beta = blocks(q), blocks(k), blocks(v), blocks(g), blocks(beta)
    gc = jnp.cumsum(g, axis=-1)
    idx = jnp.arange(CHUNK)
    causal = idx[:, None] >= idx[None, :]
    strict = idx[:, None] > idx[None, :]
    decay = jnp.exp(jnp.where(causal, gc[..., :, None] - gc[..., None, :], -jnp.inf))
    kb = k * beta[..., None]
    a_mat = jnp.where(strict, jnp.einsum('bhnid,bhnjd->bhnij', kb, k) * decay, 0.0)
    t_mat = a_mat + jnp.eye(CHUNK, dtype=jnp.float32)
    solve = functools.partial(lax.linalg.triangular_solve, left_side=True, lower=True,
                              unit_diagonal=True)
    u = solve(t_mat, v * beta[..., None])
    w = solve(t_mat, kb * jnp.exp(gc)[..., None])
    qk = jnp.einsum('bhnid,bhnjd->bhnij', q, k) * decay
    q_dec = q * jnp.exp(gc)[..., None]
    k_dec = k * jnp.exp(gc[..., -1:] - gc)[..., None]
    g_last = jnp.exp(gc[..., -1])
    xs = tuple(jnp.moveaxis(a, 2, 0) for a in (u, w, qk, q_dec, k_dec, g_last))

    def step(S, inp):
        u_i, w_i, qk_i, qd_i, kd_i, gl_i = inp
        v_new = u_i - jnp.einsum('bhck,bhkv->bhcv', w_i, S)
        o_i = jnp.einsum('bhck,bhkv->bhcv', qd_i, S) + jnp.einsum('bhcs,bhsv->bhcv', qk_i, v_new)
        S = S * gl_i[..., None, None] + jnp.einsum('bhck,bhcv->bhkv', kd_i, v_new)
        return S, o_i

    s_fin, o = lax.scan(step, s0, xs)
    o = jnp.transpose(o, (1, 0, 3, 2, 4)).reshape(B, n * CHUNK, H, o.shape[-1])[:, :L]
    return o, s_fin


def gdn_mixer(h, conv_buf, s0, w_in, conv_w, a_log, dt_bias, o_norm, w_out):
    B, L, _ = h.shape
    proj = h @ w_in
    qkv = proj[..., :GDN_CONV_DIM]
    z = proj[..., GDN_CONV_DIM:GDN_CONV_DIM + GDN_V_DIM]
    b = proj[..., GDN_CONV_DIM + GDN_V_DIM:GDN_CONV_DIM + GDN_V_DIM + GDN_V_HEADS]
    a = proj[..., GDN_CONV_DIM + GDN_V_DIM + GDN_V_HEADS:]
    qkv_c, new_conv = short_conv(qkv, conv_buf, conv_w)
    rep = GDN_V_HEADS // GDN_K_HEADS
    q = l2norm(qkv_c[..., :GDN_QK_DIM].reshape(B, L, GDN_K_HEADS, GDN_HEAD_K)) * (GDN_HEAD_K ** -0.5)
    k = l2norm(qkv_c[..., GDN_QK_DIM:2 * GDN_QK_DIM].reshape(B, L, GDN_K_HEADS, GDN_HEAD_K))
    q = jnp.repeat(q, rep, axis=2)
    k = jnp.repeat(k, rep, axis=2)
    v = qkv_c[..., 2 * GDN_QK_DIM:].reshape(B, L, GDN_V_HEADS, GDN_HEAD_V).astype(jnp.float32)
    beta = jax.nn.sigmoid(b.astype(jnp.float32))
    g = -jnp.exp(a_log.astype(jnp.float32)) * jax.nn.softplus(a.astype(jnp.float32) + dt_bias.astype(jnp.float32))
    o, s_new = gated_delta_chunked(q, k, v, g, beta, s0.astype(jnp.float32))
    zf = z.reshape(B, L, GDN_V_HEADS, GDN_HEAD_V).astype(jnp.float32)
    o = rms_norm(o, o_norm) * jax.nn.silu(zf)
    y = o.reshape(B, L, GDN_V_DIM).astype(h.dtype) @ w_out
    return y, new_conv, s_new.astype(s0.dtype)


def trunk(x, pool_bufs, conv_bufs, rec_states, n_past_pool, norm_mix_pre, norm_mix_post,
          norm_ffn_pre, norm_ffn_post, pool_w, pool_scale, gdn_w_in, gdn_conv_w, gdn_a_log,
          gdn_dt_bias, gdn_o_norm, gdn_w_out, ffn_w_in, ffn_w_out):
    new_pool, new_conv, new_rec = [], [], []
    for i in range(DEPTH):
        j = i // 2
        h = rms_norm(x, norm_mix_pre[i])
        if i % 2 == 0:
            m, nb = pool_mixer(h, pool_bufs[j], pool_w[j], pool_scale[j], n_past_pool)
            new_pool.append(nb)
        else:
            m, nc, ns = gdn_mixer(h, conv_bufs[j], rec_states[j], gdn_w_in[j], gdn_conv_w[j],
                                  gdn_a_log[j], gdn_dt_bias[j], gdn_o_norm[j], gdn_w_out[j])
            new_conv.append(nc)
            new_rec.append(ns)
        x = x + rms_norm(m, norm_mix_post[i])
        h = rms_norm(x, norm_ffn_pre[i])
        x = x + rms_norm(swiglu(h, ffn_w_in[i], ffn_w_out[i]), norm_ffn_post[i])
    return x, jnp.stack(new_pool), jnp.stack(new_conv), jnp.stack(new_rec)


def setup_inputs(seed: int = 0) -> dict:
    key = jax.random.key(seed)
    ks = jax.random.split(key, 20)
    f32 = jnp.float32
    nrm = lambda k, s: jax.random.normal(k, s, f32)
    dt = jnp.exp(jax.random.uniform(ks[13], (N_GDN_LAYERS, GDN_V_HEADS), f32,
                                    jnp.log(0.001), jnp.log(0.1)))
    return {
        'x_prompt': nrm(ks[0], (BATCH, SEQ, D_MODEL)),
        'x_sample': nrm(ks[1], (DEC_BATCH, DEC_SEQ, D_MODEL)),
        'state_pool': nrm(ks[2], (N_POOL_LAYERS, DEC_BATCH, POOL_BUF, D_MODEL)),
        'state_gdn_conv': nrm(ks[3], (N_GDN_LAYERS, DEC_BATCH, CONV_WIDTH - 1, GDN_CONV_DIM)),
        'state_gdn_rec': 0.1 * nrm(ks[4], (N_GDN_LAYERS, DEC_BATCH, GDN_V_HEADS, GDN_HEAD_K, GDN_HEAD_V)),
        'norm_mix_pre': 1.0 + 0.05 * nrm(ks[5], (DEPTH, D_MODEL)),
        'norm_mix_post': 1.0 + 0.05 * nrm(ks[6], (DEPTH, D_MODEL)),
        'norm_ffn_pre': 1.0 + 0.05 * nrm(ks[7], (DEPTH, D_MODEL)),
        'norm_ffn_post': 1.0 + 0.05 * nrm(ks[8], (DEPTH, D_MODEL)),
        'pool_w': nrm(ks[9], (N_POOL_LAYERS, N_POOL_GROUPS, POOL_GROUP_DIM, POOL_GROUP_DIM)) * POOL_GROUP_DIM ** -0.5,
        'pool_scale': 1.0 + 0.1 * nrm(ks[10], (N_POOL_LAYERS, D_MODEL)),
        'gdn_w_in': nrm(ks[11], (N_GDN_LAYERS, D_MODEL, GDN_IN_DIM)) * D_MODEL ** -0.5,
        'gdn_conv_w': nrm(ks[12], (N_GDN_LAYERS, CONV_WIDTH, GDN_CONV_DIM)) * CONV_WIDTH ** -0.5,
        'gdn_a_log': jnp.log(jax.random.uniform(ks[14], (N_GDN_LAYERS, GDN_V_HEADS), f32, 1.0, 16.0)),
        'gdn_dt_bias': dt + jnp.log(-jnp.expm1(-dt)),
        'gdn_o_norm': 1.0 + 0.05 * nrm(ks[15], (N_GDN_LAYERS, GDN_HEAD_V)),
        'gdn_w_out': nrm(ks[16], (N_GDN_LAYERS, GDN_V_DIM, D_MODEL)) * GDN_V_DIM ** -0.5,
        'ffn_w_in': nrm(ks[17], (DEPTH, D_MODEL, 2 * D_FF)) * D_MODEL ** -0.5,
        'ffn_w_out': nrm(ks[18], (DEPTH, D_FF, D_MODEL)) * D_FF ** -0.5,
    }


def reference(x_prompt, x_sample, state_pool, state_gdn_conv, state_gdn_rec, norm_mix_pre,
              norm_mix_post, norm_ffn_pre, norm_ffn_post, pool_w, pool_scale, gdn_w_in,
              gdn_conv_w, gdn_a_log, gdn_dt_bias, gdn_o_norm, gdn_w_out, ffn_w_in, ffn_w_out):
    bp = x_prompt.shape[0]
    zero_pool = jnp.zeros((N_POOL_LAYERS, bp, POOL_BUF, D_MODEL), x_prompt.dtype)
    zero_conv = jnp.zeros((N_GDN_LAYERS, bp, CONV_WIDTH - 1, GDN_CONV_DIM), x_prompt.dtype)
    zero_rec = jnp.zeros((N_GDN_LAYERS, bp, GDN_V_HEADS, GDN_HEAD_K, GDN_HEAD_V), state_gdn_rec.dtype)
    y_prompt, pool_p, conv_p, rec_p = trunk(
        x_prompt, zero_pool, zero_conv, zero_rec, 0, norm_mix_pre, norm_mix_post, norm_ffn_pre,
        norm_ffn_post, pool_w, pool_scale, gdn_w_in, gdn_conv_w, gdn_a_log, gdn_dt_bias,
        gdn_o_norm, gdn_w_out, ffn_w_in, ffn_w_out)
    y_sample, pool_s, conv_s, rec_s = trunk(
        x_sample, state_pool, state_gdn_conv, state_gdn_rec, min(PAST_LEN, POOL_BUF), norm_mix_pre,
        norm_mix_post, norm_ffn_pre, norm_ffn_post, pool_w, pool_scale, gdn_w_in, gdn_conv_w,
        gdn_a_log, gdn_dt_bias, gdn_o_norm, gdn_w_out, ffn_w_in, ffn_w_out)
    return (y_prompt, y_sample, pool_p, pool_s, conv_p, conv_s, rec_p, rec_s)
```

```python
import functools

import jax
import jax.numpy as jnp
from jax import lax
from jax.experimental import pallas as pl
from jax.experimental.pallas import tpu as pltpu

D_MODEL = 1024
POOL_WINDOWS = (2, 4, 8, 16)
POOL_GROUP_DIM = D_MODEL // len(POOL_WINDOWS)
POOL_BUF = max(POOL_WINDOWS) - 1
K_HEADS = 8
V_HEADS = 16
HEAD_DIM = 128
QK_DIM = K_HEADS * HEAD_DIM
V_DIM = V_HEADS * HEAD_DIM
CONV_DIM = 2 * QK_DIM + V_DIM
CONV_WIDTH = 4
D_FF = 2816
EPS = 1e-6

F32 = jnp.float32
BF16 = jnp.bfloat16

SUBLANES = 8
LANES = 128
POOL_HALO = 16
CONV_HALO = SUBLANES
VMEM_LIMIT = 56 * 1024 * 1024

PROMPT_TILE = 512
GDN_PRE_TILE = 256
SAMPLE_BTILE = 32
PROMPT_CHUNK = 64


def _rms(x, gain):
    ms = jnp.mean(x * x, axis=-1, keepdims=True)
    return x * lax.rsqrt(ms + EPS) * gain


def _sigmoid(x):
    return 1.0 / (1.0 + jnp.exp(-x))


def _silu(x):
    return x * _sigmoid(x)


def _softplus(x):
    return jnp.maximum(x, 0.0) + jnp.log1p(jnp.exp(-jnp.abs(x)))


def _dot(a, b):
    return jnp.dot(a.astype(BF16), b.astype(BF16), preferred_element_type=F32)


def _dot_nt(a, b):
    return lax.dot_general(a.astype(BF16), b.astype(BF16), (((1,), (1,)), ((), ())),
                           preferred_element_type=F32)


def _dot_tn(a, b):
    return lax.dot_general(a.astype(BF16), b.astype(BF16), (((0,), (0,)), ((), ())),
                           preferred_element_type=F32)


def _const_spec(shape):
    nd = len(shape)
    return pl.BlockSpec(shape, lambda *_: (0,) * nd, pipeline_mode=pl.Buffered(1))


def _params(*sem):
    return pltpu.CompilerParams(dimension_semantics=sem, vmem_limit_bytes=VMEM_LIMIT)


def _residual_ffn(x, m, g_post, g_fpre, g_fpost, win_ref, wout_ref):
    x1 = x + _rms(m, g_post)
    h = _rms(x1, g_fpre).astype(BF16)
    gate = jnp.dot(h, win_ref[:, :D_FF], preferred_element_type=F32)
    up = jnp.dot(h, win_ref[:, D_FF:], preferred_element_type=F32)
    act = (_silu(gate) * up).astype(BF16)
    f = jnp.dot(act, wout_ref[...], preferred_element_type=F32)
    return x1 + _rms(f, g_fpost)


def _pool_project(diffs, pw_ref, scale):
    parts = [_dot(d, pw_ref[gi]) for gi, d in enumerate(diffs)]
    return jnp.concatenate(parts, axis=-1) * scale


def _pool_layer_prompt_kernel(x_ref, gains_ref, pw_ref, ps_ref, win_ref, wout_ref,
                              y_ref, pool_ref, hp_ref, *, tm):
    j = pl.program_id(1)
    x = x_ref[0]
    h = _rms(x, gains_ref[0:1, :])

    @pl.when(j == 0)
    def _():
        hp_ref[0:POOL_HALO, :] = jnp.zeros((POOL_HALO, D_MODEL), F32)

    hp_ref[POOL_HALO:POOL_HALO + tm, :] = h
    t = lax.broadcasted_iota(jnp.int32, (tm, 1), 0) + j * tm
    diffs = []
    for gi, win in enumerate(POOL_WINDOWS):
        c0, c1 = gi * POOL_GROUP_DIM, (gi + 1) * POOL_GROUP_DIM
        cur = hp_ref[POOL_HALO:POOL_HALO + tm, c0:c1]
        tot = cur
        for s in range(1, win):
            tot = tot + hp_ref[POOL_HALO - s:POOL_HALO - s + tm, c0:c1]
        cnt = jnp.minimum(win, t + 1).astype(F32)
        diffs.append(tot / cnt - cur)
    m = _pool_project(diffs, pw_ref, ps_ref[...])

    @pl.when(j == pl.num_programs(1) - 1)
    def _():
        pool_ref[0] = hp_ref[tm + POOL_HALO - POOL_BUF:tm + POOL_HALO, :]

    hp_ref[0:POOL_HALO, :] = hp_ref[tm:tm + POOL_HALO, :]
    y_ref[0] = _residual_ffn(x, m, gains_ref[1:2, :], gains_ref[2:3, :], gains_ref[3:4, :],
                             win_ref, wout_ref)


def _pool_layer_sample_kernel(x_ref, buf_ref, gains_ref, pw_ref, ps_ref, win_ref, wout_ref,
                              y_ref, pool_ref, hp_ref, *, tb, seq, n_past):
    x = x_ref[...]
    h = _rms(x, gains_ref[0:1, :])
    hp_ref[:, POOL_HALO - POOL_BUF:POOL_HALO, :] = buf_ref[...]
    hp_ref[:, POOL_HALO:POOL_HALO + seq, :] = h
    t = lax.broadcasted_iota(jnp.int32, (1, seq, 1), 1)
    diffs = []
    for gi, win in enumerate(POOL_WINDOWS):
        c0, c1 = gi * POOL_GROUP_DIM, (gi + 1) * POOL_GROUP_DIM
        cur = hp_ref[:, POOL_HALO:POOL_HALO + seq, c0:c1]
        tot = cur
        for s in range(1, win):
            tot = tot + hp_ref[:, POOL_HALO - s:POOL_HALO - s + seq, c0:c1]
        cnt = jnp.minimum(win, t + 1 + n_past).astype(F32)
        diffs.append((tot / cnt - cur).reshape(tb * seq, POOL_GROUP_DIM))
    m = _pool_project(diffs, pw_ref, ps_ref[...])
    pool_ref[...] = hp_ref[:, POOL_HALO + seq - POOL_BUF:POOL_HALO + seq, :]
    y = _residual_ffn(x.reshape(tb * seq, D_MODEL), m, gains_ref[1:2, :], gains_ref[2:3, :],
                      gains_ref[3:4, :], win_ref, wout_ref)
    y_ref[...] = y.reshape(tb, seq, D_MODEL)


def _pool_layer_prompt(x, gains, pw, ps, win, wout):
    b, l, d = x.shape
    tm = PROMPT_TILE
    return pl.pallas_call(
        functools.partial(_pool_layer_prompt_kernel, tm=tm),
        grid=(b, l // tm),
        in_specs=[
            pl.BlockSpec((1, tm, d), lambda i, j: (i, j, 0)),
            _const_spec(gains.shape), _const_spec(pw.shape), _const_spec(ps.shape),
            _const_spec(win.shape), _const_spec(wout.shape),
        ],
        out_specs=[
            pl.BlockSpec((1, tm, d), lambda i, j: (i, j, 0)),
            pl.BlockSpec((1, POOL_BUF, d), lambda i, j: (i, 0, 0)),
        ],
        out_shape=[jax.ShapeDtypeStruct((b, l, d), F32),
                   jax.ShapeDtypeStruct((b, POOL_BUF, d), F32)],
        scratch_shapes=[pltpu.VMEM((POOL_HALO + tm, d), F32)],
        compiler_params=_params("arbitrary", "arbitrary"),
        name="pool_layer_prompt",
    )(x, gains, pw, ps, win, wout)


def _pool_layer_sample(x, buf, gains, pw, ps, win, wout):
    b, l, d = x.shape
    tb = SAMPLE_BTILE
    n_past = buf.shape[1]
    return pl.pallas_call(
        functools.partial(_pool_layer_sample_kernel, tb=tb, seq=l, n_past=n_past),
        grid=(b // tb,),
        in_specs=[
            pl.BlockSpec((tb, l, d), lambda i: (i, 0, 0)),
            pl.BlockSpec((tb, POOL_BUF, d), lambda i: (i, 0, 0)),
            _const_spec(gains.shape), _const_spec(pw.shape), _const_spec(ps.shape),
            _const_spec(win.shape), _const_spec(wout.shape),
        ],
        out_specs=[
            pl.BlockSpec((tb, l, d), lambda i: (i, 0, 0)),
            pl.BlockSpec((tb, POOL_BUF, d), lambda i: (i, 0, 0)),
        ],
        out_shape=[jax.ShapeDtypeStruct((b, l, d), F32),
                   jax.ShapeDtypeStruct((b, POOL_BUF, d), F32)],
        scratch_shapes=[pltpu.VMEM((tb, POOL_HALO + l, d), F32)],
        compiler_params=_params("arbitrary"),
        name="pool_layer_sample",
    )(x, buf, gains, pw, ps, win, wout)


def _gdn_features(conv, ba, alog, dtb, q_ref, k_ref, v_ref, beta_ref, g_ref):
    for hh in range(K_HEADS):
        c0 = hh * HEAD_DIM
        qh = conv[:, c0:c0 + HEAD_DIM]
        kh = conv[:, QK_DIM + c0:QK_DIM + c0 + HEAD_DIM]
        qn = qh * lax.rsqrt(jnp.sum(qh * qh, axis=-1, keepdims=True) + EPS)
        q_ref[:, c0:c0 + HEAD_DIM] = qn * (HEAD_DIM ** -0.5)
        k_ref[:, c0:c0 + HEAD_DIM] = kh * lax.rsqrt(jnp.sum(kh * kh, axis=-1, keepdims=True) + EPS)
    v_ref[...] = conv[:, 2 * QK_DIM:]
    beta_ref[...] = _sigmoid(ba[:, :LANES])
    g_ref[...] = -jnp.exp(alog) * _softplus(ba[:, LANES:] + dtb)


def _gdn_pre_prompt_kernel(x_ref, gain_ref, wqkvz_ref, wba_ref, cw_ref, alog_ref, dtb_ref,
                           q_ref, k_ref, v_ref, z_ref, beta_ref, g_ref, conv_ref, up_ref, *, tm):
    j = pl.program_id(1)
    h = _rms(x_ref[0], gain_ref[...]).astype(BF16)
    proj = jnp.dot(h, wqkvz_ref[...], preferred_element_type=F32)
    ba = jnp.dot(h, wba_ref[...], preferred_element_type=F32)
    z_ref[...] = proj[:, CONV_DIM:]

    @pl.when(j == 0)
    def _():
        up_ref[0:CONV_HALO, :] = jnp.zeros((CONV_HALO, CONV_DIM), F32)

    up_ref[CONV_HALO:CONV_HALO + tm, :] = proj[:, :CONV_DIM]
    base = CONV_HALO - (CONV_WIDTH - 1)
    acc = up_ref[base:base + tm, :] * cw_ref[0:1, :]
    for tap in range(1, CONV_WIDTH):
        acc = acc + up_ref[base + tap:base + tap + tm, :] * cw_ref[tap:tap + 1, :]

    @pl.when(j == pl.num_programs(1) - 1)
    def _():
        conv_ref[0] = up_ref[tm + CONV_HALO - (CONV_WIDTH - 1):tm + CONV_HALO, :]

    up_ref[0:CONV_HALO, :] = up_ref[tm:tm + CONV_HALO, :]
    _gdn_features(_silu(acc), ba, alog_ref[...], dtb_ref[...], q_ref, k_ref, v_ref, beta_ref, g_ref)


def _gdn_pre_sample_kernel(x_ref, buf_ref, gain_ref, wqkvz_ref, wba_ref, cw_ref, alog_ref,
                           dtb_ref, q_ref, k_ref, v_ref, z_ref, beta_ref, g_ref, conv_ref, up_ref,
                           *, tb, seq):
    m = tb * seq
    h = _rms(x_ref[...].reshape(m, D_MODEL), gain_ref[...]).astype(BF16)
    proj = jnp.dot(h, wqkvz_ref[...], preferred_element_type=F32)
    ba = jnp.dot(h, wba_ref[...], preferred_element_type=F32)
    z_ref[...] = proj[:, CONV_DIM:]
    base = CONV_HALO - (CONV_WIDTH - 1)
    up_ref[:, base:CONV_HALO, :] = buf_ref[...]
    up_ref[:, CONV_HALO:CONV_HALO + seq, :] = proj[:, :CONV_DIM].reshape(tb, seq, CONV_DIM)
    acc = up_ref[:, base:base + seq, :] * cw_ref[0:1, :]
    for tap in range(1, CONV_WIDTH):
        acc = acc + up_ref[:, base + tap:base + tap + seq, :] * cw_ref[tap:tap + 1, :]
    conv_ref[...] = up_ref[:, CONV_HALO + seq - (CONV_WIDTH - 1):CONV_HALO + seq, :]
    _gdn_features(_silu(acc).reshape(m, CONV_DIM), ba, alog_ref[...], dtb_ref[...],
                  q_ref, k_ref, v_ref, beta_ref, g_ref)


def _gdn_pre_out_shapes(n):
    return [jax.ShapeDtypeStruct((n, QK_DIM), F32), jax.ShapeDtypeStruct((n, QK_DIM), F32),
            jax.ShapeDtypeStruct((n, V_DIM), F32), jax.ShapeDtypeStruct((n, V_DIM), F32),
            jax.ShapeDtypeStruct((n, LANES), F32), jax.ShapeDtypeStruct((n, LANES), F32)]


def _gdn_pre_prompt(x, gain, wqkvz, wba, cw, alog, dtb):
    b, l, d = x.shape
    tm = GDN_PRE_TILE
    nj = l // tm
    tok = lambda w: pl.BlockSpec((tm, w), lambda i, j: (i * nj + j, 0))
    return pl.pallas_call(
        functools.partial(_gdn_pre_prompt_kernel, tm=tm),
        grid=(b, nj),
        in_specs=[pl.BlockSpec((1, tm, d), lambda i, j: (i, j, 0))]
        + [_const_spec(a.shape) for a in (gain, wqkvz, wba, cw, alog, dtb)],
        out_specs=[tok(QK_DIM), tok(QK_DIM), tok(V_DIM), tok(V_DIM), tok(LANES), tok(LANES),
                   pl.BlockSpec((1, CONV_WIDTH - 1, CONV_DIM), lambda i, j: (i, 0, 0))],
        out_shape=_gdn_pre_out_shapes(b * l)
        + [jax.ShapeDtypeStruct((b, CONV_WIDTH - 1, CONV_DIM), F32)],
        scratch_shapes=[pltpu.VMEM((CONV_HALO + tm, CONV_DIM), F32)],
        compiler_params=_params("arbitrary", "arbitrary"),
        name="gdn_pre_prompt",
    )(x, gain, wqkvz, wba, cw, alog, dtb)


def _gdn_pre_sample(x, buf, gain, wqkvz, wba, cw, alog, dtb):
    b, l, d = x.shape
    tb = SAMPLE_BTILE
    tok = lambda w: pl.BlockSpec((tb * l, w), lambda i: (i, 0))
    return pl.pallas_call(
        functools.partial(_gdn_pre_sample_kernel, tb=tb, seq=l),
        grid=(b // tb,),
        in_specs=[pl.BlockSpec((tb, l, d), lambda i: (i, 0, 0)),
                  pl.BlockSpec((tb, CONV_WIDTH - 1, CONV_DIM), lambda i: (i, 0, 0))]
        + [_const_spec(a.shape) for a in (gain, wqkvz, wba, cw, alog, dtb)],
        out_specs=[tok(QK_DIM), tok(QK_DIM), tok(V_DIM), tok(V_DIM), tok(LANES), tok(LANES),
                   pl.BlockSpec((tb, CONV_WIDTH - 1, CONV_DIM), lambda i: (i, 0, 0))],
        out_shape=_gdn_pre_out_shapes(b * l)
        + [jax.ShapeDtypeStruct((b, CONV_WIDTH - 1, CONV_DIM), F32)],
        scratch_shapes=[pltpu.VMEM((tb, CONV_HALO + l, CONV_DIM), F32)],
        compiler_params=_params("arbitrary"),
        name="gdn_pre_sample",
    )(x, buf, gain, wqkvz, wba, cw, alog, dtb)


def _unit_lower_inverse(a, c):
    ri = lax.broadcasted_iota(jnp.int32, (c, c), 0)
    ci = lax.broadcasted_iota(jnp.int32, (c, c), 1)
    eye = (ri == ci).astype(F32)
    pair = ((ri // 2) == (ci // 2)) & (ri > ci)
    x = eye - jnp.where(pair, a, 0.0)
    blk = 2
    while blk < c:
        off = ((ri // (2 * blk)) == (ci // (2 * blk))) & ((ri // blk) > (ci // blk))
        x = x - _dot(x, _dot(jnp.where(off, a, 0.0), x))
        blk *= 2
    return x


def _delta_chunk(q, k, v, z, gcum, beta, onorm, state_load, state_store, og_ref, c):
    ri = lax.broadcasted_iota(jnp.int32, (c, c), 0)
    ci = lax.broadcasted_iota(jnp.int32, (c, c), 1)
    causal = ri >= ci
    strict = ri > ci
    gcum_t = gcum.T
    rep = V_HEADS // K_HEADS
    for hh in range(V_HEADS):
        kk = hh // rep
        qh = q[:, kk * HEAD_DIM:(kk + 1) * HEAD_DIM]
        kh = k[:, kk * HEAD_DIM:(kk + 1) * HEAD_DIM]
        vh = v[:, hh * HEAD_DIM:(hh + 1) * HEAD_DIM]
        zh = z[:, hh * HEAD_DIM:(hh + 1) * HEAD_DIM]
        gcol = gcum[:, hh:hh + 1]
        grow = gcum_t[hh:hh + 1, :]
        bcol = beta[:, hh:hh + 1]
        glast = gcum[c - 1:c, hh:hh + 1]
        decay = jnp.where(causal, jnp.exp(jnp.minimum(gcol - grow, 0.0)), 0.0)
        egc = jnp.exp(gcol)
        kb = kh * bcol
        a_mat = jnp.where(strict, _dot_nt(kb, kh) * decay, 0.0)
        t_inv = _unit_lower_inverse(a_mat, c)
        uw = _dot(t_inv, jnp.concatenate([vh * bcol, kb * egc], axis=1))
        u, w = uw[:, :HEAD_DIM], uw[:, HEAD_DIM:]
        qk = _dot_nt(qh, kh) * decay
        s = state_load(hh)
        ws = _dot(jnp.concatenate([w, qh * egc], axis=0), s)
        v_new = u - ws[:c]
        o = ws[c:] + _dot(qk, v_new)
        k_dec = kh * jnp.exp(glast - gcol)
        state_store(hh, s * jnp.exp(glast) + _dot_tn(k_dec, v_new))
        og_ref[:, hh * HEAD_DIM:(hh + 1) * HEAD_DIM] = (_rms(o, onorm) * _silu(zh)).astype(BF16)


def _cumsum_rows(g, c):
    ri = lax.broadcasted_iota(jnp.int32, (c, c), 0)
    ci = lax.broadcasted_iota(jnp.int32, (c, c), 1)
    tri = (ri >= ci).astype(F32)
    return jnp.dot(tri, g, preferred_element_type=F32, precision=lax.Precision.HIGHEST)


def _gdn_scan_prompt_kernel(q_ref, k_ref, v_ref, z_ref, beta_ref, g_ref, onorm_ref,
                            og_ref, s_ref, *, c):
    @pl.when(pl.program_id(1) == 0)
    def _():
        s_ref[...] = jnp.zeros(s_ref.shape, F32)

    def load(hh):
        return s_ref[0, hh]

    def store(hh, val):
        s_ref[0, hh] = val

    _delta_chunk(q_ref[...], k_ref[...], v_ref[...], z_ref[...], _cumsum_rows(g_ref[...], c),
                 beta_ref[...], onorm_ref[...], load, store, og_ref, c)


def _gdn_scan_sample_kernel(q_ref, k_ref, v_ref, z_ref, beta_ref, g_ref, onorm_ref, s0_ref,
                            og_ref, s_ref, *, tb, c):
    def body(i, carry):
        rows = pl.ds(pl.multiple_of(i * c, c), c)

        def load(hh):
            return s0_ref[i, hh]

        def store(hh, val):
            s_ref[i, hh] = val

        _delta_chunk(q_ref[rows, :], k_ref[rows, :], v_ref[rows, :], z_ref[rows, :],
                     _cumsum_rows(g_ref[rows, :], c), beta_ref[rows, :], onorm_ref[...],
                     load, store, og_ref.at[rows, :], c)
        return carry

    lax.fori_loop(0, tb, body, 0)


def _gdn_scan_prompt(q, k, v, z, beta, g, onorm, b, l):
    c = PROMPT_CHUNK
    nc = l // c
    tok = lambda w: pl.BlockSpec((c, w), lambda i, j: (i * nc + j, 0))
    return pl.pallas_call(
        functools.partial(_gdn_scan_prompt_kernel, c=c),
        grid=(b, nc),
        in_specs=[tok(QK_DIM), tok(QK_DIM), tok(V_DIM), tok(V_DIM), tok(LANES), tok(LANES),
                  _const_spec(onorm.shape)],
        out_specs=[tok(V_DIM),
                   pl.BlockSpec((1, V_HEADS, HEAD_DIM, HEAD_DIM), lambda i, j: (i, 0, 0, 0))],
        out_shape=[jax.ShapeDtypeStruct((b * l, V_DIM), BF16),
                   jax.ShapeDtypeStruct((b, V_HEADS, HEAD_DIM, HEAD_DIM), F32)],
        compiler_params=_params("arbitrary", "arbitrary"),
        name="gdn_scan_prompt",
    )(q, k, v, z, beta, g, onorm)


def _gdn_scan_sample(q, k, v, z, beta, g, onorm, s0, b, l):
    tb = 4
    tok = lambda w: pl.BlockSpec((tb * l, w), lambda i: (i, 0))
    st = pl.BlockSpec((tb, V_HEADS, HEAD_DIM, HEAD_DIM), lambda i: (i, 0, 0, 0))
    return pl.pallas_call(
        functools.partial(_gdn_scan_sample_kernel, tb=tb, c=l),
        grid=(b // tb,),
        in_specs=[tok(QK_DIM), tok(QK_DIM), tok(V_DIM), tok(V_DIM), tok(LANES), tok(LANES),
                  _const_spec(onorm.shape), st],
        out_specs=[tok(V_DIM), st],
        out_shape=[jax.ShapeDtypeStruct((b * l, V_DIM), BF16),
                   jax.ShapeDtypeStruct((b, V_HEADS, HEAD_DIM, HEAD_DIM), F32)],
        compiler_params=_params("arbitrary"),
        name="gdn_scan_sample",
    )(q, k, v, z, beta, g, onorm, s0)


def _gdn_post_kernel(og_ref, x_ref, gains_ref, wo_ref, win_ref, wout_ref, y_ref):
    m = jnp.dot(og_ref[...], wo_ref[...], preferred_element_type=F32)
    y_ref[...] = _residual_ffn(x_ref[...], m, gains_ref[0:1, :], gains_ref[1:2, :],
                               gains_ref[2:3, :], win_ref, wout_ref)


def _gdn_post(og, x, gains, wo, win, wout):
    n, d = x.shape
    tm = PROMPT_TILE
    return pl.pallas_call(
        _gdn_post_kernel,
        grid=(n // tm,),
        in_specs=[pl.BlockSpec((tm, V_DIM), lambda i: (i, 0)),
                  pl.BlockSpec((tm, d), lambda i: (i, 0))]
        + [_const_spec(a.shape) for a in (gains, wo, win, wout)],
        out_specs=pl.BlockSpec((tm, d), lambda i: (i, 0)),
        out_shape=jax.ShapeDtypeStruct((n, d), F32),
        compiler_params=_params("arbitrary"),
        name="gdn_post",
    )(og, x, gains, wo, win, wout)


def _head_lanes(vec):
    return jnp.pad(vec.astype(F32), (0, LANES - V_HEADS)).reshape(1, LANES)


def kernel(x_prompt, x_sample, state_pool, state_gdn_conv, state_gdn_rec, norm_mix_pre,
           norm_mix_post, norm_ffn_pre, norm_ffn_post, pool_w, pool_scale, gdn_w_in,
           gdn_conv_w, gdn_a_log, gdn_dt_bias, gdn_o_norm, gdn_w_out, ffn_w_in, ffn_w_out):
    bp, lp, d = x_prompt.shape
    bs, ls, _ = x_sample.shape

    gains0 = jnp.stack([norm_mix_pre[0], norm_mix_post[0], norm_ffn_pre[0], norm_ffn_post[0]])
    gains1 = jnp.stack([norm_mix_post[1], norm_ffn_pre[1], norm_ffn_post[1]])
    gain1_pre = norm_mix_pre[1].reshape(1, d)
    pw = pool_w[0].astype(BF16)
    ps = pool_scale[0].reshape(1, d)
    win0, wout0 = ffn_w_in[0].astype(BF16), ffn_w_out[0].astype(BF16)
    win1, wout1 = ffn_w_in[1].astype(BF16), ffn_w_out[1].astype(BF16)
    w_in = gdn_w_in[0]
    wqkvz = w_in[:, :CONV_DIM + V_DIM].astype(BF16)
    w_b = w_in[:, CONV_DIM + V_DIM:CONV_DIM + V_DIM + V_HEADS]
    w_a = w_in[:, CONV_DIM + V_DIM + V_HEADS:]
    lane_pad = ((0, 0), (0, LANES - V_HEADS))
    wba = jnp.concatenate([jnp.pad(w_b, lane_pad), jnp.pad(w_a, lane_pad)], axis=1).astype(BF16)
    cw = gdn_conv_w[0]
    alog, dtb = _head_lanes(gdn_a_log[0]), _head_lanes(gdn_dt_bias[0])
    onorm = gdn_o_norm[0].reshape(1, HEAD_DIM)
    wo = gdn_w_out[0].astype(BF16)

    xp1, pool_p = _pool_layer_prompt(x_prompt, gains0, pw, ps, win0, wout0)
    xs1, pool_s = _pool_layer_sample(x_sample, state_pool[0], gains0, pw, ps, win0, wout0)

    qp, kp, vp, zp, betap, gp, conv_p = _gdn_pre_prompt(xp1, gain1_pre, wqkvz, wba, cw, alog, dtb)
    qs, ks, vs, zs, betas, gs, conv_s = _gdn_pre_sample(xs1, state_gdn_conv[0], gain1_pre, wqkvz,
                                                        wba, cw, alog, dtb)

    ogp, rec_p = _gdn_scan_prompt(qp, kp, vp, zp, betap, gp, onorm, bp, lp)
    ogs, rec_s = _gdn_scan_sample(qs, ks, vs, zs, betas, gs, onorm, state_gdn_rec[0], bs, ls)

    yp = _gdn_post(ogp, xp1.reshape(bp * lp, d), gains1, wo, win1, wout1).reshape(bp, lp, d)
    ys = _gdn_post(ogs, xs1.reshape(bs * ls, d), gains1, wo, win1, wout1).reshape(bs, ls, d)

    return (yp, ys, pool_p[None], pool_s[None], conv_p[None], conv_s[None], rec_p[None],
            rec_s[None])
```

```python
import functools

import jax
import jax.numpy as jnp
from jax import lax
from jax.experimental import pallas as pl
from jax.experimental.pallas import tpu as pltpu

D_MODEL = 1024
POOL_WINDOWS = (2, 4, 8, 16)
POOL_GROUP_DIM = D_MODEL // len(POOL_WINDOWS)
POOL_BUF = max(POOL_WINDOWS) - 1
K_HEADS = 8
V_HEADS = 16
HEAD_DIM = 128
QK_DIM = K_HEADS * HEAD_DIM
V_DIM = V_HEADS * HEAD_DIM
CONV_DIM = 2 * QK_DIM + V_DIM
CONV_WIDTH = 4
D_FF = 2816
EPS = 1e-6

F32 = jnp.float32
BF16 = jnp.bfloat16

SUBLANES = 8
LANES = 128
POOL_HALO = 16
CONV_HALO = SUBLANES
VMEM_LIMIT = 56 * 1024 * 1024

PROMPT_TILE = 512
GDN_PRE_TILE = 256
SAMPLE_BTILE = 32
PROMPT_CHUNK = 64


def _rms(x, gain):
    ms = jnp.mean(x * x, axis=-1, keepdims=True)
    return x * lax.rsqrt(ms + EPS) * gain


def _sigmoid(x):
    return 1.0 / (1.0 + jnp.exp(-x))


def _silu(x):
    return x * _sigmoid(x)


def _softplus(x):
    return jnp.maximum(x, 0.0) + jnp.log1p(jnp.exp(-jnp.abs(x)))


def _dot(a, b):
    return jnp.dot(a.astype(BF16), b.astype(BF16), preferred_element_type=F32)


def _dot_nt(a, b):
    return lax.dot_general(a.astype(BF16), b.astype(BF16), (((1,), (1,)), ((), ())),
                           preferred_element_type=F32)


def _dot_tn(a, b):
    return lax.dot_general(a.astype(BF16), b.astype(BF16), (((0,), (0,)), ((), ())),
                           preferred_element_type=F32)


def _const_spec(shape):
    nd = len(shape)
    return pl.BlockSpec(shape, lambda *_: (0,) * nd, pipeline_mode=pl.Buffered(1))


def _params(*sem):
    return pltpu.CompilerParams(dimension_semantics=sem, vmem_limit_bytes=VMEM_LIMIT)


def _residual_ffn(x, m, g_post, g_fpre, g_fpost, win_ref, wout_ref):
    x1 = x + _rms(m, g_post)
    h = _rms(x1, g_fpre).astype(BF16)
    gate = jnp.dot(h, win_ref[:, :D_FF], preferred_element_type=F32)
    up = jnp.dot(h, win_ref[:, D_FF:], preferred_element_type=F32)
    act = (_silu(gate) * up).astype(BF16)
    f = jnp.dot(act, wout_ref[...], preferred_element_type=F32)
    return x1 + _rms(f, g_fpost)


def _pool_project(diffs, pw_ref, scale):
    parts = [_dot(d, pw_ref[gi]) for gi, d in enumerate(diffs)]
    return jnp.concatenate(parts, axis=-1) * scale


def _pool_layer_prompt_kernel(x_ref, gains_ref, pw_ref, ps_ref, win_ref, wout_ref,
                              y_ref, pool_ref, hp_ref, *, tm):
    j = pl.program_id(1)
    x = x_ref[0]
    h = _rms(x, gains_ref[0:1, :])

    @pl.when(j == 0)
    def _():
        hp_ref[0:POOL_HALO, :] = jnp.zeros((POOL_HALO, D_MODEL), F32)

    hp_ref[POOL_HALO:POOL_HALO + tm, :] = h
    t = lax.broadcasted_iota(jnp.int32, (tm, 1), 0) + j * tm
    diffs = []
    for gi, win in enumerate(POOL_WINDOWS):
        c0, c1 = gi * POOL_GROUP_DIM, (gi + 1) * POOL_GROUP_DIM
        cur = hp_ref[POOL_HALO:POOL_HALO + tm, c0:c1]
        tot = cur
        for s in range(1, win):
            tot = tot + hp_ref[POOL_HALO - s:POOL_HALO - s + tm, c0:c1]
        cnt = jnp.minimum(win, t + 1).astype(F32)
        diffs.append(tot / cnt - cur)
    m = _pool_project(diffs, pw_ref, ps_ref[...])

    @pl.when(j == pl.num_programs(1) - 1)
    def _():
        pool_ref[0] = hp_ref[tm + POOL_HALO - POOL_BUF:tm + POOL_HALO, :]

    hp_ref[0:POOL_HALO, :] = hp_ref[tm:tm + POOL_HALO, :]
    y_ref[0] = _residual_ffn(x, m, gains_ref[1:2, :], gains_ref[2:3, :], gains_ref[3:4, :],
                             win_ref, wout_ref)


def _pool_layer_sample_kernel(x_ref, buf_ref, gains_ref, pw_ref, ps_ref, win_ref, wout_ref,
                              y_ref, pool_ref, hp_ref, *, tb, seq, n_past):
    x = x_ref[...]
    h = _rms(x, gains_ref[0:1, :])
    hp_ref[:, POOL_HALO - POOL_BUF:POOL_HALO, :] = buf_ref[...]
    hp_ref[:, POOL_HALO:POOL_HALO + seq, :] = h
    t = lax.broadcasted_iota(jnp.int32, (1, seq, 1), 1)
    diffs = []
    for gi, win in enumerate(POOL_WINDOWS):
        c0, c1 = gi * POOL_GROUP_DIM, (gi + 1) * POOL_GROUP_DIM
        cur = hp_ref[:, POOL_HALO:POOL_HALO + seq, c0:c1]
        tot = cur
        for s in range(1, win):
            tot = tot + hp_ref[:, POOL_HALO - s:POOL_HALO - s + seq, c0:c1]
        cnt = jnp.minimum(win, t + 1 + n_past).astype(F32)
        diffs.append((tot / cnt - cur).reshape(tb * seq, POOL_GROUP_DIM))
    m = _pool_project(diffs, pw_ref, ps_ref[...])
    pool_ref[...] = hp_ref[:, POOL_HALO + seq - POOL_BUF:POOL_HALO + seq, :]
    y = _residual_ffn(x.reshape(tb * seq, D_MODEL), m, gains_ref[1:2, :], gains_ref[2:3, :],
                      gains_ref[3:4, :], win_ref, wout_ref)
    y_ref[...] = y.reshape(tb, seq, D_MODEL)


def _pool_layer_prompt(x, gains, pw, ps, win, wout):
    b, l, d = x.shape
    tm = PROMPT_TILE
    return pl.pallas_call(
        functools.partial(_pool_layer_prompt_kernel, tm=tm),
        grid=(b, l // tm),
        in_specs=[
            pl.BlockSpec((1, tm, d), lambda i, j: (i, j, 0)),
            _const_spec(gains.shape), _const_spec(pw.shape), _const_spec(ps.shape),
            _const_spec(win.shape), _const_spec(wout.shape),
        ],
        out_specs=[
            pl.BlockSpec((1, tm, d), lambda i, j: (i, j, 0)),
            pl.BlockSpec((1, POOL_BUF, d), lambda i, j: (i, 0, 0)),
        ],
        out_shape=[jax.ShapeDtypeStruct((b, l, d), F32),
                   jax.ShapeDtypeStruct((b, POOL_BUF, d), F32)],
        scratch_shapes=[pltpu.VMEM((POOL_HALO + tm, d), F32)],
        compiler_params=_params("arbitrary", "arbitrary"),
        name="pool_layer_prompt",
    )(x, gains, pw, ps, win, wout)


def _pool_layer_sample(x, buf, gains, pw, ps, win, wout):
    b, l, d = x.shape
    tb = SAMPLE_BTILE
    n_past = buf.shape[1]
    return pl.pallas_call(
        functools.partial(_pool_layer_sample_kernel, tb=tb, seq=l, n_past=n_past),
        grid=(b // tb,),
        in_specs=[
            pl.BlockSpec((tb, l, d), lambda i: (i, 0, 0)),
            pl.BlockSpec((tb, POOL_BUF, d), lambda i: (i, 0, 0)),
            _const_spec(gains.shape), _const_spec(pw.shape), _const_spec(ps.shape),
            _const_spec(win.shape), _const_spec(wout.shape),
        ],
        out_specs=[
            pl.BlockSpec((tb, l, d), lambda i: (i, 0, 0)),
            pl.BlockSpec((tb, POOL_BUF, d), lambda i: (i, 0, 0)),
        ],
        out_shape=[jax.ShapeDtypeStruct((b, l, d), F32),
                   jax.ShapeDtypeStruct((b, POOL_BUF, d), F32)],
        scratch_shapes=[pltpu.VMEM((tb, POOL_HALO + l, d), F32)],
        compiler_params=_params("arbitrary"),
        name="pool_layer_sample",
    )(x, buf, gains, pw, ps, win, wout)


def _gdn_features(conv, ba, alog, dtb, q_ref, k_ref, v_ref, beta_ref, g_ref):
    for hh in range(K_HEADS):
        c0 = hh * HEAD_DIM
        qh = conv[:, c0:c0 + HEAD_DIM]
        kh = conv[:, QK_DIM + c0:QK_DIM + c0 + HEAD_DIM]
        qn = qh * lax.rsqrt(jnp.sum(qh * qh, axis=-1, keepdims=True) + EPS)
        q_ref[:, c0:c0 + HEAD_DIM] = qn * (HEAD_DIM ** -0.5)
        k_ref[:, c0:c0 + HEAD_DIM] = kh * lax.rsqrt(jnp.sum(kh * kh, axis=-1, keepdims=True) + EPS)
    v_ref[...] = conv[:, 2 * QK_DIM:]
    beta_ref[...] = _sigmoid(ba[:, :LANES])
    g_ref[...] = -jnp.exp(alog) * _softplus(ba[:, LANES:] + dtb)


def _gdn_pre_prompt_kernel(x_ref, gain_ref, wqkvz_ref, wba_ref, cw_ref, alog_ref, dtb_ref,
                           q_ref, k_ref, v_ref, z_ref, beta_ref, g_ref, conv_ref, up_ref, *, tm):
    j = pl.program_id(1)
    h = _rms(x_ref[0], gain_ref[...]).astype(BF16)
    proj = jnp.dot(h, wqkvz_ref[...], preferred_element_type=F32)
    ba = jnp.dot(h, wba_ref[...], preferred_element_type=F32)
    z_ref[...] = proj[:, CONV_DIM:]

    @pl.when(j == 0)
    def _():
        up_ref[0:CONV_HALO, :] = jnp.zeros((CONV_HALO, CONV_DIM), F32)

    up_ref[CONV_HALO:CONV_HALO + tm, :] = proj[:, :CONV_DIM]
    base = CONV_HALO - (CONV_WIDTH - 1)
    acc = up_ref[base:base + tm, :] * cw_ref[0:1, :]
    for tap in range(1, CONV_WIDTH):
        acc = acc + up_ref[base + tap:base + tap + tm, :] * cw_ref[tap:tap + 1, :]

    @pl.when(j == pl.num_programs(1) - 1)
    def _():
        conv_ref[0] = up_ref[tm + CONV_HALO - (CONV_WIDTH - 1):tm + CONV_HALO, :]

    up_ref[0:CONV_HALO, :] = up_ref[tm:tm + CONV_HALO, :]
    _gdn_features(_silu(acc), ba, alog_ref[...], dtb_ref[...], q_ref, k_ref, v_ref, beta_ref, g_ref)


def _gdn_pre_sample_kernel(x_ref, buf_ref, gain_ref, wqkvz_ref, wba_ref, cw_ref, alog_ref,
                           dtb_ref, q_ref, k_ref, v_ref, z_ref, beta_ref, g_ref, conv_ref, up_ref,
                           *, tb, seq):
    m = tb * seq
    h = _rms(x_ref[...].reshape(m, D_MODEL), gain_ref[...]).astype(BF16)
    proj = jnp.dot(h, wqkvz_ref[...], preferred_element_type=F32)
    ba = jnp.dot(h, wba_ref[...], preferred_element_type=F32)
    z_ref[...] = proj[:, CONV_DIM:]
    base = CONV_HALO - (CONV_WIDTH - 1)
    up_ref[:, base:CONV_HALO, :] = buf_ref[...]
    up_ref[:, CONV_HALO:CONV_HALO + seq, :] = proj[:, :CONV_DIM].reshape(tb, seq, CONV_DIM)
    acc = up_ref[:, base:base + seq, :] * cw_ref[0:1, :]
    for tap in range(1, CONV_WIDTH):
        acc = acc + up_ref[:, base + tap:base + tap + seq, :] * cw_ref[tap:tap + 1, :]
    conv_ref[...] = up_ref[:, CONV_HALO + seq - (CONV_WIDTH - 1):CONV_HALO + seq, :]
    _gdn_features(_silu(acc).reshape(m, CONV_DIM), ba, alog_ref[...], dtb_ref[...],
                  q_ref, k_ref, v_ref, beta_ref, g_ref)


def _gdn_pre_out_shapes(n):
    return [jax.ShapeDtypeStruct((n, QK_DIM), F32), jax.ShapeDtypeStruct((n, QK_DIM), F32),
            jax.ShapeDtypeStruct((n, V_DIM), F32), jax.ShapeDtypeStruct((n, V_DIM), F32),
            jax.ShapeDtypeStruct((n, LANES), F32), jax.ShapeDtypeStruct((n, LANES), F32)]


def _gdn_pre_prompt(x, gain, wqkvz, wba, cw, alog, dtb):
    b, l, d = x.shape
    tm = GDN_PRE_TILE
    nj = l // tm
    tok = lambda w: pl.BlockSpec((tm, w), lambda i, j: (i * nj + j, 0))
    return pl.pallas_call(
        functools.partial(_gdn_pre_prompt_kernel, tm=tm),
        grid=(b, nj),
        in_specs=[pl.BlockSpec((1, tm, d), lambda i, j: (i, j, 0))]
        + [_const_spec(a.shape) for a in (gain, wqkvz, wba, cw, alog, dtb)],
        out_specs=[tok(QK_DIM), tok(QK_DIM), tok(V_DIM), tok(V_DIM), tok(LANES), tok(LANES),
                   pl.BlockSpec((1, CONV_WIDTH - 1, CONV_DIM), lambda i, j: (i, 0, 0))],
        out_shape=_gdn_pre_out_shapes(b * l)
        + [jax.ShapeDtypeStruct((b, CONV_WIDTH - 1, CONV_DIM), F32)],
        scratch_shapes=[pltpu.VMEM((CONV_HALO + tm, CONV_DIM), F32)],
        compiler_params=_params("arbitrary", "arbitrary"),
        name="gdn_pre_prompt",
    )(x, gain, wqkvz, wba, cw, alog, dtb)


def _gdn_pre_sample(x, buf, gain, wqkvz, wba, cw, alog, dtb):
    b, l, d = x.shape
    tb = SAMPLE_BTILE
    tok = lambda w: pl.BlockSpec((tb * l, w), lambda i: (i, 0))
    return pl.pallas_call(
        functools.partial(_gdn_pre_sample_kernel, tb=tb, seq=l),
        grid=(b // tb,),
        in_specs=[pl.BlockSpec((tb, l, d), lambda i: (i, 0, 0)),
                  pl.BlockSpec((tb, CONV_WIDTH - 1, CONV_DIM), lambda i: (i, 0, 0))]
        + [_const_spec(a.shape) for a in (gain, wqkvz, wba, cw, alog, dtb)],
        out_specs=[tok(QK_DIM), tok(QK_DIM), tok(V_DIM), tok(V_DIM), tok(LANES), tok(LANES),
                   pl.BlockSpec((tb, CONV_WIDTH - 1, CONV_DIM), lambda i: (i, 0, 0))],
        out_shape=_gdn_pre_out_shapes(b * l)
        + [jax.ShapeDtypeStruct((b, CONV_WIDTH - 1, CONV_DIM), F32)],
        scratch_shapes=[pltpu.VMEM((tb, CONV_HALO + l, CONV_DIM), F32)],
        compiler_params=_params("arbitrary"),
        name="gdn_pre_sample",
    )(x, buf, gain, wqkvz, wba, cw, alog, dtb)


def _unit_lower_inverses(mats, c):
    ri = lax.broadcasted_iota(jnp.int32, (c, c), 0)
    ci = lax.broadcasted_iota(jnp.int32, (c, c), 1)
    eye = (ri == ci).astype(F32)
    pair = ((ri // 2) == (ci // 2)) & (ri > ci)
    xs = [eye - jnp.where(pair, a, 0.0) for a in mats]
    blk = 2
    while blk < c:
        off = ((ri // (2 * blk)) == (ci // (2 * blk))) & ((ri // blk) > (ci // blk))
        ys = [_dot(jnp.where(off, a, 0.0), x) for a, x in zip(mats, xs)]
        xs = [x - _dot(x, y) for x, y in zip(xs, ys)]
        blk *= 2
    return xs


def _delta_chunk(q, k, v, z, gcum, beta, onorm, state_load, state_store, og_ref, c):
    ri = lax.broadcasted_iota(jnp.int32, (c, c), 0)
    ci = lax.broadcasted_iota(jnp.int32, (c, c), 1)
    causal = ri >= ci
    strict = ri > ci
    gcum_t = gcum.T
    rep = V_HEADS // K_HEADS
    heads = range(V_HEADS)
    head = lambda x, i: x[:, i * HEAD_DIM:(i + 1) * HEAD_DIM]
    qs = [head(q, j).astype(BF16) for j in range(K_HEADS)]
    ks = [head(k, j) for j in range(K_HEADS)]
    kbf = [x.astype(BF16) for x in ks]
    kk = [_dot_nt(kbf[j], kbf[j]) for j in range(K_HEADS)]
    qk = [_dot_nt(qs[j], kbf[j]) for j in range(K_HEADS)]
    gcol = [gcum[:, h:h + 1] for h in heads]
    bcol = [beta[:, h:h + 1] for h in heads]
    glast = [gcum[c - 1:c, h:h + 1] for h in heads]
    decay = [jnp.where(causal, jnp.exp(jnp.minimum(gcol[h] - gcum_t[h:h + 1, :], 0.0)), 0.0)
             for h in heads]
    egc = [jnp.exp(gcol[h]) for h in heads]
    a_mats = [jnp.where(strict, kk[h // rep] * bcol[h] * decay[h], 0.0) for h in heads]
    t_inv = _unit_lower_inverses(a_mats, c)
    uw = [_dot(t_inv[h], jnp.concatenate([head(v, h) * bcol[h],
                                          ks[h // rep] * (bcol[h] * egc[h])], axis=1))
          for h in heads]
    s_old = [state_load(h) for h in heads]
    ws = [_dot(jnp.concatenate([uw[h][:, HEAD_DIM:], head(q, h // rep) * egc[h]], axis=0),
               s_old[h]) for h in heads]
    v_new = [uw[h][:, :HEAD_DIM] - ws[h][:c] for h in heads]
    o = [ws[h][c:] + _dot(qk[h // rep] * decay[h], v_new[h]) for h in heads]
    for h in heads:
        k_dec = ks[h // rep] * jnp.exp(glast[h] - gcol[h])
        state_store(h, s_old[h] * jnp.exp(glast[h]) + _dot_tn(k_dec, v_new[h]))
    for h in heads:
        og_ref[:, h * HEAD_DIM:(h + 1) * HEAD_DIM] = (
            _rms(o[h], onorm) * _silu(head(z, h))).astype(BF16)


def _cumsum_rows(g, c):
    ri = lax.broadcasted_iota(jnp.int32, (c, c), 0)
    ci = lax.broadcasted_iota(jnp.int32, (c, c), 1)
    tri = (ri >= ci).astype(F32)
    return jnp.dot(tri, g, preferred_element_type=F32, precision=lax.Precision.HIGHEST)


def _gdn_scan_prompt_kernel(q_ref, k_ref, v_ref, z_ref, beta_ref, g_ref, onorm_ref,
                            og_ref, s_ref, *, c):
    @pl.when(pl.program_id(1) == 0)
    def _():
        s_ref[...] = jnp.zeros(s_ref.shape, F32)

    def load(hh):
        return s_ref[0, hh]

    def store(hh, val):
        s_ref[0, hh] = val

    _delta_chunk(q_ref[...], k_ref[...], v_ref[...], z_ref[...], _cumsum_rows(g_ref[...], c),
                 beta_ref[...], onorm_ref[...], load, store, og_ref, c)


def _gdn_scan_sample_kernel(q_ref, k_ref, v_ref, z_ref, beta_ref, g_ref, onorm_ref, s0_ref,
                            og_ref, s_ref, *, tb, c):
    def body(i, carry):
        rows = pl.ds(pl.multiple_of(i * c, c), c)

        def load(hh):
            return s0_ref[i, hh]

        def store(hh, val):
            s_ref[i, hh] = val

        _delta_chunk(q_ref[rows, :], k_ref[rows, :], v_ref[rows, :], z_ref[rows, :],
                     _cumsum_rows(g_ref[rows, :], c), beta_ref[rows, :], onorm_ref[...],
                     load, store, og_ref.at[rows, :], c)
        return carry

    lax.fori_loop(0, tb, body, 0)


def _gdn_scan_prompt(q, k, v, z, beta, g, onorm, b, l):
    c = PROMPT_CHUNK
    nc = l // c
    tok = lambda w: pl.BlockSpec((c, w), lambda i, j: (i * nc + j, 0))
    return pl.pallas_call(
        functools.partial(_gdn_scan_prompt_kernel, c=c),
        grid=(b, nc),
        in_specs=[tok(QK_DIM), tok(QK_DIM), tok(V_DIM), tok(V_DIM), tok(LANES), tok(LANES),
                  _const_spec(onorm.shape)],
        out_specs=[tok(V_DIM),
                   pl.BlockSpec((1, V_HEADS, HEAD_DIM, HEAD_DIM), lambda i, j: (i, 0, 0, 0))],
        out_shape=[jax.ShapeDtypeStruct((b * l, V_DIM), BF16),
                   jax.ShapeDtypeStruct((b, V_HEADS, HEAD_DIM, HEAD_DIM), F32)],
        compiler_params=_params("arbitrary", "arbitrary"),
        name="gdn_scan_prompt",
    )(q, k, v, z, beta, g, onorm)


def _gdn_scan_sample(q, k, v, z, beta, g, onorm, s0, b, l):
    tb = 4
    tok = lambda w: pl.BlockSpec((tb * l, w), lambda i: (i, 0))
    st = pl.BlockSpec((tb, V_HEADS, HEAD_DIM, HEAD_DIM), lambda i: (i, 0, 0, 0))
    return pl.pallas_call(
        functools.partial(_gdn_scan_sample_kernel, tb=tb, c=l),
        grid=(b // tb,),
        in_specs=[tok(QK_DIM), tok(QK_DIM), tok(V_DIM), tok(V_DIM), tok(LANES), tok(LANES),
                  _const_spec(onorm.shape), st],
        out_specs=[tok(V_DIM), st],
        out_shape=[jax.ShapeDtypeStruct((b * l, V_DIM), BF16),
                   jax.ShapeDtypeStruct((b, V_HEADS, HEAD_DIM, HEAD_DIM), F32)],
        compiler_params=_params("arbitrary"),
        name="gdn_scan_sample",
    )(q, k, v, z, beta, g, onorm, s0)


def _gdn_post_kernel(og_ref, x_ref, gains_ref, wo_ref, win_ref, wout_ref, y_ref):
    m = jnp.dot(og_ref[...], wo_ref[...], preferred_element_type=F32)
    y_ref[...] = _residual_ffn(x_ref[...], m, gains_ref[0:1, :], gains_ref[1:2, :],
                               gains_ref[2:3, :], win_ref, wout_ref)


def _gdn_post(og, x, gains, wo, win, wout):
    n, d = x.shape
    tm = PROMPT_TILE
    return pl.pallas_call(
        _gdn_post_kernel,
        grid=(n // tm,),
        in_specs=[pl.BlockSpec((tm, V_DIM), lambda i: (i, 0)),
                  pl.BlockSpec((tm, d), lambda i: (i, 0))]
        + [_const_spec(a.shape) for a in (gains, wo, win, wout)],
        out_specs=pl.BlockSpec((tm, d), lambda i: (i, 0)),
        out_shape=jax.ShapeDtypeStruct((n, d), F32),
        compiler_params=_params("arbitrary"),
        name="gdn_post",
    )(og, x, gains, wo, win, wout)


def _head_lanes(vec):
    return jnp.pad(vec.astype(F32), (0, LANES - V_HEADS)).reshape(1, LANES)


def kernel(x_prompt, x_sample, state_pool, state_gdn_conv, state_gdn_rec, norm_mix_pre,
           norm_mix_post, norm_ffn_pre, norm_ffn_post, pool_w, pool_scale, gdn_w_in,
           gdn_conv_w, gdn_a_log, gdn_dt_bias, gdn_o_norm, gdn_w_out, ffn_w_in, ffn_w_out):
    bp, lp, d = x_prompt.shape
    bs, ls, _ = x_sample.shape

    gains0 = jnp.stack([norm_mix_pre[0], norm_mix_post[0], norm_ffn_pre[0], norm_ffn_post[0]])
    gains1 = jnp.stack([norm_mix_post[1], norm_ffn_pre[1], norm_ffn_post[1]])
    gain1_pre = norm_mix_pre[1].reshape(1, d)
    pw = pool_w[0].astype(BF16)
    ps = pool_scale[0].reshape(1, d)
    win0, wout0 = ffn_w_in[0].astype(BF16), ffn_w_out[0].astype(BF16)
    win1, wout1 = ffn_w_in[1].astype(BF16), ffn_w_out[1].astype(BF16)
    w_in = gdn_w_in[0]
    wqkvz = w_in[:, :CONV_DIM + V_DIM].astype(BF16)
    w_b = w_in[:, CONV_DIM + V_DIM:CONV_DIM + V_DIM + V_HEADS]
    w_a = w_in[:, CONV_DIM + V_DIM + V_HEADS:]
    lane_pad = ((0, 0), (0, LANES - V_HEADS))
    wba = jnp.concatenate([jnp.pad(w_b, lane_pad), jnp.pad(w_a, lane_pad)], axis=1).astype(BF16)
    cw = gdn_conv_w[0]
    alog, dtb = _head_lanes(gdn_a_log[0]), _head_lanes(gdn_dt_bias[0])
    onorm = gdn_o_norm[0].reshape(1, HEAD_DIM)
    wo = gdn_w_out[0].astype(BF16)

    xp1, pool_p = _pool_layer_prompt(x_prompt, gains0, pw, ps, win0, wout0)
    xs1, pool_s = _pool_layer_sample(x_sample, state_pool[0], gains0, pw, ps, win0, wout0)

    qp, kp, vp, zp, betap, gp, conv_p = _gdn_pre_prompt(xp1, gain1_pre, wqkvz, wba, cw, alog, dtb)
    qs, ks, vs, zs, betas, gs, conv_s = _gdn_pre_sample(xs1, state_gdn_conv[0], gain1_pre, wqkvz,
                                                        wba, cw, alog, dtb)

    ogp, rec_p = _gdn_scan_prompt(qp, kp, vp, zp, betap, gp, onorm, bp, lp)
    ogs, rec_s = _gdn_scan_sample(qs, ks, vs, zs, betas, gs, onorm, state_gdn_rec[0], bs, ls)

    yp = _gdn_post(ogp, xp1.reshape(bp * lp, d), gains1, wo, win1, wout1).reshape(bp, lp, d)
    ys = _gdn_post(ogs, xs1.reshape(bs * ls, d), gains1, wo, win1, wout1).reshape(bs, ls, d)

    return (yp, ys, pool_p[None], pool_s[None], conv_p[None], conv_s[None], rec_p[None],
            rec_s[None])
```

```python
import functools

import jax
import jax.numpy as jnp
from jax import lax
from jax.experimental import pallas as pl
from jax.experimental.pallas import tpu as pltpu

D_MODEL = 1024
POOL_WINDOWS = (2, 4, 8, 16)
POOL_GROUP_DIM = D_MODEL // len(POOL_WINDOWS)
POOL_BUF = max(POOL_WINDOWS) - 1
K_HEADS = 8
V_HEADS = 16
HEAD_DIM = 128
QK_DIM = K_HEADS * HEAD_DIM
V_DIM = V_HEADS * HEAD_DIM
CONV_DIM = 2 * QK_DIM + V_DIM
CONV_WIDTH = 4
D_FF = 2816
EPS = 1e-6

F32 = jnp.float32
BF16 = jnp.bfloat16

SUBLANES = 8
LANES = 128
POOL_HALO = 16
CONV_HALO = SUBLANES
VMEM_LIMIT = 56 * 1024 * 1024
ROW_STRIDE = 4
ROW_GROUP = SUBLANES * ROW_STRIDE

PROMPT_TILE = 512
GDN_PRE_TILE = 256
SAMPLE_BTILE = 32
PROMPT_CHUNK = 64


def _rms(x, gain):
    ms = jnp.mean(x * x, axis=-1, keepdims=True)
    return x * lax.rsqrt(ms + EPS) * gain


def _sigmoid(x):
    return 1.0 / (1.0 + jnp.exp(-x))


def _silu(x):
    return x * _sigmoid(x)


def _softplus(x):
    return jnp.maximum(x, 0.0) + jnp.log1p(jnp.exp(-jnp.abs(x)))


def _dot(a, b):
    return jnp.dot(a.astype(BF16), b.astype(BF16), preferred_element_type=F32)


def _dot_nt(a, b):
    return lax.dot_general(a.astype(BF16), b.astype(BF16), (((1,), (1,)), ((), ())),
                           preferred_element_type=F32)


def _dot_tn(a, b):
    return lax.dot_general(a.astype(BF16), b.astype(BF16), (((0,), (0,)), ((), ())),
                           preferred_element_type=F32)


def _const_spec(shape):
    nd = len(shape)
    return pl.BlockSpec(shape, lambda *_: (0,) * nd, pipeline_mode=pl.Buffered(1))


def _params(*sem):
    return pltpu.CompilerParams(dimension_semantics=sem, vmem_limit_bytes=VMEM_LIMIT)


def _residual_ffn(x, m, g_post, g_fpre, g_fpost, win_ref, wout_ref):
    x1 = x + _rms(m, g_post)
    h = _rms(x1, g_fpre).astype(BF16)
    gate = jnp.dot(h, win_ref[:, :D_FF], preferred_element_type=F32)
    up = jnp.dot(h, win_ref[:, D_FF:], preferred_element_type=F32)
    act = (_silu(gate) * up).astype(BF16)
    f = jnp.dot(act, wout_ref[...], preferred_element_type=F32)
    return x1 + _rms(f, g_fpost)


def _pool_project(diffs, pw_ref, scale):
    parts = [_dot(d, pw_ref[gi]) for gi, d in enumerate(diffs)]
    return jnp.concatenate(parts, axis=-1) * scale


def _pool_layer_prompt_kernel(x_ref, gains_ref, pw_ref, ps_ref, win_ref, wout_ref,
                              y_ref, pool_ref, hp_ref, d_ref, *, tm):
    j = pl.program_id(1)
    x = x_ref[0]
    h = _rms(x, gains_ref[0:1, :])

    n_blk = D_MODEL // LANES
    blk_per_grp = POOL_GROUP_DIM // LANES

    @pl.when(j == 0)
    def _():
        hp_ref[:, 0:POOL_HALO, :] = jnp.zeros((n_blk, POOL_HALO, LANES), F32)

    for cb in range(n_blk):
        hp_ref[cb, POOL_HALO:POOL_HALO + tm, :] = h[:, cb * LANES:(cb + 1) * LANES]
    t_tile = lax.broadcasted_iota(jnp.int32, (SUBLANES, 1), 0) * ROW_STRIDE + (j * tm + 1)

    def pool_group(grp, carry):
        g0 = pl.multiple_of(grp * ROW_GROUP, ROW_GROUP)
        for r in range(ROW_STRIDE):
            for gi, win in enumerate(POOL_WINDOWS):
                inv = 1.0 / jnp.minimum(win, t_tile + (g0 + r)).astype(F32)
                for cb in range(gi * blk_per_grp, (gi + 1) * blk_per_grp):
                    cur = hp_ref[cb, _strided_rows(g0 + (POOL_HALO + r)), :]
                    tot = cur
                    for s in range(1, win):
                        tot = tot + hp_ref[cb, _strided_rows(g0 + (POOL_HALO + r - s)), :]
                    d_ref[cb, _strided_rows(g0 + r), :] = tot * inv - cur
        return carry

    lax.fori_loop(0, tm // ROW_GROUP, pool_group, 0)
    diffs = [jnp.concatenate([d_ref[cb] for cb in range(gi * blk_per_grp, (gi + 1) * blk_per_grp)],
                             axis=1) for gi in range(len(POOL_WINDOWS))]
    m = _pool_project(diffs, pw_ref, ps_ref[...])

    @pl.when(j == pl.num_programs(1) - 1)
    def _():
        for cb in range(n_blk):
            pool_ref[0, :, cb * LANES:(cb + 1) * LANES] = hp_ref[
                cb, tm + POOL_HALO - POOL_BUF:tm + POOL_HALO, :]

    hp_ref[:, 0:POOL_HALO, :] = hp_ref[:, tm:tm + POOL_HALO, :]
    y_ref[0] = _residual_ffn(x, m, gains_ref[1:2, :], gains_ref[2:3, :], gains_ref[3:4, :],
                             win_ref, wout_ref)


def _pool_layer_sample_kernel(x_ref, buf_ref, gains_ref, pw_ref, ps_ref, win_ref, wout_ref,
                              y_ref, pool_ref, hp_ref, *, tb, seq, n_past):
    x = x_ref[...]
    h = _rms(x, gains_ref[0:1, :])
    hp_ref[:, POOL_HALO - POOL_BUF:POOL_HALO, :] = buf_ref[...]
    hp_ref[:, POOL_HALO:POOL_HALO + seq, :] = h
    t = lax.broadcasted_iota(jnp.int32, (1, seq, 1), 1)
    diffs = []
    for gi, win in enumerate(POOL_WINDOWS):
        c0, c1 = gi * POOL_GROUP_DIM, (gi + 1) * POOL_GROUP_DIM
        cur = hp_ref[:, POOL_HALO:POOL_HALO + seq, c0:c1]
        tot = cur
        for s in range(1, win):
            tot = tot + hp_ref[:, POOL_HALO - s:POOL_HALO - s + seq, c0:c1]
        cnt = jnp.minimum(win, t + 1 + n_past).astype(F32)
        diffs.append((tot / cnt - cur).reshape(tb * seq, POOL_GROUP_DIM))
    m = _pool_project(diffs, pw_ref, ps_ref[...])
    pool_ref[...] = hp_ref[:, POOL_HALO + seq - POOL_BUF:POOL_HALO + seq, :]
    y = _residual_ffn(x.reshape(tb * seq, D_MODEL), m, gains_ref[1:2, :], gains_ref[2:3, :],
                      gains_ref[3:4, :], win_ref, wout_ref)
    y_ref[...] = y.reshape(tb, seq, D_MODEL)


def _pool_layer_prompt(x, gains, pw, ps, win, wout):
    b, l, d = x.shape
    tm = PROMPT_TILE
    return pl.pallas_call(
        functools.partial(_pool_layer_prompt_kernel, tm=tm),
        grid=(b, l // tm),
        in_specs=[
            pl.BlockSpec((1, tm, d), lambda i, j: (i, j, 0)),
            _const_spec(gains.shape), _const_spec(pw.shape), _const_spec(ps.shape),
            _const_spec(win.shape), _const_spec(wout.shape),
        ],
        out_specs=[
            pl.BlockSpec((1, tm, d), lambda i, j: (i, j, 0)),
            pl.BlockSpec((1, POOL_BUF, d), lambda i, j: (i, 0, 0)),
        ],
        out_shape=[jax.ShapeDtypeStruct((b, l, d), F32),
                   jax.ShapeDtypeStruct((b, POOL_BUF, d), F32)],
        scratch_shapes=[pltpu.VMEM((d // LANES, POOL_HALO + tm, LANES), F32),
                        pltpu.VMEM((d // LANES, tm, LANES), F32)],
        compiler_params=_params("arbitrary", "arbitrary"),
        name="pool_layer_prompt",
    )(x, gains, pw, ps, win, wout)


def _pool_layer_sample(x, buf, gains, pw, ps, win, wout):
    b, l, d = x.shape
    tb = SAMPLE_BTILE
    n_past = buf.shape[1]
    return pl.pallas_call(
        functools.partial(_pool_layer_sample_kernel, tb=tb, seq=l, n_past=n_past),
        grid=(b // tb,),
        in_specs=[
            pl.BlockSpec((tb, l, d), lambda i: (i, 0, 0)),
            pl.BlockSpec((tb, POOL_BUF, d), lambda i: (i, 0, 0)),
            _const_spec(gains.shape), _const_spec(pw.shape), _const_spec(ps.shape),
            _const_spec(win.shape), _const_spec(wout.shape),
        ],
        out_specs=[
            pl.BlockSpec((tb, l, d), lambda i: (i, 0, 0)),
            pl.BlockSpec((tb, POOL_BUF, d), lambda i: (i, 0, 0)),
        ],
        out_shape=[jax.ShapeDtypeStruct((b, l, d), F32),
                   jax.ShapeDtypeStruct((b, POOL_BUF, d), F32)],
        scratch_shapes=[pltpu.VMEM((tb, POOL_HALO + l, d), F32)],
        compiler_params=_params("arbitrary"),
        name="pool_layer_sample",
    )(x, buf, gains, pw, ps, win, wout)


def _gdn_qkv(conv, q_ref, k_ref, v_ref, rows):
    for hh in range(K_HEADS):
        qh = conv[hh]
        kh = conv[K_HEADS + hh]
        qn = qh * lax.rsqrt(jnp.sum(qh * qh, axis=-1, keepdims=True) + EPS)
        q_ref[hh, rows, :] = qn * (HEAD_DIM ** -0.5)
        k_ref[hh, rows, :] = kh * lax.rsqrt(jnp.sum(kh * kh, axis=-1, keepdims=True) + EPS)
    for hh in range(V_HEADS):
        v_ref[hh, rows, :] = conv[2 * K_HEADS + hh]


def _gdn_gates(ba, alog, dtb, beta_ref, g_ref):
    beta_ref[...] = _sigmoid(ba[:, :LANES])
    g_ref[...] = -jnp.exp(alog) * _softplus(ba[:, LANES:] + dtb)


def _strided_rows(first):
    return pl.ds(first, SUBLANES, stride=ROW_STRIDE)


def _gdn_pre_prompt_kernel(x_ref, gain_ref, wqkvz_ref, wba_ref, cw_ref, alog_ref, dtb_ref,
                           q_ref, k_ref, v_ref, z_ref, beta_ref, g_ref, conv_ref, up_ref, *, tm):
    j = pl.program_id(1)
    h = _rms(x_ref[0], gain_ref[...]).astype(BF16)
    proj = jnp.dot(h, wqkvz_ref[...], preferred_element_type=F32)
    ba = jnp.dot(h, wba_ref[...], preferred_element_type=F32)
    for hh in range(V_HEADS):
        z_ref[hh] = proj[:, CONV_DIM + hh * HEAD_DIM:CONV_DIM + (hh + 1) * HEAD_DIM]
    n_slab = CONV_DIM // HEAD_DIM

    @pl.when(j == 0)
    def _():
        up_ref[:, 0:CONV_HALO, :] = jnp.zeros((n_slab, CONV_HALO, HEAD_DIM), F32)

    for sl in range(n_slab):
        up_ref[sl, CONV_HALO:CONV_HALO + tm, :] = proj[:, sl * HEAD_DIM:(sl + 1) * HEAD_DIM]
    _gdn_gates(ba, alog_ref[...], dtb_ref[...], beta_ref, g_ref)
    base = CONV_HALO - (CONV_WIDTH - 1)

    def conv_group(grp, carry):
        g0 = pl.multiple_of(grp * ROW_GROUP, ROW_GROUP)
        for r in range(ROW_STRIDE):
            conv = []
            for sl in range(n_slab):
                lanes = slice(sl * HEAD_DIM, (sl + 1) * HEAD_DIM)
                acc = up_ref[sl, _strided_rows(g0 + (base + r)), :] * cw_ref[0:1, lanes]
                for tap in range(1, CONV_WIDTH):
                    acc = acc + (up_ref[sl, _strided_rows(g0 + (base + tap + r)), :]
                                 * cw_ref[tap:tap + 1, lanes])
                conv.append(_silu(acc))
            _gdn_qkv(conv, q_ref, k_ref, v_ref, _strided_rows(g0 + r))
        return carry

    lax.fori_loop(0, tm // ROW_GROUP, conv_group, 0)

    @pl.when(j == pl.num_programs(1) - 1)
    def _():
        for sl in range(n_slab):
            conv_ref[0, :, sl * HEAD_DIM:(sl + 1) * HEAD_DIM] = up_ref[
                sl, tm + CONV_HALO - (CONV_WIDTH - 1):tm + CONV_HALO, :]

    up_ref[:, 0:CONV_HALO, :] = up_ref[:, tm:tm + CONV_HALO, :]


def _gdn_pre_sample_kernel(x_ref, buf_ref, gain_ref, wqkvz_ref, wba_ref, cw_ref, alog_ref,
                           dtb_ref, q_ref, k_ref, v_ref, z_ref, beta_ref, g_ref, conv_ref, up_ref,
                           *, tb, seq):
    m = tb * seq
    h = _rms(x_ref[...].reshape(m, D_MODEL), gain_ref[...]).astype(BF16)
    proj = jnp.dot(h, wqkvz_ref[...], preferred_element_type=F32)
    ba = jnp.dot(h, wba_ref[...], preferred_element_type=F32)
    for hh in range(V_HEADS):
        z_ref[hh] = proj[:, CONV_DIM + hh * HEAD_DIM:CONV_DIM + (hh + 1) * HEAD_DIM]
    base = CONV_HALO - (CONV_WIDTH - 1)
    up_ref[:, base:CONV_HALO, :] = buf_ref[...]
    up_ref[:, CONV_HALO:CONV_HALO + seq, :] = proj[:, :CONV_DIM].reshape(tb, seq, CONV_DIM)
    acc = up_ref[:, base:base + seq, :] * cw_ref[0:1, :]
    for tap in range(1, CONV_WIDTH):
        acc = acc + up_ref[:, base + tap:base + tap + seq, :] * cw_ref[tap:tap + 1, :]
    conv_ref[...] = up_ref[:, CONV_HALO + seq - (CONV_WIDTH - 1):CONV_HALO + seq, :]
    _gdn_gates(ba, alog_ref[...], dtb_ref[...], beta_ref, g_ref)
    conv = _silu(acc).reshape(m, CONV_DIM)
    _gdn_qkv([conv[:, sl * HEAD_DIM:(sl + 1) * HEAD_DIM] for sl in range(CONV_DIM // HEAD_DIM)],
             q_ref, k_ref, v_ref, slice(None))


def _gdn_pre_out_shapes(n):
    return [jax.ShapeDtypeStruct((K_HEADS, n, HEAD_DIM), F32),
            jax.ShapeDtypeStruct((K_HEADS, n, HEAD_DIM), F32),
            jax.ShapeDtypeStruct((V_HEADS, n, HEAD_DIM), F32),
            jax.ShapeDtypeStruct((V_HEADS, n, HEAD_DIM), F32),
            jax.ShapeDtypeStruct((n, LANES), F32), jax.ShapeDtypeStruct((n, LANES), F32)]


def _gdn_pre_out_specs(rows, index):
    heads = lambda n: pl.BlockSpec((n, rows, HEAD_DIM), lambda *g: (0, index(*g), 0))
    lane = pl.BlockSpec((rows, LANES), lambda *g: (index(*g), 0))
    return [heads(K_HEADS), heads(K_HEADS), heads(V_HEADS), heads(V_HEADS), lane, lane]


def _gdn_pre_prompt(x, gain, wqkvz, wba, cw, alog, dtb):
    b, l, d = x.shape
    tm = GDN_PRE_TILE
    nj = l // tm
    return pl.pallas_call(
        functools.partial(_gdn_pre_prompt_kernel, tm=tm),
        grid=(b, nj),
        in_specs=[pl.BlockSpec((1, tm, d), lambda i, j: (i, j, 0))]
        + [_const_spec(a.shape) for a in (gain, wqkvz, wba, cw, alog, dtb)],
        out_specs=_gdn_pre_out_specs(tm, lambda i, j: i * nj + j)
        + [pl.BlockSpec((1, CONV_WIDTH - 1, CONV_DIM), lambda i, j: (i, 0, 0))],
        out_shape=_gdn_pre_out_shapes(b * l)
        + [jax.ShapeDtypeStruct((b, CONV_WIDTH - 1, CONV_DIM), F32)],
        scratch_shapes=[pltpu.VMEM((CONV_DIM // HEAD_DIM, CONV_HALO + tm, HEAD_DIM), F32)],
        compiler_params=_params("arbitrary", "arbitrary"),
        name="gdn_pre_prompt",
    )(x, gain, wqkvz, wba, cw, alog, dtb)


def _gdn_pre_sample(x, buf, gain, wqkvz, wba, cw, alog, dtb):
    b, l, d = x.shape
    tb = SAMPLE_BTILE
    return pl.pallas_call(
        functools.partial(_gdn_pre_sample_kernel, tb=tb, seq=l),
        grid=(b // tb,),
        in_specs=[pl.BlockSpec((tb, l, d), lambda i: (i, 0, 0)),
                  pl.BlockSpec((tb, CONV_WIDTH - 1, CONV_DIM), lambda i: (i, 0, 0))]
        + [_const_spec(a.shape) for a in (gain, wqkvz, wba, cw, alog, dtb)],
        out_specs=_gdn_pre_out_specs(tb * l, lambda i: i)
        + [pl.BlockSpec((tb, CONV_WIDTH - 1, CONV_DIM), lambda i: (i, 0, 0))],
        out_shape=_gdn_pre_out_shapes(b * l)
        + [jax.ShapeDtypeStruct((b, CONV_WIDTH - 1, CONV_DIM), F32)],
        scratch_shapes=[pltpu.VMEM((tb, CONV_HALO + l, CONV_DIM), F32)],
        compiler_params=_params("arbitrary"),
        name="gdn_pre_sample",
    )(x, buf, gain, wqkvz, wba, cw, alog, dtb)


def _unit_lower_inverses(mats, c):
    ri = lax.broadcasted_iota(jnp.int32, (c, c), 0)
    ci = lax.broadcasted_iota(jnp.int32, (c, c), 1)
    eye = (ri == ci).astype(F32)
    pair = ((ri // 2) == (ci // 2)) & (ri > ci)
    xs = [eye - jnp.where(pair, a, 0.0) for a in mats]
    blk = 2
    while blk < c:
        off = ((ri // (2 * blk)) == (ci // (2 * blk))) & ((ri // blk) > (ci // blk))
        ys = [_dot(jnp.where(off, a, 0.0), x) for a, x in zip(mats, xs)]
        xs = [x - _dot(x, y) for x, y in zip(xs, ys)]
        blk *= 2
    return xs


def _delta_chunk(q, k, v, z, gcum, beta, onorm, state_load, state_store, og_ref, c):
    ri = lax.broadcasted_iota(jnp.int32, (c, c), 0)
    ci = lax.broadcasted_iota(jnp.int32, (c, c), 1)
    causal = ri >= ci
    strict = ri > ci
    gcum_t = gcum.T
    rep = V_HEADS // K_HEADS
    heads = range(V_HEADS)
    head = lambda x, i: x[i]
    qs = [head(q, j).astype(BF16) for j in range(K_HEADS)]
    ks = [head(k, j) for j in range(K_HEADS)]
    kbf = [x.astype(BF16) for x in ks]
    kk = [_dot_nt(kbf[j], kbf[j]) for j in range(K_HEADS)]
    qk = [_dot_nt(qs[j], kbf[j]) for j in range(K_HEADS)]
    gcol = [gcum[:, h:h + 1] for h in heads]
    bcol = [beta[:, h:h + 1] for h in heads]
    glast = [gcum[c - 1:c, h:h + 1] for h in heads]
    decay = [jnp.where(causal, jnp.exp(jnp.minimum(gcol[h] - gcum_t[h:h + 1, :], 0.0)), 0.0)
             for h in heads]
    egc = [jnp.exp(gcol[h]) for h in heads]
    a_mats = [jnp.where(strict, kk[h // rep] * bcol[h] * decay[h], 0.0) for h in heads]
    t_inv = _unit_lower_inverses(a_mats, c)
    uw = [_dot(t_inv[h], jnp.concatenate([head(v, h) * bcol[h],
                                          ks[h // rep] * (bcol[h] * egc[h])], axis=1))
          for h in heads]
    s_old = [state_load(h) for h in heads]
    ws = [_dot(jnp.concatenate([uw[h][:, HEAD_DIM:], head(q, h // rep) * egc[h]], axis=0),
               s_old[h]) for h in heads]
    v_new = [uw[h][:, :HEAD_DIM] - ws[h][:c] for h in heads]
    o = [ws[h][c:] + _dot(qk[h // rep] * decay[h], v_new[h]) for h in heads]
    for h in heads:
        k_dec = ks[h // rep] * jnp.exp(glast[h] - gcol[h])
        state_store(h, s_old[h] * jnp.exp(glast[h]) + _dot_tn(k_dec, v_new[h]))
    for h in heads:
        og_ref[:, h * HEAD_DIM:(h + 1) * HEAD_DIM] = (
            _rms(o[h], onorm) * _silu(head(z, h))).astype(BF16)


def _cumsum_rows(g, c):
    ri = lax.broadcasted_iota(jnp.int32, (c, c), 0)
    ci = lax.broadcasted_iota(jnp.int32, (c, c), 1)
    tri = (ri >= ci).astype(F32)
    return jnp.dot(tri, g, preferred_element_type=F32, precision=lax.Precision.HIGHEST)


def _gdn_scan_prompt_kernel(q_ref, k_ref, v_ref, z_ref, beta_ref, g_ref, onorm_ref,
                            og_ref, s_ref, *, c):
    @pl.when(pl.program_id(1) == 0)
    def _():
        s_ref[...] = jnp.zeros(s_ref.shape, F32)

    def load(hh):
        return s_ref[0, hh]

    def store(hh, val):
        s_ref[0, hh] = val

    _delta_chunk(q_ref[...], k_ref[...], v_ref[...], z_ref[...], _cumsum_rows(g_ref[...], c),
                 beta_ref[...], onorm_ref[...], load, store, og_ref, c)


def _gdn_scan_sample_kernel(q_ref, k_ref, v_ref, z_ref, beta_ref, g_ref, onorm_ref, s0_ref,
                            og_ref, s_ref, *, tb, c):
    def body(i, carry):
        rows = pl.ds(pl.multiple_of(i * c, c), c)

        def load(hh):
            return s0_ref[i, hh]

        def store(hh, val):
            s_ref[i, hh] = val

        _delta_chunk(q_ref[:, rows, :], k_ref[:, rows, :], v_ref[:, rows, :], z_ref[:, rows, :],
                     _cumsum_rows(g_ref[rows, :], c), beta_ref[rows, :], onorm_ref[...],
                     load, store, og_ref.at[rows, :], c)
        return carry

    lax.fori_loop(0, tb, body, 0)


def _gdn_scan_prompt(q, k, v, z, beta, g, onorm, b, l):
    c = PROMPT_CHUNK
    nc = l // c
    tok = lambda w: pl.BlockSpec((c, w), lambda i, j: (i * nc + j, 0))
    return pl.pallas_call(
        functools.partial(_gdn_scan_prompt_kernel, c=c),
        grid=(b, nc),
        in_specs=_gdn_pre_out_specs(c, lambda i, j: i * nc + j) + [_const_spec(onorm.shape)],
        out_specs=[tok(V_DIM),
                   pl.BlockSpec((1, V_HEADS, HEAD_DIM, HEAD_DIM), lambda i, j: (i, 0, 0, 0))],
        out_shape=[jax.ShapeDtypeStruct((b * l, V_DIM), BF16),
                   jax.ShapeDtypeStruct((b, V_HEADS, HEAD_DIM, HEAD_DIM), F32)],
        compiler_params=_params("arbitrary", "arbitrary"),
        name="gdn_scan_prompt",
    )(q, k, v, z, beta, g, onorm)


def _gdn_scan_sample(q, k, v, z, beta, g, onorm, s0, b, l):
    tb = 4
    tok = lambda w: pl.BlockSpec((tb * l, w), lambda i: (i, 0))
    st = pl.BlockSpec((tb, V_HEADS, HEAD_DIM, HEAD_DIM), lambda i: (i, 0, 0, 0))
    return pl.pallas_call(
        functools.partial(_gdn_scan_sample_kernel, tb=tb, c=l),
        grid=(b // tb,),
        in_specs=_gdn_pre_out_specs(tb * l, lambda i: i) + [_const_spec(onorm.shape), st],
        out_specs=[tok(V_DIM), st],
        out_shape=[jax.ShapeDtypeStruct((b * l, V_DIM), BF16),
                   jax.ShapeDtypeStruct((b, V_HEADS, HEAD_DIM, HEAD_DIM), F32)],
        compiler_params=_params("arbitrary"),
        name="gdn_scan_sample",
    )(q, k, v, z, beta, g, onorm, s0)


def _gdn_post_kernel(og_ref, x_ref, gains_ref, wo_ref, win_ref, wout_ref, y_ref):
    m = jnp.dot(og_ref[...], wo_ref[...], preferred_element_type=F32)
    y_ref[...] = _residual_ffn(x_ref[...], m, gains_ref[0:1, :], gains_ref[1:2, :],
                               gains_ref[2:3, :], win_ref, wout_ref)


def _gdn_post(og, x, gains, wo, win, wout):
    n, d = x.shape
    tm = PROMPT_TILE
    return pl.pallas_call(
        _gdn_post_kernel,
        grid=(n // tm,),
        in_specs=[pl.BlockSpec((tm, V_DIM), lambda i: (i, 0)),
                  pl.BlockSpec((tm, d), lambda i: (i, 0))]
        + [_const_spec(a.shape) for a in (gains, wo, win, wout)],
        out_specs=pl.BlockSpec((tm, d), lambda i: (i, 0)),
        out_shape=jax.ShapeDtypeStruct((n, d), F32),
        compiler_params=_params("arbitrary"),
        name="gdn_post",
    )(og, x, gains, wo, win, wout)


def _head_lanes(vec):
    return jnp.pad(vec.astype(F32), (0, LANES - V_HEADS)).reshape(1, LANES)


def kernel(x_prompt, x_sample, state_pool, state_gdn_conv, state_gdn_rec, norm_mix_pre,
           norm_mix_post, norm_ffn_pre, norm_ffn_post, pool_w, pool_scale, gdn_w_in,
           gdn_conv_w, gdn_a_log, gdn_dt_bias, gdn_o_norm, gdn_w_out, ffn_w_in, ffn_w_out):
    bp, lp, d = x_prompt.shape
    bs, ls, _ = x_sample.shape

    gains0 = jnp.stack([norm_mix_pre[0], norm_mix_post[0], norm_ffn_pre[0], norm_ffn_post[0]])
    gains1 = jnp.stack([norm_mix_post[1], norm_ffn_pre[1], norm_ffn_post[1]])
    gain1_pre = norm_mix_pre[1].reshape(1, d)
    pw = pool_w[0].astype(BF16)
    ps = pool_scale[0].reshape(1, d)
    win0, wout0 = ffn_w_in[0].astype(BF16), ffn_w_out[0].astype(BF16)
    win1, wout1 = ffn_w_in[1].astype(BF16), ffn_w_out[1].astype(BF16)
    w_in = gdn_w_in[0]
    wqkvz = w_in[:, :CONV_DIM + V_DIM].astype(BF16)
    w_b = w_in[:, CONV_DIM + V_DIM:CONV_DIM + V_DIM + V_HEADS]
    w_a = w_in[:, CONV_DIM + V_DIM + V_HEADS:]
    lane_pad = ((0, 0), (0, LANES - V_HEADS))
    wba = jnp.concatenate([jnp.pad(w_b, lane_pad), jnp.pad(w_a, lane_pad)], axis=1).astype(BF16)
    cw = gdn_conv_w[0]
    alog, dtb = _head_lanes(gdn_a_log[0]), _head_lanes(gdn_dt_bias[0])
    onorm = gdn_o_norm[0].reshape(1, HEAD_DIM)
    wo = gdn_w_out[0].astype(BF16)

    xp1, pool_p = _pool_layer_prompt(x_prompt, gains0, pw, ps, win0, wout0)
    xs1, pool_s = _pool_layer_sample(x_sample, state_pool[0], gains0, pw, ps, win0, wout0)

    qp, kp, vp, zp, betap, gp, conv_p = _gdn_pre_prompt(xp1, gain1_pre, wqkvz, wba, cw, alog, dtb)
    qs, ks, vs, zs, betas, gs, conv_s = _gdn_pre_sample(xs1, state_gdn_conv[0], gain1_pre, wqkvz,
                                                        wba, cw, alog, dtb)

    ogp, rec_p = _gdn_scan_prompt(qp, kp, vp, zp, betap, gp, onorm, bp, lp)
    ogs, rec_s = _gdn_scan_sample(qs, ks, vs, zs, betas, gs, onorm, state_gdn_rec[0], bs, ls)

    yp = _gdn_post(ogp, xp1.reshape(bp * lp, d), gains1, wo, win1, wout1).reshape(bp, lp, d)
    ys = _gdn_post(ogs, xs1.reshape(bs * ls, d), gains1, wo, win1, wout1).reshape(bs, ls, d)

    return (yp, ys, pool_p[None], pool_s[None], conv_p[None], conv_s[None], rec_p[None],
            rec_s[None])
```

```python
import functools

import jax
import jax.numpy as jnp
from jax import lax
from jax.experimental import pallas as pl
from jax.experimental.pallas import tpu as pltpu

D_MODEL = 1024
POOL_WINDOWS = (2, 4, 8, 16)
POOL_GROUP_DIM = D_MODEL // len(POOL_WINDOWS)
POOL_BUF = max(POOL_WINDOWS) - 1
K_HEADS = 8
V_HEADS = 16
HEAD_DIM = 128
QK_DIM = K_HEADS * HEAD_DIM
V_DIM = V_HEADS * HEAD_DIM
CONV_DIM = 2 * QK_DIM + V_DIM
CONV_WIDTH = 4
D_FF = 2816
EPS = 1e-6

F32 = jnp.float32
BF16 = jnp.bfloat16

SUBLANES = 8
LANES = 128
POOL_HALO = 16
CONV_HALO = SUBLANES
VMEM_LIMIT = 56 * 1024 * 1024
ROW_STRIDE = 4
ROW_GROUP = SUBLANES * ROW_STRIDE

PROMPT_TILE = 512
GDN_PRE_TILE = 256
SAMPLE_BTILE = 32
PROMPT_CHUNK = 64
SCAN_TILE = 256
SCAN_GROUP = 2
SAMPLE_SCAN_BTILE = 4


def _rms(x, gain):
    ms = jnp.mean(x * x, axis=-1, keepdims=True)
    return x * lax.rsqrt(ms + EPS) * gain


def _sigmoid(x):
    return 1.0 / (1.0 + jnp.exp(-x))


def _silu(x):
    return x * _sigmoid(x)


def _softplus(x):
    return jnp.maximum(x, 0.0) + jnp.log1p(jnp.exp(-jnp.abs(x)))


def _dot(a, b):
    return jnp.dot(a.astype(BF16), b.astype(BF16), preferred_element_type=F32)


def _dot_nt(a, b):
    return lax.dot_general(a.astype(BF16), b.astype(BF16), (((1,), (1,)), ((), ())),
                           preferred_element_type=F32)


def _dot_tn(a, b):
    return lax.dot_general(a.astype(BF16), b.astype(BF16), (((0,), (0,)), ((), ())),
                           preferred_element_type=F32)


def _const_spec(shape):
    nd = len(shape)
    return pl.BlockSpec(shape, lambda *_: (0,) * nd, pipeline_mode=pl.Buffered(1))


def _params(*sem):
    return pltpu.CompilerParams(dimension_semantics=sem, vmem_limit_bytes=VMEM_LIMIT)


def _residual_ffn(x, m, g_post, g_fpre, g_fpost, win_ref, wout_ref):
    x1 = x + _rms(m, g_post)
    h = _rms(x1, g_fpre).astype(BF16)
    gate = jnp.dot(h, win_ref[:, :D_FF], preferred_element_type=F32)
    up = jnp.dot(h, win_ref[:, D_FF:], preferred_element_type=F32)
    act = (_silu(gate) * up).astype(BF16)
    f = jnp.dot(act, wout_ref[...], preferred_element_type=F32)
    return x1 + _rms(f, g_fpost)


def _pool_project(diffs, pw_ref, scale):
    parts = [_dot(d, pw_ref[gi]) for gi, d in enumerate(diffs)]
    return jnp.concatenate(parts, axis=-1) * scale


def _pool_layer_prompt_kernel(x_ref, gains_ref, pw_ref, ps_ref, win_ref, wout_ref,
                              y_ref, pool_ref, hp_ref, d_ref, *, tm):
    j = pl.program_id(1)
    x = x_ref[0]
    h = _rms(x, gains_ref[0:1, :])

    n_blk = D_MODEL // LANES
    blk_per_grp = POOL_GROUP_DIM // LANES

    @pl.when(j == 0)
    def _():
        hp_ref[:, 0:POOL_HALO, :] = jnp.zeros((n_blk, POOL_HALO, LANES), F32)

    for cb in range(n_blk):
        hp_ref[cb, POOL_HALO:POOL_HALO + tm, :] = h[:, cb * LANES:(cb + 1) * LANES]
    t_tile = lax.broadcasted_iota(jnp.int32, (SUBLANES, 1), 0) * ROW_STRIDE + (j * tm + 1)

    def pool_group(grp, carry):
        g0 = pl.multiple_of(grp * ROW_GROUP, ROW_GROUP)
        for r in range(ROW_STRIDE):
            for gi, win in enumerate(POOL_WINDOWS):
                inv = 1.0 / jnp.minimum(win, t_tile + (g0 + r)).astype(F32)
                for cb in range(gi * blk_per_grp, (gi + 1) * blk_per_grp):
                    cur = hp_ref[cb, _strided_rows(g0 + (POOL_HALO + r)), :]
                    tot = cur
                    for s in range(1, win):
                        tot = tot + hp_ref[cb, _strided_rows(g0 + (POOL_HALO + r - s)), :]
                    d_ref[cb, _strided_rows(g0 + r), :] = tot * inv - cur
        return carry

    lax.fori_loop(0, tm // ROW_GROUP, pool_group, 0)
    diffs = [jnp.concatenate([d_ref[cb] for cb in range(gi * blk_per_grp, (gi + 1) * blk_per_grp)],
                             axis=1) for gi in range(len(POOL_WINDOWS))]
    m = _pool_project(diffs, pw_ref, ps_ref[...])

    @pl.when(j == pl.num_programs(1) - 1)
    def _():
        for cb in range(n_blk):
            pool_ref[0, :, cb * LANES:(cb + 1) * LANES] = hp_ref[
                cb, tm + POOL_HALO - POOL_BUF:tm + POOL_HALO, :]

    hp_ref[:, 0:POOL_HALO, :] = hp_ref[:, tm:tm + POOL_HALO, :]
    y_ref[0] = _residual_ffn(x, m, gains_ref[1:2, :], gains_ref[2:3, :], gains_ref[3:4, :],
                             win_ref, wout_ref)


def _pool_layer_sample_kernel(x_ref, buf_ref, gains_ref, pw_ref, ps_ref, win_ref, wout_ref,
                              y_ref, pool_ref, hp_ref, *, tb, seq, n_past):
    x = x_ref[...]
    h = _rms(x, gains_ref[0:1, :])
    hp_ref[:, POOL_HALO - POOL_BUF:POOL_HALO, :] = buf_ref[...]
    hp_ref[:, POOL_HALO:POOL_HALO + seq, :] = h
    t = lax.broadcasted_iota(jnp.int32, (1, seq, 1), 1)
    diffs = []
    for gi, win in enumerate(POOL_WINDOWS):
        c0, c1 = gi * POOL_GROUP_DIM, (gi + 1) * POOL_GROUP_DIM
        cur = hp_ref[:, POOL_HALO:POOL_HALO + seq, c0:c1]
        tot = cur
        for s in range(1, win):
            tot = tot + hp_ref[:, POOL_HALO - s:POOL_HALO - s + seq, c0:c1]
        cnt = jnp.minimum(win, t + 1 + n_past).astype(F32)
        diffs.append((tot / cnt - cur).reshape(tb * seq, POOL_GROUP_DIM))
    m = _pool_project(diffs, pw_ref, ps_ref[...])
    pool_ref[...] = hp_ref[:, POOL_HALO + seq - POOL_BUF:POOL_HALO + seq, :]
    y = _residual_ffn(x.reshape(tb * seq, D_MODEL), m, gains_ref[1:2, :], gains_ref[2:3, :],
                      gains_ref[3:4, :], win_ref, wout_ref)
    y_ref[...] = y.reshape(tb, seq, D_MODEL)


def _pool_layer_prompt(x, gains, pw, ps, win, wout):
    b, l, d = x.shape
    tm = PROMPT_TILE
    return pl.pallas_call(
        functools.partial(_pool_layer_prompt_kernel, tm=tm),
        grid=(b, l // tm),
        in_specs=[
            pl.BlockSpec((1, tm, d), lambda i, j: (i, j, 0)),
            _const_spec(gains.shape), _const_spec(pw.shape), _const_spec(ps.shape),
            _const_spec(win.shape), _const_spec(wout.shape),
        ],
        out_specs=[
            pl.BlockSpec((1, tm, d), lambda i, j: (i, j, 0)),
            pl.BlockSpec((1, POOL_BUF, d), lambda i, j: (i, 0, 0)),
        ],
        out_shape=[jax.ShapeDtypeStruct((b, l, d), F32),
                   jax.ShapeDtypeStruct((b, POOL_BUF, d), F32)],
        scratch_shapes=[pltpu.VMEM((d // LANES, POOL_HALO + tm, LANES), F32),
                        pltpu.VMEM((d // LANES, tm, LANES), F32)],
        compiler_params=_params("arbitrary", "arbitrary"),
        name="pool_layer_prompt",
    )(x, gains, pw, ps, win, wout)


def _pool_layer_sample(x, buf, gains, pw, ps, win, wout):
    b, l, d = x.shape
    tb = SAMPLE_BTILE
    n_past = buf.shape[1]
    return pl.pallas_call(
        functools.partial(_pool_layer_sample_kernel, tb=tb, seq=l, n_past=n_past),
        grid=(b // tb,),
        in_specs=[
            pl.BlockSpec((tb, l, d), lambda i: (i, 0, 0)),
            pl.BlockSpec((tb, POOL_BUF, d), lambda i: (i, 0, 0)),
            _const_spec(gains.shape), _const_spec(pw.shape), _const_spec(ps.shape),
            _const_spec(win.shape), _const_spec(wout.shape),
        ],
        out_specs=[
            pl.BlockSpec((tb, l, d), lambda i: (i, 0, 0)),
            pl.BlockSpec((tb, POOL_BUF, d), lambda i: (i, 0, 0)),
        ],
        out_shape=[jax.ShapeDtypeStruct((b, l, d), F32),
                   jax.ShapeDtypeStruct((b, POOL_BUF, d), F32)],
        scratch_shapes=[pltpu.VMEM((tb, POOL_HALO + l, d), F32)],
        compiler_params=_params("arbitrary"),
        name="pool_layer_sample",
    )(x, buf, gains, pw, ps, win, wout)


def _gdn_qkv(conv, q_ref, k_ref, v_ref, rows):
    for hh in range(K_HEADS):
        qh = conv[hh]
        kh = conv[K_HEADS + hh]
        qn = qh * lax.rsqrt(jnp.sum(qh * qh, axis=-1, keepdims=True) + EPS)
        q_ref[hh, rows, :] = qn * (HEAD_DIM ** -0.5)
        k_ref[hh, rows, :] = kh * lax.rsqrt(jnp.sum(kh * kh, axis=-1, keepdims=True) + EPS)
    for hh in range(V_HEADS):
        v_ref[hh, rows, :] = conv[2 * K_HEADS + hh]


def _gdn_gates(ba, alog, dtb, beta_ref, g_ref):
    beta_ref[...] = _sigmoid(ba[:, :LANES])
    g_ref[...] = -jnp.exp(alog) * _softplus(ba[:, LANES:] + dtb)


def _strided_rows(first):
    return pl.ds(first, SUBLANES, stride=ROW_STRIDE)


def _gdn_pre_prompt_kernel(x_ref, gain_ref, wqkvz_ref, wba_ref, cw_ref, alog_ref, dtb_ref,
                           q_ref, k_ref, v_ref, z_ref, beta_ref, g_ref, conv_ref, up_ref, *, tm):
    j = pl.program_id(1)
    h = _rms(x_ref[0], gain_ref[...]).astype(BF16)
    proj = jnp.dot(h, wqkvz_ref[...], preferred_element_type=F32)
    ba = jnp.dot(h, wba_ref[...], preferred_element_type=F32)
    for hh in range(V_HEADS):
        z_ref[hh] = proj[:, CONV_DIM + hh * HEAD_DIM:CONV_DIM + (hh + 1) * HEAD_DIM]
    n_slab = CONV_DIM // HEAD_DIM

    @pl.when(j == 0)
    def _():
        up_ref[:, 0:CONV_HALO, :] = jnp.zeros((n_slab, CONV_HALO, HEAD_DIM), F32)

    for sl in range(n_slab):
        up_ref[sl, CONV_HALO:CONV_HALO + tm, :] = proj[:, sl * HEAD_DIM:(sl + 1) * HEAD_DIM]
    _gdn_gates(ba, alog_ref[...], dtb_ref[...], beta_ref, g_ref)
    base = CONV_HALO - (CONV_WIDTH - 1)

    def conv_group(grp, carry):
        g0 = pl.multiple_of(grp * ROW_GROUP, ROW_GROUP)
        for r in range(ROW_STRIDE):
            conv = []
            for sl in range(n_slab):
                lanes = slice(sl * HEAD_DIM, (sl + 1) * HEAD_DIM)
                acc = up_ref[sl, _strided_rows(g0 + (base + r)), :] * cw_ref[0:1, lanes]
                for tap in range(1, CONV_WIDTH):
                    acc = acc + (up_ref[sl, _strided_rows(g0 + (base + tap + r)), :]
                                 * cw_ref[tap:tap + 1, lanes])
                conv.append(_silu(acc))
            _gdn_qkv(conv, q_ref, k_ref, v_ref, _strided_rows(g0 + r))
        return carry

    lax.fori_loop(0, tm // ROW_GROUP, conv_group, 0)

    @pl.when(j == pl.num_programs(1) - 1)
    def _():
        for sl in range(n_slab):
            conv_ref[0, :, sl * HEAD_DIM:(sl + 1) * HEAD_DIM] = up_ref[
                sl, tm + CONV_HALO - (CONV_WIDTH - 1):tm + CONV_HALO, :]

    up_ref[:, 0:CONV_HALO, :] = up_ref[:, tm:tm + CONV_HALO, :]


def _gdn_pre_sample_kernel(x_ref, buf_ref, gain_ref, wqkvz_ref, wba_ref, cw_ref, alog_ref,
                           dtb_ref, q_ref, k_ref, v_ref, z_ref, beta_ref, g_ref, conv_ref, up_ref,
                           *, tb, seq):
    m = tb * seq
    h = _rms(x_ref[...].reshape(m, D_MODEL), gain_ref[...]).astype(BF16)
    proj = jnp.dot(h, wqkvz_ref[...], preferred_element_type=F32)
    ba = jnp.dot(h, wba_ref[...], preferred_element_type=F32)
    for hh in range(V_HEADS):
        z_ref[hh] = proj[:, CONV_DIM + hh * HEAD_DIM:CONV_DIM + (hh + 1) * HEAD_DIM]
    base = CONV_HALO - (CONV_WIDTH - 1)
    up_ref[:, base:CONV_HALO, :] = buf_ref[...]
    up_ref[:, CONV_HALO:CONV_HALO + seq, :] = proj[:, :CONV_DIM].reshape(tb, seq, CONV_DIM)
    acc = up_ref[:, base:base + seq, :] * cw_ref[0:1, :]
    for tap in range(1, CONV_WIDTH):
        acc = acc + up_ref[:, base + tap:base + tap + seq, :] * cw_ref[tap:tap + 1, :]
    conv_ref[...] = up_ref[:, CONV_HALO + seq - (CONV_WIDTH - 1):CONV_HALO + seq, :]
    _gdn_gates(ba, alog_ref[...], dtb_ref[...], beta_ref, g_ref)
    conv = _silu(acc).reshape(m, CONV_DIM)
    _gdn_qkv([conv[:, sl * HEAD_DIM:(sl + 1) * HEAD_DIM] for sl in range(CONV_DIM // HEAD_DIM)],
             q_ref, k_ref, v_ref, slice(None))


def _gdn_pre_out_shapes(n):
    return [jax.ShapeDtypeStruct((K_HEADS, n, HEAD_DIM), F32),
            jax.ShapeDtypeStruct((K_HEADS, n, HEAD_DIM), F32),
            jax.ShapeDtypeStruct((V_HEADS, n, HEAD_DIM), F32),
            jax.ShapeDtypeStruct((V_HEADS, n, HEAD_DIM), F32),
            jax.ShapeDtypeStruct((n, LANES), F32), jax.ShapeDtypeStruct((n, LANES), F32)]


def _gdn_pre_out_specs(rows, index):
    heads = lambda n: pl.BlockSpec((n, rows, HEAD_DIM), lambda *g: (0, index(*g), 0))
    lane = pl.BlockSpec((rows, LANES), lambda *g: (index(*g), 0))
    return [heads(K_HEADS), heads(K_HEADS), heads(V_HEADS), heads(V_HEADS), lane, lane]


def _gdn_pre_prompt(x, gain, wqkvz, wba, cw, alog, dtb):
    b, l, d = x.shape
    tm = GDN_PRE_TILE
    nj = l // tm
    return pl.pallas_call(
        functools.partial(_gdn_pre_prompt_kernel, tm=tm),
        grid=(b, nj),
        in_specs=[pl.BlockSpec((1, tm, d), lambda i, j: (i, j, 0))]
        + [_const_spec(a.shape) for a in (gain, wqkvz, wba, cw, alog, dtb)],
        out_specs=_gdn_pre_out_specs(tm, lambda i, j: i * nj + j)
        + [pl.BlockSpec((1, CONV_WIDTH - 1, CONV_DIM), lambda i, j: (i, 0, 0))],
        out_shape=_gdn_pre_out_shapes(b * l)
        + [jax.ShapeDtypeStruct((b, CONV_WIDTH - 1, CONV_DIM), F32)],
        scratch_shapes=[pltpu.VMEM((CONV_DIM // HEAD_DIM, CONV_HALO + tm, HEAD_DIM), F32)],
        compiler_params=_params("arbitrary", "arbitrary"),
        name="gdn_pre_prompt",
    )(x, gain, wqkvz, wba, cw, alog, dtb)


def _gdn_pre_sample(x, buf, gain, wqkvz, wba, cw, alog, dtb):
    b, l, d = x.shape
    tb = SAMPLE_BTILE
    return pl.pallas_call(
        functools.partial(_gdn_pre_sample_kernel, tb=tb, seq=l),
        grid=(b // tb,),
        in_specs=[pl.BlockSpec((tb, l, d), lambda i: (i, 0, 0)),
                  pl.BlockSpec((tb, CONV_WIDTH - 1, CONV_DIM), lambda i: (i, 0, 0))]
        + [_const_spec(a.shape) for a in (gain, wqkvz, wba, cw, alog, dtb)],
        out_specs=_gdn_pre_out_specs(tb * l, lambda i: i)
        + [pl.BlockSpec((tb, CONV_WIDTH - 1, CONV_DIM), lambda i: (i, 0, 0))],
        out_shape=_gdn_pre_out_shapes(b * l)
        + [jax.ShapeDtypeStruct((b, CONV_WIDTH - 1, CONV_DIM), F32)],
        scratch_shapes=[pltpu.VMEM((tb, CONV_HALO + l, CONV_DIM), F32)],
        compiler_params=_params("arbitrary"),
        name="gdn_pre_sample",
    )(x, buf, gain, wqkvz, wba, cw, alog, dtb)


def _unit_lower_inverses(mats, c):
    ri = lax.broadcasted_iota(jnp.int32, (c, c), 0)
    ci = lax.broadcasted_iota(jnp.int32, (c, c), 1)
    eye = (ri == ci).astype(F32)
    pair = ((ri // 2) == (ci // 2)) & (ri > ci)
    xs = [eye - jnp.where(pair, a, 0.0) for a in mats]
    blk = 2
    while blk < c:
        off = ((ri // (2 * blk)) == (ci // (2 * blk))) & ((ri // blk) > (ci // blk))
        ys = [_dot(jnp.where(off, a, 0.0), x) for a, x in zip(mats, xs)]
        xs = [x - _dot(x, y) for x, y in zip(xs, ys)]
        blk *= 2
    return xs


def _delta_chunks(q_ref, k_ref, v_ref, z_ref, beta_ref, g_ref, og_ref, onorm, rows,
                  state_load, state_store, c):
    n = len(rows)
    ri = lax.broadcasted_iota(jnp.int32, (c, c), 0)
    ci = lax.broadcasted_iota(jnp.int32, (c, c), 1)
    causal = ri >= ci
    strict = ri > ci
    rep = V_HEADS // K_HEADS
    units = [(i, h) for i in range(n) for h in range(V_HEADS)]
    kunits = [(i, j) for i in range(n) for j in range(K_HEADS)]

    gcum = [_cumsum_rows(g_ref[rows[i], :], c) for i in range(n)]
    gcum_t = [x.T for x in gcum]
    egcum = [jnp.exp(x) for x in gcum]
    etail_t = [x[:, c - 1:c] - x for x in gcum_t]
    etail_t = [jnp.exp(x) for x in etail_t]
    beta = [beta_ref[rows[i], :] for i in range(n)]
    ks = {(i, j): k_ref[j, rows[i], :] for i, j in kunits}
    kts = {u: ks[u].T for u in kunits}
    kq = {(i, j): _dot(jnp.concatenate([ks[i, j], q_ref[j, rows[i], :]], axis=0), kts[i, j])
          for i, j in kunits}
    gcol = {(i, h): gcum[i][:, h:h + 1] for i, h in units}
    bcol = {(i, h): beta[i][:, h:h + 1] for i, h in units}
    egc = {(i, h): egcum[i][:, h:h + 1] for i, h in units}
    decay = {(i, h): jnp.where(
        causal, jnp.exp(jnp.minimum(gcol[i, h] - gcum_t[i][h:h + 1, :], 0.0)), 0.0)
        for i, h in units}
    a_mats = [jnp.where(strict, kq[i, h // rep][:c] * bcol[i, h] * decay[i, h], 0.0)
              for i, h in units]
    t_inv = dict(zip(units, _unit_lower_inverses(a_mats, c)))
    uw = {(i, h): _dot(t_inv[i, h], jnp.concatenate(
        [v_ref[h, rows[i], :] * bcol[i, h], ks[i, h // rep] * (bcol[i, h] * egc[i, h])], axis=1))
        for i, h in units}
    wq = {(i, h): jnp.concatenate(
        [uw[i, h][:, HEAD_DIM:], q_ref[h // rep, rows[i], :] * egc[i, h]], axis=0).astype(BF16)
        for i, h in units}
    qkd = {(i, h): (kq[i, h // rep][c:] * decay[i, h]).astype(BF16) for i, h in units}

    heads = range(V_HEADS)
    for i in range(n):
        s_old = [state_load(i, h) for h in heads]
        ws = [_dot(wq[i, h], s_old[h]) for h in heads]
        v_new = [uw[i, h][:, :HEAD_DIM] - ws[h][:c] for h in heads]
        o = [ws[h][c:] + _dot(qkd[i, h], v_new[h]) for h in heads]
        for h in heads:
            k_dec_t = kts[i, h // rep] * etail_t[i][h:h + 1, :]
            state_store(i, h, s_old[h] * egcum[i][c - 1:c, h:h + 1] + _dot(k_dec_t, v_new[h]))
        for h in heads:
            og_ref[rows[i], h * HEAD_DIM:(h + 1) * HEAD_DIM] = (
                _rms(o[h], onorm) * _silu(z_ref[h, rows[i], :])).astype(BF16)


def _cumsum_rows(g, c):
    ri = lax.broadcasted_iota(jnp.int32, (c, c), 0)
    ci = lax.broadcasted_iota(jnp.int32, (c, c), 1)
    tri = (ri >= ci).astype(F32)
    return jnp.dot(tri, g, preferred_element_type=F32, precision=lax.Precision.HIGHEST)


def _gdn_scan_prompt_kernel(q_ref, k_ref, v_ref, z_ref, beta_ref, g_ref, onorm_ref,
                            og_ref, s_ref, *, c, n_chunk):
    @pl.when(pl.program_id(1) == 0)
    def _():
        s_ref[...] = jnp.zeros(s_ref.shape, F32)

    def load(i, hh):
        return s_ref[0, hh]

    def store(i, hh, val):
        s_ref[0, hh] = val

    def body(it, carry):
        r0 = pl.multiple_of(it * (SCAN_GROUP * c), SCAN_GROUP * c)
        rows = [pl.ds(r0 + i * c, c) for i in range(SCAN_GROUP)]
        _delta_chunks(q_ref, k_ref, v_ref, z_ref, beta_ref, g_ref, og_ref, onorm_ref[...], rows,
                      load, store, c)
        return carry

    lax.fori_loop(0, n_chunk // SCAN_GROUP, body, 0)


def _gdn_scan_sample_kernel(q_ref, k_ref, v_ref, z_ref, beta_ref, g_ref, onorm_ref, s0_ref,
                            og_ref, s_ref, *, tb, c):
    def load(i, hh):
        return s0_ref[i, hh]

    def store(i, hh, val):
        s_ref[i, hh] = val

    rows = [pl.ds(i * c, c) for i in range(tb)]
    _delta_chunks(q_ref, k_ref, v_ref, z_ref, beta_ref, g_ref, og_ref, onorm_ref[...], rows,
                  load, store, c)


def _gdn_scan_prompt(q, k, v, z, beta, g, onorm, b, l):
    c = PROMPT_CHUNK
    tm = SCAN_TILE
    nc = l // tm
    tok = lambda w: pl.BlockSpec((tm, w), lambda i, j: (i * nc + j, 0))
    return pl.pallas_call(
        functools.partial(_gdn_scan_prompt_kernel, c=c, n_chunk=tm // c),
        grid=(b, nc),
        in_specs=_gdn_pre_out_specs(tm, lambda i, j: i * nc + j) + [_const_spec(onorm.shape)],
        out_specs=[tok(V_DIM),
                   pl.BlockSpec((1, V_HEADS, HEAD_DIM, HEAD_DIM), lambda i, j: (i, 0, 0, 0))],
        out_shape=[jax.ShapeDtypeStruct((b * l, V_DIM), BF16),
                   jax.ShapeDtypeStruct((b, V_HEADS, HEAD_DIM, HEAD_DIM), F32)],
        compiler_params=_params("arbitrary", "arbitrary"),
        name="gdn_scan_prompt",
    )(q, k, v, z, beta, g, onorm)


def _gdn_scan_sample(q, k, v, z, beta, g, onorm, s0, b, l):
    tb = SAMPLE_SCAN_BTILE
    tok = lambda w: pl.BlockSpec((tb * l, w), lambda i: (i, 0))
    st = pl.BlockSpec((tb, V_HEADS, HEAD_DIM, HEAD_DIM), lambda i: (i, 0, 0, 0))
    return pl.pallas_call(
        functools.partial(_gdn_scan_sample_kernel, tb=tb, c=l),
        grid=(b // tb,),
        in_specs=_gdn_pre_out_specs(tb * l, lambda i: i) + [_const_spec(onorm.shape), st],
        out_specs=[tok(V_DIM), st],
        out_shape=[jax.ShapeDtypeStruct((b * l, V_DIM), BF16),
                   jax.ShapeDtypeStruct((b, V_HEADS, HEAD_DIM, HEAD_DIM), F32)],
        compiler_params=_params("arbitrary"),
        name="gdn_scan_sample",
    )(q, k, v, z, beta, g, onorm, s0)


def _gdn_post_kernel(og_ref, x_ref, gains_ref, wo_ref, win_ref, wout_ref, y_ref):
    m = jnp.dot(og_ref[...], wo_ref[...], preferred_element_type=F32)
    y_ref[...] = _residual_ffn(x_ref[...], m, gains_ref[0:1, :], gains_ref[1:2, :],
                               gains_ref[2:3, :], win_ref, wout_ref)


def _gdn_post(og, x, gains, wo, win, wout):
    n, d = x.shape
    tm = PROMPT_TILE
    return pl.pallas_call(
        _gdn_post_kernel,
        grid=(n // tm,),
        in_specs=[pl.BlockSpec((tm, V_DIM), lambda i: (i, 0)),
                  pl.BlockSpec((tm, d), lambda i: (i, 0))]
        + [_const_spec(a.shape) for a in (gains, wo, win, wout)],
        out_specs=pl.BlockSpec((tm, d), lambda i: (i, 0)),
        out_shape=jax.ShapeDtypeStruct((n, d), F32),
        compiler_params=_params("arbitrary"),
        name="gdn_post",
    )(og, x, gains, wo, win, wout)


def _head_lanes(vec):
    return jnp.pad(vec.astype(F32), (0, LANES - V_HEADS)).reshape(1, LANES)


def kernel(x_prompt, x_sample, state_pool, state_gdn_conv, state_gdn_rec, norm_mix_pre,
           norm_mix_post, norm_ffn_pre, norm_ffn_post, pool_w, pool_scale, gdn_w_in,
           gdn_conv_w, gdn_a_log, gdn_dt_bias, gdn_o_norm, gdn_w_out, ffn_w_in, ffn_w_out):
    bp, lp, d = x_prompt.shape
    bs, ls, _ = x_sample.shape

    gains0 = jnp.stack([norm_mix_pre[0], norm_mix_post[0], norm_ffn_pre[0], norm_ffn_post[0]])
    gains1 = jnp.stack([norm_mix_post[1], norm_ffn_pre[1], norm_ffn_post[1]])
    gain1_pre = norm_mix_pre[1].reshape(1, d)
    pw = pool_w[0].astype(BF16)
    ps = pool_scale[0].reshape(1, d)
    win0, wout0 = ffn_w_in[0].astype(BF16), ffn_w_out[0].astype(BF16)
    win1, wout1 = ffn_w_in[1].astype(BF16), ffn_w_out[1].astype(BF16)
    w_in = gdn_w_in[0]
    wqkvz = w_in[:, :CONV_DIM + V_DIM].astype(BF16)
    w_b = w_in[:, CONV_DIM + V_DIM:CONV_DIM + V_DIM + V_HEADS]
    w_a = w_in[:, CONV_DIM + V_DIM + V_HEADS:]
    lane_pad = ((0, 0), (0, LANES - V_HEADS))
    wba = jnp.concatenate([jnp.pad(w_b, lane_pad), jnp.pad(w_a, lane_pad)], axis=1).astype(BF16)
    cw = gdn_conv_w[0]
    alog, dtb = _head_lanes(gdn_a_log[0]), _head_lanes(gdn_dt_bias[0])
    onorm = gdn_o_norm[0].reshape(1, HEAD_DIM)
    wo = gdn_w_out[0].astype(BF16)

    xp1, pool_p = _pool_layer_prompt(x_prompt, gains0, pw, ps, win0, wout0)
    xs1, pool_s = _pool_layer_sample(x_sample, state_pool[0], gains0, pw, ps, win0, wout0)

    qp, kp, vp, zp, betap, gp, conv_p = _gdn_pre_prompt(xp1, gain1_pre, wqkvz, wba, cw, alog, dtb)
    qs, ks, vs, zs, betas, gs, conv_s = _gdn_pre_sample(xs1, state_gdn_conv[0], gain1_pre, wqkvz,
                                                        wba, cw, alog, dtb)

    ogp, rec_p = _gdn_scan_prompt(qp, kp, vp, zp, betap, gp, onorm, bp, lp)
    ogs, rec_s = _gdn_scan_sample(qs, ks, vs, zs, betas, gs, onorm, state_gdn_rec[0], bs, ls)

    yp = _gdn_post(ogp, xp1.reshape(bp * lp, d), gains1, wo, win1, wout1).reshape(bp, lp, d)
    ys = _gdn_post(ogs, xs1.reshape(bs * ls, d), gains1, wo, win1, wout1).reshape(bs, ls, d)

    return (yp, ys, pool_p[None], pool_s[None], conv_p[None], conv_s[None], rec_p[None],
            rec_s[None])
```

```python
import functools

import jax
import jax.numpy as jnp
from jax import lax
from jax.experimental import pallas as pl
from jax.experimental.pallas import tpu as pltpu

D_MODEL = 1024
POOL_WINDOWS = (2, 4, 8, 16)
POOL_GROUP_DIM = D_MODEL // len(POOL_WINDOWS)
POOL_BUF = max(POOL_WINDOWS) - 1
K_HEADS = 8
V_HEADS = 16
HEAD_DIM = 128
QK_DIM = K_HEADS * HEAD_DIM
V_DIM = V_HEADS * HEAD_DIM
CONV_DIM = 2 * QK_DIM + V_DIM
CONV_WIDTH = 4
D_FF = 2816
EPS = 1e-6

F32 = jnp.float32
BF16 = jnp.bfloat16

SUBLANES = 8
LANES = 128
POOL_HALO = 16
CONV_HALO = SUBLANES
VMEM_LIMIT = 56 * 1024 * 1024
ROW_STRIDE = 4
ROW_GROUP = SUBLANES * ROW_STRIDE

PROMPT_TILE = 512
GDN_PRE_TILE = 256
PRE_SLABS = 4
SAMPLE_BTILE = 32
PROMPT_CHUNK = 64
SCAN_TILE = 256
SCAN_GROUP = 2
SAMPLE_SCAN_BTILE = 4


def _rms(x, gain):
    ms = jnp.mean(x * x, axis=-1, keepdims=True)
    return x * lax.rsqrt(ms + EPS) * gain


def _sigmoid(x):
    return 1.0 / (1.0 + jnp.exp(-x))


def _silu(x):
    return x * _sigmoid(x)


def _softplus(x):
    return jnp.maximum(x, 0.0) + jnp.log1p(jnp.exp(-jnp.abs(x)))


def _dot(a, b):
    return jnp.dot(a.astype(BF16), b.astype(BF16), preferred_element_type=F32)


def _dot_nt(a, b):
    return lax.dot_general(a.astype(BF16), b.astype(BF16), (((1,), (1,)), ((), ())),
                           preferred_element_type=F32)


def _dot_tn(a, b):
    return lax.dot_general(a.astype(BF16), b.astype(BF16), (((0,), (0,)), ((), ())),
                           preferred_element_type=F32)


def _const_spec(shape):
    nd = len(shape)
    return pl.BlockSpec(shape, lambda *_: (0,) * nd, pipeline_mode=pl.Buffered(1))


def _params(*sem):
    return pltpu.CompilerParams(dimension_semantics=sem, vmem_limit_bytes=VMEM_LIMIT)


def _residual_ffn(x, m, g_post, g_fpre, g_fpost, win_ref, wout_ref):
    x1 = x + _rms(m, g_post)
    h = _rms(x1, g_fpre).astype(BF16)
    gate = jnp.dot(h, win_ref[:, :D_FF], preferred_element_type=F32)
    up = jnp.dot(h, win_ref[:, D_FF:], preferred_element_type=F32)
    act = (_silu(gate) * up).astype(BF16)
    f = jnp.dot(act, wout_ref[...], preferred_element_type=F32)
    return x1 + _rms(f, g_fpost)


def _pool_project(diffs, pw_ref, scale):
    parts = [_dot(d, pw_ref[gi]) for gi, d in enumerate(diffs)]
    return jnp.concatenate(parts, axis=-1) * scale


def _pool_layer_prompt_kernel(x_ref, gains_ref, pw_ref, ps_ref, win_ref, wout_ref,
                              y_ref, pool_ref, hp_ref, d_ref, *, tm):
    j = pl.program_id(1)
    x = x_ref[0]
    h = _rms(x, gains_ref[0:1, :])

    n_blk = D_MODEL // LANES
    blk_per_grp = POOL_GROUP_DIM // LANES

    @pl.when(j == 0)
    def _():
        hp_ref[:, 0:POOL_HALO, :] = jnp.zeros((n_blk, POOL_HALO, LANES), F32)

    for cb in range(n_blk):
        hp_ref[cb, POOL_HALO:POOL_HALO + tm, :] = h[:, cb * LANES:(cb + 1) * LANES]
    t_tile = lax.broadcasted_iota(jnp.int32, (SUBLANES, 1), 0) * ROW_STRIDE + (j * tm + 1)

    def pool_group(grp, carry):
        g0 = pl.multiple_of(grp * ROW_GROUP, ROW_GROUP)
        for r in range(ROW_STRIDE):
            for gi, win in enumerate(POOL_WINDOWS):
                inv = 1.0 / jnp.minimum(win, t_tile + (g0 + r)).astype(F32)
                for cb in range(gi * blk_per_grp, (gi + 1) * blk_per_grp):
                    cur = hp_ref[cb, _strided_rows(g0 + (POOL_HALO + r)), :]
                    tot = cur
                    for s in range(1, win):
                        tot = tot + hp_ref[cb, _strided_rows(g0 + (POOL_HALO + r - s)), :]
                    d_ref[cb, _strided_rows(g0 + r), :] = tot * inv - cur
        return carry

    lax.fori_loop(0, tm // ROW_GROUP, pool_group, 0)
    diffs = [jnp.concatenate([d_ref[cb] for cb in range(gi * blk_per_grp, (gi + 1) * blk_per_grp)],
                             axis=1) for gi in range(len(POOL_WINDOWS))]
    m = _pool_project(diffs, pw_ref, ps_ref[...])

    @pl.when(j == pl.num_programs(1) - 1)
    def _():
        for cb in range(n_blk):
            pool_ref[0, :, cb * LANES:(cb + 1) * LANES] = hp_ref[
                cb, tm + POOL_HALO - POOL_BUF:tm + POOL_HALO, :]

    hp_ref[:, 0:POOL_HALO, :] = hp_ref[:, tm:tm + POOL_HALO, :]
    y_ref[0] = _residual_ffn(x, m, gains_ref[1:2, :], gains_ref[2:3, :], gains_ref[3:4, :],
                             win_ref, wout_ref)


def _pool_layer_sample_kernel(x_ref, buf_ref, gains_ref, pw_ref, ps_ref, win_ref, wout_ref,
                              y_ref, pool_ref, hp_ref, *, tb, seq, n_past):
    x = x_ref[...]
    h = _rms(x, gains_ref[0:1, :])
    hp_ref[:, POOL_HALO - POOL_BUF:POOL_HALO, :] = buf_ref[...]
    hp_ref[:, POOL_HALO:POOL_HALO + seq, :] = h
    t = lax.broadcasted_iota(jnp.int32, (1, seq, 1), 1)
    diffs = []
    for gi, win in enumerate(POOL_WINDOWS):
        c0, c1 = gi * POOL_GROUP_DIM, (gi + 1) * POOL_GROUP_DIM
        cur = hp_ref[:, POOL_HALO:POOL_HALO + seq, c0:c1]
        tot = cur
        for s in range(1, win):
            tot = tot + hp_ref[:, POOL_HALO - s:POOL_HALO - s + seq, c0:c1]
        cnt = jnp.minimum(win, t + 1 + n_past).astype(F32)
        diffs.append((tot / cnt - cur).reshape(tb * seq, POOL_GROUP_DIM))
    m = _pool_project(diffs, pw_ref, ps_ref[...])
    pool_ref[...] = hp_ref[:, POOL_HALO + seq - POOL_BUF:POOL_HALO + seq, :]
    y = _residual_ffn(x.reshape(tb * seq, D_MODEL), m, gains_ref[1:2, :], gains_ref[2:3, :],
                      gains_ref[3:4, :], win_ref, wout_ref)
    y_ref[...] = y.reshape(tb, seq, D_MODEL)


def _pool_layer_prompt(x, gains, pw, ps, win, wout):
    b, l, d = x.shape
    tm = PROMPT_TILE
    return pl.pallas_call(
        functools.partial(_pool_layer_prompt_kernel, tm=tm),
        grid=(b, l // tm),
        in_specs=[
            pl.BlockSpec((1, tm, d), lambda i, j: (i, j, 0)),
            _const_spec(gains.shape), _const_spec(pw.shape), _const_spec(ps.shape),
            _const_spec(win.shape), _const_spec(wout.shape),
        ],
        out_specs=[
            pl.BlockSpec((1, tm, d), lambda i, j: (i, j, 0)),
            pl.BlockSpec((1, POOL_BUF, d), lambda i, j: (i, 0, 0)),
        ],
        out_shape=[jax.ShapeDtypeStruct((b, l, d), F32),
                   jax.ShapeDtypeStruct((b, POOL_BUF, d), F32)],
        scratch_shapes=[pltpu.VMEM((d // LANES, POOL_HALO + tm, LANES), F32),
                        pltpu.VMEM((d // LANES, tm, LANES), F32)],
        compiler_params=_params("arbitrary", "arbitrary"),
        name="pool_layer_prompt",
    )(x, gains, pw, ps, win, wout)


def _pool_layer_sample(x, buf, gains, pw, ps, win, wout):
    b, l, d = x.shape
    tb = SAMPLE_BTILE
    n_past = buf.shape[1]
    return pl.pallas_call(
        functools.partial(_pool_layer_sample_kernel, tb=tb, seq=l, n_past=n_past),
        grid=(b // tb,),
        in_specs=[
            pl.BlockSpec((tb, l, d), lambda i: (i, 0, 0)),
            pl.BlockSpec((tb, POOL_BUF, d), lambda i: (i, 0, 0)),
            _const_spec(gains.shape), _const_spec(pw.shape), _const_spec(ps.shape),
            _const_spec(win.shape), _const_spec(wout.shape),
        ],
        out_specs=[
            pl.BlockSpec((tb, l, d), lambda i: (i, 0, 0)),
            pl.BlockSpec((tb, POOL_BUF, d), lambda i: (i, 0, 0)),
        ],
        out_shape=[jax.ShapeDtypeStruct((b, l, d), F32),
                   jax.ShapeDtypeStruct((b, POOL_BUF, d), F32)],
        scratch_shapes=[pltpu.VMEM((tb, POOL_HALO + l, d), F32)],
        compiler_params=_params("arbitrary"),
        name="pool_layer_sample",
    )(x, buf, gains, pw, ps, win, wout)


def _gdn_qkv_slab(sl, conv, q_ref, k_ref, v_ref, rows):
    if sl >= 2 * K_HEADS:
        v_ref[sl - 2 * K_HEADS, rows, :] = conv
        return
    unit = conv * lax.rsqrt(jnp.sum(conv * conv, axis=-1, keepdims=True) + EPS)
    if sl < K_HEADS:
        q_ref[sl, rows, :] = unit * (HEAD_DIM ** -0.5)
    else:
        k_ref[sl - K_HEADS, rows, :] = unit


def _gdn_gates(ba, alog, dtb, beta_ref, g_ref):
    beta_ref[...] = _sigmoid(ba[:, :LANES])
    g_ref[...] = -jnp.exp(alog) * _softplus(ba[:, LANES:] + dtb)


def _strided_rows(first):
    return pl.ds(first, SUBLANES, stride=ROW_STRIDE)


def _gdn_pre_prompt_kernel(x_ref, gain_ref, wqkvz_ref, wba_ref, cw_ref, alog_ref, dtb_ref,
                           q_ref, k_ref, v_ref, z_ref, beta_ref, g_ref, conv_ref, up_ref, *, tm):
    j = pl.program_id(1)
    h = _rms(x_ref[0], gain_ref[...]).astype(BF16)
    n_slab = CONV_DIM // HEAD_DIM
    n_grp = n_slab // PRE_SLABS

    @pl.when(j == 0)
    def _():
        up_ref[:, 0:CONV_HALO, :] = jnp.zeros((n_slab, CONV_HALO, HEAD_DIM), F32)

    def project(grp):
        c0 = grp * PRE_SLABS * HEAD_DIM
        p = jnp.dot(h, wqkvz_ref[:, c0:c0 + PRE_SLABS * HEAD_DIM], preferred_element_type=F32)
        for t in range(PRE_SLABS):
            up_ref[grp * PRE_SLABS + t, CONV_HALO:CONV_HALO + tm, :] = (
                p[:, t * HEAD_DIM:(t + 1) * HEAD_DIM])

    def project_z(half):
        c0 = CONV_DIM + half * (V_DIM // 2)
        p = jnp.dot(h, wqkvz_ref[:, c0:c0 + V_DIM // 2], preferred_element_type=F32)
        for t in range(V_HEADS // 2):
            z_ref[half * (V_HEADS // 2) + t] = p[:, t * HEAD_DIM:(t + 1) * HEAD_DIM]

    base = CONV_HALO - (CONV_WIDTH - 1)

    def convolve(grp):
        for sl in range(grp * PRE_SLABS, (grp + 1) * PRE_SLABS):
            lanes = slice(sl * HEAD_DIM, (sl + 1) * HEAD_DIM)
            for row0 in range(0, tm, ROW_GROUP):
                for r in range(ROW_STRIDE):
                    acc = up_ref[sl, _strided_rows(row0 + base + r), :] * cw_ref[0:1, lanes]
                    for tap in range(1, CONV_WIDTH):
                        acc = acc + (up_ref[sl, _strided_rows(row0 + base + tap + r), :]
                                     * cw_ref[tap:tap + 1, lanes])
                    _gdn_qkv_slab(sl, _silu(acc), q_ref, k_ref, v_ref, _strided_rows(row0 + r))

    project(0)
    for grp in range(n_grp):
        if grp + 1 < n_grp:
            project(grp + 1)
        else:
            project_z(0)
        convolve(grp)
    project_z(1)
    ba = jnp.dot(h, wba_ref[...], preferred_element_type=F32)
    _gdn_gates(ba, alog_ref[...], dtb_ref[...], beta_ref, g_ref)

    @pl.when(j == pl.num_programs(1) - 1)
    def _():
        for sl in range(n_slab):
            conv_ref[0, :, sl * HEAD_DIM:(sl + 1) * HEAD_DIM] = up_ref[
                sl, tm + CONV_HALO - (CONV_WIDTH - 1):tm + CONV_HALO, :]

    up_ref[:, 0:CONV_HALO, :] = up_ref[:, tm:tm + CONV_HALO, :]


def _gdn_pre_sample_kernel(x_ref, buf_ref, gain_ref, wqkvz_ref, wba_ref, cw_ref, alog_ref,
                           dtb_ref, q_ref, k_ref, v_ref, z_ref, beta_ref, g_ref, conv_ref, up_ref,
                           *, tb, seq):
    m = tb * seq
    h = _rms(x_ref[...].reshape(m, D_MODEL), gain_ref[...]).astype(BF16)
    proj = jnp.dot(h, wqkvz_ref[...], preferred_element_type=F32)
    ba = jnp.dot(h, wba_ref[...], preferred_element_type=F32)
    for hh in range(V_HEADS):
        z_ref[hh] = proj[:, CONV_DIM + hh * HEAD_DIM:CONV_DIM + (hh + 1) * HEAD_DIM]
    base = CONV_HALO - (CONV_WIDTH - 1)
    up_ref[:, base:CONV_HALO, :] = buf_ref[...]
    up_ref[:, CONV_HALO:CONV_HALO + seq, :] = proj[:, :CONV_DIM].reshape(tb, seq, CONV_DIM)
    acc = up_ref[:, base:base + seq, :] * cw_ref[0:1, :]
    for tap in range(1, CONV_WIDTH):
        acc = acc + up_ref[:, base + tap:base + tap + seq, :] * cw_ref[tap:tap + 1, :]
    conv_ref[...] = up_ref[:, CONV_HALO + seq - (CONV_WIDTH - 1):CONV_HALO + seq, :]
    _gdn_gates(ba, alog_ref[...], dtb_ref[...], beta_ref, g_ref)
    conv = _silu(acc).reshape(m, CONV_DIM)
    for sl in range(CONV_DIM // HEAD_DIM):
        _gdn_qkv_slab(sl, conv[:, sl * HEAD_DIM:(sl + 1) * HEAD_DIM], q_ref, k_ref, v_ref,
                      slice(None))


def _gdn_pre_out_shapes(n):
    return [jax.ShapeDtypeStruct((K_HEADS, n, HEAD_DIM), F32),
            jax.ShapeDtypeStruct((K_HEADS, n, HEAD_DIM), F32),
            jax.ShapeDtypeStruct((V_HEADS, n, HEAD_DIM), F32),
            jax.ShapeDtypeStruct((V_HEADS, n, HEAD_DIM), F32),
            jax.ShapeDtypeStruct((n, LANES), F32), jax.ShapeDtypeStruct((n, LANES), F32)]


def _gdn_pre_out_specs(rows, index):
    heads = lambda n: pl.BlockSpec((n, rows, HEAD_DIM), lambda *g: (0, index(*g), 0))
    lane = pl.BlockSpec((rows, LANES), lambda *g: (index(*g), 0))
    return [heads(K_HEADS), heads(K_HEADS), heads(V_HEADS), heads(V_HEADS), lane, lane]


def _gdn_pre_prompt(x, gain, wqkvz, wba, cw, alog, dtb):
    b, l, d = x.shape
    tm = GDN_PRE_TILE
    nj = l // tm
    return pl.pallas_call(
        functools.partial(_gdn_pre_prompt_kernel, tm=tm),
        grid=(b, nj),
        in_specs=[pl.BlockSpec((1, tm, d), lambda i, j: (i, j, 0))]
        + [_const_spec(a.shape) for a in (gain, wqkvz, wba, cw, alog, dtb)],
        out_specs=_gdn_pre_out_specs(tm, lambda i, j: i * nj + j)
        + [pl.BlockSpec((1, CONV_WIDTH - 1, CONV_DIM), lambda i, j: (i, 0, 0))],
        out_shape=_gdn_pre_out_shapes(b * l)
        + [jax.ShapeDtypeStruct((b, CONV_WIDTH - 1, CONV_DIM), F32)],
        scratch_shapes=[pltpu.VMEM((CONV_DIM // HEAD_DIM, CONV_HALO + tm, HEAD_DIM), F32)],
        compiler_params=_params("arbitrary", "arbitrary"),
        name="gdn_pre_prompt",
    )(x, gain, wqkvz, wba, cw, alog, dtb)


def _gdn_pre_sample(x, buf, gain, wqkvz, wba, cw, alog, dtb):
    b, l, d = x.shape
    tb = SAMPLE_BTILE
    return pl.pallas_call(
        functools.partial(_gdn_pre_sample_kernel, tb=tb, seq=l),
        grid=(b // tb,),
        in_specs=[pl.BlockSpec((tb, l, d), lambda i: (i, 0, 0)),
                  pl.BlockSpec((tb, CONV_WIDTH - 1, CONV_DIM), lambda i: (i, 0, 0))]
        + [_const_spec(a.shape) for a in (gain, wqkvz, wba, cw, alog, dtb)],
        out_specs=_gdn_pre_out_specs(tb * l, lambda i: i)
        + [pl.BlockSpec((tb, CONV_WIDTH - 1, CONV_DIM), lambda i: (i, 0, 0))],
        out_shape=_gdn_pre_out_shapes(b * l)
        + [jax.ShapeDtypeStruct((b, CONV_WIDTH - 1, CONV_DIM), F32)],
        scratch_shapes=[pltpu.VMEM((tb, CONV_HALO + l, CONV_DIM), F32)],
        compiler_params=_params("arbitrary"),
        name="gdn_pre_sample",
    )(x, buf, gain, wqkvz, wba, cw, alog, dtb)


def _unit_lower_inverses(mats, c):
    ri = lax.broadcasted_iota(jnp.int32, (c, c), 0)
    ci = lax.broadcasted_iota(jnp.int32, (c, c), 1)
    eye = (ri == ci).astype(F32)
    pair = ((ri // 2) == (ci // 2)) & (ri > ci)
    xs = [eye - jnp.where(pair, a, 0.0) for a in mats]
    blk = 2
    while blk < c:
        off = ((ri // (2 * blk)) == (ci // (2 * blk))) & ((ri // blk) > (ci // blk))
        ys = [_dot(jnp.where(off, a, 0.0), x) for a, x in zip(mats, xs)]
        xs = [x - _dot(x, y) for x, y in zip(xs, ys)]
        blk *= 2
    return xs


def _delta_chunks(q_ref, k_ref, v_ref, z_ref, beta_ref, g_ref, og_ref, onorm, rows,
                  state_load, state_store, c):
    n = len(rows)
    ri = lax.broadcasted_iota(jnp.int32, (c, c), 0)
    ci = lax.broadcasted_iota(jnp.int32, (c, c), 1)
    causal = ri >= ci
    strict = ri > ci
    rep = V_HEADS // K_HEADS
    units = [(i, h) for i in range(n) for h in range(V_HEADS)]
    kunits = [(i, j) for i in range(n) for j in range(K_HEADS)]

    gcum = [_cumsum_rows(g_ref[rows[i], :], c) for i in range(n)]
    gcum_t = [x.T for x in gcum]
    egcum = [jnp.exp(x) for x in gcum]
    etail_t = [x[:, c - 1:c] - x for x in gcum_t]
    etail_t = [jnp.exp(x) for x in etail_t]
    beta = [beta_ref[rows[i], :] for i in range(n)]
    ks = {(i, j): k_ref[j, rows[i], :] for i, j in kunits}
    kts = {u: ks[u].T for u in kunits}
    kq = {(i, j): _dot(jnp.concatenate([ks[i, j], q_ref[j, rows[i], :]], axis=0), kts[i, j])
          for i, j in kunits}
    gcol = {(i, h): gcum[i][:, h:h + 1] for i, h in units}
    bcol = {(i, h): beta[i][:, h:h + 1] for i, h in units}
    egc = {(i, h): egcum[i][:, h:h + 1] for i, h in units}
    decay = {(i, h): jnp.where(
        causal, jnp.exp(jnp.minimum(gcol[i, h] - gcum_t[i][h:h + 1, :], 0.0)), 0.0)
        for i, h in units}
    a_mats = [jnp.where(strict, kq[i, h // rep][:c] * bcol[i, h] * decay[i, h], 0.0)
              for i, h in units]
    t_inv = dict(zip(units, _unit_lower_inverses(a_mats, c)))
    uw = {(i, h): _dot(t_inv[i, h], jnp.concatenate(
        [v_ref[h, rows[i], :] * bcol[i, h], ks[i, h // rep] * (bcol[i, h] * egc[i, h])], axis=1))
        for i, h in units}
    wq = {(i, h): jnp.concatenate(
        [uw[i, h][:, HEAD_DIM:], q_ref[h // rep, rows[i], :] * egc[i, h]], axis=0).astype(BF16)
        for i, h in units}
    qkd = {(i, h): (kq[i, h // rep][c:] * decay[i, h]).astype(BF16) for i, h in units}

    heads = range(V_HEADS)
    for i in range(n):
        s_old = [state_load(i, h) for h in heads]
        ws = [_dot(wq[i, h], s_old[h]) for h in heads]
        v_new = [uw[i, h][:, :HEAD_DIM] - ws[h][:c] for h in heads]
        o = [ws[h][c:] + _dot(qkd[i, h], v_new[h]) for h in heads]
        for h in heads:
            k_dec_t = kts[i, h // rep] * etail_t[i][h:h + 1, :]
            state_store(i, h, s_old[h] * egcum[i][c - 1:c, h:h + 1] + _dot(k_dec_t, v_new[h]))
        for h in heads:
            og_ref[rows[i], h * HEAD_DIM:(h + 1) * HEAD_DIM] = (
                _rms(o[h], onorm) * _silu(z_ref[h, rows[i], :])).astype(BF16)


def _cumsum_rows(g, c):
    ri = lax.broadcasted_iota(jnp.int32, (c, c), 0)
    ci = lax.broadcasted_iota(jnp.int32, (c, c), 1)
    tri = (ri >= ci).astype(F32)
    return jnp.dot(tri, g, preferred_element_type=F32, precision=lax.Precision.HIGHEST)


def _gdn_scan_prompt_kernel(q_ref, k_ref, v_ref, z_ref, beta_ref, g_ref, onorm_ref,
                            og_ref, s_ref, *, c, n_chunk):
    @pl.when(pl.program_id(1) == 0)
    def _():
        s_ref[...] = jnp.zeros(s_ref.shape, F32)

    def load(i, hh):
        return s_ref[0, hh]

    def store(i, hh, val):
        s_ref[0, hh] = val

    def body(it, carry):
        r0 = pl.multiple_of(it * (SCAN_GROUP * c), SCAN_GROUP * c)
        rows = [pl.ds(r0 + i * c, c) for i in range(SCAN_GROUP)]
        _delta_chunks(q_ref, k_ref, v_ref, z_ref, beta_ref, g_ref, og_ref, onorm_ref[...], rows,
                      load, store, c)
        return carry

    lax.fori_loop(0, n_chunk // SCAN_GROUP, body, 0)


def _gdn_scan_sample_kernel(q_ref, k_ref, v_ref, z_ref, beta_ref, g_ref, onorm_ref, s0_ref,
                            og_ref, s_ref, *, tb, c):
    def load(i, hh):
        return s0_ref[i, hh]

    def store(i, hh, val):
        s_ref[i, hh] = val

    rows = [pl.ds(i * c, c) for i in range(tb)]
    _delta_chunks(q_ref, k_ref, v_ref, z_ref, beta_ref, g_ref, og_ref, onorm_ref[...], rows,
                  load, store, c)


def _gdn_scan_prompt(q, k, v, z, beta, g, onorm, b, l):
    c = PROMPT_CHUNK
    tm = SCAN_TILE
    nc = l // tm
    tok = lambda w: pl.BlockSpec((tm, w), lambda i, j: (i * nc + j, 0))
    return pl.pallas_call(
        functools.partial(_gdn_scan_prompt_kernel, c=c, n_chunk=tm // c),
        grid=(b, nc),
        in_specs=_gdn_pre_out_specs(tm, lambda i, j: i * nc + j) + [_const_spec(onorm.shape)],
        out_specs=[tok(V_DIM),
                   pl.BlockSpec((1, V_HEADS, HEAD_DIM, HEAD_DIM), lambda i, j: (i, 0, 0, 0))],
        out_shape=[jax.ShapeDtypeStruct((b * l, V_DIM), BF16),
                   jax.ShapeDtypeStruct((b, V_HEADS, HEAD_DIM, HEAD_DIM), F32)],
        compiler_params=_params("arbitrary", "arbitrary"),
        name="gdn_scan_prompt",
    )(q, k, v, z, beta, g, onorm)


def _gdn_scan_sample(q, k, v, z, beta, g, onorm, s0, b, l):
    tb = SAMPLE_SCAN_BTILE
    tok = lambda w: pl.BlockSpec((tb * l, w), lambda i: (i, 0))
    st = pl.BlockSpec((tb, V_HEADS, HEAD_DIM, HEAD_DIM), lambda i: (i, 0, 0, 0))
    return pl.pallas_call(
        functools.partial(_gdn_scan_sample_kernel, tb=tb, c=l),
        grid=(b // tb,),
        in_specs=_gdn_pre_out_specs(tb * l, lambda i: i) + [_const_spec(onorm.shape), st],
        out_specs=[tok(V_DIM), st],
        out_shape=[jax.ShapeDtypeStruct((b * l, V_DIM), BF16),
                   jax.ShapeDtypeStruct((b, V_HEADS, HEAD_DIM, HEAD_DIM), F32)],
        compiler_params=_params("arbitrary"),
        name="gdn_scan_sample",
    )(q, k, v, z, beta, g, onorm, s0)


def _gdn_post_kernel(og_ref, x_ref, gains_ref, wo_ref, win_ref, wout_ref, y_ref):
    m = jnp.dot(og_ref[...], wo_ref[...], preferred_element_type=F32)
    y_ref[...] = _residual_ffn(x_ref[...], m, gains_ref[0:1, :], gains_ref[1:2, :],
                               gains_ref[2:3, :], win_ref, wout_ref)


def _gdn_post(og, x, gains, wo, win, wout):
    n, d = x.shape
    tm = PROMPT_TILE
    return pl.pallas_call(
        _gdn_post_kernel,
        grid=(n // tm,),
        in_specs=[pl.BlockSpec((tm, V_DIM), lambda i: (i, 0)),
                  pl.BlockSpec((tm, d), lambda i: (i, 0))]
        + [_const_spec(a.shape) for a in (gains, wo, win, wout)],
        out_specs=pl.BlockSpec((tm, d), lambda i: (i, 0)),
        out_shape=jax.ShapeDtypeStruct((n, d), F32),
        compiler_params=_params("arbitrary"),
        name="gdn_post",
    )(og, x, gains, wo, win, wout)


def _head_lanes(vec):
    return jnp.pad(vec.astype(F32), (0, LANES - V_HEADS)).reshape(1, LANES)


def kernel(x_prompt, x_sample, state_pool, state_gdn_conv, state_gdn_rec, norm_mix_pre,
           norm_mix_post, norm_ffn_pre, norm_ffn_post, pool_w, pool_scale, gdn_w_in,
           gdn_conv_w, gdn_a_log, gdn_dt_bias, gdn_o_norm, gdn_w_out, ffn_w_in, ffn_w_out):
    bp, lp, d = x_prompt.shape
    bs, ls, _ = x_sample.shape

    gains0 = jnp.stack([norm_mix_pre[0], norm_mix_post[0], norm_ffn_pre[0], norm_ffn_post[0]])
    gains1 = jnp.stack([norm_mix_post[1], norm_ffn_pre[1], norm_ffn_post[1]])
    gain1_pre = norm_mix_pre[1].reshape(1, d)
    pw = pool_w[0].astype(BF16)
    ps = pool_scale[0].reshape(1, d)
    win0, wout0 = ffn_w_in[0].astype(BF16), ffn_w_out[0].astype(BF16)
    win1, wout1 = ffn_w_in[1].astype(BF16), ffn_w_out[1].astype(BF16)
    w_in = gdn_w_in[0]
    wqkvz = w_in[:, :CONV_DIM + V_DIM].astype(BF16)
    w_b = w_in[:, CONV_DIM + V_DIM:CONV_DIM + V_DIM + V_HEADS]
    w_a = w_in[:, CONV_DIM + V_DIM + V_HEADS:]
    lane_pad = ((0, 0), (0, LANES - V_HEADS))
    wba = jnp.concatenate([jnp.pad(w_b, lane_pad), jnp.pad(w_a, lane_pad)], axis=1).astype(BF16)
    cw = gdn_conv_w[0]
    alog, dtb = _head_lanes(gdn_a_log[0]), _head_lanes(gdn_dt_bias[0])
    onorm = gdn_o_norm[0].reshape(1, HEAD_DIM)
    wo = gdn_w_out[0].astype(BF16)

    xp1, pool_p = _pool_layer_prompt(x_prompt, gains0, pw, ps, win0, wout0)
    xs1, pool_s = _pool_layer_sample(x_sample, state_pool[0], gains0, pw, ps, win0, wout0)

    qp, kp, vp, zp, betap, gp, conv_p = _gdn_pre_prompt(xp1, gain1_pre, wqkvz, wba, cw, alog, dtb)
    qs, ks, vs, zs, betas, gs, conv_s = _gdn_pre_sample(xs1, state_gdn_conv[0], gain1_pre, wqkvz,
                                                        wba, cw, alog, dtb)

    ogp, rec_p = _gdn_scan_prompt(qp, kp, vp, zp, betap, gp, onorm, bp, lp)
    ogs, rec_s = _gdn_scan_sample(qs, ks, vs, zs, betas, gs, onorm, state_gdn_rec[0], bs, ls)

    yp = _gdn_post(ogp, xp1.reshape(bp * lp, d), gains1, wo, win1, wout1).reshape(bp, lp, d)
    ys = _gdn_post(ogs, xs1.reshape(bs * ls, d), gains1, wo, win1, wout1).reshape(bs, ls, d)

    return (yp, ys, pool_p[None], pool_s[None], conv_p[None], conv_s[None], rec_p[None],
            rec_s[None])
```

```python
import functools

import jax
import jax.numpy as jnp
from jax import lax
from jax.experimental import pallas as pl
from jax.experimental.pallas import tpu as pltpu

D_MODEL = 1024
POOL_WINDOWS = (2, 4, 8, 16)
POOL_GROUP_DIM = D_MODEL // len(POOL_WINDOWS)
POOL_BUF = max(POOL_WINDOWS) - 1
K_HEADS = 8
V_HEADS = 16
HEAD_DIM = 128
QK_DIM = K_HEADS * HEAD_DIM
V_DIM = V_HEADS * HEAD_DIM
CONV_DIM = 2 * QK_DIM + V_DIM
CONV_WIDTH = 4
D_FF = 2816
EPS = 1e-6

F32 = jnp.float32
BF16 = jnp.bfloat16

SUBLANES = 8
LANES = 128
POOL_HALO = 16
CONV_HALO = SUBLANES
VMEM_LIMIT = 56 * 1024 * 1024
ROW_STRIDE = 4
ROW_GROUP = SUBLANES * ROW_STRIDE

PROMPT_TILE = 512
GDN_PRE_TILE = 256
PRE_SLABS = 4
SAMPLE_BTILE = 32
PROMPT_CHUNK = 64
SCAN_TILE = 256
SCAN_GROUP = 2
SAMPLE_SCAN_BTILE = 4


def _rms(x, gain):
    ms = jnp.mean(x * x, axis=-1, keepdims=True)
    return x * lax.rsqrt(ms + EPS) * gain


def _sigmoid(x):
    return 1.0 / (1.0 + jnp.exp(-x))


def _silu(x):
    return x * _sigmoid(x)


def _softplus(x):
    return jnp.maximum(x, 0.0) + jnp.log1p(jnp.exp(-jnp.abs(x)))


def _dot(a, b):
    return jnp.dot(a.astype(BF16), b.astype(BF16), preferred_element_type=F32)


def _dot_nt(a, b):
    return lax.dot_general(a.astype(BF16), b.astype(BF16), (((1,), (1,)), ((), ())),
                           preferred_element_type=F32)


def _dot_tn(a, b):
    return lax.dot_general(a.astype(BF16), b.astype(BF16), (((0,), (0,)), ((), ())),
                           preferred_element_type=F32)


def _const_spec(shape):
    nd = len(shape)
    return pl.BlockSpec(shape, lambda *_: (0,) * nd, pipeline_mode=pl.Buffered(1))


def _layer_spec(stacked, layer):
    nd = stacked.ndim - 1
    return pl.BlockSpec((None,) + stacked.shape[1:], lambda *_: (layer,) + (0,) * nd,
                        pipeline_mode=pl.Buffered(1))


def _params(*sem):
    return pltpu.CompilerParams(dimension_semantics=sem, vmem_limit_bytes=VMEM_LIMIT)


def _residual_ffn(x, m, g_post, g_fpre, g_fpost, win_ref, wout_ref):
    x1 = x + _rms(m, g_post)
    h = _rms(x1, g_fpre).astype(BF16)
    gate = jnp.dot(h, win_ref[:, :D_FF], preferred_element_type=F32)
    up = jnp.dot(h, win_ref[:, D_FF:], preferred_element_type=F32)
    act = (_silu(gate) * up).astype(BF16)
    f = jnp.dot(act, wout_ref[...], preferred_element_type=F32)
    return x1 + _rms(f, g_fpost)


def _pool_project(diffs, pw_ref, scale):
    parts = [_dot(d, pw_ref[gi]) for gi, d in enumerate(diffs)]
    return jnp.concatenate(parts, axis=-1) * scale


def _pool_layer_prompt_kernel(x_ref, gains_ref, pw_ref, ps_ref, win_ref, wout_ref,
                              y_ref, pool_ref, hp_ref, d_ref, *, tm):
    j = pl.program_id(1)
    x = x_ref[0]
    h = _rms(x, gains_ref[0:1, :])

    n_blk = D_MODEL // LANES
    blk_per_grp = POOL_GROUP_DIM // LANES

    @pl.when(j == 0)
    def _():
        hp_ref[:, 0:POOL_HALO, :] = jnp.zeros((n_blk, POOL_HALO, LANES), F32)

    for cb in range(n_blk):
        hp_ref[cb, POOL_HALO:POOL_HALO + tm, :] = h[:, cb * LANES:(cb + 1) * LANES]
    t_tile = lax.broadcasted_iota(jnp.int32, (SUBLANES, 1), 0) * ROW_STRIDE + (j * tm + 1)

    def pool_group(grp, carry):
        g0 = pl.multiple_of(grp * ROW_GROUP, ROW_GROUP)
        for r in range(ROW_STRIDE):
            for gi, win in enumerate(POOL_WINDOWS):
                inv = 1.0 / jnp.minimum(win, t_tile + (g0 + r)).astype(F32)
                for cb in range(gi * blk_per_grp, (gi + 1) * blk_per_grp):
                    cur = hp_ref[cb, _strided_rows(g0 + (POOL_HALO + r)), :]
                    tot = cur
                    for s in range(1, win):
                        tot = tot + hp_ref[cb, _strided_rows(g0 + (POOL_HALO + r - s)), :]
                    d_ref[cb, _strided_rows(g0 + r), :] = tot * inv - cur
        return carry

    lax.fori_loop(0, tm // ROW_GROUP, pool_group, 0)
    diffs = [jnp.concatenate([d_ref[cb] for cb in range(gi * blk_per_grp, (gi + 1) * blk_per_grp)],
                             axis=1) for gi in range(len(POOL_WINDOWS))]
    m = _pool_project(diffs, pw_ref, ps_ref[...])

    @pl.when(j == pl.num_programs(1) - 1)
    def _():
        for cb in range(n_blk):
            pool_ref[0, :, cb * LANES:(cb + 1) * LANES] = hp_ref[
                cb, tm + POOL_HALO - POOL_BUF:tm + POOL_HALO, :]

    hp_ref[:, 0:POOL_HALO, :] = hp_ref[:, tm:tm + POOL_HALO, :]
    y_ref[0] = _residual_ffn(x, m, gains_ref[1:2, :], gains_ref[2:3, :], gains_ref[3:4, :],
                             win_ref, wout_ref)


def _pool_layer_sample_kernel(x_ref, buf_ref, gains_ref, pw_ref, ps_ref, win_ref, wout_ref,
                              y_ref, pool_ref, hp_ref, *, tb, seq, n_past):
    x = x_ref[...]
    h = _rms(x, gains_ref[0:1, :])
    hp_ref[:, POOL_HALO - POOL_BUF:POOL_HALO, :] = buf_ref[...]
    hp_ref[:, POOL_HALO:POOL_HALO + seq, :] = h
    t = lax.broadcasted_iota(jnp.int32, (1, seq, 1), 1)
    diffs = []
    for gi, win in enumerate(POOL_WINDOWS):
        c0, c1 = gi * POOL_GROUP_DIM, (gi + 1) * POOL_GROUP_DIM
        cur = hp_ref[:, POOL_HALO:POOL_HALO + seq, c0:c1]
        tot = cur
        for s in range(1, win):
            tot = tot + hp_ref[:, POOL_HALO - s:POOL_HALO - s + seq, c0:c1]
        cnt = jnp.minimum(win, t + 1 + n_past).astype(F32)
        diffs.append((tot / cnt - cur).reshape(tb * seq, POOL_GROUP_DIM))
    m = _pool_project(diffs, pw_ref, ps_ref[...])
    pool_ref[...] = hp_ref[:, POOL_HALO + seq - POOL_BUF:POOL_HALO + seq, :]
    y = _residual_ffn(x.reshape(tb * seq, D_MODEL), m, gains_ref[1:2, :], gains_ref[2:3, :],
                      gains_ref[3:4, :], win_ref, wout_ref)
    y_ref[...] = y.reshape(tb, seq, D_MODEL)


def _pool_layer_prompt(x, gains, pw, ps, win, wout, layer):
    b, l, d = x.shape
    tm = PROMPT_TILE
    return pl.pallas_call(
        functools.partial(_pool_layer_prompt_kernel, tm=tm),
        grid=(b, l // tm),
        in_specs=[
            pl.BlockSpec((1, tm, d), lambda i, j: (i, j, 0)),
            _const_spec(gains.shape), _const_spec(pw.shape), _const_spec(ps.shape),
            _layer_spec(win, layer), _layer_spec(wout, layer),
        ],
        out_specs=[
            pl.BlockSpec((1, tm, d), lambda i, j: (i, j, 0)),
            pl.BlockSpec((1, POOL_BUF, d), lambda i, j: (i, 0, 0)),
        ],
        out_shape=[jax.ShapeDtypeStruct((b, l, d), F32),
                   jax.ShapeDtypeStruct((b, POOL_BUF, d), F32)],
        scratch_shapes=[pltpu.VMEM((d // LANES, POOL_HALO + tm, LANES), F32),
                        pltpu.VMEM((d // LANES, tm, LANES), F32)],
        compiler_params=_params("arbitrary", "arbitrary"),
        name="pool_layer_prompt",
    )(x, gains, pw, ps, win, wout)


def _pool_layer_sample(x, buf, gains, pw, ps, win, wout, layer):
    b, l, d = x.shape
    tb = SAMPLE_BTILE
    n_past = buf.shape[1]
    return pl.pallas_call(
        functools.partial(_pool_layer_sample_kernel, tb=tb, seq=l, n_past=n_past),
        grid=(b // tb,),
        in_specs=[
            pl.BlockSpec((tb, l, d), lambda i: (i, 0, 0)),
            pl.BlockSpec((tb, POOL_BUF, d), lambda i: (i, 0, 0)),
            _const_spec(gains.shape), _const_spec(pw.shape), _const_spec(ps.shape),
            _layer_spec(win, layer), _layer_spec(wout, layer),
        ],
        out_specs=[
            pl.BlockSpec((tb, l, d), lambda i: (i, 0, 0)),
            pl.BlockSpec((tb, POOL_BUF, d), lambda i: (i, 0, 0)),
        ],
        out_shape=[jax.ShapeDtypeStruct((b, l, d), F32),
                   jax.ShapeDtypeStruct((b, POOL_BUF, d), F32)],
        scratch_shapes=[pltpu.VMEM((tb, POOL_HALO + l, d), F32)],
        compiler_params=_params("arbitrary"),
        name="pool_layer_sample",
    )(x, buf, gains, pw, ps, win, wout)


def _gdn_qkv_slab(sl, conv, q_ref, k_ref, v_ref, rows):
    if sl >= 2 * K_HEADS:
        v_ref[sl - 2 * K_HEADS, rows, :] = conv
        return
    unit = conv * lax.rsqrt(jnp.sum(conv * conv, axis=-1, keepdims=True) + EPS)
    if sl < K_HEADS:
        q_ref[sl, rows, :] = unit * (HEAD_DIM ** -0.5)
    else:
        k_ref[sl - K_HEADS, rows, :] = unit


def _gdn_gates(ba, alog, dtb, beta_ref, g_ref):
    beta_ref[...] = _sigmoid(ba[:, :LANES])
    g_ref[...] = -jnp.exp(alog) * _softplus(ba[:, LANES:] + dtb)


def _strided_rows(first):
    return pl.ds(first, SUBLANES, stride=ROW_STRIDE)


def _gdn_pre_prompt_kernel(x_ref, gain_ref, wqkvz_ref, wba_ref, cw_ref, alog_ref, dtb_ref,
                           q_ref, k_ref, v_ref, z_ref, beta_ref, g_ref, conv_ref, up_ref, *, tm):
    j = pl.program_id(1)
    h = _rms(x_ref[0], gain_ref[...]).astype(BF16)
    n_slab = CONV_DIM // HEAD_DIM
    n_grp = n_slab // PRE_SLABS

    @pl.when(j == 0)
    def _():
        up_ref[:, 0:CONV_HALO, :] = jnp.zeros((n_slab, CONV_HALO, HEAD_DIM), F32)

    def project(grp):
        c0 = grp * PRE_SLABS * HEAD_DIM
        p = jnp.dot(h, wqkvz_ref[:, c0:c0 + PRE_SLABS * HEAD_DIM], preferred_element_type=F32)
        for t in range(PRE_SLABS):
            up_ref[grp * PRE_SLABS + t, CONV_HALO:CONV_HALO + tm, :] = (
                p[:, t * HEAD_DIM:(t + 1) * HEAD_DIM])

    def project_z(half):
        c0 = CONV_DIM + half * (V_DIM // 2)
        p = jnp.dot(h, wqkvz_ref[:, c0:c0 + V_DIM // 2], preferred_element_type=F32)
        for t in range(V_HEADS // 2):
            z_ref[half * (V_HEADS // 2) + t] = p[:, t * HEAD_DIM:(t + 1) * HEAD_DIM]

    base = CONV_HALO - (CONV_WIDTH - 1)

    def convolve(grp):
        for sl in range(grp * PRE_SLABS, (grp + 1) * PRE_SLABS):
            lanes = slice(sl * HEAD_DIM, (sl + 1) * HEAD_DIM)
            for row0 in range(0, tm, ROW_GROUP):
                for r in range(ROW_STRIDE):
                    acc = up_ref[sl, _strided_rows(row0 + base + r), :] * cw_ref[0:1, lanes]
                    for tap in range(1, CONV_WIDTH):
                        acc = acc + (up_ref[sl, _strided_rows(row0 + base + tap + r), :]
                                     * cw_ref[tap:tap + 1, lanes])
                    _gdn_qkv_slab(sl, _silu(acc), q_ref, k_ref, v_ref, _strided_rows(row0 + r))

    project(0)
    for grp in range(n_grp):
        if grp + 1 < n_grp:
            project(grp + 1)
        else:
            project_z(0)
        convolve(grp)
    project_z(1)
    ba = jnp.dot(h, wba_ref[...], preferred_element_type=F32)
    _gdn_gates(ba, alog_ref[...], dtb_ref[...], beta_ref, g_ref)

    @pl.when(j == pl.num_programs(1) - 1)
    def _():
        for sl in range(n_slab):
            conv_ref[0, :, sl * HEAD_DIM:(sl + 1) * HEAD_DIM] = up_ref[
                sl, tm + CONV_HALO - (CONV_WIDTH - 1):tm + CONV_HALO, :]

    up_ref[:, 0:CONV_HALO, :] = up_ref[:, tm:tm + CONV_HALO, :]


def _gdn_pre_sample_kernel(x_ref, buf_ref, gain_ref, wqkvz_ref, wba_ref, cw_ref, alog_ref,
                           dtb_ref, q_ref, k_ref, v_ref, z_ref, beta_ref, g_ref, conv_ref, up_ref,
                           *, tb, seq):
    m = tb * seq
    h = _rms(x_ref[...].reshape(m, D_MODEL), gain_ref[...]).astype(BF16)
    proj = jnp.dot(h, wqkvz_ref[:, :CONV_DIM + V_DIM], preferred_element_type=F32)
    ba = jnp.dot(h, wba_ref[...], preferred_element_type=F32)
    for hh in range(V_HEADS):
        z_ref[hh] = proj[:, CONV_DIM + hh * HEAD_DIM:CONV_DIM + (hh + 1) * HEAD_DIM]
    base = CONV_HALO - (CONV_WIDTH - 1)
    up_ref[:, base:CONV_HALO, :] = buf_ref[...]
    up_ref[:, CONV_HALO:CONV_HALO + seq, :] = proj[:, :CONV_DIM].reshape(tb, seq, CONV_DIM)
    acc = up_ref[:, base:base + seq, :] * cw_ref[0:1, :]
    for tap in range(1, CONV_WIDTH):
        acc = acc + up_ref[:, base + tap:base + tap + seq, :] * cw_ref[tap:tap + 1, :]
    conv_ref[...] = up_ref[:, CONV_HALO + seq - (CONV_WIDTH - 1):CONV_HALO + seq, :]
    _gdn_gates(ba, alog_ref[...], dtb_ref[...], beta_ref, g_ref)
    conv = _silu(acc).reshape(m, CONV_DIM)
    for sl in range(CONV_DIM // HEAD_DIM):
        _gdn_qkv_slab(sl, conv[:, sl * HEAD_DIM:(sl + 1) * HEAD_DIM], q_ref, k_ref, v_ref,
                      slice(None))


def _gdn_pre_out_shapes(n):
    return [jax.ShapeDtypeStruct((K_HEADS, n, HEAD_DIM), F32),
            jax.ShapeDtypeStruct((K_HEADS, n, HEAD_DIM), F32),
            jax.ShapeDtypeStruct((V_HEADS, n, HEAD_DIM), F32),
            jax.ShapeDtypeStruct((V_HEADS, n, HEAD_DIM), F32),
            jax.ShapeDtypeStruct((n, LANES), F32), jax.ShapeDtypeStruct((n, LANES), F32)]


def _gdn_pre_out_specs(rows, index):
    heads = lambda n: pl.BlockSpec((n, rows, HEAD_DIM), lambda *g: (0, index(*g), 0))
    lane = pl.BlockSpec((rows, LANES), lambda *g: (index(*g), 0))
    return [heads(K_HEADS), heads(K_HEADS), heads(V_HEADS), heads(V_HEADS), lane, lane]


def _gdn_pre_prompt(x, gain, wqkvz, wba, cw, alog, dtb):
    b, l, d = x.shape
    tm = GDN_PRE_TILE
    nj = l // tm
    return pl.pallas_call(
        functools.partial(_gdn_pre_prompt_kernel, tm=tm),
        grid=(b, nj),
        in_specs=[pl.BlockSpec((1, tm, d), lambda i, j: (i, j, 0))]
        + [_const_spec(a.shape) for a in (gain, wqkvz, wba, cw, alog, dtb)],
        out_specs=_gdn_pre_out_specs(tm, lambda i, j: i * nj + j)
        + [pl.BlockSpec((1, CONV_WIDTH - 1, CONV_DIM), lambda i, j: (i, 0, 0))],
        out_shape=_gdn_pre_out_shapes(b * l)
        + [jax.ShapeDtypeStruct((b, CONV_WIDTH - 1, CONV_DIM), F32)],
        scratch_shapes=[pltpu.VMEM((CONV_DIM // HEAD_DIM, CONV_HALO + tm, HEAD_DIM), F32)],
        compiler_params=_params("arbitrary", "arbitrary"),
        name="gdn_pre_prompt",
    )(x, gain, wqkvz, wba, cw, alog, dtb)


def _gdn_pre_sample(x, buf, gain, wqkvz, wba, cw, alog, dtb):
    b, l, d = x.shape
    tb = SAMPLE_BTILE
    return pl.pallas_call(
        functools.partial(_gdn_pre_sample_kernel, tb=tb, seq=l),
        grid=(b // tb,),
        in_specs=[pl.BlockSpec((tb, l, d), lambda i: (i, 0, 0)),
                  pl.BlockSpec((tb, CONV_WIDTH - 1, CONV_DIM), lambda i: (i, 0, 0))]
        + [_const_spec(a.shape) for a in (gain, wqkvz, wba, cw, alog, dtb)],
        out_specs=_gdn_pre_out_specs(tb * l, lambda i: i)
        + [pl.BlockSpec((tb, CONV_WIDTH - 1, CONV_DIM), lambda i: (i, 0, 0))],
        out_shape=_gdn_pre_out_shapes(b * l)
        + [jax.ShapeDtypeStruct((b, CONV_WIDTH - 1, CONV_DIM), F32)],
        scratch_shapes=[pltpu.VMEM((tb, CONV_HALO + l, CONV_DIM), F32)],
        compiler_params=_params("arbitrary"),
        name="gdn_pre_sample",
    )(x, buf, gain, wqkvz, wba, cw, alog, dtb)


def _unit_lower_inverses(mats, c):
    ri = lax.broadcasted_iota(jnp.int32, (c, c), 0)
    ci = lax.broadcasted_iota(jnp.int32, (c, c), 1)
    eye = (ri == ci).astype(F32)
    pair = ((ri // 2) == (ci // 2)) & (ri > ci)
    xs = [eye - jnp.where(pair, a, 0.0) for a in mats]
    blk = 2
    while blk < c:
        off = ((ri // (2 * blk)) == (ci // (2 * blk))) & ((ri // blk) > (ci // blk))
        ys = [_dot(jnp.where(off, a, 0.0), x) for a, x in zip(mats, xs)]
        xs = [x - _dot(x, y) for x, y in zip(xs, ys)]
        blk *= 2
    return xs


def _delta_chunks(q_ref, k_ref, v_ref, z_ref, beta_ref, g_ref, og_ref, onorm, rows,
                  state_load, state_store, c):
    n = len(rows)
    ri = lax.broadcasted_iota(jnp.int32, (c, c), 0)
    ci = lax.broadcasted_iota(jnp.int32, (c, c), 1)
    causal = ri >= ci
    strict = ri > ci
    rep = V_HEADS // K_HEADS
    units = [(i, h) for i in range(n) for h in range(V_HEADS)]
    kunits = [(i, j) for i in range(n) for j in range(K_HEADS)]

    gcum = [_cumsum_rows(g_ref[rows[i], :], c) for i in range(n)]
    gcum_t = [x.T for x in gcum]
    egcum = [jnp.exp(x) for x in gcum]
    etail_t = [x[:, c - 1:c] - x for x in gcum_t]
    etail_t = [jnp.exp(x) for x in etail_t]
    beta = [beta_ref[rows[i], :] for i in range(n)]
    ks = {(i, j): k_ref[j, rows[i], :] for i, j in kunits}
    kts = {u: ks[u].T for u in kunits}
    kq = {(i, j): _dot(jnp.concatenate([ks[i, j], q_ref[j, rows[i], :]], axis=0), kts[i, j])
          for i, j in kunits}
    gcol = {(i, h): gcum[i][:, h:h + 1] for i, h in units}
    bcol = {(i, h): beta[i][:, h:h + 1] for i, h in units}
    egc = {(i, h): egcum[i][:, h:h + 1] for i, h in units}
    decay = {(i, h): jnp.where(
        causal, jnp.exp(jnp.minimum(gcol[i, h] - gcum_t[i][h:h + 1, :], 0.0)), 0.0)
        for i, h in units}
    a_mats = [jnp.where(strict, kq[i, h // rep][:c] * bcol[i, h] * decay[i, h], 0.0)
              for i, h in units]
    t_inv = dict(zip(units, _unit_lower_inverses(a_mats, c)))
    uw = {(i, h): _dot(t_inv[i, h], jnp.concatenate(
        [v_ref[h, rows[i], :] * bcol[i, h], ks[i, h // rep] * (bcol[i, h] * egc[i, h])], axis=1))
        for i, h in units}
    wq = {(i, h): jnp.concatenate(
        [uw[i, h][:, HEAD_DIM:], q_ref[h // rep, rows[i], :] * egc[i, h]], axis=0).astype(BF16)
        for i, h in units}
    qkd = {(i, h): (kq[i, h // rep][c:] * decay[i, h]).astype(BF16) for i, h in units}

    heads = range(V_HEADS)
    for i in range(n):
        s_old = [state_load(i, h) for h in heads]
        ws = [_dot(wq[i, h], s_old[h]) for h in heads]
        v_new = [uw[i, h][:, :HEAD_DIM] - ws[h][:c] for h in heads]
        o = [ws[h][c:] + _dot(qkd[i, h], v_new[h]) for h in heads]
        for h in heads:
            k_dec_t = kts[i, h // rep] * etail_t[i][h:h + 1, :]
            state_store(i, h, s_old[h] * egcum[i][c - 1:c, h:h + 1] + _dot(k_dec_t, v_new[h]))
        for h in heads:
            og_ref[rows[i], h * HEAD_DIM:(h + 1) * HEAD_DIM] = (
                _rms(o[h], onorm) * _silu(z_ref[h, rows[i], :])).astype(BF16)


def _cumsum_rows(g, c):
    ri = lax.broadcasted_iota(jnp.int32, (c, c), 0)
    ci = lax.broadcasted_iota(jnp.int32, (c, c), 1)
    tri = (ri >= ci).astype(F32)
    return jnp.dot(tri, g, preferred_element_type=F32, precision=lax.Precision.HIGHEST)


def _gdn_scan_prompt_kernel(q_ref, k_ref, v_ref, z_ref, beta_ref, g_ref, onorm_ref,
                            og_ref, s_ref, *, c, n_chunk):
    @pl.when(pl.program_id(1) == 0)
    def _():
        s_ref[...] = jnp.zeros(s_ref.shape, F32)

    def load(i, hh):
        return s_ref[0, hh]

    def store(i, hh, val):
        s_ref[0, hh] = val

    def body(it, carry):
        r0 = pl.multiple_of(it * (SCAN_GROUP * c), SCAN_GROUP * c)
        rows = [pl.ds(r0 + i * c, c) for i in range(SCAN_GROUP)]
        _delta_chunks(q_ref, k_ref, v_ref, z_ref, beta_ref, g_ref, og_ref, onorm_ref[...], rows,
                      load, store, c)
        return carry

    lax.fori_loop(0, n_chunk // SCAN_GROUP, body, 0)


def _gdn_scan_sample_kernel(q_ref, k_ref, v_ref, z_ref, beta_ref, g_ref, onorm_ref, s0_ref,
                            og_ref, s_ref, *, tb, c):
    def load(i, hh):
        return s0_ref[i, hh]

    def store(i, hh, val):
        s_ref[i, hh] = val

    rows = [pl.ds(i * c, c) for i in range(tb)]
    _delta_chunks(q_ref, k_ref, v_ref, z_ref, beta_ref, g_ref, og_ref, onorm_ref[...], rows,
                  load, store, c)


def _gdn_scan_prompt(q, k, v, z, beta, g, onorm, b, l):
    c = PROMPT_CHUNK
    tm = SCAN_TILE
    nc = l // tm
    tok = lambda w: pl.BlockSpec((tm, w), lambda i, j: (i * nc + j, 0))
    return pl.pallas_call(
        functools.partial(_gdn_scan_prompt_kernel, c=c, n_chunk=tm // c),
        grid=(b, nc),
        in_specs=_gdn_pre_out_specs(tm, lambda i, j: i * nc + j) + [_const_spec(onorm.shape)],
        out_specs=[tok(V_DIM),
                   pl.BlockSpec((1, V_HEADS, HEAD_DIM, HEAD_DIM), lambda i, j: (i, 0, 0, 0))],
        out_shape=[jax.ShapeDtypeStruct((b * l, V_DIM), BF16),
                   jax.ShapeDtypeStruct((b, V_HEADS, HEAD_DIM, HEAD_DIM), F32)],
        compiler_params=_params("arbitrary", "arbitrary"),
        name="gdn_scan_prompt",
    )(q, k, v, z, beta, g, onorm)


def _gdn_scan_sample(q, k, v, z, beta, g, onorm, s0, b, l):
    tb = SAMPLE_SCAN_BTILE
    tok = lambda w: pl.BlockSpec((tb * l, w), lambda i: (i, 0))
    st = pl.BlockSpec((tb, V_HEADS, HEAD_DIM, HEAD_DIM), lambda i: (i, 0, 0, 0))
    return pl.pallas_call(
        functools.partial(_gdn_scan_sample_kernel, tb=tb, c=l),
        grid=(b // tb,),
        in_specs=_gdn_pre_out_specs(tb * l, lambda i: i) + [_const_spec(onorm.shape), st],
        out_specs=[tok(V_DIM), st],
        out_shape=[jax.ShapeDtypeStruct((b * l, V_DIM), BF16),
                   jax.ShapeDtypeStruct((b, V_HEADS, HEAD_DIM, HEAD_DIM), F32)],
        compiler_params=_params("arbitrary"),
        name="gdn_scan_sample",
    )(q, k, v, z, beta, g, onorm, s0)


def _gdn_post_kernel(og_ref, x_ref, gains_ref, wo_ref, win_ref, wout_ref, y_ref):
    m = jnp.dot(og_ref[...], wo_ref[...], preferred_element_type=F32)
    y_ref[...] = _residual_ffn(x_ref[...], m, gains_ref[0:1, :], gains_ref[1:2, :],
                               gains_ref[2:3, :], win_ref, wout_ref)


def _gdn_post(og, x, gains, wo, win, wout, layer):
    n, d = x.shape
    tm = PROMPT_TILE
    return pl.pallas_call(
        _gdn_post_kernel,
        grid=(n // tm,),
        in_specs=[pl.BlockSpec((tm, V_DIM), lambda i: (i, 0)),
                  pl.BlockSpec((tm, d), lambda i: (i, 0))]
        + [_const_spec(gains.shape), _const_spec(wo.shape), _layer_spec(win, layer),
           _layer_spec(wout, layer)],
        out_specs=pl.BlockSpec((tm, d), lambda i: (i, 0)),
        out_shape=jax.ShapeDtypeStruct((n, d), F32),
        compiler_params=_params("arbitrary"),
        name="gdn_post",
    )(og, x, gains, wo, win, wout)


def _head_lanes(vec):
    return jnp.pad(vec.astype(F32), (0, LANES - V_HEADS)).reshape(1, LANES)


def kernel(x_prompt, x_sample, state_pool, state_gdn_conv, state_gdn_rec, norm_mix_pre,
           norm_mix_post, norm_ffn_pre, norm_ffn_post, pool_w, pool_scale, gdn_w_in,
           gdn_conv_w, gdn_a_log, gdn_dt_bias, gdn_o_norm, gdn_w_out, ffn_w_in, ffn_w_out):
    bp, lp, d = x_prompt.shape
    bs, ls, _ = x_sample.shape

    gains0 = jnp.stack([norm_mix_pre[0], norm_mix_post[0], norm_ffn_pre[0], norm_ffn_post[0]])
    gains1 = jnp.stack([norm_mix_post[1], norm_ffn_pre[1], norm_ffn_post[1]])
    gain1_pre = norm_mix_pre[1].reshape(1, d)
    pw = pool_w[0].astype(BF16)
    ps = pool_scale[0].reshape(1, d)
    win, wout = ffn_w_in.astype(BF16), ffn_w_out.astype(BF16)
    w_in = gdn_w_in[0]
    wqkvz = w_in.astype(BF16)
    w_b = w_in[:, CONV_DIM + V_DIM:CONV_DIM + V_DIM + V_HEADS]
    w_a = w_in[:, CONV_DIM + V_DIM + V_HEADS:]
    lane_pad = ((0, 0), (0, LANES - V_HEADS))
    wba = jnp.concatenate([jnp.pad(w_b, lane_pad), jnp.pad(w_a, lane_pad)], axis=1).astype(BF16)
    cw = gdn_conv_w[0]
    alog, dtb = _head_lanes(gdn_a_log[0]), _head_lanes(gdn_dt_bias[0])
    onorm = gdn_o_norm[0].reshape(1, HEAD_DIM)
    wo = gdn_w_out[0].astype(BF16)

    xp1, pool_p = _pool_layer_prompt(x_prompt, gains0, pw, ps, win, wout, 0)
    xs1, pool_s = _pool_layer_sample(x_sample, state_pool[0], gains0, pw, ps, win, wout, 0)

    qp, kp, vp, zp, betap, gp, conv_p = _gdn_pre_prompt(xp1, gain1_pre, wqkvz, wba, cw, alog, dtb)
    qs, ks, vs, zs, betas, gs, conv_s = _gdn_pre_sample(xs1, state_gdn_conv[0], gain1_pre, wqkvz,
                                                        wba, cw, alog, dtb)

    ogp, rec_p = _gdn_scan_prompt(qp, kp, vp, zp, betap, gp, onorm, bp, lp)
    ogs, rec_s = _gdn_scan_sample(qs, ks, vs, zs, betas, gs, onorm, state_gdn_rec[0], bs, ls)

    yp = _gdn_post(ogp, xp1.reshape(bp * lp, d), gains1, wo, win, wout, 1).reshape(bp, lp, d)
    ys = _gdn_post(ogs, xs1.reshape(bs * ls, d), gains1, wo, win, wout, 1).reshape(bs, ls, d)

    return (yp, ys, pool_p[None], pool_s[None], conv_p[None], conv_s[None], rec_p[None],
            rec_s[None])
```

```python
import functools

import jax
import jax.numpy as jnp
from jax import lax
from jax.experimental import pallas as pl
from jax.experimental.pallas import tpu as pltpu

D_MODEL = 1024
POOL_WINDOWS = (2, 4, 8, 16)
POOL_GROUP_DIM = D_MODEL // len(POOL_WINDOWS)
POOL_BUF = max(POOL_WINDOWS) - 1
K_HEADS = 8
V_HEADS = 16
HEAD_DIM = 128
QK_DIM = K_HEADS * HEAD_DIM
V_DIM = V_HEADS * HEAD_DIM
CONV_DIM = 2 * QK_DIM + V_DIM
CONV_WIDTH = 4
D_FF = 2816
EPS = 1e-6

F32 = jnp.float32
BF16 = jnp.bfloat16

SUBLANES = 8
LANES = 128
POOL_HALO = 16
CONV_HALO = SUBLANES
VMEM_LIMIT = 56 * 1024 * 1024
ROW_STRIDE = 4
ROW_GROUP = SUBLANES * ROW_STRIDE

PROMPT_TILE = 512
GDN_PRE_TILE = 512
PRE_SLABS = 4
SAMPLE_BTILE = 32
PROMPT_CHUNK = 64
SCAN_TILE = 512
SCAN_GROUP = 2
SAMPLE_SCAN_BTILE = 4


def _rms(x, gain):
    ms = jnp.mean(x * x, axis=-1, keepdims=True)
    return x * lax.rsqrt(ms + EPS) * gain


def _sigmoid(x):
    return 1.0 / (1.0 + jnp.exp(-x))


def _silu(x):
    return x * _sigmoid(x)


def _softplus(x):
    return jnp.maximum(x, 0.0) + jnp.log1p(jnp.exp(-jnp.abs(x)))


def _dot(a, b):
    return jnp.dot(a.astype(BF16), b.astype(BF16), preferred_element_type=F32)


def _dot_nt(a, b):
    return lax.dot_general(a.astype(BF16), b.astype(BF16), (((1,), (1,)), ((), ())),
                           preferred_element_type=F32)


def _dot_tn(a, b):
    return lax.dot_general(a.astype(BF16), b.astype(BF16), (((0,), (0,)), ((), ())),
                           preferred_element_type=F32)


def _const_spec(shape):
    nd = len(shape)
    return pl.BlockSpec(shape, lambda *_: (0,) * nd, pipeline_mode=pl.Buffered(1))


def _layer_spec(stacked, layer):
    nd = stacked.ndim - 1
    return pl.BlockSpec((None,) + stacked.shape[1:], lambda *_: (layer,) + (0,) * nd,
                        pipeline_mode=pl.Buffered(1))


def _params(*sem):
    return pltpu.CompilerParams(dimension_semantics=sem, vmem_limit_bytes=VMEM_LIMIT)


def _residual_ffn(x, m, g_post, g_fpre, g_fpost, win_ref, wout_ref):
    x1 = x + _rms(m, g_post)
    h = _rms(x1, g_fpre).astype(BF16)
    gate = jnp.dot(h, win_ref[:, :D_FF], preferred_element_type=F32)
    up = jnp.dot(h, win_ref[:, D_FF:], preferred_element_type=F32)
    act = (_silu(gate) * up).astype(BF16)
    f = jnp.dot(act, wout_ref[...], preferred_element_type=F32)
    return x1 + _rms(f, g_fpost)


def _pool_project(diffs, pw_ref, scale):
    parts = [_dot(d, pw_ref[gi]) for gi, d in enumerate(diffs)]
    return jnp.concatenate(parts, axis=-1) * scale


def _pool_layer_prompt_kernel(x_ref, gains_ref, pw_ref, ps_ref, win_ref, wout_ref,
                              y_ref, pool_ref, hp_ref, d_ref, *, tm):
    j = pl.program_id(1)
    x = x_ref[0]
    h = _rms(x, gains_ref[0:1, :])

    n_blk = D_MODEL // LANES
    blk_per_grp = POOL_GROUP_DIM // LANES

    @pl.when(j == 0)
    def _():
        hp_ref[:, 0:POOL_HALO, :] = jnp.zeros((n_blk, POOL_HALO, LANES), F32)

    for cb in range(n_blk):
        hp_ref[cb, POOL_HALO:POOL_HALO + tm, :] = h[:, cb * LANES:(cb + 1) * LANES]
    t_tile = lax.broadcasted_iota(jnp.int32, (SUBLANES, 1), 0) * ROW_STRIDE + (j * tm + 1)

    def pool_group(grp, carry):
        g0 = pl.multiple_of(grp * ROW_GROUP, ROW_GROUP)
        for r in range(ROW_STRIDE):
            for gi, win in enumerate(POOL_WINDOWS):
                inv = 1.0 / jnp.minimum(win, t_tile + (g0 + r)).astype(F32)
                for cb in range(gi * blk_per_grp, (gi + 1) * blk_per_grp):
                    cur = hp_ref[cb, _strided_rows(g0 + (POOL_HALO + r)), :]
                    tot = cur
                    for s in range(1, win):
                        tot = tot + hp_ref[cb, _strided_rows(g0 + (POOL_HALO + r - s)), :]
                    d_ref[cb, _strided_rows(g0 + r), :] = tot * inv - cur
        return carry

    lax.fori_loop(0, tm // ROW_GROUP, pool_group, 0)
    diffs = [jnp.concatenate([d_ref[cb] for cb in range(gi * blk_per_grp, (gi + 1) * blk_per_grp)],
                             axis=1) for gi in range(len(POOL_WINDOWS))]
    m = _pool_project(diffs, pw_ref, ps_ref[...])

    @pl.when(j == pl.num_programs(1) - 1)
    def _():
        for cb in range(n_blk):
            pool_ref[0, :, cb * LANES:(cb + 1) * LANES] = hp_ref[
                cb, tm + POOL_HALO - POOL_BUF:tm + POOL_HALO, :]

    hp_ref[:, 0:POOL_HALO, :] = hp_ref[:, tm:tm + POOL_HALO, :]
    y_ref[0] = _residual_ffn(x, m, gains_ref[1:2, :], gains_ref[2:3, :], gains_ref[3:4, :],
                             win_ref, wout_ref)


def _pool_layer_sample_kernel(x_ref, buf_ref, gains_ref, pw_ref, ps_ref, win_ref, wout_ref,
                              y_ref, pool_ref, hp_ref, *, tb, seq, n_past):
    x = x_ref[...]
    h = _rms(x, gains_ref[0:1, :])
    hp_ref[:, POOL_HALO - POOL_BUF:POOL_HALO, :] = buf_ref[...]
    hp_ref[:, POOL_HALO:POOL_HALO + seq, :] = h
    t = lax.broadcasted_iota(jnp.int32, (1, seq, 1), 1)
    diffs = []
    for gi, win in enumerate(POOL_WINDOWS):
        c0, c1 = gi * POOL_GROUP_DIM, (gi + 1) * POOL_GROUP_DIM
        cur = hp_ref[:, POOL_HALO:POOL_HALO + seq, c0:c1]
        tot = cur
        for s in range(1, win):
            tot = tot + hp_ref[:, POOL_HALO - s:POOL_HALO - s + seq, c0:c1]
        cnt = jnp.minimum(win, t + 1 + n_past).astype(F32)
        diffs.append((tot / cnt - cur).reshape(tb * seq, POOL_GROUP_DIM))
    m = _pool_project(diffs, pw_ref, ps_ref[...])
    pool_ref[...] = hp_ref[:, POOL_HALO + seq - POOL_BUF:POOL_HALO + seq, :]
    y = _residual_ffn(x.reshape(tb * seq, D_MODEL), m, gains_ref[1:2, :], gains_ref[2:3, :],
                      gains_ref[3:4, :], win_ref, wout_ref)
    y_ref[...] = y.reshape(tb, seq, D_MODEL)


def _pool_layer_prompt(x, gains, pw, ps, win, wout, layer):
    b, l, d = x.shape
    tm = PROMPT_TILE
    return pl.pallas_call(
        functools.partial(_pool_layer_prompt_kernel, tm=tm),
        grid=(b, l // tm),
        in_specs=[
            pl.BlockSpec((1, tm, d), lambda i, j: (i, j, 0)),
            _const_spec(gains.shape), _const_spec(pw.shape), _const_spec(ps.shape),
            _layer_spec(win, layer), _layer_spec(wout, layer),
        ],
        out_specs=[
            pl.BlockSpec((1, tm, d), lambda i, j: (i, j, 0)),
            pl.BlockSpec((1, POOL_BUF, d), lambda i, j: (i, 0, 0)),
        ],
        out_shape=[jax.ShapeDtypeStruct((b, l, d), F32),
                   jax.ShapeDtypeStruct((b, POOL_BUF, d), F32)],
        scratch_shapes=[pltpu.VMEM((d // LANES, POOL_HALO + tm, LANES), F32),
                        pltpu.VMEM((d // LANES, tm, LANES), F32)],
        compiler_params=_params("arbitrary", "arbitrary"),
        name="pool_layer_prompt",
    )(x, gains, pw, ps, win, wout)


def _pool_layer_sample(x, buf, gains, pw, ps, win, wout, layer):
    b, l, d = x.shape
    tb = SAMPLE_BTILE
    n_past = buf.shape[1]
    return pl.pallas_call(
        functools.partial(_pool_layer_sample_kernel, tb=tb, seq=l, n_past=n_past),
        grid=(b // tb,),
        in_specs=[
            pl.BlockSpec((tb, l, d), lambda i: (i, 0, 0)),
            pl.BlockSpec((tb, POOL_BUF, d), lambda i: (i, 0, 0)),
            _const_spec(gains.shape), _const_spec(pw.shape), _const_spec(ps.shape),
            _layer_spec(win, layer), _layer_spec(wout, layer),
        ],
        out_specs=[
            pl.BlockSpec((tb, l, d), lambda i: (i, 0, 0)),
            pl.BlockSpec((tb, POOL_BUF, d), lambda i: (i, 0, 0)),
        ],
        out_shape=[jax.ShapeDtypeStruct((b, l, d), F32),
                   jax.ShapeDtypeStruct((b, POOL_BUF, d), F32)],
        scratch_shapes=[pltpu.VMEM((tb, POOL_HALO + l, d), F32)],
        compiler_params=_params("arbitrary"),
        name="pool_layer_sample",
    )(x, buf, gains, pw, ps, win, wout)


def _gdn_qkv_slab(sl, conv, q_ref, k_ref, v_ref, rows):
    if sl >= 2 * K_HEADS:
        v_ref[sl - 2 * K_HEADS, rows, :] = conv
        return
    unit = conv * lax.rsqrt(jnp.sum(conv * conv, axis=-1, keepdims=True) + EPS)
    if sl < K_HEADS:
        q_ref[sl, rows, :] = unit * (HEAD_DIM ** -0.5)
    else:
        k_ref[sl - K_HEADS, rows, :] = unit


def _gdn_gates(ba, alog, dtb, beta_ref, g_ref):
    beta_ref[...] = _sigmoid(ba[:, :LANES])
    g_ref[...] = -jnp.exp(alog) * _softplus(ba[:, LANES:] + dtb)


def _strided_rows(first):
    return pl.ds(first, SUBLANES, stride=ROW_STRIDE)


def _gdn_pre_prompt_kernel(x_ref, gain_ref, wqkvz_ref, wba_ref, cw_ref, alog_ref, dtb_ref,
                           q_ref, k_ref, v_ref, z_ref, beta_ref, g_ref, conv_ref, up_ref, *, tm):
    j = pl.program_id(1)
    h = _rms(x_ref[0], gain_ref[...]).astype(BF16)
    n_slab = CONV_DIM // HEAD_DIM
    n_grp = n_slab // PRE_SLABS

    @pl.when(j == 0)
    def _():
        up_ref[:, 0:CONV_HALO, :] = jnp.zeros((n_slab, CONV_HALO, HEAD_DIM), F32)

    def project(grp):
        c0 = grp * PRE_SLABS * HEAD_DIM
        p = jnp.dot(h, wqkvz_ref[:, c0:c0 + PRE_SLABS * HEAD_DIM], preferred_element_type=F32)
        for t in range(PRE_SLABS):
            up_ref[grp * PRE_SLABS + t, CONV_HALO:CONV_HALO + tm, :] = (
                p[:, t * HEAD_DIM:(t + 1) * HEAD_DIM])

    def project_z(half):
        c0 = CONV_DIM + half * (V_DIM // 2)
        p = jnp.dot(h, wqkvz_ref[:, c0:c0 + V_DIM // 2], preferred_element_type=F32)
        for t in range(V_HEADS // 2):
            z_ref[half * (V_HEADS // 2) + t] = p[:, t * HEAD_DIM:(t + 1) * HEAD_DIM]

    base = CONV_HALO - (CONV_WIDTH - 1)

    def convolve(grp):
        for sl in range(grp * PRE_SLABS, (grp + 1) * PRE_SLABS):
            lanes = slice(sl * HEAD_DIM, (sl + 1) * HEAD_DIM)
            for row0 in range(0, tm, ROW_GROUP):
                for r in range(ROW_STRIDE):
                    acc = up_ref[sl, _strided_rows(row0 + base + r), :] * cw_ref[0:1, lanes]
                    for tap in range(1, CONV_WIDTH):
                        acc = acc + (up_ref[sl, _strided_rows(row0 + base + tap + r), :]
                                     * cw_ref[tap:tap + 1, lanes])
                    _gdn_qkv_slab(sl, _silu(acc), q_ref, k_ref, v_ref, _strided_rows(row0 + r))

    project(0)
    for grp in range(n_grp):
        if grp + 1 < n_grp:
            project(grp + 1)
        else:
            project_z(0)
        convolve(grp)
    project_z(1)
    ba = jnp.dot(h, wba_ref[...], preferred_element_type=F32)
    _gdn_gates(ba, alog_ref[...], dtb_ref[...], beta_ref, g_ref)

    @pl.when(j == pl.num_programs(1) - 1)
    def _():
        for sl in range(n_slab):
            conv_ref[0, :, sl * HEAD_DIM:(sl + 1) * HEAD_DIM] = up_ref[
                sl, tm + CONV_HALO - (CONV_WIDTH - 1):tm + CONV_HALO, :]

    up_ref[:, 0:CONV_HALO, :] = up_ref[:, tm:tm + CONV_HALO, :]


def _gdn_pre_sample_kernel(x_ref, buf_ref, gain_ref, wqkvz_ref, wba_ref, cw_ref, alog_ref,
                           dtb_ref, q_ref, k_ref, v_ref, z_ref, beta_ref, g_ref, conv_ref, up_ref,
                           *, tb, seq):
    m = tb * seq
    h = _rms(x_ref[...].reshape(m, D_MODEL), gain_ref[...]).astype(BF16)
    proj = jnp.dot(h, wqkvz_ref[:, :CONV_DIM + V_DIM], preferred_element_type=F32)
    ba = jnp.dot(h, wba_ref[...], preferred_element_type=F32)
    for hh in range(V_HEADS):
        z_ref[hh] = proj[:, CONV_DIM + hh * HEAD_DIM:CONV_DIM + (hh + 1) * HEAD_DIM]
    base = CONV_HALO - (CONV_WIDTH - 1)
    up_ref[:, base:CONV_HALO, :] = buf_ref[...]
    up_ref[:, CONV_HALO:CONV_HALO + seq, :] = proj[:, :CONV_DIM].reshape(tb, seq, CONV_DIM)
    acc = up_ref[:, base:base + seq, :] * cw_ref[0:1, :]
    for tap in range(1, CONV_WIDTH):
        acc = acc + up_ref[:, base + tap:base + tap + seq, :] * cw_ref[tap:tap + 1, :]
    conv_ref[...] = up_ref[:, CONV_HALO + seq - (CONV_WIDTH - 1):CONV_HALO + seq, :]
    _gdn_gates(ba, alog_ref[...], dtb_ref[...], beta_ref, g_ref)
    conv = _silu(acc).reshape(m, CONV_DIM)
    for sl in range(CONV_DIM // HEAD_DIM):
        _gdn_qkv_slab(sl, conv[:, sl * HEAD_DIM:(sl + 1) * HEAD_DIM], q_ref, k_ref, v_ref,
                      slice(None))


def _gdn_pre_out_shapes(n):
    return [jax.ShapeDtypeStruct((K_HEADS, n, HEAD_DIM), F32),
            jax.ShapeDtypeStruct((K_HEADS, n, HEAD_DIM), F32),
            jax.ShapeDtypeStruct((V_HEADS, n, HEAD_DIM), F32),
            jax.ShapeDtypeStruct((V_HEADS, n, HEAD_DIM), F32),
            jax.ShapeDtypeStruct((n, LANES), F32), jax.ShapeDtypeStruct((n, LANES), F32)]


def _gdn_pre_out_specs(rows, index):
    heads = lambda n: pl.BlockSpec((n, rows, HEAD_DIM), lambda *g: (0, index(*g), 0))
    lane = pl.BlockSpec((rows, LANES), lambda *g: (index(*g), 0))
    return [heads(K_HEADS), heads(K_HEADS), heads(V_HEADS), heads(V_HEADS), lane, lane]


def _gdn_pre_prompt(x, gain, wqkvz, wba, cw, alog, dtb):
    b, l, d = x.shape
    tm = GDN_PRE_TILE
    nj = l // tm
    return pl.pallas_call(
        functools.partial(_gdn_pre_prompt_kernel, tm=tm),
        grid=(b, nj),
        in_specs=[pl.BlockSpec((1, tm, d), lambda i, j: (i, j, 0))]
        + [_const_spec(a.shape) for a in (gain, wqkvz, wba, cw, alog, dtb)],
        out_specs=_gdn_pre_out_specs(tm, lambda i, j: i * nj + j)
        + [pl.BlockSpec((1, CONV_WIDTH - 1, CONV_DIM), lambda i, j: (i, 0, 0))],
        out_shape=_gdn_pre_out_shapes(b * l)
        + [jax.ShapeDtypeStruct((b, CONV_WIDTH - 1, CONV_DIM), F32)],
        scratch_shapes=[pltpu.VMEM((CONV_DIM // HEAD_DIM, CONV_HALO + tm, HEAD_DIM), F32)],
        compiler_params=_params("arbitrary", "arbitrary"),
        name="gdn_pre_prompt",
    )(x, gain, wqkvz, wba, cw, alog, dtb)


def _gdn_pre_sample(x, buf, gain, wqkvz, wba, cw, alog, dtb):
    b, l, d = x.shape
    tb = SAMPLE_BTILE
    return pl.pallas_call(
        functools.partial(_gdn_pre_sample_kernel, tb=tb, seq=l),
        grid=(b // tb,),
        in_specs=[pl.BlockSpec((tb, l, d), lambda i: (i, 0, 0)),
                  pl.BlockSpec((tb, CONV_WIDTH - 1, CONV_DIM), lambda i: (i, 0, 0))]
        + [_const_spec(a.shape) for a in (gain, wqkvz, wba, cw, alog, dtb)],
        out_specs=_gdn_pre_out_specs(tb * l, lambda i: i)
        + [pl.BlockSpec((tb, CONV_WIDTH - 1, CONV_DIM), lambda i: (i, 0, 0))],
        out_shape=_gdn_pre_out_shapes(b * l)
        + [jax.ShapeDtypeStruct((b, CONV_WIDTH - 1, CONV_DIM), F32)],
        scratch_shapes=[pltpu.VMEM((tb, CONV_HALO + l, CONV_DIM), F32)],
        compiler_params=_params("arbitrary"),
        name="gdn_pre_sample",
    )(x, buf, gain, wqkvz, wba, cw, alog, dtb)


def _unit_lower_inverses(mats, c):
    ri = lax.broadcasted_iota(jnp.int32, (c, c), 0)
    ci = lax.broadcasted_iota(jnp.int32, (c, c), 1)
    eye = (ri == ci).astype(F32)
    pair = ((ri // 2) == (ci // 2)) & (ri > ci)
    xs = [eye - jnp.where(pair, a, 0.0) for a in mats]
    mats = [a.astype(BF16) for a in mats]
    blk = 2
    while blk < c:
        off = ((ri // (2 * blk)) == (ci // (2 * blk))) & ((ri // blk) > (ci // blk))
        xbs = [x.astype(BF16) for x in xs]
        ys = [_dot(jnp.where(off, a, jnp.zeros_like(a)), xb) for a, xb in zip(mats, xbs)]
        xs = [x - _dot(xb, y) for x, xb, y in zip(xs, xbs, ys)]
        blk *= 2
    return xs


def _delta_chunks(q_ref, k_ref, v_ref, z_ref, beta_ref, g_ref, og_ref, onorm, rows,
                  state_load, state_store, c):
    n = len(rows)
    ri = lax.broadcasted_iota(jnp.int32, (c, c), 0)
    ci = lax.broadcasted_iota(jnp.int32, (c, c), 1)
    causal = ri >= ci
    strict = ri > ci
    rep = V_HEADS // K_HEADS
    units = [(i, h) for i in range(n) for h in range(V_HEADS)]
    kunits = [(i, j) for i in range(n) for j in range(K_HEADS)]

    gcum = [_cumsum_rows(g_ref[rows[i], :], c) for i in range(n)]
    gcum_t = [x.T for x in gcum]
    egcum = [jnp.exp(x) for x in gcum]
    etail_t = [x[:, c - 1:c] - x for x in gcum_t]
    etail_t = [jnp.exp(x) for x in etail_t]
    beta = [beta_ref[rows[i], :] for i in range(n)]
    ks = {(i, j): k_ref[j, rows[i], :] for i, j in kunits}
    kts = {u: ks[u].T for u in kunits}
    kq = {(i, j): _dot(jnp.concatenate([ks[i, j], q_ref[j, rows[i], :]], axis=0), kts[i, j])
          for i, j in kunits}
    gcol = {(i, h): gcum[i][:, h:h + 1] for i, h in units}
    bcol = {(i, h): beta[i][:, h:h + 1] for i, h in units}
    egc = {(i, h): egcum[i][:, h:h + 1] for i, h in units}
    decay = {(i, h): jnp.where(
        causal, jnp.exp(jnp.minimum(gcol[i, h] - gcum_t[i][h:h + 1, :], 0.0)), 0.0)
        for i, h in units}
    a_mats = [jnp.where(strict, kq[i, h // rep][:c] * bcol[i, h] * decay[i, h], 0.0)
              for i, h in units]
    t_inv = dict(zip(units, _unit_lower_inverses(a_mats, c)))
    uw = {(i, h): _dot(t_inv[i, h], jnp.concatenate(
        [v_ref[h, rows[i], :] * bcol[i, h], ks[i, h // rep] * (bcol[i, h] * egc[i, h])], axis=1))
        for i, h in units}
    wq = {(i, h): jnp.concatenate(
        [uw[i, h][:, HEAD_DIM:], q_ref[h // rep, rows[i], :] * egc[i, h]], axis=0).astype(BF16)
        for i, h in units}
    qkd = {(i, h): (kq[i, h // rep][c:] * decay[i, h]).astype(BF16) for i, h in units}

    heads = range(V_HEADS)
    for i in range(n):
        s_old = [state_load(i, h) for h in heads]
        ws = [_dot(wq[i, h], s_old[h]) for h in heads]
        v_new = [uw[i, h][:, :HEAD_DIM] - ws[h][:c] for h in heads]
        o = [ws[h][c:] + _dot(qkd[i, h], v_new[h]) for h in heads]
        for h in heads:
            k_dec_t = kts[i, h // rep] * etail_t[i][h:h + 1, :]
            state_store(i, h, s_old[h] * egcum[i][c - 1:c, h:h + 1] + _dot(k_dec_t, v_new[h]))
        for h in heads:
            og_ref[rows[i], h * HEAD_DIM:(h + 1) * HEAD_DIM] = (
                _rms(o[h], onorm) * _silu(z_ref[h, rows[i], :])).astype(BF16)


def _cumsum_rows(g, c):
    ri = lax.broadcasted_iota(jnp.int32, (c, c), 0)
    ci = lax.broadcasted_iota(jnp.int32, (c, c), 1)
    tri = (ri >= ci).astype(F32)
    return jnp.dot(tri, g, preferred_element_type=F32, precision=lax.Precision.HIGHEST)


def _gdn_scan_prompt_kernel(q_ref, k_ref, v_ref, z_ref, beta_ref, g_ref, onorm_ref,
                            og_ref, s_ref, *, c, n_chunk):
    @pl.when(pl.program_id(1) == 0)
    def _():
        s_ref[...] = jnp.zeros(s_ref.shape, F32)

    def load(i, hh):
        return s_ref[0, hh]

    def store(i, hh, val):
        s_ref[0, hh] = val

    def body(it, carry):
        r0 = pl.multiple_of(it * (SCAN_GROUP * c), SCAN_GROUP * c)
        rows = [pl.ds(r0 + i * c, c) for i in range(SCAN_GROUP)]
        _delta_chunks(q_ref, k_ref, v_ref, z_ref, beta_ref, g_ref, og_ref, onorm_ref[...], rows,
                      load, store, c)
        return carry

    lax.fori_loop(0, n_chunk // SCAN_GROUP, body, 0)


def _gdn_scan_sample_kernel(q_ref, k_ref, v_ref, z_ref, beta_ref, g_ref, onorm_ref, s0_ref,
                            og_ref, s_ref, *, tb, c):
    def load(i, hh):
        return s0_ref[i, hh]

    def store(i, hh, val):
        s_ref[i, hh] = val

    rows = [pl.ds(i * c, c) for i in range(tb)]
    _delta_chunks(q_ref, k_ref, v_ref, z_ref, beta_ref, g_ref, og_ref, onorm_ref[...], rows,
                  load, store, c)


def _gdn_scan_prompt(q, k, v, z, beta, g, onorm, b, l):
    c = PROMPT_CHUNK
    tm = SCAN_TILE
    nc = l // tm
    tok = lambda w: pl.BlockSpec((tm, w), lambda i, j: (i * nc + j, 0))
    return pl.pallas_call(
        functools.partial(_gdn_scan_prompt_kernel, c=c, n_chunk=tm // c),
        grid=(b, nc),
        in_specs=_gdn_pre_out_specs(tm, lambda i, j: i * nc + j) + [_const_spec(onorm.shape)],
        out_specs=[tok(V_DIM),
                   pl.BlockSpec((1, V_HEADS, HEAD_DIM, HEAD_DIM), lambda i, j: (i, 0, 0, 0))],
        out_shape=[jax.ShapeDtypeStruct((b * l, V_DIM), BF16),
                   jax.ShapeDtypeStruct((b, V_HEADS, HEAD_DIM, HEAD_DIM), F32)],
        compiler_params=_params("arbitrary", "arbitrary"),
        name="gdn_scan_prompt",
    )(q, k, v, z, beta, g, onorm)


def _gdn_scan_sample(q, k, v, z, beta, g, onorm, s0, b, l):
    tb = SAMPLE_SCAN_BTILE
    tok = lambda w: pl.BlockSpec((tb * l, w), lambda i: (i, 0))
    st = pl.BlockSpec((tb, V_HEADS, HEAD_DIM, HEAD_DIM), lambda i: (i, 0, 0, 0))
    return pl.pallas_call(
        functools.partial(_gdn_scan_sample_kernel, tb=tb, c=l),
        grid=(b // tb,),
        in_specs=_gdn_pre_out_specs(tb * l, lambda i: i) + [_const_spec(onorm.shape), st],
        out_specs=[tok(V_DIM), st],
        out_shape=[jax.ShapeDtypeStruct((b * l, V_DIM), BF16),
                   jax.ShapeDtypeStruct((b, V_HEADS, HEAD_DIM, HEAD_DIM), F32)],
        compiler_params=_params("arbitrary"),
        name="gdn_scan_sample",
    )(q, k, v, z, beta, g, onorm, s0)


def _gdn_post_kernel(og_ref, x_ref, gains_ref, wo_ref, win_ref, wout_ref, y_ref):
    m = jnp.dot(og_ref[...], wo_ref[...], preferred_element_type=F32)
    y_ref[...] = _residual_ffn(x_ref[...], m, gains_ref[0:1, :], gains_ref[1:2, :],
                               gains_ref[2:3, :], win_ref, wout_ref)


def _gdn_post(og, x, gains, wo, win, wout, layer):
    n, d = x.shape
    tm = PROMPT_TILE
    return pl.pallas_call(
        _gdn_post_kernel,
        grid=(n // tm,),
        in_specs=[pl.BlockSpec((tm, V_DIM), lambda i: (i, 0)),
                  pl.BlockSpec((tm, d), lambda i: (i, 0))]
        + [_const_spec(gains.shape), _const_spec(wo.shape), _layer_spec(win, layer),
           _layer_spec(wout, layer)],
        out_specs=pl.BlockSpec((tm, d), lambda i: (i, 0)),
        out_shape=jax.ShapeDtypeStruct((n, d), F32),
        compiler_params=_params("arbitrary"),
        name="gdn_post",
    )(og, x, gains, wo, win, wout)


def _head_lanes(vec):
    return jnp.pad(vec.astype(F32), (0, LANES - V_HEADS)).reshape(1, LANES)


def kernel(x_prompt, x_sample, state_pool, state_gdn_conv, state_gdn_rec, norm_mix_pre,
           norm_mix_post, norm_ffn_pre, norm_ffn_post, pool_w, pool_scale, gdn_w_in,
           gdn_conv_w, gdn_a_log, gdn_dt_bias, gdn_o_norm, gdn_w_out, ffn_w_in, ffn_w_out):
    bp, lp, d = x_prompt.shape
    bs, ls, _ = x_sample.shape

    gains0 = jnp.stack([norm_mix_pre[0], norm_mix_post[0], norm_ffn_pre[0], norm_ffn_post[0]])
    gains1 = jnp.stack([norm_mix_post[1], norm_ffn_pre[1], norm_ffn_post[1]])
    gain1_pre = norm_mix_pre[1].reshape(1, d)
    pw = pool_w[0].astype(BF16)
    ps = pool_scale[0].reshape(1, d)
    win, wout = ffn_w_in.astype(BF16), ffn_w_out.astype(BF16)
    w_in = gdn_w_in[0]
    wqkvz = w_in.astype(BF16)
    w_b = w_in[:, CONV_DIM + V_DIM:CONV_DIM + V_DIM + V_HEADS]
    w_a = w_in[:, CONV_DIM + V_DIM + V_HEADS:]
    lane_pad = ((0, 0), (0, LANES - V_HEADS))
    wba = jnp.concatenate([jnp.pad(w_b, lane_pad), jnp.pad(w_a, lane_pad)], axis=1).astype(BF16)
    cw = gdn_conv_w[0]
    alog, dtb = _head_lanes(gdn_a_log[0]), _head_lanes(gdn_dt_bias[0])
    onorm = gdn_o_norm[0].reshape(1, HEAD_DIM)
    wo = gdn_w_out[0].astype(BF16)

    xp1, pool_p = _pool_layer_prompt(x_prompt, gains0, pw, ps, win, wout, 0)
    xs1, pool_s = _pool_layer_sample(x_sample, state_pool[0], gains0, pw, ps, win, wout, 0)

    qp, kp, vp, zp, betap, gp, conv_p = _gdn_pre_prompt(xp1, gain1_pre, wqkvz, wba, cw, alog, dtb)
    qs, ks, vs, zs, betas, gs, conv_s = _gdn_pre_sample(xs1, state_gdn_conv[0], gain1_pre, wqkvz,
                                                        wba, cw, alog, dtb)

    ogp, rec_p = _gdn_scan_prompt(qp, kp, vp, zp, betap, gp, onorm, bp, lp)
    ogs, rec_s = _gdn_scan_sample(qs, ks, vs, zs, betas, gs, onorm, state_gdn_rec[0], bs, ls)

    yp = _gdn_post(ogp, xp1.reshape(bp * lp, d), gains1, wo, win, wout, 1).reshape(bp, lp, d)
    ys = _gdn_post(ogs, xs1.reshape(bs * ls, d), gains1, wo, win, wout, 1).reshape(bs, ls, d)

    return (yp, ys, pool_p[None], pool_s[None], conv_p[None], conv_s[None], rec_p[None],
            rec_s[None])
```

```python
import functools

import jax
import jax.numpy as jnp
from jax import lax
from jax.experimental import pallas as pl
from jax.experimental.pallas import tpu as pltpu

D_MODEL = 1024
POOL_WINDOWS = (2, 4, 8, 16)
POOL_GROUP_DIM = D_MODEL // len(POOL_WINDOWS)
POOL_BUF = max(POOL_WINDOWS) - 1
K_HEADS = 8
V_HEADS = 16
HEAD_DIM = 128
QK_DIM = K_HEADS * HEAD_DIM
V_DIM = V_HEADS * HEAD_DIM
CONV_DIM = 2 * QK_DIM + V_DIM
CONV_WIDTH = 4
D_FF = 2816
EPS = 1e-6

F32 = jnp.float32
BF16 = jnp.bfloat16

SUBLANES = 8
LANES = 128
POOL_HALO = 16
CONV_HALO = SUBLANES
VMEM_LIMIT = 56 * 1024 * 1024
ROW_STRIDE = 4
ROW_GROUP = SUBLANES * ROW_STRIDE

PROMPT_TILE = 512
GDN_PRE_TILE = 256
PRE_SLABS = 8
SAMPLE_BTILE = 32
PROMPT_CHUNK = 64
SCAN_TILE = 256
SCAN_GROUP = 2
SAMPLE_SCAN_BTILE = 4


def _rms(x, gain):
    ms = jnp.mean(x * x, axis=-1, keepdims=True)
    return x * lax.rsqrt(ms + EPS) * gain


def _sigmoid(x):
    return 1.0 / (1.0 + jnp.exp(-x))


def _silu(x):
    return x * _sigmoid(x)


def _softplus(x):
    return jnp.maximum(x, 0.0) + jnp.log1p(jnp.exp(-jnp.abs(x)))


def _dot(a, b):
    return jnp.dot(a.astype(BF16), b.astype(BF16), preferred_element_type=F32)


def _const_spec(shape):
    nd = len(shape)
    return pl.BlockSpec(shape, lambda *_: (0,) * nd, pipeline_mode=pl.Buffered(1))


def _layer_spec(stacked, layer):
    nd = stacked.ndim - 1
    return pl.BlockSpec((None,) + stacked.shape[1:], lambda *_: (layer,) + (0,) * nd,
                        pipeline_mode=pl.Buffered(1))


def _params(*sem):
    return pltpu.CompilerParams(dimension_semantics=sem, vmem_limit_bytes=VMEM_LIMIT)


def _residual_ffn(x, m, g_post, g_fpre, g_fpost, win_ref, wout_ref):
    x1 = x + _rms(m, g_post)
    h = _rms(x1, g_fpre).astype(BF16)
    gate = jnp.dot(h, win_ref[:, :D_FF], preferred_element_type=F32)
    up = jnp.dot(h, win_ref[:, D_FF:], preferred_element_type=F32)
    act = (_silu(gate) * up).astype(BF16)
    f = jnp.dot(act, wout_ref[...], preferred_element_type=F32)
    return x1 + _rms(f, g_fpost)


def _pool_project(diffs, pw_ref, scale):
    parts = [_dot(d, pw_ref[gi]) for gi, d in enumerate(diffs)]
    return jnp.concatenate(parts, axis=-1) * scale


def _strided_rows(first):
    return pl.ds(first, SUBLANES, stride=ROW_STRIDE)


def _pool_layer_prompt_kernel(x_ref, gains_ref, pw_ref, ps_ref, win_ref, wout_ref,
                              y_ref, pool_ref, hp_ref, d_ref, x1_s, h2_s, *, tm, nj, n_tile):
    t = pl.program_id(0)
    wr = lax.rem(t, 2)
    rd = 1 - wr
    jp = lax.rem(jnp.minimum(t, n_tile - 1), nj)
    n_blk = D_MODEL // LANES
    blk_per_grp = POOL_GROUP_DIM // LANES

    @pl.when(t == 0)
    def _():
        x1_s[1] = jnp.zeros(x1_s.shape[1:], F32)
        h2_s[1] = jnp.zeros(h2_s.shape[1:], BF16)

    @pl.when(jp == 0)
    def _():
        hp_ref[:, 0:POOL_HALO, :] = jnp.zeros((n_blk, POOL_HALO, LANES), F32)

    h2 = h2_s[rd]
    gate = jnp.dot(h2, win_ref[:, :D_FF], preferred_element_type=F32)

    x = x_ref[0]
    h = _rms(x, gains_ref[0:1, :])
    for cb in range(n_blk):
        hp_ref[cb, POOL_HALO:POOL_HALO + tm, :] = h[:, cb * LANES:(cb + 1) * LANES]
    t_tile = lax.broadcasted_iota(jnp.int32, (SUBLANES, 1), 0) * ROW_STRIDE + (jp * tm + 1)
    for row0 in range(0, tm, ROW_GROUP):
        for r in range(ROW_STRIDE):
            for gi, win in enumerate(POOL_WINDOWS):
                inv = 1.0 / jnp.minimum(win, t_tile + (row0 + r)).astype(F32)
                for cb in range(gi * blk_per_grp, (gi + 1) * blk_per_grp):
                    cur = hp_ref[cb, _strided_rows(row0 + POOL_HALO + r), :]
                    tot = cur
                    for s in range(1, win):
                        tot = tot + hp_ref[cb, _strided_rows(row0 + POOL_HALO + r - s), :]
                    d_ref[cb, _strided_rows(row0 + r), :] = tot * inv - cur
    diffs = [jnp.concatenate([d_ref[cb] for cb in range(gi * blk_per_grp, (gi + 1) * blk_per_grp)],
                             axis=1) for gi in range(len(POOL_WINDOWS))]
    m = _pool_project(diffs, pw_ref, ps_ref[...])
    x1_new = x + _rms(m, gains_ref[1:2, :])
    x1_s[wr] = x1_new
    h2_s[wr] = _rms(x1_new, gains_ref[2:3, :]).astype(BF16)

    up = jnp.dot(h2, win_ref[:, D_FF:], preferred_element_type=F32)
    act = (_silu(gate) * up).astype(BF16)
    f = jnp.dot(act, wout_ref[...], preferred_element_type=F32)
    y_ref[0] = x1_s[rd] + _rms(f, gains_ref[3:4, :])

    @pl.when((jp == nj - 1) & (t < n_tile))
    def _():
        for cb in range(n_blk):
            pool_ref[0, :, cb * LANES:(cb + 1) * LANES] = hp_ref[
                cb, tm + POOL_HALO - POOL_BUF:tm + POOL_HALO, :]

    hp_ref[:, 0:POOL_HALO, :] = hp_ref[:, tm:tm + POOL_HALO, :]


def _pool_layer_sample_kernel(x_ref, buf_ref, gains_ref, pw_ref, ps_ref, win_ref, wout_ref,
                              y_ref, pool_ref, hp_ref, *, tb, seq, n_past):
    x = x_ref[...]
    h = _rms(x, gains_ref[0:1, :])
    hp_ref[:, POOL_HALO - POOL_BUF:POOL_HALO, :] = buf_ref[...]
    hp_ref[:, POOL_HALO:POOL_HALO + seq, :] = h
    t = lax.broadcasted_iota(jnp.int32, (1, seq, 1), 1)
    diffs = []
    for gi, win in enumerate(POOL_WINDOWS):
        c0, c1 = gi * POOL_GROUP_DIM, (gi + 1) * POOL_GROUP_DIM
        cur = hp_ref[:, POOL_HALO:POOL_HALO + seq, c0:c1]
        tot = cur
        for s in range(1, win):
            tot = tot + hp_ref[:, POOL_HALO - s:POOL_HALO - s + seq, c0:c1]
        cnt = jnp.minimum(win, t + 1 + n_past).astype(F32)
        diffs.append((tot / cnt - cur).reshape(tb * seq, POOL_GROUP_DIM))
    m = _pool_project(diffs, pw_ref, ps_ref[...])
    pool_ref[...] = hp_ref[:, POOL_HALO + seq - POOL_BUF:POOL_HALO + seq, :]
    y = _residual_ffn(x.reshape(tb * seq, D_MODEL), m, gains_ref[1:2, :], gains_ref[2:3, :],
                      gains_ref[3:4, :], win_ref, wout_ref)
    y_ref[...] = y.reshape(tb, seq, D_MODEL)


def _pool_layer_prompt(x, gains, pw, ps, win, wout, layer):
    b, l, d = x.shape
    tm = PROMPT_TILE
    nj = l // tm
    n_tile = b * nj
    mixed = lambda t: jnp.minimum(t, n_tile - 1)
    done = lambda t: jnp.maximum(t - 1, 0)
    return pl.pallas_call(
        functools.partial(_pool_layer_prompt_kernel, tm=tm, nj=nj, n_tile=n_tile),
        grid=(n_tile + 1,),
        in_specs=[
            pl.BlockSpec((1, tm, d), lambda t: (mixed(t) // nj, mixed(t) % nj, 0)),
            _const_spec(gains.shape), _const_spec(pw.shape), _const_spec(ps.shape),
            _layer_spec(win, layer), _layer_spec(wout, layer),
        ],
        out_specs=[
            pl.BlockSpec((1, tm, d), lambda t: (done(t) // nj, done(t) % nj, 0)),
            pl.BlockSpec((1, POOL_BUF, d), lambda t: (mixed(t) // nj, 0, 0)),
        ],
        out_shape=[jax.ShapeDtypeStruct((b, l, d), F32),
                   jax.ShapeDtypeStruct((b, POOL_BUF, d), F32)],
        scratch_shapes=[pltpu.VMEM((d // LANES, POOL_HALO + tm, LANES), F32),
                        pltpu.VMEM((d // LANES, tm, LANES), F32),
                        pltpu.VMEM((2, tm, d), F32), pltpu.VMEM((2, tm, d), BF16)],
        compiler_params=_params("arbitrary"),
        name="pool_layer_prompt",
    )(x, gains, pw, ps, win, wout)


def _pool_layer_sample(x, buf, gains, pw, ps, win, wout, layer):
    b, l, d = x.shape
    tb = SAMPLE_BTILE
    n_past = buf.shape[1]
    return pl.pallas_call(
        functools.partial(_pool_layer_sample_kernel, tb=tb, seq=l, n_past=n_past),
        grid=(b // tb,),
        in_specs=[
            pl.BlockSpec((tb, l, d), lambda i: (i, 0, 0)),
            pl.BlockSpec((tb, POOL_BUF, d), lambda i: (i, 0, 0)),
            _const_spec(gains.shape), _const_spec(pw.shape), _const_spec(ps.shape),
            _layer_spec(win, layer), _layer_spec(wout, layer),
        ],
        out_specs=[
            pl.BlockSpec((tb, l, d), lambda i: (i, 0, 0)),
            pl.BlockSpec((tb, POOL_BUF, d), lambda i: (i, 0, 0)),
        ],
        out_shape=[jax.ShapeDtypeStruct((b, l, d), F32),
                   jax.ShapeDtypeStruct((b, POOL_BUF, d), F32)],
        scratch_shapes=[pltpu.VMEM((tb, POOL_HALO + l, d), F32)],
        compiler_params=_params("arbitrary"),
        name="pool_layer_sample",
    )(x, buf, gains, pw, ps, win, wout)


def _gdn_qkv_slab(sl, conv, q_ref, k_ref, v_ref, rows):
    if sl >= 2 * K_HEADS:
        v_ref[sl - 2 * K_HEADS, rows, :] = conv
        return
    unit = conv * lax.rsqrt(jnp.sum(conv * conv, axis=-1, keepdims=True) + EPS)
    if sl < K_HEADS:
        q_ref[sl, rows, :] = unit * (HEAD_DIM ** -0.5)
    else:
        k_ref[sl - K_HEADS, rows, :] = unit


def _gdn_gates(ba, alog, dtb, beta_ref, g_ref):
    beta_ref[...] = _sigmoid(ba[:, :LANES])
    g_ref[...] = -jnp.exp(alog) * _softplus(ba[:, LANES:] + dtb)


def _gdn_pre_prompt_kernel(x_ref, gain_ref, wqkvz_ref, wba_ref, cw_ref, alog_ref, dtb_ref,
                           q_ref, k_ref, v_ref, z_ref, beta_ref, g_ref, conv_ref, up_ref, *, tm):
    j = pl.program_id(1)
    h = _rms(x_ref[0], gain_ref[...]).astype(BF16)
    n_slab = CONV_DIM // HEAD_DIM
    n_grp = n_slab // PRE_SLABS

    @pl.when(j == 0)
    def _():
        up_ref[:, 0:CONV_HALO, :] = jnp.zeros((n_slab, CONV_HALO, HEAD_DIM), F32)

    def project(grp):
        c0 = grp * PRE_SLABS * HEAD_DIM
        p = jnp.dot(h, wqkvz_ref[:, c0:c0 + PRE_SLABS * HEAD_DIM], preferred_element_type=F32)
        for t in range(PRE_SLABS):
            up_ref[grp * PRE_SLABS + t, CONV_HALO:CONV_HALO + tm, :] = (
                p[:, t * HEAD_DIM:(t + 1) * HEAD_DIM])

    def project_z(half):
        c0 = CONV_DIM + half * (V_DIM // 2)
        p = jnp.dot(h, wqkvz_ref[:, c0:c0 + V_DIM // 2], preferred_element_type=F32)
        for t in range(V_HEADS // 2):
            z_ref[half * (V_HEADS // 2) + t] = p[:, t * HEAD_DIM:(t + 1) * HEAD_DIM]

    base = CONV_HALO - (CONV_WIDTH - 1)

    def convolve(grp):
        for sl in range(grp * PRE_SLABS, (grp + 1) * PRE_SLABS):
            lanes = slice(sl * HEAD_DIM, (sl + 1) * HEAD_DIM)
            for row0 in range(0, tm, ROW_GROUP):
                for r in range(ROW_STRIDE):
                    acc = up_ref[sl, _strided_rows(row0 + base + r), :] * cw_ref[0:1, lanes]
                    for tap in range(1, CONV_WIDTH):
                        acc = acc + (up_ref[sl, _strided_rows(row0 + base + tap + r), :]
                                     * cw_ref[tap:tap + 1, lanes])
                    _gdn_qkv_slab(sl, _silu(acc), q_ref, k_ref, v_ref, _strided_rows(row0 + r))

    project(0)
    for grp in range(n_grp):
        if grp + 1 < n_grp:
            project(grp + 1)
        else:
            project_z(0)
        convolve(grp)
    project_z(1)
    ba = jnp.dot(h, wba_ref[...], preferred_element_type=F32)
    _gdn_gates(ba, alog_ref[...], dtb_ref[...], beta_ref, g_ref)

    @pl.when(j == pl.num_programs(1) - 1)
    def _():
        for sl in range(n_slab):
            conv_ref[0, :, sl * HEAD_DIM:(sl + 1) * HEAD_DIM] = up_ref[
                sl, tm + CONV_HALO - (CONV_WIDTH - 1):tm + CONV_HALO, :]

    up_ref[:, 0:CONV_HALO, :] = up_ref[:, tm:tm + CONV_HALO, :]


def _gdn_pre_sample_kernel(x_ref, buf_ref, gain_ref, wqkvz_ref, wba_ref, cw_ref, alog_ref,
                           dtb_ref, q_ref, k_ref, v_ref, z_ref, beta_ref, g_ref, conv_ref, up_ref,
                           *, tb, seq):
    m = tb * seq
    h = _rms(x_ref[...].reshape(m, D_MODEL), gain_ref[...]).astype(BF16)
    proj = jnp.dot(h, wqkvz_ref[:, :CONV_DIM + V_DIM], preferred_element_type=F32)
    ba = jnp.dot(h, wba_ref[...], preferred_element_type=F32)
    for hh in range(V_HEADS):
        z_ref[hh] = proj[:, CONV_DIM + hh * HEAD_DIM:CONV_DIM + (hh + 1) * HEAD_DIM]
    base = CONV_HALO - (CONV_WIDTH - 1)
    up_ref[:, base:CONV_HALO, :] = buf_ref[...]
    up_ref[:, CONV_HALO:CONV_HALO + seq, :] = proj[:, :CONV_DIM].reshape(tb, seq, CONV_DIM)
    acc = up_ref[:, base:base + seq, :] * cw_ref[0:1, :]
    for tap in range(1, CONV_WIDTH):
        acc = acc + up_ref[:, base + tap:base + tap + seq, :] * cw_ref[tap:tap + 1, :]
    conv_ref[...] = up_ref[:, CONV_HALO + seq - (CONV_WIDTH - 1):CONV_HALO + seq, :]
    _gdn_gates(ba, alog_ref[...], dtb_ref[...], beta_ref, g_ref)
    conv = _silu(acc).reshape(m, CONV_DIM)
    for sl in range(CONV_DIM // HEAD_DIM):
        _gdn_qkv_slab(sl, conv[:, sl * HEAD_DIM:(sl + 1) * HEAD_DIM], q_ref, k_ref, v_ref,
                      slice(None))


def _gdn_pre_out_shapes(n):
    return [jax.ShapeDtypeStruct((K_HEADS, n, HEAD_DIM), F32),
            jax.ShapeDtypeStruct((K_HEADS, n, HEAD_DIM), F32),
            jax.ShapeDtypeStruct((V_HEADS, n, HEAD_DIM), F32),
            jax.ShapeDtypeStruct((V_HEADS, n, HEAD_DIM), F32),
            jax.ShapeDtypeStruct((n, LANES), F32), jax.ShapeDtypeStruct((n, LANES), F32)]


def _gdn_pre_out_specs(rows, index):
    heads = lambda n: pl.BlockSpec((n, rows, HEAD_DIM), lambda *g: (0, index(*g), 0))
    lane = pl.BlockSpec((rows, LANES), lambda *g: (index(*g), 0))
    return [heads(K_HEADS), heads(K_HEADS), heads(V_HEADS), heads(V_HEADS), lane, lane]


def _gdn_pre_prompt(x, gain, wqkvz, wba, cw, alog, dtb):
    b, l, d = x.shape
    tm = GDN_PRE_TILE
    nj = l // tm
    return pl.pallas_call(
        functools.partial(_gdn_pre_prompt_kernel, tm=tm),
        grid=(b, nj),
        in_specs=[pl.BlockSpec((1, tm, d), lambda i, j: (i, j, 0))]
        + [_const_spec(a.shape) for a in (gain, wqkvz, wba, cw, alog, dtb)],
        out_specs=_gdn_pre_out_specs(tm, lambda i, j: i * nj + j)
        + [pl.BlockSpec((1, CONV_WIDTH - 1, CONV_DIM), lambda i, j: (i, 0, 0))],
        out_shape=_gdn_pre_out_shapes(b * l)
        + [jax.ShapeDtypeStruct((b, CONV_WIDTH - 1, CONV_DIM), F32)],
        scratch_shapes=[pltpu.VMEM((CONV_DIM // HEAD_DIM, CONV_HALO + tm, HEAD_DIM), F32)],
        compiler_params=_params("arbitrary", "arbitrary"),
        name="gdn_pre_prompt",
    )(x, gain, wqkvz, wba, cw, alog, dtb)


def _gdn_pre_sample(x, buf, gain, wqkvz, wba, cw, alog, dtb):
    b, l, d = x.shape
    tb = SAMPLE_BTILE
    return pl.pallas_call(
        functools.partial(_gdn_pre_sample_kernel, tb=tb, seq=l),
        grid=(b // tb,),
        in_specs=[pl.BlockSpec((tb, l, d), lambda i: (i, 0, 0)),
                  pl.BlockSpec((tb, CONV_WIDTH - 1, CONV_DIM), lambda i: (i, 0, 0))]
        + [_const_spec(a.shape) for a in (gain, wqkvz, wba, cw, alog, dtb)],
        out_specs=_gdn_pre_out_specs(tb * l, lambda i: i)
        + [pl.BlockSpec((tb, CONV_WIDTH - 1, CONV_DIM), lambda i: (i, 0, 0))],
        out_shape=_gdn_pre_out_shapes(b * l)
        + [jax.ShapeDtypeStruct((b, CONV_WIDTH - 1, CONV_DIM), F32)],
        scratch_shapes=[pltpu.VMEM((tb, CONV_HALO + l, CONV_DIM), F32)],
        compiler_params=_params("arbitrary"),
        name="gdn_pre_sample",
    )(x, buf, gain, wqkvz, wba, cw, alog, dtb)


def _unit_lower_inverses(mats, c):
    ri = lax.broadcasted_iota(jnp.int32, (c, c), 0)
    ci = lax.broadcasted_iota(jnp.int32, (c, c), 1)
    eye = (ri == ci).astype(F32)
    pair = ((ri // 2) == (ci // 2)) & (ri > ci)
    xs = [eye - jnp.where(pair, a, 0.0) for a in mats]
    mats = [a.astype(BF16) for a in mats]
    blk = 2
    while blk < c:
        off = ((ri // (2 * blk)) == (ci // (2 * blk))) & ((ri // blk) > (ci // blk))
        xbs = [x.astype(BF16) for x in xs]
        ys = [_dot(jnp.where(off, a, jnp.zeros_like(a)), xb) for a, xb in zip(mats, xbs)]
        xs = [x - _dot(xb, y) for x, xb, y in zip(xs, xbs, ys)]
        blk *= 2
    return xs


def _delta_chunks(q_ref, k_ref, v_ref, z_ref, beta_ref, g_ref, og_ref, onorm, rows,
                  state_load, state_store, c):
    n = len(rows)
    ri = lax.broadcasted_iota(jnp.int32, (c, c), 0)
    ci = lax.broadcasted_iota(jnp.int32, (c, c), 1)
    causal = ri >= ci
    strict = ri > ci
    rep = V_HEADS // K_HEADS
    units = [(i, h) for i in range(n) for h in range(V_HEADS)]
    kunits = [(i, j) for i in range(n) for j in range(K_HEADS)]

    gcum = [_cumsum_rows(g_ref[rows[i], :], c) for i in range(n)]
    gcum_t = [x.T for x in gcum]
    egcum = [jnp.exp(x) for x in gcum]
    etail_t = [x[:, c - 1:c] - x for x in gcum_t]
    etail_t = [jnp.exp(x) for x in etail_t]
    beta = [beta_ref[rows[i], :] for i in range(n)]
    ks = {(i, j): k_ref[j, rows[i], :] for i, j in kunits}
    kts = {u: ks[u].T for u in kunits}
    kq = {(i, j): _dot(jnp.concatenate([ks[i, j], q_ref[j, rows[i], :]], axis=0), kts[i, j])
          for i, j in kunits}
    gcol = {(i, h): gcum[i][:, h:h + 1] for i, h in units}
    bcol = {(i, h): beta[i][:, h:h + 1] for i, h in units}
    egc = {(i, h): egcum[i][:, h:h + 1] for i, h in units}
    decay = {(i, h): jnp.where(
        causal, jnp.exp(jnp.minimum(gcol[i, h] - gcum_t[i][h:h + 1, :], 0.0)), 0.0)
        for i, h in units}
    a_mats = [jnp.where(strict, kq[i, h // rep][:c] * bcol[i, h] * decay[i, h], 0.0)
              for i, h in units]
    t_inv = dict(zip(units, _unit_lower_inverses(a_mats, c)))
    uw = {(i, h): _dot(t_inv[i, h], jnp.concatenate(
        [v_ref[h, rows[i], :] * bcol[i, h], ks[i, h // rep] * (bcol[i, h] * egc[i, h])], axis=1))
        for i, h in units}
    wq = {(i, h): jnp.concatenate(
        [uw[i, h][:, HEAD_DIM:], q_ref[h // rep, rows[i], :] * egc[i, h]], axis=0).astype(BF16)
        for i, h in units}
    qkd = {(i, h): (kq[i, h // rep][c:] * decay[i, h]).astype(BF16) for i, h in units}

    heads = range(V_HEADS)
    for i in range(n):
        s_old = [state_load(i, h) for h in heads]
        ws = [_dot(wq[i, h], s_old[h]) for h in heads]
        v_new = [uw[i, h][:, :HEAD_DIM] - ws[h][:c] for h in heads]
        o = [ws[h][c:] + _dot(qkd[i, h], v_new[h]) for h in heads]
        for h in heads:
            k_dec_t = kts[i, h // rep] * etail_t[i][h:h + 1, :]
            state_store(i, h, s_old[h] * egcum[i][c - 1:c, h:h + 1] + _dot(k_dec_t, v_new[h]))
        for h in heads:
            og_ref[rows[i], h * HEAD_DIM:(h + 1) * HEAD_DIM] = (
                _rms(o[h], onorm) * _silu(z_ref[h, rows[i], :])).astype(BF16)


def _cumsum_rows(g, c):
    ri = lax.broadcasted_iota(jnp.int32, (c, c), 0)
    ci = lax.broadcasted_iota(jnp.int32, (c, c), 1)
    tri = (ri >= ci).astype(F32)
    return jnp.dot(tri, g, preferred_element_type=F32, precision=lax.Precision.HIGHEST)


def _gdn_scan_prompt_kernel(q_ref, k_ref, v_ref, z_ref, beta_ref, g_ref, onorm_ref,
                            og_ref, s_ref, *, c, n_chunk):
    @pl.when(pl.program_id(1) == 0)
    def _():
        s_ref[...] = jnp.zeros(s_ref.shape, F32)

    def load(i, hh):
        return s_ref[0, hh]

    def store(i, hh, val):
        s_ref[0, hh] = val

    def body(it, carry):
        r0 = pl.multiple_of(it * (SCAN_GROUP * c), SCAN_GROUP * c)
        rows = [pl.ds(r0 + i * c, c) for i in range(SCAN_GROUP)]
        _delta_chunks(q_ref, k_ref, v_ref, z_ref, beta_ref, g_ref, og_ref, onorm_ref[...], rows,
                      load, store, c)
        return carry

    lax.fori_loop(0, n_chunk // SCAN_GROUP, body, 0)


def _gdn_scan_sample_kernel(q_ref, k_ref, v_ref, z_ref, beta_ref, g_ref, onorm_ref, s0_ref,
                            og_ref, s_ref, *, tb, c):
    def load(i, hh):
        return s0_ref[i, hh]

    def store(i, hh, val):
        s_ref[i, hh] = val

    rows = [pl.ds(i * c, c) for i in range(tb)]
    _delta_chunks(q_ref, k_ref, v_ref, z_ref, beta_ref, g_ref, og_ref, onorm_ref[...], rows,
                  load, store, c)


def _gdn_scan_prompt(q, k, v, z, beta, g, onorm, b, l):
    c = PROMPT_CHUNK
    tm = SCAN_TILE
    nc = l // tm
    tok = lambda w: pl.BlockSpec((tm, w), lambda i, j: (i * nc + j, 0))
    return pl.pallas_call(
        functools.partial(_gdn_scan_prompt_kernel, c=c, n_chunk=tm // c),
        grid=(b, nc),
        in_specs=_gdn_pre_out_specs(tm, lambda i, j: i * nc + j) + [_const_spec(onorm.shape)],
        out_specs=[tok(V_DIM),
                   pl.BlockSpec((1, V_HEADS, HEAD_DIM, HEAD_DIM), lambda i, j: (i, 0, 0, 0))],
        out_shape=[jax.ShapeDtypeStruct((b * l, V_DIM), BF16),
                   jax.ShapeDtypeStruct((b, V_HEADS, HEAD_DIM, HEAD_DIM), F32)],
        compiler_params=_params("arbitrary", "arbitrary"),
        name="gdn_scan_prompt",
    )(q, k, v, z, beta, g, onorm)


def _gdn_scan_sample(q, k, v, z, beta, g, onorm, s0, b, l):
    tb = SAMPLE_SCAN_BTILE
    tok = lambda w: pl.BlockSpec((tb * l, w), lambda i: (i, 0))
    st = pl.BlockSpec((tb, V_HEADS, HEAD_DIM, HEAD_DIM), lambda i: (i, 0, 0, 0))
    return pl.pallas_call(
        functools.partial(_gdn_scan_sample_kernel, tb=tb, c=l),
        grid=(b // tb,),
        in_specs=_gdn_pre_out_specs(tb * l, lambda i: i) + [_const_spec(onorm.shape), st],
        out_specs=[tok(V_DIM), st],
        out_shape=[jax.ShapeDtypeStruct((b * l, V_DIM), BF16),
                   jax.ShapeDtypeStruct((b, V_HEADS, HEAD_DIM, HEAD_DIM), F32)],
        compiler_params=_params("arbitrary"),
        name="gdn_scan_sample",
    )(q, k, v, z, beta, g, onorm, s0)


def _gdn_post_kernel(og_ref, x_ref, gains_ref, wo_ref, win_ref, wout_ref, y_ref):
    m = jnp.dot(og_ref[...], wo_ref[...], preferred_element_type=F32)
    y_ref[...] = _residual_ffn(x_ref[...], m, gains_ref[0:1, :], gains_ref[1:2, :],
                               gains_ref[2:3, :], win_ref, wout_ref)


def _gdn_post(og, x, gains, wo, win, wout, layer):
    n, d = x.shape
    tm = PROMPT_TILE
    return pl.pallas_call(
        _gdn_post_kernel,
        grid=(n // tm,),
        in_specs=[pl.BlockSpec((tm, V_DIM), lambda i: (i, 0)),
                  pl.BlockSpec((tm, d), lambda i: (i, 0))]
        + [_const_spec(gains.shape), _const_spec(wo.shape), _layer_spec(win, layer),
           _layer_spec(wout, layer)],
        out_specs=pl.BlockSpec((tm, d), lambda i: (i, 0)),
        out_shape=jax.ShapeDtypeStruct((n, d), F32),
        compiler_params=_params("arbitrary"),
        name="gdn_post",
    )(og, x, gains, wo, win, wout)


def _head_lanes(vec):
    return jnp.pad(vec.astype(F32), (0, LANES - V_HEADS)).reshape(1, LANES)


def kernel(x_prompt, x_sample, state_pool, state_gdn_conv, state_gdn_rec, norm_mix_pre,
           norm_mix_post, norm_ffn_pre, norm_ffn_post, pool_w, pool_scale, gdn_w_in,
           gdn_conv_w, gdn_a_log, gdn_dt_bias, gdn_o_norm, gdn_w_out, ffn_w_in, ffn_w_out):
    bp, lp, d = x_prompt.shape
    bs, ls, _ = x_sample.shape

    gains0 = jnp.stack([norm_mix_pre[0], norm_mix_post[0], norm_ffn_pre[0], norm_ffn_post[0]])
    gains1 = jnp.stack([norm_mix_post[1], norm_ffn_pre[1], norm_ffn_post[1]])
    gain1_pre = norm_mix_pre[1].reshape(1, d)
    pw = pool_w[0].astype(BF16)
    ps = pool_scale[0].reshape(1, d)
    win, wout = ffn_w_in.astype(BF16), ffn_w_out.astype(BF16)
    w_in = gdn_w_in[0]
    wqkvz = w_in.astype(BF16)
    w_b = w_in[:, CONV_DIM + V_DIM:CONV_DIM + V_DIM + V_HEADS]
    w_a = w_in[:, CONV_DIM + V_DIM + V_HEADS:]
    lane_pad = ((0, 0), (0, LANES - V_HEADS))
    wba = jnp.concatenate([jnp.pad(w_b, lane_pad), jnp.pad(w_a, lane_pad)], axis=1).astype(BF16)
    cw = gdn_conv_w[0]
    alog, dtb = _head_lanes(gdn_a_log[0]), _head_lanes(gdn_dt_bias[0])
    onorm = gdn_o_norm[0].reshape(1, HEAD_DIM)
    wo = gdn_w_out[0].astype(BF16)

    xp1, pool_p = _pool_layer_prompt(x_prompt, gains0, pw, ps, win, wout, 0)
    xs1, pool_s = _pool_layer_sample(x_sample, state_pool[0], gains0, pw, ps, win, wout, 0)

    qp, kp, vp, zp, betap, gp, conv_p = _gdn_pre_prompt(xp1, gain1_pre, wqkvz, wba, cw, alog, dtb)
    qs, ks, vs, zs, betas, gs, conv_s = _gdn_pre_sample(xs1, state_gdn_conv[0], gain1_pre, wqkvz,
                                                        wba, cw, alog, dtb)

    ogp, rec_p = _gdn_scan_prompt(qp, kp, vp, zp, betap, gp, onorm, bp, lp)
    ogs, rec_s = _gdn_scan_sample(qs, ks, vs, zs, betas, gs, onorm, state_gdn_rec[0], bs, ls)

    yp = _gdn_post(ogp, xp1.reshape(bp * lp, d), gains1, wo, win, wout, 1).reshape(bp, lp, d)
    ys = _gdn_post(ogs, xs1.reshape(bs * ls, d), gains1, wo, win, wout, 1).reshape(bs, ls, d)

    return (yp, ys, pool_p[None], pool_s[None], conv_p[None], conv_s[None], rec_p[None],
            rec_s[None])
```

```python
import functools

import jax
import jax.numpy as jnp
from jax import lax
from jax.experimental import pallas as pl
from jax.experimental.pallas import tpu as pltpu

D_MODEL = 1024
POOL_WINDOWS = (2, 4, 8, 16)
POOL_GROUP_DIM = D_MODEL // len(POOL_WINDOWS)
POOL_BUF = max(POOL_WINDOWS) - 1
K_HEADS = 8
V_HEADS = 16
HEAD_DIM = 128
QK_DIM = K_HEADS * HEAD_DIM
V_DIM = V_HEADS * HEAD_DIM
CONV_DIM = 2 * QK_DIM + V_DIM
CONV_WIDTH = 4
D_FF = 2816
EPS = 1e-6

F32 = jnp.float32
BF16 = jnp.bfloat16

SUBLANES = 8
LANES = 128
POOL_HALO = 16
CONV_HALO = SUBLANES
VMEM_LIMIT = 56 * 1024 * 1024
ROW_STRIDE = 4
ROW_GROUP = SUBLANES * ROW_STRIDE

PROMPT_TILE = 512
GDN_PRE_TILE = 256
PRE_SLABS = 8
SAMPLE_BTILE = 32
PROMPT_CHUNK = 64
SCAN_TILE = 256
SCAN_GROUP = 2
SAMPLE_SCAN_BTILE = 4


def _rms(x, gain):
    ms = jnp.mean(x * x, axis=-1, keepdims=True)
    return x * lax.rsqrt(ms + EPS) * gain


def _sigmoid(x):
    return 1.0 / (1.0 + jnp.exp(-x))


def _silu(x):
    return x * _sigmoid(x)


def _softplus(x):
    return jnp.maximum(x, 0.0) + jnp.log1p(jnp.exp(-jnp.abs(x)))


def _dot(a, b):
    return jnp.dot(a.astype(BF16), b.astype(BF16), preferred_element_type=F32)


def _const_spec(shape):
    nd = len(shape)
    return pl.BlockSpec(shape, lambda *_: (0,) * nd, pipeline_mode=pl.Buffered(1))


def _layer_spec(stacked, layer):
    nd = stacked.ndim - 1
    return pl.BlockSpec((None,) + stacked.shape[1:], lambda *_: (layer,) + (0,) * nd,
                        pipeline_mode=pl.Buffered(1))


def _params(*sem):
    return pltpu.CompilerParams(dimension_semantics=sem, vmem_limit_bytes=VMEM_LIMIT)


def _residual_ffn(x, m, g_post, g_fpre, g_fpost, win_ref, wout_ref):
    x1 = x + _rms(m, g_post)
    h = _rms(x1, g_fpre).astype(BF16)
    gate = jnp.dot(h, win_ref[:, :D_FF], preferred_element_type=F32)
    up = jnp.dot(h, win_ref[:, D_FF:], preferred_element_type=F32)
    act = (_silu(gate) * up).astype(BF16)
    f = jnp.dot(act, wout_ref[...], preferred_element_type=F32)
    return x1 + _rms(f, g_fpost)


def _pool_project(diffs, pw_ref, scale):
    parts = [_dot(d, pw_ref[gi]) for gi, d in enumerate(diffs)]
    return jnp.concatenate(parts, axis=-1) * scale


def _strided_rows(first):
    return pl.ds(first, SUBLANES, stride=ROW_STRIDE)


def _pool_layer_prompt_kernel(x_ref, gains_ref, pw_ref, ps_ref, win_ref, wout_ref,
                              y_ref, pool_ref, hp_ref, d_ref, x1_s, h2_s, *, tm, nj, n_tile):
    t = pl.program_id(0)
    wr = lax.rem(t, 2)
    rd = 1 - wr
    jp = lax.rem(jnp.minimum(t, n_tile - 1), nj)
    n_blk = D_MODEL // LANES
    blk_per_grp = POOL_GROUP_DIM // LANES

    @pl.when(t == 0)
    def _():
        x1_s[1] = jnp.zeros(x1_s.shape[1:], F32)
        h2_s[1] = jnp.zeros(h2_s.shape[1:], BF16)

    @pl.when(jp == 0)
    def _():
        hp_ref[:, 0:POOL_HALO, :] = jnp.zeros((n_blk, POOL_HALO, LANES), F32)

    h2 = h2_s[rd]
    gate = jnp.dot(h2, win_ref[:, :D_FF], preferred_element_type=F32)

    x = x_ref[0]
    h = _rms(x, gains_ref[0:1, :])
    for cb in range(n_blk):
        hp_ref[cb, POOL_HALO:POOL_HALO + tm, :] = h[:, cb * LANES:(cb + 1) * LANES]
    t_tile = lax.broadcasted_iota(jnp.int32, (SUBLANES, 1), 0) * ROW_STRIDE + (jp * tm + 1)
    for row0 in range(0, tm, ROW_GROUP):
        for r in range(ROW_STRIDE):
            for gi, win in enumerate(POOL_WINDOWS):
                inv = 1.0 / jnp.minimum(win, t_tile + (row0 + r)).astype(F32)
                for cb in range(gi * blk_per_grp, (gi + 1) * blk_per_grp):
                    cur = hp_ref[cb, _strided_rows(row0 + POOL_HALO + r), :]
                    tot = cur
                    for s in range(1, win):
                        tot = tot + hp_ref[cb, _strided_rows(row0 + POOL_HALO + r - s), :]
                    d_ref[cb, _strided_rows(row0 + r), :] = tot * inv - cur
    diffs = [jnp.concatenate([d_ref[cb] for cb in range(gi * blk_per_grp, (gi + 1) * blk_per_grp)],
                             axis=1) for gi in range(len(POOL_WINDOWS))]
    m = _pool_project(diffs, pw_ref, ps_ref[...])
    x1_new = x + _rms(m, gains_ref[1:2, :])
    x1_s[wr] = x1_new
    h2_s[wr] = _rms(x1_new, gains_ref[2:3, :]).astype(BF16)

    up = jnp.dot(h2, win_ref[:, D_FF:], preferred_element_type=F32)
    act = (_silu(gate) * up).astype(BF16)
    f = jnp.dot(act, wout_ref[...], preferred_element_type=F32)
    y_ref[0] = x1_s[rd] + _rms(f, gains_ref[3:4, :])

    @pl.when((jp == nj - 1) & (t < n_tile))
    def _():
        for cb in range(n_blk):
            pool_ref[0, :, cb * LANES:(cb + 1) * LANES] = hp_ref[
                cb, tm + POOL_HALO - POOL_BUF:tm + POOL_HALO, :]

    hp_ref[:, 0:POOL_HALO, :] = hp_ref[:, tm:tm + POOL_HALO, :]


def _pool_layer_sample_kernel(x_ref, buf_ref, gains_ref, pw_ref, ps_ref, win_ref, wout_ref,
                              y_ref, pool_ref, hp_ref, *, tb, seq, n_past):
    x = x_ref[...]
    h = _rms(x, gains_ref[0:1, :])
    hp_ref[:, POOL_HALO - POOL_BUF:POOL_HALO, :] = buf_ref[...]
    hp_ref[:, POOL_HALO:POOL_HALO + seq, :] = h
    t = lax.broadcasted_iota(jnp.int32, (1, seq, 1), 1)
    diffs = []
    for gi, win in enumerate(POOL_WINDOWS):
        c0, c1 = gi * POOL_GROUP_DIM, (gi + 1) * POOL_GROUP_DIM
        cur = hp_ref[:, POOL_HALO:POOL_HALO + seq, c0:c1]
        tot = cur
        for s in range(1, win):
            tot = tot + hp_ref[:, POOL_HALO - s:POOL_HALO - s + seq, c0:c1]
        cnt = jnp.minimum(win, t + 1 + n_past).astype(F32)
        diffs.append((tot / cnt - cur).reshape(tb * seq, POOL_GROUP_DIM))
    m = _pool_project(diffs, pw_ref, ps_ref[...])
    pool_ref[...] = hp_ref[:, POOL_HALO + seq - POOL_BUF:POOL_HALO + seq, :]
    y = _residual_ffn(x.reshape(tb * seq, D_MODEL), m, gains_ref[1:2, :], gains_ref[2:3, :],
                      gains_ref[3:4, :], win_ref, wout_ref)
    y_ref[...] = y.reshape(tb, seq, D_MODEL)


def _pool_layer_prompt(x, gains, pw, ps, win, wout, layer):
    b, l, d = x.shape
    tm = PROMPT_TILE
    nj = l // tm
    n_tile = b * nj
    mixed = lambda t: jnp.minimum(t, n_tile - 1)
    done = lambda t: jnp.maximum(t - 1, 0)
    return pl.pallas_call(
        functools.partial(_pool_layer_prompt_kernel, tm=tm, nj=nj, n_tile=n_tile),
        grid=(n_tile + 1,),
        in_specs=[
            pl.BlockSpec((1, tm, d), lambda t: (mixed(t) // nj, mixed(t) % nj, 0)),
            _const_spec(gains.shape), _const_spec(pw.shape), _const_spec(ps.shape),
            _layer_spec(win, layer), _layer_spec(wout, layer),
        ],
        out_specs=[
            pl.BlockSpec((1, tm, d), lambda t: (done(t) // nj, done(t) % nj, 0)),
            pl.BlockSpec((1, POOL_BUF, d), lambda t: (mixed(t) // nj, 0, 0)),
        ],
        out_shape=[jax.ShapeDtypeStruct((b, l, d), F32),
                   jax.ShapeDtypeStruct((b, POOL_BUF, d), F32)],
        scratch_shapes=[pltpu.VMEM((d // LANES, POOL_HALO + tm, LANES), F32),
                        pltpu.VMEM((d // LANES, tm, LANES), F32),
                        pltpu.VMEM((2, tm, d), F32), pltpu.VMEM((2, tm, d), BF16)],
        compiler_params=_params("arbitrary"),
        name="pool_layer_prompt",
    )(x, gains, pw, ps, win, wout)


def _pool_layer_sample(x, buf, gains, pw, ps, win, wout, layer):
    b, l, d = x.shape
    tb = SAMPLE_BTILE
    n_past = buf.shape[1]
    return pl.pallas_call(
        functools.partial(_pool_layer_sample_kernel, tb=tb, seq=l, n_past=n_past),
        grid=(b // tb,),
        in_specs=[
            pl.BlockSpec((tb, l, d), lambda i: (i, 0, 0)),
            pl.BlockSpec((tb, POOL_BUF, d), lambda i: (i, 0, 0)),
            _const_spec(gains.shape), _const_spec(pw.shape), _const_spec(ps.shape),
            _layer_spec(win, layer), _layer_spec(wout, layer),
        ],
        out_specs=[
            pl.BlockSpec((tb, l, d), lambda i: (i, 0, 0)),
            pl.BlockSpec((tb, POOL_BUF, d), lambda i: (i, 0, 0)),
        ],
        out_shape=[jax.ShapeDtypeStruct((b, l, d), F32),
                   jax.ShapeDtypeStruct((b, POOL_BUF, d), F32)],
        scratch_shapes=[pltpu.VMEM((tb, POOL_HALO + l, d), F32)],
        compiler_params=_params("arbitrary"),
        name="pool_layer_sample",
    )(x, buf, gains, pw, ps, win, wout)


def _gdn_qkv_slab(sl, conv, q_ref, k_ref, v_ref, rows):
    if sl >= 2 * K_HEADS:
        v_ref[sl - 2 * K_HEADS, rows, :] = conv
        return
    unit = conv * lax.rsqrt(jnp.sum(conv * conv, axis=-1, keepdims=True) + EPS)
    if sl < K_HEADS:
        q_ref[sl, rows, :] = unit * (HEAD_DIM ** -0.5)
    else:
        k_ref[sl - K_HEADS, rows, :] = unit


def _gdn_gates(ba, alog, dtb, beta_ref, g_ref):
    beta_ref[...] = _sigmoid(ba[:, :LANES])
    g_ref[...] = -jnp.exp(alog) * _softplus(ba[:, LANES:] + dtb)


def _gdn_pre_prompt_kernel(x_ref, gain_ref, wqkvz_ref, wba_ref, cw_ref, alog_ref, dtb_ref,
                           q_ref, k_ref, v_ref, z_ref, beta_ref, g_ref, conv_ref, up_ref, *, tm):
    j = pl.program_id(1)
    h = _rms(x_ref[0], gain_ref[...]).astype(BF16)
    n_slab = CONV_DIM // HEAD_DIM
    n_grp = n_slab // PRE_SLABS

    @pl.when(j == 0)
    def _():
        up_ref[:, 0:CONV_HALO, :] = jnp.zeros((n_slab, CONV_HALO, HEAD_DIM), F32)

    def project(grp):
        c0 = grp * PRE_SLABS * HEAD_DIM
        p = jnp.dot(h, wqkvz_ref[:, c0:c0 + PRE_SLABS * HEAD_DIM], preferred_element_type=F32)
        for t in range(PRE_SLABS):
            up_ref[grp * PRE_SLABS + t, CONV_HALO:CONV_HALO + tm, :] = (
                p[:, t * HEAD_DIM:(t + 1) * HEAD_DIM])

    def project_z(half):
        c0 = CONV_DIM + half * (V_DIM // 2)
        p = jnp.dot(h, wqkvz_ref[:, c0:c0 + V_DIM // 2], preferred_element_type=F32)
        for t in range(V_HEADS // 2):
            z_ref[half * (V_HEADS // 2) + t] = p[:, t * HEAD_DIM:(t + 1) * HEAD_DIM].astype(
                z_ref.dtype)

    base = CONV_HALO - (CONV_WIDTH - 1)

    def convolve(grp):
        for sl in range(grp * PRE_SLABS, (grp + 1) * PRE_SLABS):
            lanes = slice(sl * HEAD_DIM, (sl + 1) * HEAD_DIM)
            for row0 in range(0, tm, ROW_GROUP):
                for r in range(ROW_STRIDE):
                    acc = up_ref[sl, _strided_rows(row0 + base + r), :] * cw_ref[0:1, lanes]
                    for tap in range(1, CONV_WIDTH):
                        acc = acc + (up_ref[sl, _strided_rows(row0 + base + tap + r), :]
                                     * cw_ref[tap:tap + 1, lanes])
                    _gdn_qkv_slab(sl, _silu(acc), q_ref, k_ref, v_ref, _strided_rows(row0 + r))

    project(0)
    for grp in range(n_grp):
        if grp + 1 < n_grp:
            project(grp + 1)
        else:
            project_z(0)
        convolve(grp)
    project_z(1)
    ba = jnp.dot(h, wba_ref[...], preferred_element_type=F32)
    _gdn_gates(ba, alog_ref[...], dtb_ref[...], beta_ref, g_ref)

    @pl.when(j == pl.num_programs(1) - 1)
    def _():
        for sl in range(n_slab):
            conv_ref[0, :, sl * HEAD_DIM:(sl + 1) * HEAD_DIM] = up_ref[
                sl, tm + CONV_HALO - (CONV_WIDTH - 1):tm + CONV_HALO, :]

    up_ref[:, 0:CONV_HALO, :] = up_ref[:, tm:tm + CONV_HALO, :]


def _gdn_pre_sample_kernel(x_ref, buf_ref, gain_ref, wqkvz_ref, wba_ref, cw_ref, alog_ref,
                           dtb_ref, q_ref, k_ref, v_ref, z_ref, beta_ref, g_ref, conv_ref, up_ref,
                           *, tb, seq):
    m = tb * seq
    h = _rms(x_ref[...].reshape(m, D_MODEL), gain_ref[...]).astype(BF16)
    proj = jnp.dot(h, wqkvz_ref[:, :CONV_DIM + V_DIM], preferred_element_type=F32)
    ba = jnp.dot(h, wba_ref[...], preferred_element_type=F32)
    for hh in range(V_HEADS):
        z_ref[hh] = proj[:, CONV_DIM + hh * HEAD_DIM:CONV_DIM + (hh + 1) * HEAD_DIM]
    base = CONV_HALO - (CONV_WIDTH - 1)
    up_ref[:, base:CONV_HALO, :] = buf_ref[...]
    up_ref[:, CONV_HALO:CONV_HALO + seq, :] = proj[:, :CONV_DIM].reshape(tb, seq, CONV_DIM)
    acc = up_ref[:, base:base + seq, :] * cw_ref[0:1, :]
    for tap in range(1, CONV_WIDTH):
        acc = acc + up_ref[:, base + tap:base + tap + seq, :] * cw_ref[tap:tap + 1, :]
    conv_ref[...] = up_ref[:, CONV_HALO + seq - (CONV_WIDTH - 1):CONV_HALO + seq, :]
    _gdn_gates(ba, alog_ref[...], dtb_ref[...], beta_ref, g_ref)
    conv = _silu(acc).reshape(m, CONV_DIM)
    for sl in range(CONV_DIM // HEAD_DIM):
        _gdn_qkv_slab(sl, conv[:, sl * HEAD_DIM:(sl + 1) * HEAD_DIM], q_ref, k_ref, v_ref,
                      slice(None))


def _gdn_pre_out_shapes(n, z_dtype=F32):
    return [jax.ShapeDtypeStruct((K_HEADS, n, HEAD_DIM), F32),
            jax.ShapeDtypeStruct((K_HEADS, n, HEAD_DIM), F32),
            jax.ShapeDtypeStruct((V_HEADS, n, HEAD_DIM), F32),
            jax.ShapeDtypeStruct((V_HEADS, n, HEAD_DIM), z_dtype),
            jax.ShapeDtypeStruct((n, LANES), F32), jax.ShapeDtypeStruct((n, LANES), F32)]


def _gdn_pre_out_specs(rows, index):
    heads = lambda n: pl.BlockSpec((n, rows, HEAD_DIM), lambda *g: (0, index(*g), 0))
    lane = pl.BlockSpec((rows, LANES), lambda *g: (index(*g), 0))
    return [heads(K_HEADS), heads(K_HEADS), heads(V_HEADS), heads(V_HEADS), lane, lane]


def _gdn_pre_prompt(x, gain, wqkvz, wba, cw, alog, dtb):
    b, l, d = x.shape
    tm = GDN_PRE_TILE
    nj = l // tm
    return pl.pallas_call(
        functools.partial(_gdn_pre_prompt_kernel, tm=tm),
        grid=(b, nj),
        in_specs=[pl.BlockSpec((1, tm, d), lambda i, j: (i, j, 0))]
        + [_const_spec(a.shape) for a in (gain, wqkvz, wba, cw, alog, dtb)],
        out_specs=_gdn_pre_out_specs(tm, lambda i, j: i * nj + j)
        + [pl.BlockSpec((1, CONV_WIDTH - 1, CONV_DIM), lambda i, j: (i, 0, 0))],
        out_shape=_gdn_pre_out_shapes(b * l, BF16)
        + [jax.ShapeDtypeStruct((b, CONV_WIDTH - 1, CONV_DIM), F32)],
        scratch_shapes=[pltpu.VMEM((CONV_DIM // HEAD_DIM, CONV_HALO + tm, HEAD_DIM), F32)],
        compiler_params=_params("arbitrary", "arbitrary"),
        name="gdn_pre_prompt",
    )(x, gain, wqkvz, wba, cw, alog, dtb)


def _gdn_pre_sample(x, buf, gain, wqkvz, wba, cw, alog, dtb):
    b, l, d = x.shape
    tb = SAMPLE_BTILE
    return pl.pallas_call(
        functools.partial(_gdn_pre_sample_kernel, tb=tb, seq=l),
        grid=(b // tb,),
        in_specs=[pl.BlockSpec((tb, l, d), lambda i: (i, 0, 0)),
                  pl.BlockSpec((tb, CONV_WIDTH - 1, CONV_DIM), lambda i: (i, 0, 0))]
        + [_const_spec(a.shape) for a in (gain, wqkvz, wba, cw, alog, dtb)],
        out_specs=_gdn_pre_out_specs(tb * l, lambda i: i)
        + [pl.BlockSpec((tb, CONV_WIDTH - 1, CONV_DIM), lambda i: (i, 0, 0))],
        out_shape=_gdn_pre_out_shapes(b * l)
        + [jax.ShapeDtypeStruct((b, CONV_WIDTH - 1, CONV_DIM), F32)],
        scratch_shapes=[pltpu.VMEM((tb, CONV_HALO + l, CONV_DIM), F32)],
        compiler_params=_params("arbitrary"),
        name="gdn_pre_sample",
    )(x, buf, gain, wqkvz, wba, cw, alog, dtb)


def _unit_lower_inverses(mats, c):
    ri = lax.broadcasted_iota(jnp.int32, (c, c), 0)
    ci = lax.broadcasted_iota(jnp.int32, (c, c), 1)
    eye = (ri == ci).astype(F32)
    pair = ((ri // 2) == (ci // 2)) & (ri > ci)
    xs = [eye - jnp.where(pair, a, 0.0) for a in mats]
    mats = [a.astype(BF16) for a in mats]
    blk = 2
    while blk < c:
        off = ((ri // (2 * blk)) == (ci // (2 * blk))) & ((ri // blk) > (ci // blk))
        xbs = [x.astype(BF16) for x in xs]
        ys = [_dot(jnp.where(off, a, jnp.zeros_like(a)), xb) for a, xb in zip(mats, xbs)]
        xs = [x - _dot(xb, y) for x, xb, y in zip(xs, xbs, ys)]
        blk *= 2
    return xs


def _delta_chunks(q_ref, k_ref, v_ref, z_ref, beta_ref, g_ref, og_ref, onorm, rows,
                  state_load, state_store, c):
    n = len(rows)
    ri = lax.broadcasted_iota(jnp.int32, (c, c), 0)
    ci = lax.broadcasted_iota(jnp.int32, (c, c), 1)
    causal = ri >= ci
    strict = ri > ci
    rep = V_HEADS // K_HEADS
    units = [(i, h) for i in range(n) for h in range(V_HEADS)]
    kunits = [(i, j) for i in range(n) for j in range(K_HEADS)]

    gcum = [_cumsum_rows(g_ref[rows[i], :], c) for i in range(n)]
    gcum_t = [x.T for x in gcum]
    egcum = [jnp.exp(x) for x in gcum]
    etail_t = [x[:, c - 1:c] - x for x in gcum_t]
    etail_t = [jnp.exp(x) for x in etail_t]
    beta = [beta_ref[rows[i], :] for i in range(n)]
    ks = {(i, j): k_ref[j, rows[i], :] for i, j in kunits}
    kts = {u: ks[u].T for u in kunits}
    kq = {(i, j): _dot(jnp.concatenate([ks[i, j], q_ref[j, rows[i], :]], axis=0), kts[i, j])
          for i, j in kunits}
    gcol = {(i, h): gcum[i][:, h:h + 1] for i, h in units}
    bcol = {(i, h): beta[i][:, h:h + 1] for i, h in units}
    egc = {(i, h): egcum[i][:, h:h + 1] for i, h in units}
    decay = {(i, h): jnp.where(
        causal, jnp.exp(jnp.minimum(gcol[i, h] - gcum_t[i][h:h + 1, :], 0.0)), 0.0)
        for i, h in units}
    a_mats = [jnp.where(strict, kq[i, h // rep][:c] * bcol[i, h] * decay[i, h], 0.0)
              for i, h in units]
    t_inv = dict(zip(units, _unit_lower_inverses(a_mats, c)))
    uw = {(i, h): _dot(t_inv[i, h], jnp.concatenate(
        [v_ref[h, rows[i], :] * bcol[i, h], ks[i, h // rep] * (bcol[i, h] * egc[i, h])], axis=1))
        for i, h in units}
    wq = {(i, h): jnp.concatenate(
        [uw[i, h][:, HEAD_DIM:], q_ref[h // rep, rows[i], :] * egc[i, h]], axis=0).astype(BF16)
        for i, h in units}
    qkd = {(i, h): (kq[i, h // rep][c:] * decay[i, h]).astype(BF16) for i, h in units}

    heads = range(V_HEADS)
    for i in range(n):
        s_old = [state_load(i, h) for h in heads]
        ws = [_dot(wq[i, h], s_old[h]) for h in heads]
        v_new = [uw[i, h][:, :HEAD_DIM] - ws[h][:c] for h in heads]
        o = [ws[h][c:] + _dot(qkd[i, h], v_new[h]) for h in heads]
        for h in heads:
            k_dec_t = kts[i, h // rep] * etail_t[i][h:h + 1, :]
            state_store(i, h, s_old[h] * egcum[i][c - 1:c, h:h + 1] + _dot(k_dec_t, v_new[h]))
        for h in heads:
            og_ref[rows[i], h * HEAD_DIM:(h + 1) * HEAD_DIM] = (
                _rms(o[h], onorm) * _silu(z_ref[h, rows[i], :].astype(F32))).astype(BF16)


def _cumsum_rows(g, c):
    ri = lax.broadcasted_iota(jnp.int32, (c, c), 0)
    ci = lax.broadcasted_iota(jnp.int32, (c, c), 1)
    tri = (ri >= ci).astype(F32)
    return jnp.dot(tri, g, preferred_element_type=F32, precision=lax.Precision.HIGHEST)


def _gdn_scan_prompt_kernel(q_ref, k_ref, v_ref, z_ref, beta_ref, g_ref, onorm_ref,
                            og_ref, s_ref, *, c, n_chunk):
    @pl.when(pl.program_id(1) == 0)
    def _():
        s_ref[...] = jnp.zeros(s_ref.shape, F32)

    def load(i, hh):
        return s_ref[0, hh]

    def store(i, hh, val):
        s_ref[0, hh] = val

    def body(it, carry):
        r0 = pl.multiple_of(it * (SCAN_GROUP * c), SCAN_GROUP * c)
        rows = [pl.ds(r0 + i * c, c) for i in range(SCAN_GROUP)]
        _delta_chunks(q_ref, k_ref, v_ref, z_ref, beta_ref, g_ref, og_ref, onorm_ref[...], rows,
                      load, store, c)
        return carry

    lax.fori_loop(0, n_chunk // SCAN_GROUP, body, 0)


def _gdn_scan_sample_kernel(q_ref, k_ref, v_ref, z_ref, beta_ref, g_ref, onorm_ref, s0_ref,
                            og_ref, s_ref, *, tb, c):
    def load(i, hh):
        return s0_ref[i, hh]

    def store(i, hh, val):
        s_ref[i, hh] = val

    rows = [pl.ds(i * c, c) for i in range(tb)]
    _delta_chunks(q_ref, k_ref, v_ref, z_ref, beta_ref, g_ref, og_ref, onorm_ref[...], rows,
                  load, store, c)


def _gdn_scan_prompt(q, k, v, z, beta, g, onorm, b, l):
    c = PROMPT_CHUNK
    tm = SCAN_TILE
    nc = l // tm
    tok = lambda w: pl.BlockSpec((tm, w), lambda i, j: (i * nc + j, 0))
    return pl.pallas_call(
        functools.partial(_gdn_scan_prompt_kernel, c=c, n_chunk=tm // c),
        grid=(b, nc),
        in_specs=_gdn_pre_out_specs(tm, lambda i, j: i * nc + j) + [_const_spec(onorm.shape)],
        out_specs=[tok(V_DIM),
                   pl.BlockSpec((1, V_HEADS, HEAD_DIM, HEAD_DIM), lambda i, j: (i, 0, 0, 0))],
        out_shape=[jax.ShapeDtypeStruct((b * l, V_DIM), BF16),
                   jax.ShapeDtypeStruct((b, V_HEADS, HEAD_DIM, HEAD_DIM), F32)],
        compiler_params=_params("arbitrary", "arbitrary"),
        name="gdn_scan_prompt",
    )(q, k, v, z, beta, g, onorm)


def _gdn_scan_sample(q, k, v, z, beta, g, onorm, s0, b, l):
    tb = SAMPLE_SCAN_BTILE
    tok = lambda w: pl.BlockSpec((tb * l, w), lambda i: (i, 0))
    st = pl.BlockSpec((tb, V_HEADS, HEAD_DIM, HEAD_DIM), lambda i: (i, 0, 0, 0))
    return pl.pallas_call(
        functools.partial(_gdn_scan_sample_kernel, tb=tb, c=l),
        grid=(b // tb,),
        in_specs=_gdn_pre_out_specs(tb * l, lambda i: i) + [_const_spec(onorm.shape), st],
        out_specs=[tok(V_DIM), st],
        out_shape=[jax.ShapeDtypeStruct((b * l, V_DIM), BF16),
                   jax.ShapeDtypeStruct((b, V_HEADS, HEAD_DIM, HEAD_DIM), F32)],
        compiler_params=_params("arbitrary"),
        name="gdn_scan_sample",
    )(q, k, v, z, beta, g, onorm, s0)


def _gdn_post_kernel(og_ref, x_ref, gains_ref, wo_ref, win_ref, wout_ref, y_ref):
    m = jnp.dot(og_ref[...], wo_ref[...], preferred_element_type=F32)
    y_ref[...] = _residual_ffn(x_ref[...], m, gains_ref[0:1, :], gains_ref[1:2, :],
                               gains_ref[2:3, :], win_ref, wout_ref)


def _gdn_post(og, x, gains, wo, win, wout, layer):
    n, d = x.shape
    tm = PROMPT_TILE
    return pl.pallas_call(
        _gdn_post_kernel,
        grid=(n // tm,),
        in_specs=[pl.BlockSpec((tm, V_DIM), lambda i: (i, 0)),
                  pl.BlockSpec((tm, d), lambda i: (i, 0))]
        + [_const_spec(gains.shape), _const_spec(wo.shape), _layer_spec(win, layer),
           _layer_spec(wout, layer)],
        out_specs=pl.BlockSpec((tm, d), lambda i: (i, 0)),
        out_shape=jax.ShapeDtypeStruct((n, d), F32),
        compiler_params=_params("arbitrary"),
        name="gdn_post",
    )(og, x, gains, wo, win, wout)


def _head_lanes(vec):
    return jnp.pad(vec.astype(F32), (0, LANES - V_HEADS)).reshape(1, LANES)


def kernel(x_prompt, x_sample, state_pool, state_gdn_conv, state_gdn_rec, norm_mix_pre,
           norm_mix_post, norm_ffn_pre, norm_ffn_post, pool_w, pool_scale, gdn_w_in,
           gdn_conv_w, gdn_a_log, gdn_dt_bias, gdn_o_norm, gdn_w_out, ffn_w_in, ffn_w_out):
    bp, lp, d = x_prompt.shape
    bs, ls, _ = x_sample.shape

    gains0 = jnp.stack([norm_mix_pre[0], norm_mix_post[0], norm_ffn_pre[0], norm_ffn_post[0]])
    gains1 = jnp.stack([norm_mix_post[1], norm_ffn_pre[1], norm_ffn_post[1]])
    gain1_pre = norm_mix_pre[1].reshape(1, d)
    pw = pool_w[0].astype(BF16)
    ps = pool_scale[0].reshape(1, d)
    win, wout = ffn_w_in.astype(BF16), ffn_w_out.astype(BF16)
    w_in = gdn_w_in[0]
    wqkvz = w_in.astype(BF16)
    w_b = w_in[:, CONV_DIM + V_DIM:CONV_DIM + V_DIM + V_HEADS]
    w_a = w_in[:, CONV_DIM + V_DIM + V_HEADS:]
    lane_pad = ((0, 0), (0, LANES - V_HEADS))
    wba = jnp.concatenate([jnp.pad(w_b, lane_pad), jnp.pad(w_a, lane_pad)], axis=1).astype(BF16)
    cw = gdn_conv_w[0]
    alog, dtb = _head_lanes(gdn_a_log[0]), _head_lanes(gdn_dt_bias[0])
    onorm = gdn_o_norm[0].reshape(1, HEAD_DIM)
    wo = gdn_w_out[0].astype(BF16)

    xp1, pool_p = _pool_layer_prompt(x_prompt, gains0, pw, ps, win, wout, 0)
    xs1, pool_s = _pool_layer_sample(x_sample, state_pool[0], gains0, pw, ps, win, wout, 0)

    qp, kp, vp, zp, betap, gp, conv_p = _gdn_pre_prompt(xp1, gain1_pre, wqkvz, wba, cw, alog, dtb)
    qs, ks, vs, zs, betas, gs, conv_s = _gdn_pre_sample(xs1, state_gdn_conv[0], gain1_pre, wqkvz,
                                                        wba, cw, alog, dtb)

    ogp, rec_p = _gdn_scan_prompt(qp, kp, vp, zp, betap, gp, onorm, bp, lp)
    ogs, rec_s = _gdn_scan_sample(qs, ks, vs, zs, betas, gs, onorm, state_gdn_rec[0], bs, ls)

    yp = _gdn_post(ogp, xp1.reshape(bp * lp, d), gains1, wo, win, wout, 1).reshape(bp, lp, d)
    ys = _gdn_post(ogs, xs1.reshape(bs * ls, d), gains1, wo, win, wout, 1).reshape(bs, ls, d)

    return (yp, ys, pool_p[None], pool_s[None], conv_p[None], conv_s[None], rec_p[None],
            rec_s[None])
```

```python
import functools

import jax
import jax.numpy as jnp
from jax import lax
from jax.experimental import pallas as pl
from jax.experimental.pallas import tpu as pltpu

D_MODEL = 1024
POOL_WINDOWS = (2, 4, 8, 16)
POOL_GROUP_DIM = D_MODEL // len(POOL_WINDOWS)
POOL_BUF = max(POOL_WINDOWS) - 1
K_HEADS = 8
V_HEADS = 16
HEAD_DIM = 128
QK_DIM = K_HEADS * HEAD_DIM
V_DIM = V_HEADS * HEAD_DIM
CONV_DIM = 2 * QK_DIM + V_DIM
CONV_WIDTH = 4
D_FF = 2816
EPS = 1e-6

F32 = jnp.float32
BF16 = jnp.bfloat16

SUBLANES = 8
LANES = 128
POOL_HALO = 16
CONV_HALO = SUBLANES
VMEM_LIMIT = 56 * 1024 * 1024
ROW_STRIDE = 4
ROW_GROUP = SUBLANES * ROW_STRIDE

PROMPT_TILE = 512
GDN_PRE_TILE = 256
PRE_SLABS = 8
SAMPLE_BTILE = 32
PROMPT_CHUNK = 64
SCAN_TILE = 256
SCAN_GROUP = 2
SAMPLE_SCAN_BTILE = 4


def _rms(x, gain):
    ms = jnp.mean(x * x, axis=-1, keepdims=True)
    return x * lax.rsqrt(ms + EPS) * gain


def _sigmoid(x):
    return 1.0 / (1.0 + jnp.exp(-x))


def _silu(x):
    return x * _sigmoid(x)


def _softplus(x):
    return jnp.maximum(x, 0.0) + jnp.log1p(jnp.exp(-jnp.abs(x)))


def _dot(a, b):
    return jnp.dot(a.astype(BF16), b.astype(BF16), preferred_element_type=F32)


def _const_spec(shape):
    nd = len(shape)
    return pl.BlockSpec(shape, lambda *_: (0,) * nd, pipeline_mode=pl.Buffered(1))


def _layer_spec(stacked, layer):
    nd = stacked.ndim - 1
    return pl.BlockSpec((None,) + stacked.shape[1:], lambda *_: (layer,) + (0,) * nd,
                        pipeline_mode=pl.Buffered(1))


def _params(*sem):
    return pltpu.CompilerParams(dimension_semantics=sem, vmem_limit_bytes=VMEM_LIMIT)


def _residual_ffn(x, m, g_post, g_fpre, g_fpost, win_ref, wout_ref):
    x1 = x + _rms(m, g_post)
    h = _rms(x1, g_fpre).astype(BF16)
    gate = jnp.dot(h, win_ref[:, :D_FF], preferred_element_type=F32)
    up = jnp.dot(h, win_ref[:, D_FF:], preferred_element_type=F32)
    act = (_silu(gate) * up).astype(BF16)
    f = jnp.dot(act, wout_ref[...], preferred_element_type=F32)
    return x1 + _rms(f, g_fpost)


def _pool_project(diffs, pw_ref, scale):
    parts = [_dot(d, pw_ref[gi]) for gi, d in enumerate(diffs)]
    return jnp.concatenate(parts, axis=-1) * scale


def _strided_rows(first):
    return pl.ds(first, SUBLANES, stride=ROW_STRIDE)


def _pool_layer_prompt_kernel(x_ref, gains_ref, pw_ref, ps_ref, win_ref, wout_ref,
                              y_ref, pool_ref, hp_ref, d_ref, x1_s, h2_s, *, tm, nj, n_tile):
    t = pl.program_id(0)
    wr = lax.rem(t, 2)
    rd = 1 - wr
    jp = lax.rem(jnp.minimum(t, n_tile - 1), nj)
    n_blk = D_MODEL // LANES
    blk_per_grp = POOL_GROUP_DIM // LANES

    @pl.when(t == 0)
    def _():
        x1_s[1] = jnp.zeros(x1_s.shape[1:], F32)
        h2_s[1] = jnp.zeros(h2_s.shape[1:], BF16)

    @pl.when(jp == 0)
    def _():
        hp_ref[:, 0:POOL_HALO, :] = jnp.zeros((n_blk, POOL_HALO, LANES), F32)

    h2 = h2_s[rd]
    gate = jnp.dot(h2, win_ref[:, :D_FF], preferred_element_type=F32)

    x = x_ref[0]
    h = _rms(x, gains_ref[0:1, :])
    for cb in range(n_blk):
        hp_ref[cb, POOL_HALO:POOL_HALO + tm, :] = h[:, cb * LANES:(cb + 1) * LANES]
    t_tile = lax.broadcasted_iota(jnp.int32, (SUBLANES, 1), 0) * ROW_STRIDE + (jp * tm + 1)
    for row0 in range(0, tm, ROW_GROUP):
        for r in range(ROW_STRIDE):
            for gi, win in enumerate(POOL_WINDOWS):
                inv = 1.0 / jnp.minimum(win, t_tile + (row0 + r)).astype(F32)
                for cb in range(gi * blk_per_grp, (gi + 1) * blk_per_grp):
                    cur = hp_ref[cb, _strided_rows(row0 + POOL_HALO + r), :]
                    tot = cur
                    for s in range(1, win):
                        tot = tot + hp_ref[cb, _strided_rows(row0 + POOL_HALO + r - s), :]
                    d_ref[cb, _strided_rows(row0 + r), :] = tot * inv - cur
    diffs = [jnp.concatenate([d_ref[cb] for cb in range(gi * blk_per_grp, (gi + 1) * blk_per_grp)],
                             axis=1) for gi in range(len(POOL_WINDOWS))]
    m = _pool_project(diffs, pw_ref, ps_ref[...])
    x1_new = x + _rms(m, gains_ref[1:2, :])
    x1_s[wr] = x1_new
    h2_s[wr] = _rms(x1_new, gains_ref[2:3, :]).astype(BF16)

    up = jnp.dot(h2, win_ref[:, D_FF:], preferred_element_type=F32)
    act = (_silu(gate) * up).astype(BF16)
    f = jnp.dot(act, wout_ref[...], preferred_element_type=F32)
    y_ref[0] = x1_s[rd] + _rms(f, gains_ref[3:4, :])

    @pl.when((jp == nj - 1) & (t < n_tile))
    def _():
        for cb in range(n_blk):
            pool_ref[0, :, cb * LANES:(cb + 1) * LANES] = hp_ref[
                cb, tm + POOL_HALO - POOL_BUF:tm + POOL_HALO, :]

    hp_ref[:, 0:POOL_HALO, :] = hp_ref[:, tm:tm + POOL_HALO, :]


def _pool_layer_sample_kernel(x_ref, buf_ref, gains_ref, pw_ref, ps_ref, win_ref, wout_ref,
                              y_ref, pool_ref, hs_ref, d_ref, *, tb, seq, n_past):
    m_rows = tb * seq
    x = x_ref[...].reshape(m_rows, D_MODEL)
    h = _rms(x, gains_ref[0:1, :])
    n_blk = D_MODEL // LANES
    blk_per_grp = POOL_GROUP_DIM // LANES
    at_time = lambda t: pl.ds(t, tb, stride=seq)
    for cb in range(n_blk):
        lanes = slice(cb * LANES, (cb + 1) * LANES)
        hs_ref[cb] = h[:, lanes]
        win = POOL_WINDOWS[cb // blk_per_grp]
        hist = [buf_ref[s, :, lanes] for s in range(n_past)]
        hist += [hs_ref[cb, at_time(t), :] for t in range(seq)]
        for t in range(seq):
            tot = hist[n_past + t]
            for s in range(1, win):
                tot = tot + hist[n_past + t - s]
            d_ref[cb, at_time(t), :] = tot * (1.0 / min(win, t + 1 + n_past)) - hist[n_past + t]
        for s in range(n_past):
            pool_ref[s, :, lanes] = hist[seq + s]
    diffs = [jnp.concatenate([d_ref[cb] for cb in range(gi * blk_per_grp, (gi + 1) * blk_per_grp)],
                             axis=1) for gi in range(len(POOL_WINDOWS))]
    m = _pool_project(diffs, pw_ref, ps_ref[...])
    y = _residual_ffn(x, m, gains_ref[1:2, :], gains_ref[2:3, :], gains_ref[3:4, :],
                      win_ref, wout_ref)
    y_ref[...] = y.reshape(tb, seq, D_MODEL)


def _pool_layer_prompt(x, gains, pw, ps, win, wout, layer):
    b, l, d = x.shape
    tm = PROMPT_TILE
    nj = l // tm
    n_tile = b * nj
    mixed = lambda t: jnp.minimum(t, n_tile - 1)
    done = lambda t: jnp.maximum(t - 1, 0)
    return pl.pallas_call(
        functools.partial(_pool_layer_prompt_kernel, tm=tm, nj=nj, n_tile=n_tile),
        grid=(n_tile + 1,),
        in_specs=[
            pl.BlockSpec((1, tm, d), lambda t: (mixed(t) // nj, mixed(t) % nj, 0)),
            _const_spec(gains.shape), _const_spec(pw.shape), _const_spec(ps.shape),
            _layer_spec(win, layer), _layer_spec(wout, layer),
        ],
        out_specs=[
            pl.BlockSpec((1, tm, d), lambda t: (done(t) // nj, done(t) % nj, 0)),
            pl.BlockSpec((1, POOL_BUF, d), lambda t: (mixed(t) // nj, 0, 0)),
        ],
        out_shape=[jax.ShapeDtypeStruct((b, l, d), F32),
                   jax.ShapeDtypeStruct((b, POOL_BUF, d), F32)],
        scratch_shapes=[pltpu.VMEM((d // LANES, POOL_HALO + tm, LANES), F32),
                        pltpu.VMEM((d // LANES, tm, LANES), F32),
                        pltpu.VMEM((2, tm, d), F32), pltpu.VMEM((2, tm, d), BF16)],
        compiler_params=_params("arbitrary"),
        name="pool_layer_prompt",
    )(x, gains, pw, ps, win, wout)


def _pool_layer_sample(x, buf, gains, pw, ps, win, wout, layer):
    b, l, d = x.shape
    tb = SAMPLE_BTILE
    n_past = buf.shape[0]
    assert n_past == POOL_BUF
    return pl.pallas_call(
        functools.partial(_pool_layer_sample_kernel, tb=tb, seq=l, n_past=n_past),
        grid=(b // tb,),
        in_specs=[
            pl.BlockSpec((tb, l, d), lambda i: (i, 0, 0)),
            pl.BlockSpec((POOL_BUF, tb, d), lambda i: (0, i, 0)),
            _const_spec(gains.shape), _const_spec(pw.shape), _const_spec(ps.shape),
            _layer_spec(win, layer), _layer_spec(wout, layer),
        ],
        out_specs=[
            pl.BlockSpec((tb, l, d), lambda i: (i, 0, 0)),
            pl.BlockSpec((POOL_BUF, tb, d), lambda i: (0, i, 0)),
        ],
        out_shape=[jax.ShapeDtypeStruct((b, l, d), F32),
                   jax.ShapeDtypeStruct((POOL_BUF, b, d), F32)],
        scratch_shapes=[pltpu.VMEM((d // LANES, tb * l, LANES), F32),
                        pltpu.VMEM((d // LANES, tb * l, LANES), F32)],
        compiler_params=_params("arbitrary"),
        name="pool_layer_sample",
    )(x, buf, gains, pw, ps, win, wout)


def _gdn_qkv_slab(sl, conv, q_ref, k_ref, v_ref, rows):
    if sl >= 2 * K_HEADS:
        v_ref[sl - 2 * K_HEADS, rows, :] = conv
        return
    unit = conv * lax.rsqrt(jnp.sum(conv * conv, axis=-1, keepdims=True) + EPS)
    if sl < K_HEADS:
        q_ref[sl, rows, :] = unit * (HEAD_DIM ** -0.5)
    else:
        k_ref[sl - K_HEADS, rows, :] = unit


def _gdn_gates(ba, alog, dtb, beta_ref, g_ref):
    beta_ref[...] = _sigmoid(ba[:, :LANES])
    g_ref[...] = -jnp.exp(alog) * _softplus(ba[:, LANES:] + dtb)


def _gdn_pre_prompt_kernel(x_ref, gain_ref, wqkvz_ref, wba_ref, cw_ref, alog_ref, dtb_ref,
                           q_ref, k_ref, v_ref, z_ref, beta_ref, g_ref, conv_ref, up_ref, *, tm):
    j = pl.program_id(1)
    h = _rms(x_ref[0], gain_ref[...]).astype(BF16)
    n_slab = CONV_DIM // HEAD_DIM
    n_grp = n_slab // PRE_SLABS

    @pl.when(j == 0)
    def _():
        up_ref[:, 0:CONV_HALO, :] = jnp.zeros((n_slab, CONV_HALO, HEAD_DIM), F32)

    def project(grp):
        c0 = grp * PRE_SLABS * HEAD_DIM
        p = jnp.dot(h, wqkvz_ref[:, c0:c0 + PRE_SLABS * HEAD_DIM], preferred_element_type=F32)
        for t in range(PRE_SLABS):
            up_ref[grp * PRE_SLABS + t, CONV_HALO:CONV_HALO + tm, :] = (
                p[:, t * HEAD_DIM:(t + 1) * HEAD_DIM])

    def project_z(half):
        c0 = CONV_DIM + half * (V_DIM // 2)
        p = jnp.dot(h, wqkvz_ref[:, c0:c0 + V_DIM // 2], preferred_element_type=F32)
        for t in range(V_HEADS // 2):
            z_ref[half * (V_HEADS // 2) + t] = p[:, t * HEAD_DIM:(t + 1) * HEAD_DIM]

    base = CONV_HALO - (CONV_WIDTH - 1)

    def convolve(grp):
        for sl in range(grp * PRE_SLABS, (grp + 1) * PRE_SLABS):
            lanes = slice(sl * HEAD_DIM, (sl + 1) * HEAD_DIM)
            for row0 in range(0, tm, ROW_GROUP):
                for r in range(ROW_STRIDE):
                    acc = up_ref[sl, _strided_rows(row0 + base + r), :] * cw_ref[0:1, lanes]
                    for tap in range(1, CONV_WIDTH):
                        acc = acc + (up_ref[sl, _strided_rows(row0 + base + tap + r), :]
                                     * cw_ref[tap:tap + 1, lanes])
                    _gdn_qkv_slab(sl, _silu(acc), q_ref, k_ref, v_ref, _strided_rows(row0 + r))

    project(0)
    for grp in range(n_grp):
        if grp + 1 < n_grp:
            project(grp + 1)
        else:
            project_z(0)
        convolve(grp)
    project_z(1)
    ba = jnp.dot(h, wba_ref[...], preferred_element_type=F32)
    _gdn_gates(ba, alog_ref[...], dtb_ref[...], beta_ref, g_ref)

    @pl.when(j == pl.num_programs(1) - 1)
    def _():
        for sl in range(n_slab):
            conv_ref[0, :, sl * HEAD_DIM:(sl + 1) * HEAD_DIM] = up_ref[
                sl, tm + CONV_HALO - (CONV_WIDTH - 1):tm + CONV_HALO, :]

    up_ref[:, 0:CONV_HALO, :] = up_ref[:, tm:tm + CONV_HALO, :]


def _gdn_pre_sample_kernel(x_ref, buf_ref, gain_ref, wqkvz_ref, wba_ref, cw_ref, alog_ref,
                           dtb_ref, q_ref, k_ref, v_ref, z_ref, beta_ref, g_ref, conv_ref, up_ref,
                           *, tb, seq):
    m = tb * seq
    n_buf = CONV_WIDTH - 1
    h = _rms(x_ref[...].reshape(m, D_MODEL), gain_ref[...]).astype(BF16)
    proj = jnp.dot(h, wqkvz_ref[:, :CONV_DIM + V_DIM], preferred_element_type=F32)
    ba = jnp.dot(h, wba_ref[...], preferred_element_type=F32)
    for hh in range(V_HEADS):
        z_ref[hh] = proj[:, CONV_DIM + hh * HEAD_DIM:CONV_DIM + (hh + 1) * HEAD_DIM]
    _gdn_gates(ba, alog_ref[...], dtb_ref[...], beta_ref, g_ref)
    at_time = lambda t: pl.ds(t, tb, stride=seq)
    for sl in range(CONV_DIM // HEAD_DIM):
        lanes = slice(sl * HEAD_DIM, (sl + 1) * HEAD_DIM)
        up_ref[sl] = proj[:, lanes]
        ups = [buf_ref[s, :, lanes] for s in range(n_buf)]
        ups += [up_ref[sl, at_time(t), :] for t in range(seq)]
        for t in range(seq):
            acc = ups[t] * cw_ref[0:1, lanes]
            for tap in range(1, CONV_WIDTH):
                acc = acc + ups[t + tap] * cw_ref[tap:tap + 1, lanes]
            _gdn_qkv_slab(sl, _silu(acc), q_ref, k_ref, v_ref, at_time(t))
        for s in range(n_buf):
            conv_ref[s, :, lanes] = ups[seq + s]


def _gdn_pre_out_shapes(n):
    return [jax.ShapeDtypeStruct((K_HEADS, n, HEAD_DIM), F32),
            jax.ShapeDtypeStruct((K_HEADS, n, HEAD_DIM), F32),
            jax.ShapeDtypeStruct((V_HEADS, n, HEAD_DIM), F32),
            jax.ShapeDtypeStruct((V_HEADS, n, HEAD_DIM), F32),
            jax.ShapeDtypeStruct((n, LANES), F32), jax.ShapeDtypeStruct((n, LANES), F32)]


def _gdn_pre_out_specs(rows, index):
    heads = lambda n: pl.BlockSpec((n, rows, HEAD_DIM), lambda *g: (0, index(*g), 0))
    lane = pl.BlockSpec((rows, LANES), lambda *g: (index(*g), 0))
    return [heads(K_HEADS), heads(K_HEADS), heads(V_HEADS), heads(V_HEADS), lane, lane]


def _gdn_pre_prompt(x, gain, wqkvz, wba, cw, alog, dtb):
    b, l, d = x.shape
    tm = GDN_PRE_TILE
    nj = l // tm
    return pl.pallas_call(
        functools.partial(_gdn_pre_prompt_kernel, tm=tm),
        grid=(b, nj),
        in_specs=[pl.BlockSpec((1, tm, d), lambda i, j: (i, j, 0))]
        + [_const_spec(a.shape) for a in (gain, wqkvz, wba, cw, alog, dtb)],
        out_specs=_gdn_pre_out_specs(tm, lambda i, j: i * nj + j)
        + [pl.BlockSpec((1, CONV_WIDTH - 1, CONV_DIM), lambda i, j: (i, 0, 0))],
        out_shape=_gdn_pre_out_shapes(b * l)
        + [jax.ShapeDtypeStruct((b, CONV_WIDTH - 1, CONV_DIM), F32)],
        scratch_shapes=[pltpu.VMEM((CONV_DIM // HEAD_DIM, CONV_HALO + tm, HEAD_DIM), F32)],
        compiler_params=_params("arbitrary", "arbitrary"),
        name="gdn_pre_prompt",
    )(x, gain, wqkvz, wba, cw, alog, dtb)


def _gdn_pre_sample(x, buf, gain, wqkvz, wba, cw, alog, dtb):
    b, l, d = x.shape
    tb = SAMPLE_BTILE
    state = pl.BlockSpec((CONV_WIDTH - 1, tb, CONV_DIM), lambda i: (0, i, 0))
    return pl.pallas_call(
        functools.partial(_gdn_pre_sample_kernel, tb=tb, seq=l),
        grid=(b // tb,),
        in_specs=[pl.BlockSpec((tb, l, d), lambda i: (i, 0, 0)), state]
        + [_const_spec(a.shape) for a in (gain, wqkvz, wba, cw, alog, dtb)],
        out_specs=_gdn_pre_out_specs(tb * l, lambda i: i) + [state],
        out_shape=_gdn_pre_out_shapes(b * l)
        + [jax.ShapeDtypeStruct((CONV_WIDTH - 1, b, CONV_DIM), F32)],
        scratch_shapes=[pltpu.VMEM((CONV_DIM // HEAD_DIM, tb * l, HEAD_DIM), F32)],
        compiler_params=_params("arbitrary"),
        name="gdn_pre_sample",
    )(x, buf, gain, wqkvz, wba, cw, alog, dtb)


def _unit_lower_inverses(mats, c):
    ri = lax.broadcasted_iota(jnp.int32, (c, c), 0)
    ci = lax.broadcasted_iota(jnp.int32, (c, c), 1)
    eye = (ri == ci).astype(F32)
    pair = ((ri // 2) == (ci // 2)) & (ri > ci)
    xs = [eye - jnp.where(pair, a, 0.0) for a in mats]
    mats = [a.astype(BF16) for a in mats]
    blk = 2
    while blk < c:
        off = ((ri // (2 * blk)) == (ci // (2 * blk))) & ((ri // blk) > (ci // blk))
        xbs = [x.astype(BF16) for x in xs]
        ys = [_dot(jnp.where(off, a, jnp.zeros_like(a)), xb) for a, xb in zip(mats, xbs)]
        xs = [x - _dot(xb, y) for x, xb, y in zip(xs, xbs, ys)]
        blk *= 2
    return xs


def _delta_chunks(q_ref, k_ref, v_ref, z_ref, beta_ref, g_ref, og_ref, onorm, rows,
                  state_load, state_store, c):
    n = len(rows)
    ri = lax.broadcasted_iota(jnp.int32, (c, c), 0)
    ci = lax.broadcasted_iota(jnp.int32, (c, c), 1)
    causal = ri >= ci
    strict = ri > ci
    rep = V_HEADS // K_HEADS
    units = [(i, h) for i in range(n) for h in range(V_HEADS)]
    kunits = [(i, j) for i in range(n) for j in range(K_HEADS)]

    gcum = [_cumsum_rows(g_ref[rows[i], :], c) for i in range(n)]
    gcum_t = [x.T for x in gcum]
    egcum = [jnp.exp(x) for x in gcum]
    etail_t = [x[:, c - 1:c] - x for x in gcum_t]
    etail_t = [jnp.exp(x) for x in etail_t]
    beta = [beta_ref[rows[i], :] for i in range(n)]
    ks = {(i, j): k_ref[j, rows[i], :] for i, j in kunits}
    kts = {u: ks[u].T for u in kunits}
    kq = {(i, j): _dot(jnp.concatenate([ks[i, j], q_ref[j, rows[i], :]], axis=0), kts[i, j])
          for i, j in kunits}
    gcol = {(i, h): gcum[i][:, h:h + 1] for i, h in units}
    bcol = {(i, h): beta[i][:, h:h + 1] for i, h in units}
    egc = {(i, h): egcum[i][:, h:h + 1] for i, h in units}
    decay = {(i, h): jnp.where(
        causal, jnp.exp(jnp.minimum(gcol[i, h] - gcum_t[i][h:h + 1, :], 0.0)), 0.0)
        for i, h in units}
    a_mats = [jnp.where(strict, kq[i, h // rep][:c] * bcol[i, h] * decay[i, h], 0.0)
              for i, h in units]
    t_inv = dict(zip(units, _unit_lower_inverses(a_mats, c)))
    uw = {(i, h): _dot(t_inv[i, h], jnp.concatenate(
        [v_ref[h, rows[i], :] * bcol[i, h], ks[i, h // rep] * (bcol[i, h] * egc[i, h])], axis=1))
        for i, h in units}
    wq = {(i, h): jnp.concatenate(
        [uw[i, h][:, HEAD_DIM:], q_ref[h // rep, rows[i], :] * egc[i, h]], axis=0).astype(BF16)
        for i, h in units}
    qkd = {(i, h): (kq[i, h // rep][c:] * decay[i, h]).astype(BF16) for i, h in units}

    heads = range(V_HEADS)
    for i in range(n):
        s_old = [state_load(i, h) for h in heads]
        ws = [_dot(wq[i, h], s_old[h]) for h in heads]
        v_new = [uw[i, h][:, :HEAD_DIM] - ws[h][:c] for h in heads]
        o = [ws[h][c:] + _dot(qkd[i, h], v_new[h]) for h in heads]
        for h in heads:
            k_dec_t = kts[i, h // rep] * etail_t[i][h:h + 1, :]
            state_store(i, h, s_old[h] * egcum[i][c - 1:c, h:h + 1] + _dot(k_dec_t, v_new[h]))
        for h in heads:
            og_ref[rows[i], h * HEAD_DIM:(h + 1) * HEAD_DIM] = (
                _rms(o[h], onorm) * _silu(z_ref[h, rows[i], :])).astype(BF16)


def _cumsum_rows(g, c):
    ri = lax.broadcasted_iota(jnp.int32, (c, c), 0)
    ci = lax.broadcasted_iota(jnp.int32, (c, c), 1)
    tri = (ri >= ci).astype(F32)
    return jnp.dot(tri, g, preferred_element_type=F32, precision=lax.Precision.HIGHEST)


def _gdn_scan_prompt_kernel(q_ref, k_ref, v_ref, z_ref, beta_ref, g_ref, onorm_ref,
                            og_ref, s_ref, *, c, n_chunk):
    @pl.when(pl.program_id(1) == 0)
    def _():
        s_ref[...] = jnp.zeros(s_ref.shape, F32)

    def load(i, hh):
        return s_ref[0, hh]

    def store(i, hh, val):
        s_ref[0, hh] = val

    def body(it, carry):
        r0 = pl.multiple_of(it * (SCAN_GROUP * c), SCAN_GROUP * c)
        rows = [pl.ds(r0 + i * c, c) for i in range(SCAN_GROUP)]
        _delta_chunks(q_ref, k_ref, v_ref, z_ref, beta_ref, g_ref, og_ref, onorm_ref[...], rows,
                      load, store, c)
        return carry

    lax.fori_loop(0, n_chunk // SCAN_GROUP, body, 0)


def _gdn_scan_sample_kernel(q_ref, k_ref, v_ref, z_ref, beta_ref, g_ref, onorm_ref, s0_ref,
                            og_ref, s_ref, *, tb, c):
    def load(i, hh):
        return s0_ref[i, hh]

    def store(i, hh, val):
        s_ref[i, hh] = val

    rows = [pl.ds(i * c, c) for i in range(tb)]
    _delta_chunks(q_ref, k_ref, v_ref, z_ref, beta_ref, g_ref, og_ref, onorm_ref[...], rows,
                  load, store, c)


def _gdn_scan_prompt(q, k, v, z, beta, g, onorm, b, l):
    c = PROMPT_CHUNK
    tm = SCAN_TILE
    nc = l // tm
    tok = lambda w: pl.BlockSpec((tm, w), lambda i, j: (i * nc + j, 0))
    return pl.pallas_call(
        functools.partial(_gdn_scan_prompt_kernel, c=c, n_chunk=tm // c),
        grid=(b, nc),
        in_specs=_gdn_pre_out_specs(tm, lambda i, j: i * nc + j) + [_const_spec(onorm.shape)],
        out_specs=[tok(V_DIM),
                   pl.BlockSpec((1, V_HEADS, HEAD_DIM, HEAD_DIM), lambda i, j: (i, 0, 0, 0))],
        out_shape=[jax.ShapeDtypeStruct((b * l, V_DIM), BF16),
                   jax.ShapeDtypeStruct((b, V_HEADS, HEAD_DIM, HEAD_DIM), F32)],
        compiler_params=_params("arbitrary", "arbitrary"),
        name="gdn_scan_prompt",
    )(q, k, v, z, beta, g, onorm)


def _gdn_scan_sample(q, k, v, z, beta, g, onorm, s0, b, l):
    tb = SAMPLE_SCAN_BTILE
    tok = lambda w: pl.BlockSpec((tb * l, w), lambda i: (i, 0))
    st = pl.BlockSpec((tb, V_HEADS, HEAD_DIM, HEAD_DIM), lambda i: (i, 0, 0, 0))
    return pl.pallas_call(
        functools.partial(_gdn_scan_sample_kernel, tb=tb, c=l),
        grid=(b // tb,),
        in_specs=_gdn_pre_out_specs(tb * l, lambda i: i) + [_const_spec(onorm.shape), st],
        out_specs=[tok(V_DIM), st],
        out_shape=[jax.ShapeDtypeStruct((b * l, V_DIM), BF16),
                   jax.ShapeDtypeStruct((b, V_HEADS, HEAD_DIM, HEAD_DIM), F32)],
        compiler_params=_params("arbitrary"),
        name="gdn_scan_sample",
    )(q, k, v, z, beta, g, onorm, s0)


def _gdn_post_kernel(og_ref, x_ref, gains_ref, wo_ref, win_ref, wout_ref, y_ref):
    m = jnp.dot(og_ref[...], wo_ref[...], preferred_element_type=F32)
    y_ref[...] = _residual_ffn(x_ref[...], m, gains_ref[0:1, :], gains_ref[1:2, :],
                               gains_ref[2:3, :], win_ref, wout_ref)


def _gdn_post(og, x, gains, wo, win, wout, layer):
    n, d = x.shape
    tm = PROMPT_TILE
    return pl.pallas_call(
        _gdn_post_kernel,
        grid=(n // tm,),
        in_specs=[pl.BlockSpec((tm, V_DIM), lambda i: (i, 0)),
                  pl.BlockSpec((tm, d), lambda i: (i, 0))]
        + [_const_spec(gains.shape), _const_spec(wo.shape), _layer_spec(win, layer),
           _layer_spec(wout, layer)],
        out_specs=pl.BlockSpec((tm, d), lambda i: (i, 0)),
        out_shape=jax.ShapeDtypeStruct((n, d), F32),
        compiler_params=_params("arbitrary"),
        name="gdn_post",
    )(og, x, gains, wo, win, wout)


def _head_lanes(vec):
    return jnp.pad(vec.astype(F32), (0, LANES - V_HEADS)).reshape(1, LANES)


def kernel(x_prompt, x_sample, state_pool, state_gdn_conv, state_gdn_rec, norm_mix_pre,
           norm_mix_post, norm_ffn_pre, norm_ffn_post, pool_w, pool_scale, gdn_w_in,
           gdn_conv_w, gdn_a_log, gdn_dt_bias, gdn_o_norm, gdn_w_out, ffn_w_in, ffn_w_out):
    bp, lp, d = x_prompt.shape
    bs, ls, _ = x_sample.shape

    gains0 = jnp.stack([norm_mix_pre[0], norm_mix_post[0], norm_ffn_pre[0], norm_ffn_post[0]])
    gains1 = jnp.stack([norm_mix_post[1], norm_ffn_pre[1], norm_ffn_post[1]])
    gain1_pre = norm_mix_pre[1].reshape(1, d)
    pw = pool_w[0].astype(BF16)
    ps = pool_scale[0].reshape(1, d)
    win, wout = ffn_w_in.astype(BF16), ffn_w_out.astype(BF16)
    w_in = gdn_w_in[0]
    wqkvz = w_in.astype(BF16)
    w_b = w_in[:, CONV_DIM + V_DIM:CONV_DIM + V_DIM + V_HEADS]
    w_a = w_in[:, CONV_DIM + V_DIM + V_HEADS:]
    lane_pad = ((0, 0), (0, LANES - V_HEADS))
    wba = jnp.concatenate([jnp.pad(w_b, lane_pad), jnp.pad(w_a, lane_pad)], axis=1).astype(BF16)
    cw = gdn_conv_w[0]
    alog, dtb = _head_lanes(gdn_a_log[0]), _head_lanes(gdn_dt_bias[0])
    onorm = gdn_o_norm[0].reshape(1, HEAD_DIM)
    wo = gdn_w_out[0].astype(BF16)

    xp1, pool_p = _pool_layer_prompt(x_prompt, gains0, pw, ps, win, wout, 0)
    time_major = lambda a: jnp.transpose(a, (1, 0, 2))
    xs1, pool_s = _pool_layer_sample(x_sample, time_major(state_pool[0]), gains0, pw, ps, win,
                                     wout, 0)

    qp, kp, vp, zp, betap, gp, conv_p = _gdn_pre_prompt(xp1, gain1_pre, wqkvz, wba, cw, alog, dtb)
    qs, ks, vs, zs, betas, gs, conv_s = _gdn_pre_sample(xs1, time_major(state_gdn_conv[0]),
                                                        gain1_pre, wqkvz, wba, cw, alog, dtb)

    ogp, rec_p = _gdn_scan_prompt(qp, kp, vp, zp, betap, gp, onorm, bp, lp)
    ogs, rec_s = _gdn_scan_sample(qs, ks, vs, zs, betas, gs, onorm, state_gdn_rec[0], bs, ls)

    yp = _gdn_post(ogp, xp1.reshape(bp * lp, d), gains1, wo, win, wout, 1).reshape(bp, lp, d)
    ys = _gdn_post(ogs, xs1.reshape(bs * ls, d), gains1, wo, win, wout, 1).reshape(bs, ls, d)

    return (yp, ys, pool_p[None], time_major(pool_s)[None], conv_p[None],
            time_major(conv_s)[None], rec_p[None], rec_s[None])
```

```python
import functools

import jax
import jax.numpy as jnp
from jax import lax
from jax.experimental import pallas as pl
from jax.experimental.pallas import tpu as pltpu

D_MODEL = 1024
POOL_WINDOWS = (2, 4, 8, 16)
POOL_GROUP_DIM = D_MODEL // len(POOL_WINDOWS)
POOL_BUF = max(POOL_WINDOWS) - 1
K_HEADS = 8
V_HEADS = 16
HEAD_DIM = 128
QK_DIM = K_HEADS * HEAD_DIM
V_DIM = V_HEADS * HEAD_DIM
CONV_DIM = 2 * QK_DIM + V_DIM
CONV_WIDTH = 4
D_FF = 2816
EPS = 1e-6

F32 = jnp.float32
BF16 = jnp.bfloat16

SUBLANES = 8
LANES = 128
POOL_HALO = 16
CONV_HALO = SUBLANES
VMEM_LIMIT = 56 * 1024 * 1024
ROW_STRIDE = 4
ROW_GROUP = SUBLANES * ROW_STRIDE

PROMPT_TILE = 512
GDN_PRE_TILE = 256
PRE_SLABS = 8
SAMPLE_BTILE = 32
PROMPT_CHUNK = 64
SCAN_TILE = 256
SCAN_GROUP = 2
SAMPLE_SCAN_BTILE = 4


def _rms(x, gain):
    ms = jnp.mean(x * x, axis=-1, keepdims=True)
    return x * lax.rsqrt(ms + EPS) * gain


def _sigmoid(x):
    return 1.0 / (1.0 + jnp.exp(-x))


def _silu(x):
    return x * _sigmoid(x)


def _softplus(x):
    return jnp.maximum(x, 0.0) + jnp.log1p(jnp.exp(-jnp.abs(x)))


def _dot(a, b):
    return jnp.dot(a.astype(BF16), b.astype(BF16), preferred_element_type=F32)


def _const_spec(shape):
    nd = len(shape)
    return pl.BlockSpec(shape, lambda *_: (0,) * nd, pipeline_mode=pl.Buffered(1))


def _layer_spec(stacked, layer):
    nd = stacked.ndim - 1
    return pl.BlockSpec((None,) + stacked.shape[1:], lambda *_: (layer,) + (0,) * nd,
                        pipeline_mode=pl.Buffered(1))


def _params(*sem):
    return pltpu.CompilerParams(dimension_semantics=sem, vmem_limit_bytes=VMEM_LIMIT)


def _residual_ffn(x, m, g_post, g_fpre, g_fpost, win_ref, wout_ref):
    x1 = x + _rms(m, g_post)
    h = _rms(x1, g_fpre).astype(BF16)
    gate = jnp.dot(h, win_ref[:, :D_FF], preferred_element_type=F32)
    up = jnp.dot(h, win_ref[:, D_FF:], preferred_element_type=F32)
    act = (_silu(gate) * up).astype(BF16)
    f = jnp.dot(act, wout_ref[...], preferred_element_type=F32)
    return x1 + _rms(f, g_fpost)


def _pool_project(diffs, pw_ref, scale):
    parts = [_dot(d, pw_ref[gi]) for gi, d in enumerate(diffs)]
    return jnp.concatenate(parts, axis=-1) * scale


def _strided_rows(first):
    return pl.ds(first, SUBLANES, stride=ROW_STRIDE)


def _pool_layer_prompt_kernel(x_ref, gains_ref, pw_ref, ps_ref, win_ref, wout_ref,
                              y_ref, pool_ref, hp_ref, d_ref, x1_s, h2_s, *, tm, nj, n_tile):
    t = pl.program_id(0)
    wr = lax.rem(t, 2)
    rd = 1 - wr
    jp = lax.rem(jnp.minimum(t, n_tile - 1), nj)
    n_blk = D_MODEL // LANES
    blk_per_grp = POOL_GROUP_DIM // LANES

    @pl.when(t == 0)
    def _():
        x1_s[1] = jnp.zeros(x1_s.shape[1:], F32)
        h2_s[1] = jnp.zeros(h2_s.shape[1:], BF16)

    @pl.when(jp == 0)
    def _():
        hp_ref[:, 0:POOL_HALO, :] = jnp.zeros((n_blk, POOL_HALO, LANES), F32)

    h2 = h2_s[rd]
    gate = jnp.dot(h2, win_ref[:, :D_FF], preferred_element_type=F32)

    x = x_ref[0]
    h = _rms(x, gains_ref[0:1, :])
    for cb in range(n_blk):
        hp_ref[cb, POOL_HALO:POOL_HALO + tm, :] = h[:, cb * LANES:(cb + 1) * LANES]
    t_tile = lax.broadcasted_iota(jnp.int32, (SUBLANES, 1), 0) * ROW_STRIDE + (jp * tm + 1)
    for row0 in range(0, tm, ROW_GROUP):
        for gi, win in enumerate(POOL_WINDOWS):
            invs = [1.0 / jnp.minimum(win, t_tile + (row0 + r)).astype(F32)
                    for r in range(ROW_STRIDE)]
            for cb in range(gi * blk_per_grp, (gi + 1) * blk_per_grp):
                tiles = [hp_ref[cb, _strided_rows(row0 + POOL_HALO - (win - 1) + k), :]
                         for k in range(win + ROW_STRIDE - 1)]
                shared = range(ROW_STRIDE - 1, win)
                common = (functools.reduce(lambda a, b: a + b, [tiles[k] for k in shared])
                          if len(shared) > 1 else None)
                for r in range(ROW_STRIDE):
                    own = [k for k in range(r, r + win) if common is None or k not in shared]
                    tot = tiles[own[0]] if common is None else common + tiles[own[0]]
                    for k in own[1:]:
                        tot = tot + tiles[k]
                    cur = tiles[r + win - 1]
                    d_ref[cb, _strided_rows(row0 + r), :] = tot * invs[r] - cur
    diffs = [jnp.concatenate([d_ref[cb] for cb in range(gi * blk_per_grp, (gi + 1) * blk_per_grp)],
                             axis=1) for gi in range(len(POOL_WINDOWS))]
    m = _pool_project(diffs, pw_ref, ps_ref[...])
    x1_new = x + _rms(m, gains_ref[1:2, :])
    x1_s[wr] = x1_new
    h2_s[wr] = _rms(x1_new, gains_ref[2:3, :]).astype(BF16)

    up = jnp.dot(h2, win_ref[:, D_FF:], preferred_element_type=F32)
    act = (_silu(gate) * up).astype(BF16)
    f = jnp.dot(act, wout_ref[...], preferred_element_type=F32)
    y_ref[0] = x1_s[rd] + _rms(f, gains_ref[3:4, :])

    @pl.when((jp == nj - 1) & (t < n_tile))
    def _():
        for cb in range(n_blk):
            pool_ref[0, :, cb * LANES:(cb + 1) * LANES] = hp_ref[
                cb, tm + POOL_HALO - POOL_BUF:tm + POOL_HALO, :]

    hp_ref[:, 0:POOL_HALO, :] = hp_ref[:, tm:tm + POOL_HALO, :]


def _pool_layer_sample_kernel(x_ref, buf_ref, gains_ref, pw_ref, ps_ref, win_ref, wout_ref,
                              y_ref, pool_ref, hs_ref, d_ref, *, tb, seq, n_past):
    m_rows = tb * seq
    x = x_ref[...].reshape(m_rows, D_MODEL)
    h = _rms(x, gains_ref[0:1, :])
    n_blk = D_MODEL // LANES
    blk_per_grp = POOL_GROUP_DIM // LANES
    at_time = lambda t: pl.ds(t, tb, stride=seq)
    for cb in range(n_blk):
        lanes = slice(cb * LANES, (cb + 1) * LANES)
        hs_ref[cb] = h[:, lanes]
        win = POOL_WINDOWS[cb // blk_per_grp]
        hist = [buf_ref[s, :, lanes] for s in range(n_past)]
        hist += [hs_ref[cb, at_time(t), :] for t in range(seq)]
        for t in range(seq):
            tot = hist[n_past + t]
            for s in range(1, win):
                tot = tot + hist[n_past + t - s]
            d_ref[cb, at_time(t), :] = tot * (1.0 / min(win, t + 1 + n_past)) - hist[n_past + t]
        for s in range(n_past):
            pool_ref[s, :, lanes] = hist[seq + s]
    diffs = [jnp.concatenate([d_ref[cb] for cb in range(gi * blk_per_grp, (gi + 1) * blk_per_grp)],
                             axis=1) for gi in range(len(POOL_WINDOWS))]
    m = _pool_project(diffs, pw_ref, ps_ref[...])
    y = _residual_ffn(x, m, gains_ref[1:2, :], gains_ref[2:3, :], gains_ref[3:4, :],
                      win_ref, wout_ref)
    y_ref[...] = y.reshape(tb, seq, D_MODEL)


def _pool_layer_prompt(x, gains, pw, ps, win, wout, layer):
    b, l, d = x.shape
    tm = PROMPT_TILE
    nj = l // tm
    n_tile = b * nj
    mixed = lambda t: jnp.minimum(t, n_tile - 1)
    done = lambda t: jnp.maximum(t - 1, 0)
    return pl.pallas_call(
        functools.partial(_pool_layer_prompt_kernel, tm=tm, nj=nj, n_tile=n_tile),
        grid=(n_tile + 1,),
        in_specs=[
            pl.BlockSpec((1, tm, d), lambda t: (mixed(t) // nj, mixed(t) % nj, 0)),
            _const_spec(gains.shape), _const_spec(pw.shape), _const_spec(ps.shape),
            _layer_spec(win, layer), _layer_spec(wout, layer),
        ],
        out_specs=[
            pl.BlockSpec((1, tm, d), lambda t: (done(t) // nj, done(t) % nj, 0)),
            pl.BlockSpec((1, POOL_BUF, d), lambda t: (mixed(t) // nj, 0, 0)),
        ],
        out_shape=[jax.ShapeDtypeStruct((b, l, d), F32),
                   jax.ShapeDtypeStruct((b, POOL_BUF, d), F32)],
        scratch_shapes=[pltpu.VMEM((d // LANES, POOL_HALO + tm, LANES), F32),
                        pltpu.VMEM((d // LANES, tm, LANES), F32),
                        pltpu.VMEM((2, tm, d), F32), pltpu.VMEM((2, tm, d), BF16)],
        compiler_params=_params("arbitrary"),
        name="pool_layer_prompt",
    )(x, gains, pw, ps, win, wout)


def _pool_layer_sample(x, buf, gains, pw, ps, win, wout, layer):
    b, l, d = x.shape
    tb = SAMPLE_BTILE
    n_past = buf.shape[0]
    assert n_past == POOL_BUF
    return pl.pallas_call(
        functools.partial(_pool_layer_sample_kernel, tb=tb, seq=l, n_past=n_past),
        grid=(b // tb,),
        in_specs=[
            pl.BlockSpec((tb, l, d), lambda i: (i, 0, 0)),
            pl.BlockSpec((POOL_BUF, tb, d), lambda i: (0, i, 0)),
            _const_spec(gains.shape), _const_spec(pw.shape), _const_spec(ps.shape),
            _layer_spec(win, layer), _layer_spec(wout, layer),
        ],
        out_specs=[
            pl.BlockSpec((tb, l, d), lambda i: (i, 0, 0)),
            pl.BlockSpec((POOL_BUF, tb, d), lambda i: (0, i, 0)),
        ],
        out_shape=[jax.ShapeDtypeStruct((b, l, d), F32),
                   jax.ShapeDtypeStruct((POOL_BUF, b, d), F32)],
        scratch_shapes=[pltpu.VMEM((d // LANES, tb * l, LANES), F32),
                        pltpu.VMEM((d // LANES, tb * l, LANES), F32)],
        compiler_params=_params("arbitrary"),
        name="pool_layer_sample",
    )(x, buf, gains, pw, ps, win, wout)


def _gdn_qkv_slab(sl, conv, q_ref, k_ref, v_ref, rows):
    if sl >= 2 * K_HEADS:
        v_ref[sl - 2 * K_HEADS, rows, :] = conv
        return
    unit = conv * lax.rsqrt(jnp.sum(conv * conv, axis=-1, keepdims=True) + EPS)
    if sl < K_HEADS:
        q_ref[sl, rows, :] = unit * (HEAD_DIM ** -0.5)
    else:
        k_ref[sl - K_HEADS, rows, :] = unit


def _gdn_gates(ba, alog, dtb, beta_ref, g_ref):
    beta_ref[...] = _sigmoid(ba[:, :LANES])
    g_ref[...] = -jnp.exp(alog) * _softplus(ba[:, LANES:] + dtb)


def _gdn_pre_prompt_kernel(x_ref, gain_ref, wqkvz_ref, wba_ref, cw_ref, alog_ref, dtb_ref,
                           q_ref, k_ref, v_ref, z_ref, beta_ref, g_ref, conv_ref, up_ref, *, tm):
    j = pl.program_id(1)
    h = _rms(x_ref[0], gain_ref[...]).astype(BF16)
    n_slab = CONV_DIM // HEAD_DIM
    n_grp = n_slab // PRE_SLABS

    @pl.when(j == 0)
    def _():
        up_ref[:, 0:CONV_HALO, :] = jnp.zeros((n_slab, CONV_HALO, HEAD_DIM), F32)

    def project(grp):
        c0 = grp * PRE_SLABS * HEAD_DIM
        p = jnp.dot(h, wqkvz_ref[:, c0:c0 + PRE_SLABS * HEAD_DIM], preferred_element_type=F32)
        for t in range(PRE_SLABS):
            up_ref[grp * PRE_SLABS + t, CONV_HALO:CONV_HALO + tm, :] = (
                p[:, t * HEAD_DIM:(t + 1) * HEAD_DIM])

    def project_z(half):
        c0 = CONV_DIM + half * (V_DIM // 2)
        p = jnp.dot(h, wqkvz_ref[:, c0:c0 + V_DIM // 2], preferred_element_type=F32)
        for t in range(V_HEADS // 2):
            z_ref[half * (V_HEADS // 2) + t] = p[:, t * HEAD_DIM:(t + 1) * HEAD_DIM]

    base = CONV_HALO - (CONV_WIDTH - 1)

    def convolve(grp):
        for sl in range(grp * PRE_SLABS, (grp + 1) * PRE_SLABS):
            lanes = slice(sl * HEAD_DIM, (sl + 1) * HEAD_DIM)
            taps = [cw_ref[tap:tap + 1, lanes] for tap in range(CONV_WIDTH)]
            for row0 in range(0, tm, ROW_GROUP):
                tiles = [up_ref[sl, _strided_rows(row0 + base + s), :]
                         for s in range(ROW_STRIDE + CONV_WIDTH - 1)]
                for r in range(ROW_STRIDE):
                    acc = tiles[r] * taps[0]
                    for tap in range(1, CONV_WIDTH):
                        acc = acc + tiles[r + tap] * taps[tap]
                    _gdn_qkv_slab(sl, _silu(acc), q_ref, k_ref, v_ref, _strided_rows(row0 + r))

    project(0)
    for grp in range(n_grp):
        if grp + 1 < n_grp:
            project(grp + 1)
        else:
            project_z(0)
        convolve(grp)
    project_z(1)
    ba = jnp.dot(h, wba_ref[...], preferred_element_type=F32)
    _gdn_gates(ba, alog_ref[...], dtb_ref[...], beta_ref, g_ref)

    @pl.when(j == pl.num_programs(1) - 1)
    def _():
        for sl in range(n_slab):
            conv_ref[0, :, sl * HEAD_DIM:(sl + 1) * HEAD_DIM] = up_ref[
                sl, tm + CONV_HALO - (CONV_WIDTH - 1):tm + CONV_HALO, :]

    up_ref[:, 0:CONV_HALO, :] = up_ref[:, tm:tm + CONV_HALO, :]


def _gdn_pre_sample_kernel(x_ref, buf_ref, gain_ref, wqkvz_ref, wba_ref, cw_ref, alog_ref,
                           dtb_ref, q_ref, k_ref, v_ref, z_ref, beta_ref, g_ref, conv_ref, up_ref,
                           *, tb, seq):
    m = tb * seq
    n_buf = CONV_WIDTH - 1
    h = _rms(x_ref[...].reshape(m, D_MODEL), gain_ref[...]).astype(BF16)
    proj = jnp.dot(h, wqkvz_ref[:, :CONV_DIM + V_DIM], preferred_element_type=F32)
    ba = jnp.dot(h, wba_ref[...], preferred_element_type=F32)
    for hh in range(V_HEADS):
        z_ref[hh] = proj[:, CONV_DIM + hh * HEAD_DIM:CONV_DIM + (hh + 1) * HEAD_DIM]
    _gdn_gates(ba, alog_ref[...], dtb_ref[...], beta_ref, g_ref)
    at_time = lambda t: pl.ds(t, tb, stride=seq)
    for sl in range(CONV_DIM // HEAD_DIM):
        lanes = slice(sl * HEAD_DIM, (sl + 1) * HEAD_DIM)
        up_ref[sl] = proj[:, lanes]
        ups = [buf_ref[s, :, lanes] for s in range(n_buf)]
        ups += [up_ref[sl, at_time(t), :] for t in range(seq)]
        for t in range(seq):
            acc = ups[t] * cw_ref[0:1, lanes]
            for tap in range(1, CONV_WIDTH):
                acc = acc + ups[t + tap] * cw_ref[tap:tap + 1, lanes]
            _gdn_qkv_slab(sl, _silu(acc), q_ref, k_ref, v_ref, at_time(t))
        for s in range(n_buf):
            conv_ref[s, :, lanes] = ups[seq + s]


def _gdn_pre_out_shapes(n):
    return [jax.ShapeDtypeStruct((K_HEADS, n, HEAD_DIM), F32),
            jax.ShapeDtypeStruct((K_HEADS, n, HEAD_DIM), F32),
            jax.ShapeDtypeStruct((V_HEADS, n, HEAD_DIM), F32),
            jax.ShapeDtypeStruct((V_HEADS, n, HEAD_DIM), F32),
            jax.ShapeDtypeStruct((n, LANES), F32), jax.ShapeDtypeStruct((n, LANES), F32)]


def _gdn_pre_out_specs(rows, index):
    heads = lambda n: pl.BlockSpec((n, rows, HEAD_DIM), lambda *g: (0, index(*g), 0))
    lane = pl.BlockSpec((rows, LANES), lambda *g: (index(*g), 0))
    return [heads(K_HEADS), heads(K_HEADS), heads(V_HEADS), heads(V_HEADS), lane, lane]


def _gdn_pre_prompt(x, gain, wqkvz, wba, cw, alog, dtb):
    b, l, d = x.shape
    tm = GDN_PRE_TILE
    nj = l // tm
    return pl.pallas_call(
        functools.partial(_gdn_pre_prompt_kernel, tm=tm),
        grid=(b, nj),
        in_specs=[pl.BlockSpec((1, tm, d), lambda i, j: (i, j, 0))]
        + [_const_spec(a.shape) for a in (gain, wqkvz, wba, cw, alog, dtb)],
        out_specs=_gdn_pre_out_specs(tm, lambda i, j: i * nj + j)
        + [pl.BlockSpec((1, CONV_WIDTH - 1, CONV_DIM), lambda i, j: (i, 0, 0))],
        out_shape=_gdn_pre_out_shapes(b * l)
        + [jax.ShapeDtypeStruct((b, CONV_WIDTH - 1, CONV_DIM), F32)],
        scratch_shapes=[pltpu.VMEM((CONV_DIM // HEAD_DIM, CONV_HALO + tm, HEAD_DIM), F32)],
        compiler_params=_params("arbitrary", "arbitrary"),
        name="gdn_pre_prompt",
    )(x, gain, wqkvz, wba, cw, alog, dtb)


def _gdn_pre_sample(x, buf, gain, wqkvz, wba, cw, alog, dtb):
    b, l, d = x.shape
    tb = SAMPLE_BTILE
    state = pl.BlockSpec((CONV_WIDTH - 1, tb, CONV_DIM), lambda i: (0, i, 0))
    return pl.pallas_call(
        functools.partial(_gdn_pre_sample_kernel, tb=tb, seq=l),
        grid=(b // tb,),
        in_specs=[pl.BlockSpec((tb, l, d), lambda i: (i, 0, 0)), state]
        + [_const_spec(a.shape) for a in (gain, wqkvz, wba, cw, alog, dtb)],
        out_specs=_gdn_pre_out_specs(tb * l, lambda i: i) + [state],
        out_shape=_gdn_pre_out_shapes(b * l)
        + [jax.ShapeDtypeStruct((CONV_WIDTH - 1, b, CONV_DIM), F32)],
        scratch_shapes=[pltpu.VMEM((CONV_DIM // HEAD_DIM, tb * l, HEAD_DIM), F32)],
        compiler_params=_params("arbitrary"),
        name="gdn_pre_sample",
    )(x, buf, gain, wqkvz, wba, cw, alog, dtb)


def _unit_lower_inverses(mats, c):
    ri = lax.broadcasted_iota(jnp.int32, (c, c), 0)
    ci = lax.broadcasted_iota(jnp.int32, (c, c), 1)
    eye = (ri == ci).astype(F32)
    pair = ((ri // 2) == (ci // 2)) & (ri > ci)
    xs = [eye - jnp.where(pair, a, 0.0) for a in mats]
    mats = [a.astype(BF16) for a in mats]
    blk = 2
    while blk < c:
        off = ((ri // (2 * blk)) == (ci // (2 * blk))) & ((ri // blk) > (ci // blk))
        xbs = [x.astype(BF16) for x in xs]
        ys = [_dot(jnp.where(off, a, jnp.zeros_like(a)), xb) for a, xb in zip(mats, xbs)]
        xs = [x - _dot(xb, y) for x, xb, y in zip(xs, xbs, ys)]
        blk *= 2
    return xs


def _delta_chunks(q_ref, k_ref, v_ref, z_ref, beta_ref, g_ref, og_ref, onorm, rows,
                  state_load, state_store, c):
    n = len(rows)
    ri = lax.broadcasted_iota(jnp.int32, (c, c), 0)
    ci = lax.broadcasted_iota(jnp.int32, (c, c), 1)
    causal = ri >= ci
    strict = ri > ci
    rep = V_HEADS // K_HEADS
    units = [(i, h) for i in range(n) for h in range(V_HEADS)]
    kunits = [(i, j) for i in range(n) for j in range(K_HEADS)]

    gcum = [_cumsum_rows(g_ref[rows[i], :], c) for i in range(n)]
    gcum_t = [x.T for x in gcum]
    egcum = [jnp.exp(x) for x in gcum]
    etail_t = [x[:, c - 1:c] - x for x in gcum_t]
    etail_t = [jnp.exp(x) for x in etail_t]
    beta = [beta_ref[rows[i], :] for i in range(n)]
    ks = {(i, j): k_ref[j, rows[i], :] for i, j in kunits}
    kts = {u: ks[u].T for u in kunits}
    kq = {(i, j): _dot(jnp.concatenate([ks[i, j], q_ref[j, rows[i], :]], axis=0), kts[i, j])
          for i, j in kunits}
    gcol = {(i, h): gcum[i][:, h:h + 1] for i, h in units}
    bcol = {(i, h): beta[i][:, h:h + 1] for i, h in units}
    egc = {(i, h): egcum[i][:, h:h + 1] for i, h in units}
    decay = {(i, h): jnp.where(
        causal, jnp.exp(jnp.minimum(gcol[i, h] - gcum_t[i][h:h + 1, :], 0.0)), 0.0)
        for i, h in units}
    a_mats = [jnp.where(strict, kq[i, h // rep][:c] * bcol[i, h] * decay[i, h], 0.0)
              for i, h in units]
    t_inv = dict(zip(units, _unit_lower_inverses(a_mats, c)))
    uw = {(i, h): _dot(t_inv[i, h], jnp.concatenate(
        [v_ref[h, rows[i], :] * bcol[i, h], ks[i, h // rep] * (bcol[i, h] * egc[i, h])], axis=1))
        for i, h in units}
    wq = {(i, h): jnp.concatenate(
        [uw[i, h][:, HEAD_DIM:], q_ref[h // rep, rows[i], :] * egc[i, h]], axis=0).astype(BF16)
        for i, h in units}
    qkd = {(i, h): (kq[i, h // rep][c:] * decay[i, h]).astype(BF16) for i, h in units}

    heads = range(V_HEADS)
    for i in range(n):
        s_old = [state_load(i, h) for h in heads]
        ws = [_dot(wq[i, h], s_old[h]) for h in heads]
        v_new = [uw[i, h][:, :HEAD_DIM] - ws[h][:c] for h in heads]
        o = [ws[h][c:] + _dot(qkd[i, h], v_new[h]) for h in heads]
        for h in heads:
            k_dec_t = kts[i, h // rep] * etail_t[i][h:h + 1, :]
            state_store(i, h, s_old[h] * egcum[i][c - 1:c, h:h + 1] + _dot(k_dec_t, v_new[h]))
        for h in heads:
            og_ref[rows[i], h * HEAD_DIM:(h + 1) * HEAD_DIM] = (
                _rms(o[h], onorm) * _silu(z_ref[h, rows[i], :])).astype(BF16)


def _cumsum_rows(g, c):
    ri = lax.broadcasted_iota(jnp.int32, (c, c), 0)
    ci = lax.broadcasted_iota(jnp.int32, (c, c), 1)
    tri = (ri >= ci).astype(F32)
    return jnp.dot(tri, g, preferred_element_type=F32, precision=lax.Precision.HIGHEST)


def _gdn_scan_prompt_kernel(q_ref, k_ref, v_ref, z_ref, beta_ref, g_ref, onorm_ref,
                            og_ref, s_ref, *, c, n_chunk):
    @pl.when(pl.program_id(1) == 0)
    def _():
        s_ref[...] = jnp.zeros(s_ref.shape, F32)

    def load(i, hh):
        return s_ref[0, hh]

    def store(i, hh, val):
        s_ref[0, hh] = val

    def body(it, carry):
        r0 = pl.multiple_of(it * (SCAN_GROUP * c), SCAN_GROUP * c)
        rows = [pl.ds(r0 + i * c, c) for i in range(SCAN_GROUP)]
        _delta_chunks(q_ref, k_ref, v_ref, z_ref, beta_ref, g_ref, og_ref, onorm_ref[...], rows,
                      load, store, c)
        return carry

    lax.fori_loop(0, n_chunk // SCAN_GROUP, body, 0)


def _gdn_scan_sample_kernel(q_ref, k_ref, v_ref, z_ref, beta_ref, g_ref, onorm_ref, s0_ref,
                            og_ref, s_ref, *, tb, c):
    def load(i, hh):
        return s0_ref[i, hh]

    def store(i, hh, val):
        s_ref[i, hh] = val

    rows = [pl.ds(i * c, c) for i in range(tb)]
    _delta_chunks(q_ref, k_ref, v_ref, z_ref, beta_ref, g_ref, og_ref, onorm_ref[...], rows,
                  load, store, c)


def _gdn_scan_prompt(q, k, v, z, beta, g, onorm, b, l):
    c = PROMPT_CHUNK
    tm = SCAN_TILE
    nc = l // tm
    tok = lambda w: pl.BlockSpec((tm, w), lambda i, j: (i * nc + j, 0))
    return pl.pallas_call(
        functools.partial(_gdn_scan_prompt_kernel, c=c, n_chunk=tm // c),
        grid=(b, nc),
        in_specs=_gdn_pre_out_specs(tm, lambda i, j: i * nc + j) + [_const_spec(onorm.shape)],
        out_specs=[tok(V_DIM),
                   pl.BlockSpec((1, V_HEADS, HEAD_DIM, HEAD_DIM), lambda i, j: (i, 0, 0, 0))],
        out_shape=[jax.ShapeDtypeStruct((b * l, V_DIM), BF16),
                   jax.ShapeDtypeStruct((b, V_HEADS, HEAD_DIM, HEAD_DIM), F32)],
        compiler_params=_params("arbitrary", "arbitrary"),
        name="gdn_scan_prompt",
    )(q, k, v, z, beta, g, onorm)


def _gdn_scan_sample(q, k, v, z, beta, g, onorm, s0, b, l):
    tb = SAMPLE_SCAN_BTILE
    tok = lambda w: pl.BlockSpec((tb * l, w), lambda i: (i, 0))
    st = pl.BlockSpec((tb, V_HEADS, HEAD_DIM, HEAD_DIM), lambda i: (i, 0, 0, 0))
    return pl.pallas_call(
        functools.partial(_gdn_scan_sample_kernel, tb=tb, c=l),
        grid=(b // tb,),
        in_specs=_gdn_pre_out_specs(tb * l, lambda i: i) + [_const_spec(onorm.shape), st],
        out_specs=[tok(V_DIM), st],
        out_shape=[jax.ShapeDtypeStruct((b * l, V_DIM), BF16),
                   jax.ShapeDtypeStruct((b, V_HEADS, HEAD_DIM, HEAD_DIM), F32)],
        compiler_params=_params("arbitrary"),
        name="gdn_scan_sample",
    )(q, k, v, z, beta, g, onorm, s0)


def _gdn_post_kernel(og_ref, x_ref, gains_ref, wo_ref, win_ref, wout_ref, y_ref):
    m = jnp.dot(og_ref[...], wo_ref[...], preferred_element_type=F32)
    y_ref[...] = _residual_ffn(x_ref[...], m, gains_ref[0:1, :], gains_ref[1:2, :],
                               gains_ref[2:3, :], win_ref, wout_ref)


def _gdn_post(og, x, gains, wo, win, wout, layer):
    n, d = x.shape
    tm = PROMPT_TILE
    return pl.pallas_call(
        _gdn_post_kernel,
        grid=(n // tm,),
        in_specs=[pl.BlockSpec((tm, V_DIM), lambda i: (i, 0)),
                  pl.BlockSpec((tm, d), lambda i: (i, 0))]
        + [_const_spec(gains.shape), _const_spec(wo.shape), _layer_spec(win, layer),
           _layer_spec(wout, layer)],
        out_specs=pl.BlockSpec((tm, d), lambda i: (i, 0)),
        out_shape=jax.ShapeDtypeStruct((n, d), F32),
        compiler_params=_params("arbitrary"),
        name="gdn_post",
    )(og, x, gains, wo, win, wout)


def _head_lanes(vec):
    return jnp.pad(vec.astype(F32), (0, LANES - V_HEADS)).reshape(1, LANES)


def kernel(x_prompt, x_sample, state_pool, state_gdn_conv, state_gdn_rec, norm_mix_pre,
           norm_mix_post, norm_ffn_pre, norm_ffn_post, pool_w, pool_scale, gdn_w_in,
           gdn_conv_w, gdn_a_log, gdn_dt_bias, gdn_o_norm, gdn_w_out, ffn_w_in, ffn_w_out):
    bp, lp, d = x_prompt.shape
    bs, ls, _ = x_sample.shape

    gains0 = jnp.stack([norm_mix_pre[0], norm_mix_post[0], norm_ffn_pre[0], norm_ffn_post[0]])
    gains1 = jnp.stack([norm_mix_post[1], norm_ffn_pre[1], norm_ffn_post[1]])
    gain1_pre = norm_mix_pre[1].reshape(1, d)
    pw = pool_w[0].astype(BF16)
    ps = pool_scale[0].reshape(1, d)
    win, wout = ffn_w_in.astype(BF16), ffn_w_out.astype(BF16)
    w_in = gdn_w_in[0]
    wqkvz = w_in.astype(BF16)
    w_b = w_in[:, CONV_DIM + V_DIM:CONV_DIM + V_DIM + V_HEADS]
    w_a = w_in[:, CONV_DIM + V_DIM + V_HEADS:]
    lane_pad = ((0, 0), (0, LANES - V_HEADS))
    wba = jnp.concatenate([jnp.pad(w_b, lane_pad), jnp.pad(w_a, lane_pad)], axis=1).astype(BF16)
    cw = gdn_conv_w[0]
    alog, dtb = _head_lanes(gdn_a_log[0]), _head_lanes(gdn_dt_bias[0])
    onorm = gdn_o_norm[0].reshape(1, HEAD_DIM)
    wo = gdn_w_out[0].astype(BF16)

    xp1, pool_p = _pool_layer_prompt(x_prompt, gains0, pw, ps, win, wout, 0)
    time_major = lambda a: jnp.transpose(a, (1, 0, 2))
    xs1, pool_s = _pool_layer_sample(x_sample, time_major(state_pool[0]), gains0, pw, ps, win,
                                     wout, 0)

    qp, kp, vp, zp, betap, gp, conv_p = _gdn_pre_prompt(xp1, gain1_pre, wqkvz, wba, cw, alog, dtb)
    qs, ks, vs, zs, betas, gs, conv_s = _gdn_pre_sample(xs1, time_major(state_gdn_conv[0]),
                                                        gain1_pre, wqkvz, wba, cw, alog, dtb)

    ogp, rec_p = _gdn_scan_prompt(qp, kp, vp, zp, betap, gp, onorm, bp, lp)
    ogs, rec_s = _gdn_scan_sample(qs, ks, vs, zs, betas, gs, onorm, state_gdn_rec[0], bs, ls)

    yp = _gdn_post(ogp, xp1.reshape(bp * lp, d), gains1, wo, win, wout, 1).reshape(bp, lp, d)
    ys = _gdn_post(ogs, xs1.reshape(bs * ls, d), gains1, wo, win, wout, 1).reshape(bs, ls, d)

    return (yp, ys, pool_p[None], time_major(pool_s)[None], conv_p[None],
            time_major(conv_s)[None], rec_p[None], rec_s[None])
```

```python
import functools

import jax
import jax.numpy as jnp
from jax import lax
from jax.experimental import pallas as pl
from jax.experimental.pallas import tpu as pltpu

D_MODEL = 1024
POOL_WINDOWS = (2, 4, 8, 16)
POOL_GROUP_DIM = D_MODEL // len(POOL_WINDOWS)
POOL_BUF = max(POOL_WINDOWS) - 1
K_HEADS = 8
V_HEADS = 16
HEAD_DIM = 128
QK_DIM = K_HEADS * HEAD_DIM
V_DIM = V_HEADS * HEAD_DIM
CONV_DIM = 2 * QK_DIM + V_DIM
CONV_WIDTH = 4
D_FF = 2816
EPS = 1e-6

F32 = jnp.float32
BF16 = jnp.bfloat16

SUBLANES = 8
LANES = 128
POOL_HALO = 16
CONV_HALO = SUBLANES
VMEM_LIMIT = 56 * 1024 * 1024
ROW_STRIDE = 4
ROW_GROUP = SUBLANES * ROW_STRIDE

PROMPT_TILE = 512
GDN_PRE_TILE = 256
PRE_SLABS = 8
SAMPLE_BTILE = 32
PROMPT_CHUNK = 64
SCAN_TILE = 256
SCAN_GROUP = 2
SAMPLE_SCAN_BTILE = 8
SAMPLE_SCAN_GROUP = 4


def _rms(x, gain):
    ms = jnp.mean(x * x, axis=-1, keepdims=True)
    return x * lax.rsqrt(ms + EPS) * gain


def _sigmoid(x):
    return 1.0 / (1.0 + jnp.exp(-x))


def _silu(x):
    return x * _sigmoid(x)


def _softplus(x):
    return jnp.maximum(x, 0.0) + jnp.log1p(jnp.exp(-jnp.abs(x)))


def _dot(a, b):
    return jnp.dot(a.astype(BF16), b.astype(BF16), preferred_element_type=F32)


def _const_spec(shape):
    nd = len(shape)
    return pl.BlockSpec(shape, lambda *_: (0,) * nd, pipeline_mode=pl.Buffered(1))


def _layer_spec(stacked, layer):
    nd = stacked.ndim - 1
    return pl.BlockSpec((None,) + stacked.shape[1:], lambda *_: (layer,) + (0,) * nd,
                        pipeline_mode=pl.Buffered(1))


def _params(*sem):
    return pltpu.CompilerParams(dimension_semantics=sem, vmem_limit_bytes=VMEM_LIMIT)


def _residual_ffn(x, m, g_post, g_fpre, g_fpost, win_ref, wout_ref):
    x1 = x + _rms(m, g_post)
    h = _rms(x1, g_fpre).astype(BF16)
    gate = jnp.dot(h, win_ref[:, :D_FF], preferred_element_type=F32)
    up = jnp.dot(h, win_ref[:, D_FF:], preferred_element_type=F32)
    act = (_silu(gate) * up).astype(BF16)
    f = jnp.dot(act, wout_ref[...], preferred_element_type=F32)
    return x1 + _rms(f, g_fpost)


def _pool_project(diffs, pw_ref, scale):
    parts = [_dot(d, pw_ref[gi]) for gi, d in enumerate(diffs)]
    return jnp.concatenate(parts, axis=-1) * scale


def _strided_rows(first):
    return pl.ds(first, SUBLANES, stride=ROW_STRIDE)


def _pool_layer_prompt_kernel(x_ref, gains_ref, pw_ref, ps_ref, win_ref, wout_ref,
                              y_ref, pool_ref, hp_ref, d_ref, x1_s, h2_s, *, tm, nj, n_tile):
    t = pl.program_id(0)
    wr = lax.rem(t, 2)
    rd = 1 - wr
    jp = lax.rem(jnp.minimum(t, n_tile - 1), nj)
    n_blk = D_MODEL // LANES
    blk_per_grp = POOL_GROUP_DIM // LANES

    @pl.when(t == 0)
    def _():
        x1_s[1] = jnp.zeros(x1_s.shape[1:], F32)
        h2_s[1] = jnp.zeros(h2_s.shape[1:], BF16)

    @pl.when(jp == 0)
    def _():
        hp_ref[:, 0:POOL_HALO, :] = jnp.zeros((n_blk, POOL_HALO, LANES), F32)

    h2 = h2_s[rd]
    gate = jnp.dot(h2, win_ref[:, :D_FF], preferred_element_type=F32)

    x = x_ref[0]
    h = _rms(x, gains_ref[0:1, :])
    for cb in range(n_blk):
        hp_ref[cb, POOL_HALO:POOL_HALO + tm, :] = h[:, cb * LANES:(cb + 1) * LANES]
    t_tile = lax.broadcasted_iota(jnp.int32, (SUBLANES, 1), 0) * ROW_STRIDE + (jp * tm + 1)
    for row0 in range(0, tm, ROW_GROUP):
        for gi, win in enumerate(POOL_WINDOWS):
            invs = [1.0 / jnp.minimum(win, t_tile + (row0 + r)).astype(F32)
                    for r in range(ROW_STRIDE)]
            for cb in range(gi * blk_per_grp, (gi + 1) * blk_per_grp):
                tiles = [hp_ref[cb, _strided_rows(row0 + POOL_HALO - (win - 1) + k), :]
                         for k in range(win + ROW_STRIDE - 1)]
                shared = range(ROW_STRIDE - 1, win)
                common = (functools.reduce(lambda a, b: a + b, [tiles[k] for k in shared])
                          if len(shared) > 1 else None)
                for r in range(ROW_STRIDE):
                    own = [k for k in range(r, r + win) if common is None or k not in shared]
                    tot = tiles[own[0]] if common is None else common + tiles[own[0]]
                    for k in own[1:]:
                        tot = tot + tiles[k]
                    cur = tiles[r + win - 1]
                    d_ref[cb, _strided_rows(row0 + r), :] = tot * invs[r] - cur
    diffs = [jnp.concatenate([d_ref[cb] for cb in range(gi * blk_per_grp, (gi + 1) * blk_per_grp)],
                             axis=1) for gi in range(len(POOL_WINDOWS))]
    m = _pool_project(diffs, pw_ref, ps_ref[...])
    x1_new = x + _rms(m, gains_ref[1:2, :])
    x1_s[wr] = x1_new
    h2_s[wr] = _rms(x1_new, gains_ref[2:3, :]).astype(BF16)

    up = jnp.dot(h2, win_ref[:, D_FF:], preferred_element_type=F32)
    act = (_silu(gate) * up).astype(BF16)
    f = jnp.dot(act, wout_ref[...], preferred_element_type=F32)
    y_ref[0] = x1_s[rd] + _rms(f, gains_ref[3:4, :])

    @pl.when((jp == nj - 1) & (t < n_tile))
    def _():
        for cb in range(n_blk):
            pool_ref[0, :, cb * LANES:(cb + 1) * LANES] = hp_ref[
                cb, tm + POOL_HALO - POOL_BUF:tm + POOL_HALO, :]

    hp_ref[:, 0:POOL_HALO, :] = hp_ref[:, tm:tm + POOL_HALO, :]


def _pool_layer_sample_kernel(x_ref, buf_ref, gains_ref, pw_ref, ps_ref, win_ref, wout_ref,
                              y_ref, pool_ref, hs_ref, d_ref, *, tb, seq, n_past):
    m_rows = tb * seq
    x = x_ref[...].reshape(m_rows, D_MODEL)
    h = _rms(x, gains_ref[0:1, :])
    n_blk = D_MODEL // LANES
    blk_per_grp = POOL_GROUP_DIM // LANES
    at_time = lambda t: pl.ds(t, tb, stride=seq)
    for cb in range(n_blk):
        lanes = slice(cb * LANES, (cb + 1) * LANES)
        hs_ref[cb] = h[:, lanes]
        win = POOL_WINDOWS[cb // blk_per_grp]
        hist = [buf_ref[s, :, lanes] for s in range(n_past)]
        hist += [hs_ref[cb, at_time(t), :] for t in range(seq)]
        for t in range(seq):
            tot = hist[n_past + t]
            for s in range(1, win):
                tot = tot + hist[n_past + t - s]
            d_ref[cb, at_time(t), :] = tot * (1.0 / min(win, t + 1 + n_past)) - hist[n_past + t]
        for s in range(n_past):
            pool_ref[s, :, lanes] = hist[seq + s]
    diffs = [jnp.concatenate([d_ref[cb] for cb in range(gi * blk_per_grp, (gi + 1) * blk_per_grp)],
                             axis=1) for gi in range(len(POOL_WINDOWS))]
    m = _pool_project(diffs, pw_ref, ps_ref[...])
    y = _residual_ffn(x, m, gains_ref[1:2, :], gains_ref[2:3, :], gains_ref[3:4, :],
                      win_ref, wout_ref)
    y_ref[...] = y.reshape(tb, seq, D_MODEL)


def _pool_layer_prompt(x, gains, pw, ps, win, wout, layer):
    b, l, d = x.shape
    tm = PROMPT_TILE
    nj = l // tm
    n_tile = b * nj
    mixed = lambda t: jnp.minimum(t, n_tile - 1)
    done = lambda t: jnp.maximum(t - 1, 0)
    return pl.pallas_call(
        functools.partial(_pool_layer_prompt_kernel, tm=tm, nj=nj, n_tile=n_tile),
        grid=(n_tile + 1,),
        in_specs=[
            pl.BlockSpec((1, tm, d), lambda t: (mixed(t) // nj, mixed(t) % nj, 0)),
            _const_spec(gains.shape), _const_spec(pw.shape), _const_spec(ps.shape),
            _layer_spec(win, layer), _layer_spec(wout, layer),
        ],
        out_specs=[
            pl.BlockSpec((1, tm, d), lambda t: (done(t) // nj, done(t) % nj, 0)),
            pl.BlockSpec((1, POOL_BUF, d), lambda t: (mixed(t) // nj, 0, 0)),
        ],
        out_shape=[jax.ShapeDtypeStruct((b, l, d), F32),
                   jax.ShapeDtypeStruct((b, POOL_BUF, d), F32)],
        scratch_shapes=[pltpu.VMEM((d // LANES, POOL_HALO + tm, LANES), F32),
                        pltpu.VMEM((d // LANES, tm, LANES), F32),
                        pltpu.VMEM((2, tm, d), F32), pltpu.VMEM((2, tm, d), BF16)],
        compiler_params=_params("arbitrary"),
        name="pool_layer_prompt",
    )(x, gains, pw, ps, win, wout)


def _pool_layer_sample(x, buf, gains, pw, ps, win, wout, layer):
    b, l, d = x.shape
    tb = SAMPLE_BTILE
    n_past = buf.shape[0]
    assert n_past == POOL_BUF
    return pl.pallas_call(
        functools.partial(_pool_layer_sample_kernel, tb=tb, seq=l, n_past=n_past),
        grid=(b // tb,),
        in_specs=[
            pl.BlockSpec((tb, l, d), lambda i: (i, 0, 0)),
            pl.BlockSpec((POOL_BUF, tb, d), lambda i: (0, i, 0)),
            _const_spec(gains.shape), _const_spec(pw.shape), _const_spec(ps.shape),
            _layer_spec(win, layer), _layer_spec(wout, layer),
        ],
        out_specs=[
            pl.BlockSpec((tb, l, d), lambda i: (i, 0, 0)),
            pl.BlockSpec((POOL_BUF, tb, d), lambda i: (0, i, 0)),
        ],
        out_shape=[jax.ShapeDtypeStruct((b, l, d), F32),
                   jax.ShapeDtypeStruct((POOL_BUF, b, d), F32)],
        scratch_shapes=[pltpu.VMEM((d // LANES, tb * l, LANES), F32),
                        pltpu.VMEM((d // LANES, tb * l, LANES), F32)],
        compiler_params=_params("arbitrary"),
        name="pool_layer_sample",
    )(x, buf, gains, pw, ps, win, wout)


def _gdn_qkv_slab(sl, conv, q_ref, k_ref, v_ref, rows):
    if sl >= 2 * K_HEADS:
        v_ref[sl - 2 * K_HEADS, rows, :] = conv
        return
    unit = conv * lax.rsqrt(jnp.sum(conv * conv, axis=-1, keepdims=True) + EPS)
    if sl < K_HEADS:
        q_ref[sl, rows, :] = unit * (HEAD_DIM ** -0.5)
    else:
        k_ref[sl - K_HEADS, rows, :] = unit


def _gdn_gates(ba, alog, dtb, beta_ref, g_ref):
    beta_ref[...] = _sigmoid(ba[:, :LANES])
    g_ref[...] = -jnp.exp(alog) * _softplus(ba[:, LANES:] + dtb)


def _gdn_pre_prompt_kernel(x_ref, gain_ref, wqkvz_ref, wba_ref, cw_ref, alog_ref, dtb_ref,
                           q_ref, k_ref, v_ref, z_ref, beta_ref, g_ref, conv_ref, up_ref, *, tm):
    j = pl.program_id(1)
    h = _rms(x_ref[0], gain_ref[...]).astype(BF16)
    n_slab = CONV_DIM // HEAD_DIM
    n_grp = n_slab // PRE_SLABS

    @pl.when(j == 0)
    def _():
        up_ref[:, 0:CONV_HALO, :] = jnp.zeros((n_slab, CONV_HALO, HEAD_DIM), F32)

    def project(grp):
        c0 = grp * PRE_SLABS * HEAD_DIM
        p = jnp.dot(h, wqkvz_ref[:, c0:c0 + PRE_SLABS * HEAD_DIM], preferred_element_type=F32)
        for t in range(PRE_SLABS):
            up_ref[grp * PRE_SLABS + t, CONV_HALO:CONV_HALO + tm, :] = (
                p[:, t * HEAD_DIM:(t + 1) * HEAD_DIM])

    def project_z(half):
        c0 = CONV_DIM + half * (V_DIM // 2)
        p = jnp.dot(h, wqkvz_ref[:, c0:c0 + V_DIM // 2], preferred_element_type=F32)
        for t in range(V_HEADS // 2):
            z_ref[half * (V_HEADS // 2) + t] = p[:, t * HEAD_DIM:(t + 1) * HEAD_DIM]

    base = CONV_HALO - (CONV_WIDTH - 1)

    def convolve(grp):
        for sl in range(grp * PRE_SLABS, (grp + 1) * PRE_SLABS):
            lanes = slice(sl * HEAD_DIM, (sl + 1) * HEAD_DIM)
            taps = [cw_ref[tap:tap + 1, lanes] for tap in range(CONV_WIDTH)]
            for row0 in range(0, tm, ROW_GROUP):
                tiles = [up_ref[sl, _strided_rows(row0 + base + s), :]
                         for s in range(ROW_STRIDE + CONV_WIDTH - 1)]
                for r in range(ROW_STRIDE):
                    acc = tiles[r] * taps[0]
                    for tap in range(1, CONV_WIDTH):
                        acc = acc + tiles[r + tap] * taps[tap]
                    _gdn_qkv_slab(sl, _silu(acc), q_ref, k_ref, v_ref, _strided_rows(row0 + r))

    project(0)
    for grp in range(n_grp):
        if grp + 1 < n_grp:
            project(grp + 1)
        else:
            project_z(0)
        convolve(grp)
    project_z(1)
    ba = jnp.dot(h, wba_ref[...], preferred_element_type=F32)
    _gdn_gates(ba, alog_ref[...], dtb_ref[...], beta_ref, g_ref)

    @pl.when(j == pl.num_programs(1) - 1)
    def _():
        for sl in range(n_slab):
            conv_ref[0, :, sl * HEAD_DIM:(sl + 1) * HEAD_DIM] = up_ref[
                sl, tm + CONV_HALO - (CONV_WIDTH - 1):tm + CONV_HALO, :]

    up_ref[:, 0:CONV_HALO, :] = up_ref[:, tm:tm + CONV_HALO, :]


def _gdn_pre_sample_kernel(x_ref, buf_ref, gain_ref, wqkvz_ref, wba_ref, cw_ref, alog_ref,
                           dtb_ref, q_ref, k_ref, v_ref, z_ref, beta_ref, g_ref, conv_ref, up_ref,
                           *, tb, seq):
    m = tb * seq
    n_buf = CONV_WIDTH - 1
    h = _rms(x_ref[...].reshape(m, D_MODEL), gain_ref[...]).astype(BF16)
    proj = jnp.dot(h, wqkvz_ref[:, :CONV_DIM + V_DIM], preferred_element_type=F32)
    ba = jnp.dot(h, wba_ref[...], preferred_element_type=F32)
    for hh in range(V_HEADS):
        z_ref[hh] = proj[:, CONV_DIM + hh * HEAD_DIM:CONV_DIM + (hh + 1) * HEAD_DIM]
    _gdn_gates(ba, alog_ref[...], dtb_ref[...], beta_ref, g_ref)
    at_time = lambda t: pl.ds(t, tb, stride=seq)
    for sl in range(CONV_DIM // HEAD_DIM):
        lanes = slice(sl * HEAD_DIM, (sl + 1) * HEAD_DIM)
        up_ref[sl] = proj[:, lanes]
        ups = [buf_ref[s, :, lanes] for s in range(n_buf)]
        ups += [up_ref[sl, at_time(t), :] for t in range(seq)]
        for t in range(seq):
            acc = ups[t] * cw_ref[0:1, lanes]
            for tap in range(1, CONV_WIDTH):
                acc = acc + ups[t + tap] * cw_ref[tap:tap + 1, lanes]
            _gdn_qkv_slab(sl, _silu(acc), q_ref, k_ref, v_ref, at_time(t))
        for s in range(n_buf):
            conv_ref[s, :, lanes] = ups[seq + s]


def _gdn_pre_out_shapes(n):
    return [jax.ShapeDtypeStruct((K_HEADS, n, HEAD_DIM), F32),
            jax.ShapeDtypeStruct((K_HEADS, n, HEAD_DIM), F32),
            jax.ShapeDtypeStruct((V_HEADS, n, HEAD_DIM), F32),
            jax.ShapeDtypeStruct((V_HEADS, n, HEAD_DIM), F32),
            jax.ShapeDtypeStruct((n, LANES), F32), jax.ShapeDtypeStruct((n, LANES), F32)]


def _gdn_pre_out_specs(rows, index):
    heads = lambda n: pl.BlockSpec((n, rows, HEAD_DIM), lambda *g: (0, index(*g), 0))
    lane = pl.BlockSpec((rows, LANES), lambda *g: (index(*g), 0))
    return [heads(K_HEADS), heads(K_HEADS), heads(V_HEADS), heads(V_HEADS), lane, lane]


def _gdn_pre_prompt(x, gain, wqkvz, wba, cw, alog, dtb):
    b, l, d = x.shape
    tm = GDN_PRE_TILE
    nj = l // tm
    return pl.pallas_call(
        functools.partial(_gdn_pre_prompt_kernel, tm=tm),
        grid=(b, nj),
        in_specs=[pl.BlockSpec((1, tm, d), lambda i, j: (i, j, 0))]
        + [_const_spec(a.shape) for a in (gain, wqkvz, wba, cw, alog, dtb)],
        out_specs=_gdn_pre_out_specs(tm, lambda i, j: i * nj + j)
        + [pl.BlockSpec((1, CONV_WIDTH - 1, CONV_DIM), lambda i, j: (i, 0, 0))],
        out_shape=_gdn_pre_out_shapes(b * l)
        + [jax.ShapeDtypeStruct((b, CONV_WIDTH - 1, CONV_DIM), F32)],
        scratch_shapes=[pltpu.VMEM((CONV_DIM // HEAD_DIM, CONV_HALO + tm, HEAD_DIM), F32)],
        compiler_params=_params("arbitrary", "arbitrary"),
        name="gdn_pre_prompt",
    )(x, gain, wqkvz, wba, cw, alog, dtb)


def _gdn_pre_sample(x, buf, gain, wqkvz, wba, cw, alog, dtb):
    b, l, d = x.shape
    tb = SAMPLE_BTILE
    state = pl.BlockSpec((CONV_WIDTH - 1, tb, CONV_DIM), lambda i: (0, i, 0))
    return pl.pallas_call(
        functools.partial(_gdn_pre_sample_kernel, tb=tb, seq=l),
        grid=(b // tb,),
        in_specs=[pl.BlockSpec((tb, l, d), lambda i: (i, 0, 0)), state]
        + [_const_spec(a.shape) for a in (gain, wqkvz, wba, cw, alog, dtb)],
        out_specs=_gdn_pre_out_specs(tb * l, lambda i: i) + [state],
        out_shape=_gdn_pre_out_shapes(b * l)
        + [jax.ShapeDtypeStruct((CONV_WIDTH - 1, b, CONV_DIM), F32)],
        scratch_shapes=[pltpu.VMEM((CONV_DIM // HEAD_DIM, tb * l, HEAD_DIM), F32)],
        compiler_params=_params("arbitrary"),
        name="gdn_pre_sample",
    )(x, buf, gain, wqkvz, wba, cw, alog, dtb)


def _unit_lower_inverses(mats, c):
    ri = lax.broadcasted_iota(jnp.int32, (c, c), 0)
    ci = lax.broadcasted_iota(jnp.int32, (c, c), 1)
    eye = (ri == ci).astype(F32)
    pair = ((ri // 2) == (ci // 2)) & (ri > ci)
    xs = [eye - jnp.where(pair, a, 0.0) for a in mats]
    mats = [a.astype(BF16) for a in mats]
    blk = 2
    while blk < c:
        off = ((ri // (2 * blk)) == (ci // (2 * blk))) & ((ri // blk) > (ci // blk))
        xbs = [x.astype(BF16) for x in xs]
        ys = [_dot(jnp.where(off, a, jnp.zeros_like(a)), xb) for a, xb in zip(mats, xbs)]
        xs = [x - _dot(xb, y) for x, xb, y in zip(xs, xbs, ys)]
        blk *= 2
    return xs


def _delta_chunks(q_ref, k_ref, v_ref, z_ref, beta_ref, g_ref, og_ref, onorm, rows,
                  state_load, state_store, c):
    n = len(rows)
    ri = lax.broadcasted_iota(jnp.int32, (c, c), 0)
    ci = lax.broadcasted_iota(jnp.int32, (c, c), 1)
    causal = ri >= ci
    strict = ri > ci
    rep = V_HEADS // K_HEADS
    units = [(i, h) for i in range(n) for h in range(V_HEADS)]
    kunits = [(i, j) for i in range(n) for j in range(K_HEADS)]

    gcum = [_cumsum_rows(g_ref[rows[i], :], c) for i in range(n)]
    gcum_t = [x.T for x in gcum]
    egcum = [jnp.exp(x) for x in gcum]
    etail_t = [x[:, c - 1:c] - x for x in gcum_t]
    etail_t = [jnp.exp(x) for x in etail_t]
    beta = [beta_ref[rows[i], :] for i in range(n)]
    ks = {(i, j): k_ref[j, rows[i], :] for i, j in kunits}
    kts = {u: ks[u].T for u in kunits}
    kq = {(i, j): _dot(jnp.concatenate([ks[i, j], q_ref[j, rows[i], :]], axis=0), kts[i, j])
          for i, j in kunits}
    gcol = {(i, h): gcum[i][:, h:h + 1] for i, h in units}
    bcol = {(i, h): beta[i][:, h:h + 1] for i, h in units}
    egc = {(i, h): egcum[i][:, h:h + 1] for i, h in units}
    decay = {(i, h): jnp.where(
        causal, jnp.exp(jnp.minimum(gcol[i, h] - gcum_t[i][h:h + 1, :], 0.0)), 0.0)
        for i, h in units}
    a_mats = [jnp.where(strict, kq[i, h // rep][:c] * bcol[i, h] * decay[i, h], 0.0)
              for i, h in units]
    t_inv = dict(zip(units, _unit_lower_inverses(a_mats, c)))
    uw = {(i, h): _dot(t_inv[i, h], jnp.concatenate(
        [v_ref[h, rows[i], :] * bcol[i, h], ks[i, h // rep] * (bcol[i, h] * egc[i, h])], axis=1))
        for i, h in units}
    wq = {(i, h): jnp.concatenate(
        [uw[i, h][:, HEAD_DIM:], q_ref[h // rep, rows[i], :] * egc[i, h]], axis=0).astype(BF16)
        for i, h in units}
    qkd = {(i, h): (kq[i, h // rep][c:] * decay[i, h]).astype(BF16) for i, h in units}

    heads = range(V_HEADS)
    for i in range(n):
        s_old = [state_load(i, h) for h in heads]
        ws = [_dot(wq[i, h], s_old[h]) for h in heads]
        v_new = [uw[i, h][:, :HEAD_DIM] - ws[h][:c] for h in heads]
        o = [ws[h][c:] + _dot(qkd[i, h], v_new[h]) for h in heads]
        for h in heads:
            k_dec_t = kts[i, h // rep] * etail_t[i][h:h + 1, :]
            state_store(i, h, s_old[h] * egcum[i][c - 1:c, h:h + 1] + _dot(k_dec_t, v_new[h]))
        for h in heads:
            og_ref[rows[i], h * HEAD_DIM:(h + 1) * HEAD_DIM] = (
                _rms(o[h], onorm) * _silu(z_ref[h, rows[i], :])).astype(BF16)


def _cumsum_rows(g, c):
    ri = lax.broadcasted_iota(jnp.int32, (c, c), 0)
    ci = lax.broadcasted_iota(jnp.int32, (c, c), 1)
    tri = (ri >= ci).astype(F32)
    return jnp.dot(tri, g, preferred_element_type=F32, precision=lax.Precision.HIGHEST)


def _gdn_scan_prompt_kernel(q_ref, k_ref, v_ref, z_ref, beta_ref, g_ref, onorm_ref,
                            og_ref, s_ref, *, c, n_chunk):
    @pl.when(pl.program_id(1) == 0)
    def _():
        s_ref[...] = jnp.zeros(s_ref.shape, F32)

    def load(i, hh):
        return s_ref[0, hh]

    def store(i, hh, val):
        s_ref[0, hh] = val

    def body(it, carry):
        r0 = pl.multiple_of(it * (SCAN_GROUP * c), SCAN_GROUP * c)
        rows = [pl.ds(r0 + i * c, c) for i in range(SCAN_GROUP)]
        _delta_chunks(q_ref, k_ref, v_ref, z_ref, beta_ref, g_ref, og_ref, onorm_ref[...], rows,
                      load, store, c)
        return carry

    lax.fori_loop(0, n_chunk // SCAN_GROUP, body, 0)


def _gdn_scan_sample_kernel(q_ref, k_ref, v_ref, z_ref, beta_ref, g_ref, onorm_ref, s0_ref,
                            og_ref, s_ref, *, tb, c):
    for b0 in range(0, tb, SAMPLE_SCAN_GROUP):
        def load(i, hh, b0=b0):
            return s0_ref[b0 + i, hh]

        def store(i, hh, val, b0=b0):
            s_ref[b0 + i, hh] = val

        rows = [pl.ds((b0 + i) * c, c) for i in range(SAMPLE_SCAN_GROUP)]
        _delta_chunks(q_ref, k_ref, v_ref, z_ref, beta_ref, g_ref, og_ref, onorm_ref[...], rows,
                      load, store, c)


def _gdn_scan_prompt(q, k, v, z, beta, g, onorm, b, l):
    c = PROMPT_CHUNK
    tm = SCAN_TILE
    nc = l // tm
    tok = lambda w: pl.BlockSpec((tm, w), lambda i, j: (i * nc + j, 0))
    return pl.pallas_call(
        functools.partial(_gdn_scan_prompt_kernel, c=c, n_chunk=tm // c),
        grid=(b, nc),
        in_specs=_gdn_pre_out_specs(tm, lambda i, j: i * nc + j) + [_const_spec(onorm.shape)],
        out_specs=[tok(V_DIM),
                   pl.BlockSpec((1, V_HEADS, HEAD_DIM, HEAD_DIM), lambda i, j: (i, 0, 0, 0))],
        out_shape=[jax.ShapeDtypeStruct((b * l, V_DIM), BF16),
                   jax.ShapeDtypeStruct((b, V_HEADS, HEAD_DIM, HEAD_DIM), F32)],
        compiler_params=_params("arbitrary", "arbitrary"),
        name="gdn_scan_prompt",
    )(q, k, v, z, beta, g, onorm)


def _gdn_scan_sample(q, k, v, z, beta, g, onorm, s0, b, l):
    tb = SAMPLE_SCAN_BTILE
    tok = lambda w: pl.BlockSpec((tb * l, w), lambda i: (i, 0))
    st = pl.BlockSpec((tb, V_HEADS, HEAD_DIM, HEAD_DIM), lambda i: (i, 0, 0, 0))
    return pl.pallas_call(
        functools.partial(_gdn_scan_sample_kernel, tb=tb, c=l),
        grid=(b // tb,),
        in_specs=_gdn_pre_out_specs(tb * l, lambda i: i) + [_const_spec(onorm.shape), st],
        out_specs=[tok(V_DIM), st],
        out_shape=[jax.ShapeDtypeStruct((b * l, V_DIM), BF16),
                   jax.ShapeDtypeStruct((b, V_HEADS, HEAD_DIM, HEAD_DIM), F32)],
        compiler_params=_params("arbitrary"),
        name="gdn_scan_sample",
    )(q, k, v, z, beta, g, onorm, s0)


def _gdn_post_kernel(og_ref, x_ref, gains_ref, wo_ref, win_ref, wout_ref, y_ref):
    m = jnp.dot(og_ref[...], wo_ref[...], preferred_element_type=F32)
    y_ref[...] = _residual_ffn(x_ref[...], m, gains_ref[0:1, :], gains_ref[1:2, :],
                               gains_ref[2:3, :], win_ref, wout_ref)


def _gdn_post(og, x, gains, wo, win, wout, layer):
    n, d = x.shape
    tm = PROMPT_TILE
    return pl.pallas_call(
        _gdn_post_kernel,
        grid=(n // tm,),
        in_specs=[pl.BlockSpec((tm, V_DIM), lambda i: (i, 0)),
                  pl.BlockSpec((tm, d), lambda i: (i, 0))]
        + [_const_spec(gains.shape), _const_spec(wo.shape), _layer_spec(win, layer),
           _layer_spec(wout, layer)],
        out_specs=pl.BlockSpec((tm, d), lambda i: (i, 0)),
        out_shape=jax.ShapeDtypeStruct((n, d), F32),
        compiler_params=_params("arbitrary"),
        name="gdn_post",
    )(og, x, gains, wo, win, wout)


def _head_lanes(vec):
    return jnp.pad(vec.astype(F32), (0, LANES - V_HEADS)).reshape(1, LANES)


def kernel(x_prompt, x_sample, state_pool, state_gdn_conv, state_gdn_rec, norm_mix_pre,
           norm_mix_post, norm_ffn_pre, norm_ffn_post, pool_w, pool_scale, gdn_w_in,
           gdn_conv_w, gdn_a_log, gdn_dt_bias, gdn_o_norm, gdn_w_out, ffn_w_in, ffn_w_out):
    bp, lp, d = x_prompt.shape
    bs, ls, _ = x_sample.shape

    gains0 = jnp.stack([norm_mix_pre[0], norm_mix_post[0], norm_ffn_pre[0], norm_ffn_post[0]])
    gains1 = jnp.stack([norm_mix_post[1], norm_ffn_pre[1], norm_ffn_post[1]])
    gain1_pre = norm_mix_pre[1].reshape(1, d)
    pw = pool_w[0].astype(BF16)
    ps = pool_scale[0].reshape(1, d)
    win, wout = ffn_w_in.astype(BF16), ffn_w_out.astype(BF16)
    w_in = gdn_w_in[0]
    wqkvz = w_in.astype(BF16)
    w_b = w_in[:, CONV_DIM + V_DIM:CONV_DIM + V_DIM + V_HEADS]
    w_a = w_in[:, CONV_DIM + V_DIM + V_HEADS:]
    lane_pad = ((0, 0), (0, LANES - V_HEADS))
    wba = jnp.concatenate([jnp.pad(w_b, lane_pad), jnp.pad(w_a, lane_pad)], axis=1).astype(BF16)
    cw = gdn_conv_w[0]
    alog, dtb = _head_lanes(gdn_a_log[0]), _head_lanes(gdn_dt_bias[0])
    onorm = gdn_o_norm[0].reshape(1, HEAD_DIM)
    wo = gdn_w_out[0].astype(BF16)

    xp1, pool_p = _pool_layer_prompt(x_prompt, gains0, pw, ps, win, wout, 0)
    time_major = lambda a: jnp.transpose(a, (1, 0, 2))
    xs1, pool_s = _pool_layer_sample(x_sample, time_major(state_pool[0]), gains0, pw, ps, win,
                                     wout, 0)

    qp, kp, vp, zp, betap, gp, conv_p = _gdn_pre_prompt(xp1, gain1_pre, wqkvz, wba, cw, alog, dtb)
    qs, ks, vs, zs, betas, gs, conv_s = _gdn_pre_sample(xs1, time_major(state_gdn_conv[0]),
                                                        gain1_pre, wqkvz, wba, cw, alog, dtb)

    ogp, rec_p = _gdn_scan_prompt(qp, kp, vp, zp, betap, gp, onorm, bp, lp)
    ogs, rec_s = _gdn_scan_sample(qs, ks, vs, zs, betas, gs, onorm, state_gdn_rec[0], bs, ls)

    yp = _gdn_post(ogp, xp1.reshape(bp * lp, d), gains1, wo, win, wout, 1).reshape(bp, lp, d)
    ys = _gdn_post(ogs, xs1.reshape(bs * ls, d), gains1, wo, win, wout, 1).reshape(bs, ls, d)

    return (yp, ys, pool_p[None], time_major(pool_s)[None], conv_p[None],
            time_major(conv_s)[None], rec_p[None], rec_s[None])
```

```python
import functools

import jax
import jax.numpy as jnp
from jax import lax
from jax.experimental import pallas as pl
from jax.experimental.pallas import tpu as pltpu

D_MODEL = 1024
POOL_WINDOWS = (2, 4, 8, 16)
POOL_GROUP_DIM = D_MODEL // len(POOL_WINDOWS)
POOL_BUF = max(POOL_WINDOWS) - 1
K_HEADS = 8
V_HEADS = 16
HEAD_DIM = 128
QK_DIM = K_HEADS * HEAD_DIM
V_DIM = V_HEADS * HEAD_DIM
CONV_DIM = 2 * QK_DIM + V_DIM
CONV_WIDTH = 4
D_FF = 2816
EPS = 1e-6

F32 = jnp.float32
BF16 = jnp.bfloat16

SUBLANES = 8
LANES = 128
POOL_HALO = 16
CONV_HALO = SUBLANES
VMEM_LIMIT = 56 * 1024 * 1024
ROW_STRIDE = 4
ROW_GROUP = SUBLANES * ROW_STRIDE

PROMPT_TILE = 512
GDN_PRE_TILE = 256
PRE_SLABS = 8
SAMPLE_BTILE = 32
PROMPT_CHUNK = 64
SCAN_TILE = 256
SCAN_GROUP = 2
SAMPLE_SCAN_BTILE = 8
SAMPLE_SCAN_GROUP = 4


def _rms(x, gain):
    ms = jnp.mean(x * x, axis=-1, keepdims=True)
    return x * lax.rsqrt(ms + EPS) * gain


def _sigmoid(x):
    return 1.0 / (1.0 + jnp.exp(-x))


def _silu(x):
    return x * _sigmoid(x)


def _softplus(x):
    return jnp.maximum(x, 0.0) + jnp.log1p(jnp.exp(-jnp.abs(x)))


def _dot(a, b):
    return jnp.dot(a.astype(BF16), b.astype(BF16), preferred_element_type=F32)


def _const_spec(shape):
    nd = len(shape)
    return pl.BlockSpec(shape, lambda *_: (0,) * nd, pipeline_mode=pl.Buffered(1))


def _layer_spec(stacked, layer):
    nd = stacked.ndim - 1
    return pl.BlockSpec((None,) + stacked.shape[1:], lambda *_: (layer,) + (0,) * nd,
                        pipeline_mode=pl.Buffered(1))


def _params(*sem):
    return pltpu.CompilerParams(dimension_semantics=sem, vmem_limit_bytes=VMEM_LIMIT)


def _residual_ffn(x, m, g_post, g_fpre, g_fpost, win_ref, wout_ref):
    x1 = x + _rms(m, g_post)
    h = _rms(x1, g_fpre).astype(BF16)
    gate = jnp.dot(h, win_ref[:, :D_FF], preferred_element_type=F32)
    up = jnp.dot(h, win_ref[:, D_FF:], preferred_element_type=F32)
    act = (_silu(gate) * up).astype(BF16)
    f = jnp.dot(act, wout_ref[...], preferred_element_type=F32)
    return x1 + _rms(f, g_fpost)


def _pool_project(diffs, pw_ref, scale):
    parts = [_dot(d, pw_ref[gi]) for gi, d in enumerate(diffs)]
    return jnp.concatenate(parts, axis=-1) * scale


def _strided_rows(first):
    return pl.ds(first, SUBLANES, stride=ROW_STRIDE)


def _pool_layer_prompt_kernel(x_ref, gains_ref, pw_ref, ps_ref, win_ref, wout_ref,
                              y_ref, pool_ref, hp_ref, d_ref, x1_s, h2_s, *, tm, nj, n_tile):
    t = pl.program_id(0)
    wr = lax.rem(t, 2)
    rd = 1 - wr
    jp = lax.rem(jnp.minimum(t, n_tile - 1), nj)
    n_blk = D_MODEL // LANES
    blk_per_grp = POOL_GROUP_DIM // LANES

    @pl.when(t == 0)
    def _():
        x1_s[1] = jnp.zeros(x1_s.shape[1:], F32)
        h2_s[1] = jnp.zeros(h2_s.shape[1:], BF16)

    @pl.when(jp == 0)
    def _():
        hp_ref[:, 0:POOL_HALO, :] = jnp.zeros((n_blk, POOL_HALO, LANES), F32)

    h2 = h2_s[rd]
    gate = jnp.dot(h2, win_ref[:, :D_FF], preferred_element_type=F32)

    x = x_ref[0]
    h = _rms(x, gains_ref[0:1, :])
    for cb in range(n_blk):
        hp_ref[cb, POOL_HALO:POOL_HALO + tm, :] = h[:, cb * LANES:(cb + 1) * LANES]
    t_tile = lax.broadcasted_iota(jnp.int32, (SUBLANES, 1), 0) * ROW_STRIDE + (jp * tm + 1)
    for row0 in range(0, tm, ROW_GROUP):
        for gi, win in enumerate(POOL_WINDOWS):
            invs = [1.0 / jnp.minimum(win, t_tile + (row0 + r)).astype(F32)
                    for r in range(ROW_STRIDE)]
            for cb in range(gi * blk_per_grp, (gi + 1) * blk_per_grp):
                tiles = [hp_ref[cb, _strided_rows(row0 + POOL_HALO - (win - 1) + k), :]
                         for k in range(win + ROW_STRIDE - 1)]
                shared = range(ROW_STRIDE - 1, win)
                common = (functools.reduce(lambda a, b: a + b, [tiles[k] for k in shared])
                          if len(shared) > 1 else None)
                for r in range(ROW_STRIDE):
                    own = [k for k in range(r, r + win) if common is None or k not in shared]
                    tot = tiles[own[0]] if common is None else common + tiles[own[0]]
                    for k in own[1:]:
                        tot = tot + tiles[k]
                    cur = tiles[r + win - 1]
                    d_ref[cb, _strided_rows(row0 + r), :] = tot * invs[r] - cur
    diffs = [jnp.concatenate([d_ref[cb] for cb in range(gi * blk_per_grp, (gi + 1) * blk_per_grp)],
                             axis=1) for gi in range(len(POOL_WINDOWS))]
    m = _pool_project(diffs, pw_ref, ps_ref[...])
    x1_new = x + _rms(m, gains_ref[1:2, :])
    x1_s[wr] = x1_new
    h2_s[wr] = _rms(x1_new, gains_ref[2:3, :]).astype(BF16)

    up = jnp.dot(h2, win_ref[:, D_FF:], preferred_element_type=F32)
    act = (_silu(gate) * up).astype(BF16)
    f = jnp.dot(act, wout_ref[...], preferred_element_type=F32)
    y_ref[0] = x1_s[rd] + _rms(f, gains_ref[3:4, :])

    @pl.when((jp == nj - 1) & (t < n_tile))
    def _():
        for cb in range(n_blk):
            pool_ref[0, :, cb * LANES:(cb + 1) * LANES] = hp_ref[
                cb, tm + POOL_HALO - POOL_BUF:tm + POOL_HALO, :]

    hp_ref[:, 0:POOL_HALO, :] = hp_ref[:, tm:tm + POOL_HALO, :]


def _pool_layer_sample_kernel(x_ref, buf_ref, gains_ref, pw_ref, ps_ref, win_ref, wout_ref,
                              y_ref, pool_ref, hs_ref, d_ref, *, tb, seq, n_past):
    m_rows = tb * seq
    x = x_ref[...].reshape(m_rows, D_MODEL)
    h = _rms(x, gains_ref[0:1, :])
    n_blk = D_MODEL // LANES
    blk_per_grp = POOL_GROUP_DIM // LANES
    at_time = lambda t: pl.ds(t, tb, stride=seq)
    for cb in range(n_blk):
        lanes = slice(cb * LANES, (cb + 1) * LANES)
        hs_ref[cb] = h[:, lanes]
        win = POOL_WINDOWS[cb // blk_per_grp]
        hist = [buf_ref[s, :, lanes] for s in range(n_past)]
        hist += [hs_ref[cb, at_time(t), :] for t in range(seq)]
        for t in range(seq):
            tot = hist[n_past + t]
            for s in range(1, win):
                tot = tot + hist[n_past + t - s]
            d_ref[cb, at_time(t), :] = tot * (1.0 / min(win, t + 1 + n_past)) - hist[n_past + t]
        for s in range(n_past):
            pool_ref[s, :, lanes] = hist[seq + s]
    diffs = [jnp.concatenate([d_ref[cb] for cb in range(gi * blk_per_grp, (gi + 1) * blk_per_grp)],
                             axis=1) for gi in range(len(POOL_WINDOWS))]
    m = _pool_project(diffs, pw_ref, ps_ref[...])
    y = _residual_ffn(x, m, gains_ref[1:2, :], gains_ref[2:3, :], gains_ref[3:4, :],
                      win_ref, wout_ref)
    y_ref[...] = y.reshape(tb, seq, D_MODEL)


def _pool_layer_prompt(x, gains, pw, ps, win, wout, layer):
    b, l, d = x.shape
    tm = PROMPT_TILE
    nj = l // tm
    n_tile = b * nj
    mixed = lambda t: jnp.minimum(t, n_tile - 1)
    done = lambda t: jnp.maximum(t - 1, 0)
    return pl.pallas_call(
        functools.partial(_pool_layer_prompt_kernel, tm=tm, nj=nj, n_tile=n_tile),
        grid=(n_tile + 1,),
        in_specs=[
            pl.BlockSpec((1, tm, d), lambda t: (mixed(t) // nj, mixed(t) % nj, 0)),
            _const_spec(gains.shape), _const_spec(pw.shape), _const_spec(ps.shape),
            _layer_spec(win, layer), _layer_spec(wout, layer),
        ],
        out_specs=[
            pl.BlockSpec((1, tm, d), lambda t: (done(t) // nj, done(t) % nj, 0)),
            pl.BlockSpec((1, POOL_BUF, d), lambda t: (mixed(t) // nj, 0, 0)),
        ],
        out_shape=[jax.ShapeDtypeStruct((b, l, d), F32),
                   jax.ShapeDtypeStruct((b, POOL_BUF, d), F32)],
        scratch_shapes=[pltpu.VMEM((d // LANES, POOL_HALO + tm, LANES), F32),
                        pltpu.VMEM((d // LANES, tm, LANES), F32),
                        pltpu.VMEM((2, tm, d), F32), pltpu.VMEM((2, tm, d), BF16)],
        compiler_params=_params("arbitrary"),
        name="pool_layer_prompt",
    )(x, gains, pw, ps, win, wout)


def _pool_layer_sample(x, buf, gains, pw, ps, win, wout, layer):
    b, l, d = x.shape
    tb = SAMPLE_BTILE
    n_past = buf.shape[0]
    assert n_past == POOL_BUF
    return pl.pallas_call(
        functools.partial(_pool_layer_sample_kernel, tb=tb, seq=l, n_past=n_past),
        grid=(b // tb,),
        in_specs=[
            pl.BlockSpec((tb, l, d), lambda i: (i, 0, 0)),
            pl.BlockSpec((POOL_BUF, tb, d), lambda i: (0, i, 0)),
            _const_spec(gains.shape), _const_spec(pw.shape), _const_spec(ps.shape),
            _layer_spec(win, layer), _layer_spec(wout, layer),
        ],
        out_specs=[
            pl.BlockSpec((tb, l, d), lambda i: (i, 0, 0)),
            pl.BlockSpec((POOL_BUF, tb, d), lambda i: (0, i, 0)),
        ],
        out_shape=[jax.ShapeDtypeStruct((b, l, d), F32),
                   jax.ShapeDtypeStruct((POOL_BUF, b, d), F32)],
        scratch_shapes=[pltpu.VMEM((d // LANES, tb * l, LANES), F32),
                        pltpu.VMEM((d // LANES, tb * l, LANES), F32)],
        compiler_params=_params("arbitrary"),
        name="pool_layer_sample",
    )(x, buf, gains, pw, ps, win, wout)


def _gdn_qkv_slab(sl, conv, q_ref, k_ref, v_ref, rows):
    if sl >= 2 * K_HEADS:
        v_ref[sl - 2 * K_HEADS, rows, :] = conv
        return
    unit = conv * lax.rsqrt(jnp.sum(conv * conv, axis=-1, keepdims=True) + EPS)
    if sl < K_HEADS:
        q_ref[sl, rows, :] = unit * (HEAD_DIM ** -0.5)
    else:
        k_ref[sl - K_HEADS, rows, :] = unit


def _gdn_gates(ba, alog, dtb, beta_ref, g_ref):
    beta_ref[...] = _sigmoid(ba[:, :LANES])
    g_ref[...] = -jnp.exp(alog) * _softplus(ba[:, LANES:] + dtb)


def _gdn_pre_prompt_kernel(x_ref, gain_ref, wqkvz_ref, wba_ref, cw_ref, alog_ref, dtb_ref,
                           q_ref, k_ref, v_ref, z_ref, beta_ref, g_ref, conv_ref, up_ref, *, tm):
    j = pl.program_id(1)
    h = _rms(x_ref[0], gain_ref[...]).astype(BF16)
    n_slab = CONV_DIM // HEAD_DIM
    n_grp = n_slab // PRE_SLABS

    @pl.when(j == 0)
    def _():
        up_ref[:, 0:CONV_HALO, :] = jnp.zeros((n_slab, CONV_HALO, HEAD_DIM), F32)

    def project(grp):
        c0 = grp * PRE_SLABS * HEAD_DIM
        p = jnp.dot(h, wqkvz_ref[:, c0:c0 + PRE_SLABS * HEAD_DIM], preferred_element_type=F32)
        for t in range(PRE_SLABS):
            up_ref[grp * PRE_SLABS + t, CONV_HALO:CONV_HALO + tm, :] = (
                p[:, t * HEAD_DIM:(t + 1) * HEAD_DIM])

    def project_z(half):
        c0 = CONV_DIM + half * (V_DIM // 2)
        p = jnp.dot(h, wqkvz_ref[:, c0:c0 + V_DIM // 2], preferred_element_type=F32)
        for t in range(V_HEADS // 2):
            z_ref[half * (V_HEADS // 2) + t] = p[:, t * HEAD_DIM:(t + 1) * HEAD_DIM]

    base = CONV_HALO - (CONV_WIDTH - 1)

    def convolve(grp):
        for sl in range(grp * PRE_SLABS, (grp + 1) * PRE_SLABS):
            lanes = slice(sl * HEAD_DIM, (sl + 1) * HEAD_DIM)
            taps = [cw_ref[tap:tap + 1, lanes] for tap in range(CONV_WIDTH)]
            for row0 in range(0, tm, ROW_GROUP):
                tiles = [up_ref[sl, _strided_rows(row0 + base + s), :]
                         for s in range(ROW_STRIDE + CONV_WIDTH - 1)]
                for r in range(ROW_STRIDE):
                    acc = tiles[r] * taps[0]
                    for tap in range(1, CONV_WIDTH):
                        acc = acc + tiles[r + tap] * taps[tap]
                    _gdn_qkv_slab(sl, _silu(acc), q_ref, k_ref, v_ref, _strided_rows(row0 + r))

    project(0)
    for grp in range(n_grp):
        if grp + 1 < n_grp:
            project(grp + 1)
        else:
            project_z(0)
        convolve(grp)
    project_z(1)
    ba = jnp.dot(h, wba_ref[...], preferred_element_type=F32)
    _gdn_gates(ba, alog_ref[...], dtb_ref[...], beta_ref, g_ref)

    @pl.when(j == pl.num_programs(1) - 1)
    def _():
        for sl in range(n_slab):
            conv_ref[0, :, sl * HEAD_DIM:(sl + 1) * HEAD_DIM] = up_ref[
                sl, tm + CONV_HALO - (CONV_WIDTH - 1):tm + CONV_HALO, :]

    up_ref[:, 0:CONV_HALO, :] = up_ref[:, tm:tm + CONV_HALO, :]


def _gdn_pre_sample_kernel(x_ref, buf_ref, gain_ref, wqkvz_ref, wba_ref, cw_ref, alog_ref,
                           dtb_ref, q_ref, k_ref, v_ref, z_ref, beta_ref, g_ref, conv_ref, up_ref,
                           *, tb, seq):
    m = tb * seq
    n_buf = CONV_WIDTH - 1
    h = _rms(x_ref[...].reshape(m, D_MODEL), gain_ref[...]).astype(BF16)
    proj = jnp.dot(h, wqkvz_ref[:, :CONV_DIM + V_DIM], preferred_element_type=F32)
    ba = jnp.dot(h, wba_ref[...], preferred_element_type=F32)
    for hh in range(V_HEADS):
        z_ref[hh] = proj[:, CONV_DIM + hh * HEAD_DIM:CONV_DIM + (hh + 1) * HEAD_DIM]
    _gdn_gates(ba, alog_ref[...], dtb_ref[...], beta_ref, g_ref)
    at_time = lambda t: pl.ds(t, tb, stride=seq)
    for sl in range(CONV_DIM // HEAD_DIM):
        lanes = slice(sl * HEAD_DIM, (sl + 1) * HEAD_DIM)
        up_ref[sl] = proj[:, lanes]
        ups = [buf_ref[s, :, lanes] for s in range(n_buf)]
        ups += [up_ref[sl, at_time(t), :] for t in range(seq)]
        for t in range(seq):
            acc = ups[t] * cw_ref[0:1, lanes]
            for tap in range(1, CONV_WIDTH):
                acc = acc + ups[t + tap] * cw_ref[tap:tap + 1, lanes]
            _gdn_qkv_slab(sl, _silu(acc), q_ref, k_ref, v_ref, at_time(t))
        for s in range(n_buf):
            conv_ref[s, :, lanes] = ups[seq + s]


def _gdn_pre_out_shapes(n):
    return [jax.ShapeDtypeStruct((K_HEADS, n, HEAD_DIM), F32),
            jax.ShapeDtypeStruct((K_HEADS, n, HEAD_DIM), F32),
            jax.ShapeDtypeStruct((V_HEADS, n, HEAD_DIM), F32),
            jax.ShapeDtypeStruct((V_HEADS, n, HEAD_DIM), F32),
            jax.ShapeDtypeStruct((n, LANES), F32), jax.ShapeDtypeStruct((n, LANES), F32)]


def _gdn_pre_out_specs(rows, index):
    heads = lambda n: pl.BlockSpec((n, rows, HEAD_DIM), lambda *g: (0, index(*g), 0))
    lane = pl.BlockSpec((rows, LANES), lambda *g: (index(*g), 0))
    return [heads(K_HEADS), heads(K_HEADS), heads(V_HEADS), heads(V_HEADS), lane, lane]


def _gdn_pre_prompt(x, gain, wqkvz, wba, cw, alog, dtb):
    b, l, d = x.shape
    tm = GDN_PRE_TILE
    nj = l // tm
    return pl.pallas_call(
        functools.partial(_gdn_pre_prompt_kernel, tm=tm),
        grid=(b, nj),
        in_specs=[pl.BlockSpec((1, tm, d), lambda i, j: (i, j, 0))]
        + [_const_spec(a.shape) for a in (gain, wqkvz, wba, cw, alog, dtb)],
        out_specs=_gdn_pre_out_specs(tm, lambda i, j: i * nj + j)
        + [pl.BlockSpec((1, CONV_WIDTH - 1, CONV_DIM), lambda i, j: (i, 0, 0))],
        out_shape=_gdn_pre_out_shapes(b * l)
        + [jax.ShapeDtypeStruct((b, CONV_WIDTH - 1, CONV_DIM), F32)],
        scratch_shapes=[pltpu.VMEM((CONV_DIM // HEAD_DIM, CONV_HALO + tm, HEAD_DIM), F32)],
        compiler_params=_params("arbitrary", "arbitrary"),
        name="gdn_pre_prompt",
    )(x, gain, wqkvz, wba, cw, alog, dtb)


def _gdn_pre_sample(x, buf, gain, wqkvz, wba, cw, alog, dtb):
    b, l, d = x.shape
    tb = SAMPLE_BTILE
    state = pl.BlockSpec((CONV_WIDTH - 1, tb, CONV_DIM), lambda i: (0, i, 0))
    return pl.pallas_call(
        functools.partial(_gdn_pre_sample_kernel, tb=tb, seq=l),
        grid=(b // tb,),
        in_specs=[pl.BlockSpec((tb, l, d), lambda i: (i, 0, 0)), state]
        + [_const_spec(a.shape) for a in (gain, wqkvz, wba, cw, alog, dtb)],
        out_specs=_gdn_pre_out_specs(tb * l, lambda i: i) + [state],
        out_shape=_gdn_pre_out_shapes(b * l)
        + [jax.ShapeDtypeStruct((CONV_WIDTH - 1, b, CONV_DIM), F32)],
        scratch_shapes=[pltpu.VMEM((CONV_DIM // HEAD_DIM, tb * l, HEAD_DIM), F32)],
        compiler_params=_params("arbitrary"),
        name="gdn_pre_sample",
    )(x, buf, gain, wqkvz, wba, cw, alog, dtb)


def _unit_lower_inverses(mats, c):
    ri = lax.broadcasted_iota(jnp.int32, (c, c), 0)
    ci = lax.broadcasted_iota(jnp.int32, (c, c), 1)
    eye = (ri == ci).astype(F32)
    pair = ((ri // 2) == (ci // 2)) & (ri > ci)
    xs = [eye - jnp.where(pair, a, 0.0) for a in mats]
    mats = [a.astype(BF16) for a in mats]
    blk = 2
    while blk < c:
        off = ((ri // (2 * blk)) == (ci // (2 * blk))) & ((ri // blk) > (ci // blk))
        xbs = [x.astype(BF16) for x in xs]
        ys = [_dot(jnp.where(off, a, jnp.zeros_like(a)), xb) for a, xb in zip(mats, xbs)]
        xs = [x - _dot(xb, y) for x, xb, y in zip(xs, xbs, ys)]
        blk *= 2
    return xs


def _delta_chunks(q_ref, k_ref, v_ref, z_ref, beta_ref, g_ref, og_ref, onorm, rows,
                  state_load, state_store, c):
    n = len(rows)
    ri = lax.broadcasted_iota(jnp.int32, (c, c), 0)
    ci = lax.broadcasted_iota(jnp.int32, (c, c), 1)
    causal = ri >= ci
    strict = ri > ci
    rep = V_HEADS // K_HEADS
    units = [(i, h) for i in range(n) for h in range(V_HEADS)]
    kunits = [(i, j) for i in range(n) for j in range(K_HEADS)]

    gcum = [_cumsum_rows(g_ref[rows[i], :], c) for i in range(n)]
    gcum_t = [x.T for x in gcum]
    egcum = [jnp.exp(x) for x in gcum]
    etail_t = [x[:, c - 1:c] - x for x in gcum_t]
    etail_t = [jnp.exp(x) for x in etail_t]
    beta = [beta_ref[rows[i], :] for i in range(n)]
    ks = {(i, j): k_ref[j, rows[i], :] for i, j in kunits}
    kts = {u: ks[u].T for u in kunits}
    kq = {(i, j): _dot(jnp.concatenate([ks[i, j], q_ref[j, rows[i], :]], axis=0), kts[i, j])
          for i, j in kunits}
    gcol = {(i, h): gcum[i][:, h:h + 1] for i, h in units}
    bcol = {(i, h): beta[i][:, h:h + 1] for i, h in units}
    egc = {(i, h): egcum[i][:, h:h + 1] for i, h in units}
    decay = {(i, h): jnp.where(
        causal, jnp.exp(jnp.minimum(gcol[i, h] - gcum_t[i][h:h + 1, :], 0.0)), 0.0)
        for i, h in units}
    a_mats = [jnp.where(strict, kq[i, h // rep][:c] * bcol[i, h] * decay[i, h], 0.0)
              for i, h in units]
    t_inv = dict(zip(units, _unit_lower_inverses(a_mats, c)))
    uw = {(i, h): _dot(t_inv[i, h], jnp.concatenate(
        [v_ref[h, rows[i], :] * bcol[i, h], ks[i, h // rep] * (bcol[i, h] * egc[i, h])], axis=1))
        for i, h in units}
    wq = {(i, h): jnp.concatenate(
        [uw[i, h][:, HEAD_DIM:], q_ref[h // rep, rows[i], :] * egc[i, h]], axis=0).astype(BF16)
        for i, h in units}
    qkd = {(i, h): (kq[i, h // rep][c:] * decay[i, h]).astype(BF16) for i, h in units}

    heads = range(V_HEADS)
    for i in range(n):
        s_old = [state_load(i, h) for h in heads]
        ws = [_dot(wq[i, h], s_old[h]) for h in heads]
        v_new = [uw[i, h][:, :HEAD_DIM] - ws[h][:c] for h in heads]
        o = [ws[h][c:] + _dot(qkd[i, h], v_new[h]) for h in heads]
        for h in heads:
            k_dec_t = kts[i, h // rep] * etail_t[i][h:h + 1, :]
            state_store(i, h, s_old[h] * egcum[i][c - 1:c, h:h + 1] + _dot(k_dec_t, v_new[h]))
        for h in heads:
            og_ref[rows[i], h * HEAD_DIM:(h + 1) * HEAD_DIM] = (
                _rms(o[h], onorm) * _silu(z_ref[h, rows[i], :])).astype(BF16)


def _cumsum_rows(g, c):
    ri = lax.broadcasted_iota(jnp.int32, (c, c), 0)
    ci = lax.broadcasted_iota(jnp.int32, (c, c), 1)
    tri = (ri >= ci).astype(BF16)
    hi = g.astype(BF16)
    rest = g - hi.astype(F32)
    mid = rest.astype(BF16)
    lo = (rest - mid.astype(F32)).astype(BF16)
    return (jnp.dot(tri, hi, preferred_element_type=F32)
            + jnp.dot(tri, mid, preferred_element_type=F32)
            + jnp.dot(tri, lo, preferred_element_type=F32))


def _gdn_scan_prompt_kernel(q_ref, k_ref, v_ref, z_ref, beta_ref, g_ref, onorm_ref,
                            og_ref, s_ref, *, c, n_chunk):
    @pl.when(pl.program_id(1) == 0)
    def _():
        s_ref[...] = jnp.zeros(s_ref.shape, F32)

    def load(i, hh):
        return s_ref[0, hh]

    def store(i, hh, val):
        s_ref[0, hh] = val

    def body(it, carry):
        r0 = pl.multiple_of(it * (SCAN_GROUP * c), SCAN_GROUP * c)
        rows = [pl.ds(r0 + i * c, c) for i in range(SCAN_GROUP)]
        _delta_chunks(q_ref, k_ref, v_ref, z_ref, beta_ref, g_ref, og_ref, onorm_ref[...], rows,
                      load, store, c)
        return carry

    lax.fori_loop(0, n_chunk // SCAN_GROUP, body, 0)


def _gdn_scan_sample_kernel(q_ref, k_ref, v_ref, z_ref, beta_ref, g_ref, onorm_ref, s0_ref,
                            og_ref, s_ref, *, tb, c):
    for b0 in range(0, tb, SAMPLE_SCAN_GROUP):
        def load(i, hh, b0=b0):
            return s0_ref[b0 + i, hh]

        def store(i, hh, val, b0=b0):
            s_ref[b0 + i, hh] = val

        rows = [pl.ds((b0 + i) * c, c) for i in range(SAMPLE_SCAN_GROUP)]
        _delta_chunks(q_ref, k_ref, v_ref, z_ref, beta_ref, g_ref, og_ref, onorm_ref[...], rows,
                      load, store, c)


def _gdn_scan_prompt(q, k, v, z, beta, g, onorm, b, l):
    c = PROMPT_CHUNK
    tm = SCAN_TILE
    nc = l // tm
    tok = lambda w: pl.BlockSpec((tm, w), lambda i, j: (i * nc + j, 0))
    return pl.pallas_call(
        functools.partial(_gdn_scan_prompt_kernel, c=c, n_chunk=tm // c),
        grid=(b, nc),
        in_specs=_gdn_pre_out_specs(tm, lambda i, j: i * nc + j) + [_const_spec(onorm.shape)],
        out_specs=[tok(V_DIM),
                   pl.BlockSpec((1, V_HEADS, HEAD_DIM, HEAD_DIM), lambda i, j: (i, 0, 0, 0))],
        out_shape=[jax.ShapeDtypeStruct((b * l, V_DIM), BF16),
                   jax.ShapeDtypeStruct((b, V_HEADS, HEAD_DIM, HEAD_DIM), F32)],
        compiler_params=_params("arbitrary", "arbitrary"),
        name="gdn_scan_prompt",
    )(q, k, v, z, beta, g, onorm)


def _gdn_scan_sample(q, k, v, z, beta, g, onorm, s0, b, l):
    tb = SAMPLE_SCAN_BTILE
    tok = lambda w: pl.BlockSpec((tb * l, w), lambda i: (i, 0))
    st = pl.BlockSpec((tb, V_HEADS, HEAD_DIM, HEAD_DIM), lambda i: (i, 0, 0, 0))
    return pl.pallas_call(
        functools.partial(_gdn_scan_sample_kernel, tb=tb, c=l),
        grid=(b // tb,),
        in_specs=_gdn_pre_out_specs(tb * l, lambda i: i) + [_const_spec(onorm.shape), st],
        out_specs=[tok(V_DIM), st],
        out_shape=[jax.ShapeDtypeStruct((b * l, V_DIM), BF16),
                   jax.ShapeDtypeStruct((b, V_HEADS, HEAD_DIM, HEAD_DIM), F32)],
        compiler_params=_params("arbitrary"),
        name="gdn_scan_sample",
    )(q, k, v, z, beta, g, onorm, s0)


def _gdn_post_kernel(og_ref, x_ref, gains_ref, wo_ref, win_ref, wout_ref, y_ref):
    m = jnp.dot(og_ref[...], wo_ref[...], preferred_element_type=F32)
    y_ref[...] = _residual_ffn(x_ref[...], m, gains_ref[0:1, :], gains_ref[1:2, :],
                               gains_ref[2:3, :], win_ref, wout_ref)


def _gdn_post(og, x, gains, wo, win, wout, layer):
    n, d = x.shape
    tm = PROMPT_TILE
    return pl.pallas_call(
        _gdn_post_kernel,
        grid=(n // tm,),
        in_specs=[pl.BlockSpec((tm, V_DIM), lambda i: (i, 0)),
                  pl.BlockSpec((tm, d), lambda i: (i, 0))]
        + [_const_spec(gains.shape), _const_spec(wo.shape), _layer_spec(win, layer),
           _layer_spec(wout, layer)],
        out_specs=pl.BlockSpec((tm, d), lambda i: (i, 0)),
        out_shape=jax.ShapeDtypeStruct((n, d), F32),
        compiler_params=_params("arbitrary"),
        name="gdn_post",
    )(og, x, gains, wo, win, wout)


def _head_lanes(vec):
    return jnp.pad(vec.astype(F32), (0, LANES - V_HEADS)).reshape(1, LANES)


def kernel(x_prompt, x_sample, state_pool, state_gdn_conv, state_gdn_rec, norm_mix_pre,
           norm_mix_post, norm_ffn_pre, norm_ffn_post, pool_w, pool_scale, gdn_w_in,
           gdn_conv_w, gdn_a_log, gdn_dt_bias, gdn_o_norm, gdn_w_out, ffn_w_in, ffn_w_out):
    bp, lp, d = x_prompt.shape
    bs, ls, _ = x_sample.shape

    gains0 = jnp.stack([norm_mix_pre[0], norm_mix_post[0], norm_ffn_pre[0], norm_ffn_post[0]])
    gains1 = jnp.stack([norm_mix_post[1], norm_ffn_pre[1], norm_ffn_post[1]])
    gain1_pre = norm_mix_pre[1].reshape(1, d)
    pw = pool_w[0].astype(BF16)
    ps = pool_scale[0].reshape(1, d)
    win, wout = ffn_w_in.astype(BF16), ffn_w_out.astype(BF16)
    w_in = gdn_w_in[0]
    wqkvz = w_in.astype(BF16)
    w_b = w_in[:, CONV_DIM + V_DIM:CONV_DIM + V_DIM + V_HEADS]
    w_a = w_in[:, CONV_DIM + V_DIM + V_HEADS:]
    lane_pad = ((0, 0), (0, LANES - V_HEADS))
    wba = jnp.concatenate([jnp.pad(w_b, lane_pad), jnp.pad(w_a, lane_pad)], axis=1).astype(BF16)
    cw = gdn_conv_w[0]
    alog, dtb = _head_lanes(gdn_a_log[0]), _head_lanes(gdn_dt_bias[0])
    onorm = gdn_o_norm[0].reshape(1, HEAD_DIM)
    wo = gdn_w_out[0].astype(BF16)

    xp1, pool_p = _pool_layer_prompt(x_prompt, gains0, pw, ps, win, wout, 0)
    time_major = lambda a: jnp.transpose(a, (1, 0, 2))
    xs1, pool_s = _pool_layer_sample(x_sample, time_major(state_pool[0]), gains0, pw, ps, win,
                                     wout, 0)

    qp, kp, vp, zp, betap, gp, conv_p = _gdn_pre_prompt(xp1, gain1_pre, wqkvz, wba, cw, alog, dtb)
    qs, ks, vs, zs, betas, gs, conv_s = _gdn_pre_sample(xs1, time_major(state_gdn_conv[0]),
                                                        gain1_pre, wqkvz, wba, cw, alog, dtb)

    ogp, rec_p = _gdn_scan_prompt(qp, kp, vp, zp, betap, gp, onorm, bp, lp)
    ogs, rec_s = _gdn_scan_sample(qs, ks, vs, zs, betas, gs, onorm, state_gdn_rec[0], bs, ls)

    yp = _gdn_post(ogp, xp1.reshape(bp * lp, d), gains1, wo, win, wout, 1).reshape(bp, lp, d)
    ys = _gdn_post(ogs, xs1.reshape(bs * ls, d), gains1, wo, win, wout, 1).reshape(bs, ls, d)

    return (yp, ys, pool_p[None], time_major(pool_s)[None], conv_p[None],
            time_major(conv_s)[None], rec_p[None], rec_s[None])
```

```python
import functools

import jax
import jax.numpy as jnp
from jax import lax
from jax.experimental import pallas as pl
from jax.experimental.pallas import tpu as pltpu

D_MODEL = 1024
POOL_WINDOWS = (2, 4, 8, 16)
POOL_GROUP_DIM = D_MODEL // len(POOL_WINDOWS)
POOL_BUF = max(POOL_WINDOWS) - 1
K_HEADS = 8
V_HEADS = 16
HEAD_DIM = 128
QK_DIM = K_HEADS * HEAD_DIM
V_DIM = V_HEADS * HEAD_DIM
CONV_DIM = 2 * QK_DIM + V_DIM
CONV_WIDTH = 4
D_FF = 2816
EPS = 1e-6

F32 = jnp.float32
BF16 = jnp.bfloat16

SUBLANES = 8
LANES = 128
POOL_HALO = 16
CONV_HALO = SUBLANES
VMEM_LIMIT = 56 * 1024 * 1024
ROW_STRIDE = 4
ROW_GROUP = SUBLANES * ROW_STRIDE

PROMPT_TILE = 512
GDN_PRE_TILE = 256
PRE_SLABS = 8
SAMPLE_BTILE = 32
PROMPT_CHUNK = 64
SCAN_TILE = 256
SCAN_GROUP = 2
SAMPLE_SCAN_BTILE = 8
SAMPLE_SCAN_GROUP = 4


def _rms(x, gain):
    ms = jnp.mean(x * x, axis=-1, keepdims=True)
    return x * lax.rsqrt(ms + EPS) * gain


def _sigmoid(x):
    return 1.0 / (1.0 + jnp.exp(-x))


def _silu(x):
    return x * _sigmoid(x)


def _softplus(x):
    return jnp.maximum(x, 0.0) + jnp.log1p(jnp.exp(-jnp.abs(x)))


def _dot(a, b):
    return jnp.dot(a.astype(BF16), b.astype(BF16), preferred_element_type=F32)


def _const_spec(shape):
    nd = len(shape)
    return pl.BlockSpec(shape, lambda *_: (0,) * nd, pipeline_mode=pl.Buffered(1))


def _layer_spec(stacked, layer):
    nd = stacked.ndim - 1
    return pl.BlockSpec((None,) + stacked.shape[1:], lambda *_: (layer,) + (0,) * nd,
                        pipeline_mode=pl.Buffered(1))


def _params(*sem):
    return pltpu.CompilerParams(dimension_semantics=sem, vmem_limit_bytes=VMEM_LIMIT)


def _residual_ffn(x, m, g_post, g_fpre, g_fpost, win_ref, wout_ref):
    x1 = x + _rms(m, g_post)
    h = _rms(x1, g_fpre).astype(BF16)
    gate = jnp.dot(h, win_ref[:, :D_FF], preferred_element_type=F32)
    up = jnp.dot(h, win_ref[:, D_FF:], preferred_element_type=F32)
    act = (_silu(gate) * up).astype(BF16)
    f = jnp.dot(act, wout_ref[...], preferred_element_type=F32)
    return x1 + _rms(f, g_fpost)


def _pool_project(diffs, pw_ref, scale):
    parts = [_dot(d, pw_ref[gi]) for gi, d in enumerate(diffs)]
    return jnp.concatenate(parts, axis=-1) * scale


def _strided_rows(first):
    return pl.ds(first, SUBLANES, stride=ROW_STRIDE)


def _pool_layer_prompt_kernel(x_ref, gains_ref, pw_ref, ps_ref, win_ref, wout_ref,
                              y_ref, pool_ref, hp_ref, d_ref, x1_s, h2_s, *, tm, nj, n_tile):
    t = pl.program_id(0)
    wr = lax.rem(t, 2)
    rd = 1 - wr
    jp = lax.rem(jnp.minimum(t, n_tile - 1), nj)
    n_blk = D_MODEL // LANES
    blk_per_grp = POOL_GROUP_DIM // LANES

    @pl.when(t == 0)
    def _():
        x1_s[1] = jnp.zeros(x1_s.shape[1:], F32)
        h2_s[1] = jnp.zeros(h2_s.shape[1:], BF16)

    @pl.when(jp == 0)
    def _():
        hp_ref[:, 0:POOL_HALO, :] = jnp.zeros((n_blk, POOL_HALO, LANES), F32)

    h2 = h2_s[rd]
    gate = jnp.dot(h2, win_ref[:, :D_FF], preferred_element_type=F32)

    x = x_ref[0]
    h = _rms(x, gains_ref[0:1, :])
    for cb in range(n_blk):
        hp_ref[cb, POOL_HALO:POOL_HALO + tm, :] = h[:, cb * LANES:(cb + 1) * LANES]
    t_tile = lax.broadcasted_iota(jnp.int32, (SUBLANES, 1), 0) * ROW_STRIDE + (jp * tm + 1)
    for row0 in range(0, tm, ROW_GROUP):
        for gi, win in enumerate(POOL_WINDOWS):
            invs = [1.0 / jnp.minimum(win, t_tile + (row0 + r)).astype(F32)
                    for r in range(ROW_STRIDE)]
            for cb in range(gi * blk_per_grp, (gi + 1) * blk_per_grp):
                tiles = [hp_ref[cb, _strided_rows(row0 + POOL_HALO - (win - 1) + k), :]
                         for k in range(win + ROW_STRIDE - 1)]
                shared = range(ROW_STRIDE - 1, win)
                common = (functools.reduce(lambda a, b: a + b, [tiles[k] for k in shared])
                          if len(shared) > 1 else None)
                for r in range(ROW_STRIDE):
                    own = [k for k in range(r, r + win) if common is None or k not in shared]
                    tot = tiles[own[0]] if common is None else common + tiles[own[0]]
                    for k in own[1:]:
                        tot = tot + tiles[k]
                    cur = tiles[r + win - 1]
                    d_ref[cb, _strided_rows(row0 + r), :] = tot * invs[r] - cur
    diffs = [jnp.concatenate([d_ref[cb] for cb in range(gi * blk_per_grp, (gi + 1) * blk_per_grp)],
                             axis=1) for gi in range(len(POOL_WINDOWS))]
    m = _pool_project(diffs, pw_ref, ps_ref[...])
    x1_new = x + _rms(m, gains_ref[1:2, :])
    x1_s[wr] = x1_new
    h2_s[wr] = _rms(x1_new, gains_ref[2:3, :]).astype(BF16)

    up = jnp.dot(h2, win_ref[:, D_FF:], preferred_element_type=F32)
    act = (_silu(gate) * up).astype(BF16)
    f = jnp.dot(act, wout_ref[...], preferred_element_type=F32)
    y_ref[0] = x1_s[rd] + _rms(f, gains_ref[3:4, :])

    @pl.when((jp == nj - 1) & (t < n_tile))
    def _():
        for cb in range(n_blk):
            pool_ref[0, :, cb * LANES:(cb + 1) * LANES] = hp_ref[
                cb, tm + POOL_HALO - POOL_BUF:tm + POOL_HALO, :]

    hp_ref[:, 0:POOL_HALO, :] = hp_ref[:, tm:tm + POOL_HALO, :]


def _pool_layer_sample_kernel(x_ref, buf_ref, gains_ref, pw_ref, ps_ref, win_ref, wout_ref,
                              y_ref, pool_ref, hs_ref, d_ref, *, tb, seq, n_past):
    m_rows = tb * seq
    x = x_ref[...].reshape(m_rows, D_MODEL)
    h = _rms(x, gains_ref[0:1, :])
    n_blk = D_MODEL // LANES
    blk_per_grp = POOL_GROUP_DIM // LANES
    at_time = lambda t: pl.ds(t, tb, stride=seq)
    for cb in range(n_blk):
        lanes = slice(cb * LANES, (cb + 1) * LANES)
        hs_ref[cb] = h[:, lanes]
        win = POOL_WINDOWS[cb // blk_per_grp]
        hist = [buf_ref[s, :, lanes] for s in range(n_past)]
        hist += [hs_ref[cb, at_time(t), :] for t in range(seq)]
        for t in range(seq):
            tot = hist[n_past + t]
            for s in range(1, win):
                tot = tot + hist[n_past + t - s]
            d_ref[cb, at_time(t), :] = tot * (1.0 / min(win, t + 1 + n_past)) - hist[n_past + t]
        for s in range(n_past):
            pool_ref[s, :, lanes] = hist[seq + s]
    diffs = [jnp.concatenate([d_ref[cb] for cb in range(gi * blk_per_grp, (gi + 1) * blk_per_grp)],
                             axis=1) for gi in range(len(POOL_WINDOWS))]
    m = _pool_project(diffs, pw_ref, ps_ref[...])
    y = _residual_ffn(x, m, gains_ref[1:2, :], gains_ref[2:3, :], gains_ref[3:4, :],
                      win_ref, wout_ref)
    y_ref[...] = y.reshape(tb, seq, D_MODEL)


def _pool_layer_prompt(x, gains, pw, ps, win, wout, layer):
    b, l, d = x.shape
    tm = PROMPT_TILE
    nj = l // tm
    n_tile = b * nj
    mixed = lambda t: jnp.minimum(t, n_tile - 1)
    done = lambda t: jnp.maximum(t - 1, 0)
    return pl.pallas_call(
        functools.partial(_pool_layer_prompt_kernel, tm=tm, nj=nj, n_tile=n_tile),
        grid=(n_tile + 1,),
        in_specs=[
            pl.BlockSpec((1, tm, d), lambda t: (mixed(t) // nj, mixed(t) % nj, 0)),
            _const_spec(gains.shape), _const_spec(pw.shape), _const_spec(ps.shape),
            _layer_spec(win, layer), _layer_spec(wout, layer),
        ],
        out_specs=[
            pl.BlockSpec((1, tm, d), lambda t: (done(t) // nj, done(t) % nj, 0)),
            pl.BlockSpec((1, POOL_BUF, d), lambda t: (mixed(t) // nj, 0, 0)),
        ],
        out_shape=[jax.ShapeDtypeStruct((b, l, d), F32),
                   jax.ShapeDtypeStruct((b, POOL_BUF, d), F32)],
        scratch_shapes=[pltpu.VMEM((d // LANES, POOL_HALO + tm, LANES), F32),
                        pltpu.VMEM((d // LANES, tm, LANES), F32),
                        pltpu.VMEM((2, tm, d), F32), pltpu.VMEM((2, tm, d), BF16)],
        compiler_params=_params("arbitrary"),
        name="pool_layer_prompt",
    )(x, gains, pw, ps, win, wout)


def _pool_layer_sample(x, buf, gains, pw, ps, win, wout, layer):
    b, l, d = x.shape
    tb = SAMPLE_BTILE
    n_past = buf.shape[0]
    assert n_past == POOL_BUF
    return pl.pallas_call(
        functools.partial(_pool_layer_sample_kernel, tb=tb, seq=l, n_past=n_past),
        grid=(b // tb,),
        in_specs=[
            pl.BlockSpec((tb, l, d), lambda i: (i, 0, 0)),
            pl.BlockSpec((POOL_BUF, tb, d), lambda i: (0, i, 0)),
            _const_spec(gains.shape), _const_spec(pw.shape), _const_spec(ps.shape),
            _layer_spec(win, layer), _layer_spec(wout, layer),
        ],
        out_specs=[
            pl.BlockSpec((tb, l, d), lambda i: (i, 0, 0)),
            pl.BlockSpec((POOL_BUF, tb, d), lambda i: (0, i, 0)),
        ],
        out_shape=[jax.ShapeDtypeStruct((b, l, d), F32),
                   jax.ShapeDtypeStruct((POOL_BUF, b, d), F32)],
        scratch_shapes=[pltpu.VMEM((d // LANES, tb * l, LANES), F32),
                        pltpu.VMEM((d // LANES, tb * l, LANES), F32)],
        compiler_params=_params("arbitrary"),
        name="pool_layer_sample",
    )(x, buf, gains, pw, ps, win, wout)


def _gdn_qkv_slab(sl, conv, q_ref, k_ref, v_ref, rows):
    if sl >= 2 * K_HEADS:
        v_ref[sl - 2 * K_HEADS, rows, :] = conv
        return
    unit = conv * lax.rsqrt(jnp.sum(conv * conv, axis=-1, keepdims=True) + EPS)
    if sl < K_HEADS:
        q_ref[sl, rows, :] = unit * (HEAD_DIM ** -0.5)
    else:
        k_ref[sl - K_HEADS, rows, :] = unit


def _gdn_gates(ba, alog, dtb, beta_ref, g_ref):
    beta_ref[...] = _sigmoid(ba[:, :LANES])
    g_ref[...] = -jnp.exp(alog) * _softplus(ba[:, LANES:] + dtb)


def _gdn_pre_prompt_kernel(x_ref, gain_ref, wqkvz_ref, wba_ref, cw_ref, alog_ref, dtb_ref,
                           q_ref, k_ref, v_ref, z_ref, beta_ref, g_ref, conv_ref, up_ref, *, tm):
    j = pl.program_id(1)
    h = _rms(x_ref[0], gain_ref[...]).astype(BF16)
    n_slab = CONV_DIM // HEAD_DIM
    n_grp = n_slab // PRE_SLABS

    @pl.when(j == 0)
    def _():
        up_ref[:, 0:CONV_HALO, :] = jnp.zeros((n_slab, CONV_HALO, HEAD_DIM), F32)

    def project(grp):
        c0 = grp * PRE_SLABS * HEAD_DIM
        p = jnp.dot(h, wqkvz_ref[:, c0:c0 + PRE_SLABS * HEAD_DIM], preferred_element_type=F32)
        for t in range(PRE_SLABS):
            up_ref[grp * PRE_SLABS + t, CONV_HALO:CONV_HALO + tm, :] = (
                p[:, t * HEAD_DIM:(t + 1) * HEAD_DIM])

    def project_z(half):
        c0 = CONV_DIM + half * (V_DIM // 2)
        p = jnp.dot(h, wqkvz_ref[:, c0:c0 + V_DIM // 2], preferred_element_type=F32)
        for t in range(V_HEADS // 2):
            z_ref[half * (V_HEADS // 2) + t] = p[:, t * HEAD_DIM:(t + 1) * HEAD_DIM]

    base = CONV_HALO - (CONV_WIDTH - 1)

    def convolve(grp):
        for sl in range(grp * PRE_SLABS, (grp + 1) * PRE_SLABS):
            lanes = slice(sl * HEAD_DIM, (sl + 1) * HEAD_DIM)
            taps = [cw_ref[tap:tap + 1, lanes] for tap in range(CONV_WIDTH)]
            for row0 in range(0, tm, ROW_GROUP):
                tiles = [up_ref[sl, _strided_rows(row0 + base + s), :]
                         for s in range(ROW_STRIDE + CONV_WIDTH - 1)]
                for r in range(ROW_STRIDE):
                    acc = tiles[r] * taps[0]
                    for tap in range(1, CONV_WIDTH):
                        acc = acc + tiles[r + tap] * taps[tap]
                    _gdn_qkv_slab(sl, _silu(acc), q_ref, k_ref, v_ref, _strided_rows(row0 + r))

    project(0)
    for grp in range(n_grp):
        if grp + 1 < n_grp:
            project(grp + 1)
        else:
            project_z(0)
        convolve(grp)
    project_z(1)
    ba = jnp.dot(h, wba_ref[...], preferred_element_type=F32)
    _gdn_gates(ba, alog_ref[...], dtb_ref[...], beta_ref, g_ref)

    @pl.when(j == pl.num_programs(1) - 1)
    def _():
        for sl in range(n_slab):
            conv_ref[0, :, sl * HEAD_DIM:(sl + 1) * HEAD_DIM] = up_ref[
                sl, tm + CONV_HALO - (CONV_WIDTH - 1):tm + CONV_HALO, :]

    up_ref[:, 0:CONV_HALO, :] = up_ref[:, tm:tm + CONV_HALO, :]


def _gdn_pre_sample_kernel(x_ref, buf_ref, gain_ref, wqkvz_ref, wba_ref, cw_ref, alog_ref,
                           dtb_ref, q_ref, k_ref, v_ref, z_ref, beta_ref, g_ref, conv_ref, up_ref,
                           *, tb, seq):
    m = tb * seq
    n_buf = CONV_WIDTH - 1
    h = _rms(x_ref[...].reshape(m, D_MODEL), gain_ref[...]).astype(BF16)
    proj = jnp.dot(h, wqkvz_ref[:, :CONV_DIM + V_DIM], preferred_element_type=F32)
    ba = jnp.dot(h, wba_ref[...], preferred_element_type=F32)
    for hh in range(V_HEADS):
        z_ref[hh] = proj[:, CONV_DIM + hh * HEAD_DIM:CONV_DIM + (hh + 1) * HEAD_DIM]
    _gdn_gates(ba, alog_ref[...], dtb_ref[...], beta_ref, g_ref)
    at_time = lambda t: pl.ds(t, tb, stride=seq)
    for sl in range(CONV_DIM // HEAD_DIM):
        lanes = slice(sl * HEAD_DIM, (sl + 1) * HEAD_DIM)
        up_ref[sl] = proj[:, lanes]
        ups = [buf_ref[s, :, lanes] for s in range(n_buf)]
        ups += [up_ref[sl, at_time(t), :] for t in range(seq)]
        for t in range(seq):
            acc = ups[t] * cw_ref[0:1, lanes]
            for tap in range(1, CONV_WIDTH):
                acc = acc + ups[t + tap] * cw_ref[tap:tap + 1, lanes]
            _gdn_qkv_slab(sl, _silu(acc), q_ref, k_ref, v_ref, at_time(t))
        for s in range(n_buf):
            conv_ref[s, :, lanes] = ups[seq + s]


def _gdn_pre_out_shapes(n):
    return [jax.ShapeDtypeStruct((K_HEADS, n, HEAD_DIM), F32),
            jax.ShapeDtypeStruct((K_HEADS, n, HEAD_DIM), F32),
            jax.ShapeDtypeStruct((V_HEADS, n, HEAD_DIM), F32),
            jax.ShapeDtypeStruct((V_HEADS, n, HEAD_DIM), F32),
            jax.ShapeDtypeStruct((n, LANES), F32), jax.ShapeDtypeStruct((n, LANES), F32)]


def _gdn_pre_out_specs(rows, index):
    heads = lambda n: pl.BlockSpec((n, rows, HEAD_DIM), lambda *g: (0, index(*g), 0))
    lane = pl.BlockSpec((rows, LANES), lambda *g: (index(*g), 0))
    return [heads(K_HEADS), heads(K_HEADS), heads(V_HEADS), heads(V_HEADS), lane, lane]


def _gdn_pre_prompt(x, gain, wqkvz, wba, cw, alog, dtb):
    b, l, d = x.shape
    tm = GDN_PRE_TILE
    nj = l // tm
    return pl.pallas_call(
        functools.partial(_gdn_pre_prompt_kernel, tm=tm),
        grid=(b, nj),
        in_specs=[pl.BlockSpec((1, tm, d), lambda i, j: (i, j, 0))]
        + [_const_spec(a.shape) for a in (gain, wqkvz, wba, cw, alog, dtb)],
        out_specs=_gdn_pre_out_specs(tm, lambda i, j: i * nj + j)
        + [pl.BlockSpec((1, CONV_WIDTH - 1, CONV_DIM), lambda i, j: (i, 0, 0))],
        out_shape=_gdn_pre_out_shapes(b * l)
        + [jax.ShapeDtypeStruct((b, CONV_WIDTH - 1, CONV_DIM), F32)],
        scratch_shapes=[pltpu.VMEM((CONV_DIM // HEAD_DIM, CONV_HALO + tm, HEAD_DIM), F32)],
        compiler_params=_params("arbitrary", "arbitrary"),
        name="gdn_pre_prompt",
    )(x, gain, wqkvz, wba, cw, alog, dtb)


def _gdn_pre_sample(x, buf, gain, wqkvz, wba, cw, alog, dtb):
    b, l, d = x.shape
    tb = SAMPLE_BTILE
    state = pl.BlockSpec((CONV_WIDTH - 1, tb, CONV_DIM), lambda i: (0, i, 0))
    return pl.pallas_call(
        functools.partial(_gdn_pre_sample_kernel, tb=tb, seq=l),
        grid=(b // tb,),
        in_specs=[pl.BlockSpec((tb, l, d), lambda i: (i, 0, 0)), state]
        + [_const_spec(a.shape) for a in (gain, wqkvz, wba, cw, alog, dtb)],
        out_specs=_gdn_pre_out_specs(tb * l, lambda i: i) + [state],
        out_shape=_gdn_pre_out_shapes(b * l)
        + [jax.ShapeDtypeStruct((CONV_WIDTH - 1, b, CONV_DIM), F32)],
        scratch_shapes=[pltpu.VMEM((CONV_DIM // HEAD_DIM, tb * l, HEAD_DIM), F32)],
        compiler_params=_params("arbitrary"),
        name="gdn_pre_sample",
    )(x, buf, gain, wqkvz, wba, cw, alog, dtb)


def _unit_lower_inverses(mats, c):
    ri = lax.broadcasted_iota(jnp.int32, (c, c), 0)
    ci = lax.broadcasted_iota(jnp.int32, (c, c), 1)
    eye = (ri == ci).astype(F32)
    pair = ((ri // 2) == (ci // 2)) & (ri > ci)
    xs = [eye - jnp.where(pair, a, 0.0) for a in mats]
    mats = [a.astype(BF16) for a in mats]
    blk = 2
    while blk < c:
        off = ((ri // (2 * blk)) == (ci // (2 * blk))) & ((ri // blk) > (ci // blk))
        xbs = [x.astype(BF16) for x in xs]
        ys = [_dot(jnp.where(off, a, jnp.zeros_like(a)), xb) for a, xb in zip(mats, xbs)]
        xs = [x - _dot(xb, y) for x, xb, y in zip(xs, xbs, ys)]
        blk *= 2
    return xs


def _delta_chunks(q_ref, k_ref, v_ref, z_ref, beta_ref, g_ref, og_ref, onorm, rows,
                  state_load, state_store, c):
    n = len(rows)
    ri = lax.broadcasted_iota(jnp.int32, (c, c), 0)
    ci = lax.broadcasted_iota(jnp.int32, (c, c), 1)
    causal = ri >= ci
    strict = ri > ci
    rep = V_HEADS // K_HEADS
    units = [(i, h) for i in range(n) for h in range(V_HEADS)]
    kunits = [(i, j) for i in range(n) for j in range(K_HEADS)]

    gcum = [_cumsum_rows(g_ref[rows[i], :], c) for i in range(n)]
    gcum_t = [x.T for x in gcum]
    egcum = [jnp.exp(x) for x in gcum]
    etail_t = [x[:, c - 1:c] - x for x in gcum_t]
    etail_t = [jnp.exp(x) for x in etail_t]
    beta = [beta_ref[rows[i], :] for i in range(n)]
    ks = {(i, j): k_ref[j, rows[i], :] for i, j in kunits}
    kts = {u: ks[u].T for u in kunits}
    kq = {(i, j): _dot(jnp.concatenate([ks[i, j], q_ref[j, rows[i], :]], axis=0), kts[i, j])
          for i, j in kunits}
    gcol = {(i, h): gcum[i][:, h:h + 1] for i, h in units}
    bcol = {(i, h): beta[i][:, h:h + 1] for i, h in units}
    egc = {(i, h): egcum[i][:, h:h + 1] for i, h in units}
    decay = {(i, h): jnp.where(
        causal, jnp.exp(jnp.minimum(gcol[i, h] - gcum_t[i][h:h + 1, :], 0.0)), 0.0)
        for i, h in units}
    a_mats = [jnp.where(strict, kq[i, h // rep][:c] * bcol[i, h] * decay[i, h], 0.0)
              for i, h in units]
    t_inv = dict(zip(units, _unit_lower_inverses(a_mats, c)))
    uw = {(i, h): _dot(t_inv[i, h], jnp.concatenate(
        [v_ref[h, rows[i], :] * bcol[i, h], ks[i, h // rep] * (bcol[i, h] * egc[i, h])], axis=1))
        for i, h in units}
    wq = {(i, h): jnp.concatenate(
        [uw[i, h][:, HEAD_DIM:], q_ref[h // rep, rows[i], :] * egc[i, h]], axis=0).astype(BF16)
        for i, h in units}
    qkd = {(i, h): kq[i, h // rep][c:] * decay[i, h] for i, h in units}
    stacked = c % (2 * SUBLANES) == 0
    if stacked:
        qkd = {(i, h): jnp.concatenate(
            [qkd[i, h], kts[i, h // rep] * etail_t[i][h:h + 1, :]], axis=0).astype(BF16)
            for i, h in units}
    else:
        qkd = {u: x.astype(BF16) for u, x in qkd.items()}

    heads = range(V_HEADS)
    for i in range(n):
        s_old = [state_load(i, h) for h in heads]
        ws = [_dot(wq[i, h], s_old[h]) for h in heads]
        v_new = [uw[i, h][:, :HEAD_DIM] - ws[h][:c] for h in heads]
        ov = [_dot(qkd[i, h], v_new[h]) for h in heads]
        for h in heads:
            if stacked:
                kv = ov[h][c:]
            else:
                kv = _dot(kts[i, h // rep] * etail_t[i][h:h + 1, :], v_new[h])
            state_store(i, h, s_old[h] * egcum[i][c - 1:c, h:h + 1] + kv)
        for h in heads:
            og_ref[rows[i], h * HEAD_DIM:(h + 1) * HEAD_DIM] = (
                _rms(ws[h][c:] + ov[h][:c], onorm) * _silu(z_ref[h, rows[i], :])).astype(BF16)


def _cumsum_rows(g, c):
    ri = lax.broadcasted_iota(jnp.int32, (c, c), 0)
    ci = lax.broadcasted_iota(jnp.int32, (c, c), 1)
    tri = (ri >= ci).astype(BF16)
    hi = g.astype(BF16)
    rest = g - hi.astype(F32)
    mid = rest.astype(BF16)
    lo = (rest - mid.astype(F32)).astype(BF16)
    return (jnp.dot(tri, hi, preferred_element_type=F32)
            + jnp.dot(tri, mid, preferred_element_type=F32)
            + jnp.dot(tri, lo, preferred_element_type=F32))


def _gdn_scan_prompt_kernel(q_ref, k_ref, v_ref, z_ref, beta_ref, g_ref, onorm_ref,
                            og_ref, s_ref, *, c, n_chunk):
    @pl.when(pl.program_id(1) == 0)
    def _():
        s_ref[...] = jnp.zeros(s_ref.shape, F32)

    def load(i, hh):
        return s_ref[0, hh]

    def store(i, hh, val):
        s_ref[0, hh] = val

    def body(it, carry):
        r0 = pl.multiple_of(it * (SCAN_GROUP * c), SCAN_GROUP * c)
        rows = [pl.ds(r0 + i * c, c) for i in range(SCAN_GROUP)]
        _delta_chunks(q_ref, k_ref, v_ref, z_ref, beta_ref, g_ref, og_ref, onorm_ref[...], rows,
                      load, store, c)
        return carry

    lax.fori_loop(0, n_chunk // SCAN_GROUP, body, 0)


def _gdn_scan_sample_kernel(q_ref, k_ref, v_ref, z_ref, beta_ref, g_ref, onorm_ref, s0_ref,
                            og_ref, s_ref, *, tb, c):
    for b0 in range(0, tb, SAMPLE_SCAN_GROUP):
        def load(i, hh, b0=b0):
            return s0_ref[b0 + i, hh]

        def store(i, hh, val, b0=b0):
            s_ref[b0 + i, hh] = val

        rows = [pl.ds((b0 + i) * c, c) for i in range(SAMPLE_SCAN_GROUP)]
        _delta_chunks(q_ref, k_ref, v_ref, z_ref, beta_ref, g_ref, og_ref, onorm_ref[...], rows,
                      load, store, c)


def _gdn_scan_prompt(q, k, v, z, beta, g, onorm, b, l):
    c = PROMPT_CHUNK
    tm = SCAN_TILE
    nc = l // tm
    tok = lambda w: pl.BlockSpec((tm, w), lambda i, j: (i * nc + j, 0))
    return pl.pallas_call(
        functools.partial(_gdn_scan_prompt_kernel, c=c, n_chunk=tm // c),
        grid=(b, nc),
        in_specs=_gdn_pre_out_specs(tm, lambda i, j: i * nc + j) + [_const_spec(onorm.shape)],
        out_specs=[tok(V_DIM),
                   pl.BlockSpec((1, V_HEADS, HEAD_DIM, HEAD_DIM), lambda i, j: (i, 0, 0, 0))],
        out_shape=[jax.ShapeDtypeStruct((b * l, V_DIM), BF16),
                   jax.ShapeDtypeStruct((b, V_HEADS, HEAD_DIM, HEAD_DIM), F32)],
        compiler_params=_params("arbitrary", "arbitrary"),
        name="gdn_scan_prompt",
    )(q, k, v, z, beta, g, onorm)


def _gdn_scan_sample(q, k, v, z, beta, g, onorm, s0, b, l):
    tb = SAMPLE_SCAN_BTILE
    tok = lambda w: pl.BlockSpec((tb * l, w), lambda i: (i, 0))
    st = pl.BlockSpec((tb, V_HEADS, HEAD_DIM, HEAD_DIM), lambda i: (i, 0, 0, 0))
    return pl.pallas_call(
        functools.partial(_gdn_scan_sample_kernel, tb=tb, c=l),
        grid=(b // tb,),
        in_specs=_gdn_pre_out_specs(tb * l, lambda i: i) + [_const_spec(onorm.shape), st],
        out_specs=[tok(V_DIM), st],
        out_shape=[jax.ShapeDtypeStruct((b * l, V_DIM), BF16),
                   jax.ShapeDtypeStruct((b, V_HEADS, HEAD_DIM, HEAD_DIM), F32)],
        compiler_params=_params("arbitrary"),
        name="gdn_scan_sample",
    )(q, k, v, z, beta, g, onorm, s0)


def _gdn_post_kernel(og_ref, x_ref, gains_ref, wo_ref, win_ref, wout_ref, y_ref):
    m = jnp.dot(og_ref[...], wo_ref[...], preferred_element_type=F32)
    y_ref[...] = _residual_ffn(x_ref[...], m, gains_ref[0:1, :], gains_ref[1:2, :],
                               gains_ref[2:3, :], win_ref, wout_ref)


def _gdn_post(og, x, gains, wo, win, wout, layer):
    n, d = x.shape
    tm = PROMPT_TILE
    return pl.pallas_call(
        _gdn_post_kernel,
        grid=(n // tm,),
        in_specs=[pl.BlockSpec((tm, V_DIM), lambda i: (i, 0)),
                  pl.BlockSpec((tm, d), lambda i: (i, 0))]
        + [_const_spec(gains.shape), _const_spec(wo.shape), _layer_spec(win, layer),
           _layer_spec(wout, layer)],
        out_specs=pl.BlockSpec((tm, d), lambda i: (i, 0)),
        out_shape=jax.ShapeDtypeStruct((n, d), F32),
        compiler_params=_params("arbitrary"),
        name="gdn_post",
    )(og, x, gains, wo, win, wout)


def _head_lanes(vec):
    return jnp.pad(vec.astype(F32), (0, LANES - V_HEADS)).reshape(1, LANES)


def kernel(x_prompt, x_sample, state_pool, state_gdn_conv, state_gdn_rec, norm_mix_pre,
           norm_mix_post, norm_ffn_pre, norm_ffn_post, pool_w, pool_scale, gdn_w_in,
           gdn_conv_w, gdn_a_log, gdn_dt_bias, gdn_o_norm, gdn_w_out, ffn_w_in, ffn_w_out):
    bp, lp, d = x_prompt.shape
    bs, ls, _ = x_sample.shape

    gains0 = jnp.stack([norm_mix_pre[0], norm_mix_post[0], norm_ffn_pre[0], norm_ffn_post[0]])
    gains1 = jnp.stack([norm_mix_post[1], norm_ffn_pre[1], norm_ffn_post[1]])
    gain1_pre = norm_mix_pre[1].reshape(1, d)
    pw = pool_w[0].astype(BF16)
    ps = pool_scale[0].reshape(1, d)
    win, wout = ffn_w_in.astype(BF16), ffn_w_out.astype(BF16)
    w_in = gdn_w_in[0]
    wqkvz = w_in.astype(BF16)
    w_b = w_in[:, CONV_DIM + V_DIM:CONV_DIM + V_DIM + V_HEADS]
    w_a = w_in[:, CONV_DIM + V_DIM + V_HEADS:]
    lane_pad = ((0, 0), (0, LANES - V_HEADS))
    wba = jnp.concatenate([jnp.pad(w_b, lane_pad), jnp.pad(w_a, lane_pad)], axis=1).astype(BF16)
    cw = gdn_conv_w[0]
    alog, dtb = _head_lanes(gdn_a_log[0]), _head_lanes(gdn_dt_bias[0])
    onorm = gdn_o_norm[0].reshape(1, HEAD_DIM)
    wo = gdn_w_out[0].astype(BF16)

    xp1, pool_p = _pool_layer_prompt(x_prompt, gains0, pw, ps, win, wout, 0)
    time_major = lambda a: jnp.transpose(a, (1, 0, 2))
    xs1, pool_s = _pool_layer_sample(x_sample, time_major(state_pool[0]), gains0, pw, ps, win,
                                     wout, 0)

    qp, kp, vp, zp, betap, gp, conv_p = _gdn_pre_prompt(xp1, gain1_pre, wqkvz, wba, cw, alog, dtb)
    qs, ks, vs, zs, betas, gs, conv_s = _gdn_pre_sample(xs1, time_major(state_gdn_conv[0]),
                                                        gain1_pre, wqkvz, wba, cw, alog, dtb)

    ogp, rec_p = _gdn_scan_prompt(qp, kp, vp, zp, betap, gp, onorm, bp, lp)
    ogs, rec_s = _gdn_scan_sample(qs, ks, vs, zs, betas, gs, onorm, state_gdn_rec[0], bs, ls)

    yp = _gdn_post(ogp, xp1.reshape(bp * lp, d), gains1, wo, win, wout, 1).reshape(bp, lp, d)
    ys = _gdn_post(ogs, xs1.reshape(bs * ls, d), gains1, wo, win, wout, 1).reshape(bs, ls, d)

    return (yp, ys, pool_p[None], time_major(pool_s)[None], conv_p[None],
            time_major(conv_s)[None], rec_p[None], rec_s[None])
```

```python
import functools

import jax
import jax.numpy as jnp
from jax import lax
from jax.experimental import pallas as pl
from jax.experimental.pallas import tpu as pltpu

D_MODEL = 1024
POOL_WINDOWS = (2, 4, 8, 16)
POOL_GROUP_DIM = D_MODEL // len(POOL_WINDOWS)
POOL_BUF = max(POOL_WINDOWS) - 1
K_HEADS = 8
V_HEADS = 16
HEAD_DIM = 128
QK_DIM = K_HEADS * HEAD_DIM
V_DIM = V_HEADS * HEAD_DIM
CONV_DIM = 2 * QK_DIM + V_DIM
CONV_WIDTH = 4
D_FF = 2816
EPS = 1e-6

F32 = jnp.float32
BF16 = jnp.bfloat16

SUBLANES = 8
LANES = 128
POOL_HALO = 16
CONV_HALO = SUBLANES
VMEM_LIMIT = 56 * 1024 * 1024
ROW_STRIDE = 4
ROW_GROUP = SUBLANES * ROW_STRIDE

PROMPT_TILE = 512
GDN_PRE_TILE = 256
PRE_SLABS = 8
SAMPLE_BTILE = 32
PROMPT_CHUNK = 64
SCAN_TILE = 512
SCAN_GROUPS = (2, 2, 2, 2)
SAMPLE_SCAN_BTILE = 8
SAMPLE_SCAN_GROUP = 4


def _rms(x, gain):
    ms = jnp.mean(x * x, axis=-1, keepdims=True)
    return x * lax.rsqrt(ms + EPS) * gain


def _sigmoid(x):
    return 1.0 / (1.0 + jnp.exp(-x))


def _silu(x):
    return x * _sigmoid(x)


def _softplus(x):
    return jnp.maximum(x, 0.0) + jnp.log1p(jnp.exp(-jnp.abs(x)))


def _dot(a, b):
    return jnp.dot(a.astype(BF16), b.astype(BF16), preferred_element_type=F32)


def _const_spec(shape):
    nd = len(shape)
    return pl.BlockSpec(shape, lambda *_: (0,) * nd, pipeline_mode=pl.Buffered(1))


def _layer_spec(stacked, layer):
    nd = stacked.ndim - 1
    return pl.BlockSpec((None,) + stacked.shape[1:], lambda *_: (layer,) + (0,) * nd,
                        pipeline_mode=pl.Buffered(1))


def _params(*sem):
    return pltpu.CompilerParams(dimension_semantics=sem, vmem_limit_bytes=VMEM_LIMIT)


def _residual_ffn(x, m, g_post, g_fpre, g_fpost, win_ref, wout_ref):
    x1 = x + _rms(m, g_post)
    h = _rms(x1, g_fpre).astype(BF16)
    gate = jnp.dot(h, win_ref[:, :D_FF], preferred_element_type=F32)
    up = jnp.dot(h, win_ref[:, D_FF:], preferred_element_type=F32)
    act = (_silu(gate) * up).astype(BF16)
    f = jnp.dot(act, wout_ref[...], preferred_element_type=F32)
    return x1 + _rms(f, g_fpost)


def _pool_project(diffs, pw_ref, scale):
    parts = [_dot(d, pw_ref[gi]) for gi, d in enumerate(diffs)]
    return jnp.concatenate(parts, axis=-1) * scale


def _strided_rows(first):
    return pl.ds(first, SUBLANES, stride=ROW_STRIDE)


def _pool_layer_prompt_kernel(x_ref, gains_ref, pw_ref, ps_ref, win_ref, wout_ref,
                              y_ref, pool_ref, hp_ref, d_ref, x1_s, h2_s, *, tm, nj, n_tile):
    t = pl.program_id(0)
    wr = lax.rem(t, 2)
    rd = 1 - wr
    jp = lax.rem(jnp.minimum(t, n_tile - 1), nj)
    n_blk = D_MODEL // LANES
    blk_per_grp = POOL_GROUP_DIM // LANES

    @pl.when(t == 0)
    def _():
        x1_s[1] = jnp.zeros(x1_s.shape[1:], F32)
        h2_s[1] = jnp.zeros(h2_s.shape[1:], BF16)

    @pl.when(jp == 0)
    def _():
        hp_ref[:, 0:POOL_HALO, :] = jnp.zeros((n_blk, POOL_HALO, LANES), F32)

    h2 = h2_s[rd]
    gate = jnp.dot(h2, win_ref[:, :D_FF], preferred_element_type=F32)

    x = x_ref[0]
    h = _rms(x, gains_ref[0:1, :])
    for cb in range(n_blk):
        hp_ref[cb, POOL_HALO:POOL_HALO + tm, :] = h[:, cb * LANES:(cb + 1) * LANES]
    t_tile = lax.broadcasted_iota(jnp.int32, (SUBLANES, 1), 0) * ROW_STRIDE + (jp * tm + 1)
    for row0 in range(0, tm, ROW_GROUP):
        for gi, win in enumerate(POOL_WINDOWS):
            invs = [1.0 / jnp.minimum(win, t_tile + (row0 + r)).astype(F32)
                    for r in range(ROW_STRIDE)]
            for cb in range(gi * blk_per_grp, (gi + 1) * blk_per_grp):
                tiles = [hp_ref[cb, _strided_rows(row0 + POOL_HALO - (win - 1) + k), :]
                         for k in range(win + ROW_STRIDE - 1)]
                shared = range(ROW_STRIDE - 1, win)
                common = (functools.reduce(lambda a, b: a + b, [tiles[k] for k in shared])
                          if len(shared) > 1 else None)
                for r in range(ROW_STRIDE):
                    own = [k for k in range(r, r + win) if common is None or k not in shared]
                    tot = tiles[own[0]] if common is None else common + tiles[own[0]]
                    for k in own[1:]:
                        tot = tot + tiles[k]
                    cur = tiles[r + win - 1]
                    d_ref[cb, _strided_rows(row0 + r), :] = tot * invs[r] - cur
    diffs = [jnp.concatenate([d_ref[cb] for cb in range(gi * blk_per_grp, (gi + 1) * blk_per_grp)],
                             axis=1) for gi in range(len(POOL_WINDOWS))]
    m = _pool_project(diffs, pw_ref, ps_ref[...])
    x1_new = x + _rms(m, gains_ref[1:2, :])
    x1_s[wr] = x1_new
    h2_s[wr] = _rms(x1_new, gains_ref[2:3, :]).astype(BF16)

    up = jnp.dot(h2, win_ref[:, D_FF:], preferred_element_type=F32)
    act = (_silu(gate) * up).astype(BF16)
    f = jnp.dot(act, wout_ref[...], preferred_element_type=F32)
    y_ref[0] = x1_s[rd] + _rms(f, gains_ref[3:4, :])

    @pl.when((jp == nj - 1) & (t < n_tile))
    def _():
        for cb in range(n_blk):
            pool_ref[0, :, cb * LANES:(cb + 1) * LANES] = hp_ref[
                cb, tm + POOL_HALO - POOL_BUF:tm + POOL_HALO, :]

    hp_ref[:, 0:POOL_HALO, :] = hp_ref[:, tm:tm + POOL_HALO, :]


def _pool_layer_sample_kernel(x_ref, buf_ref, gains_ref, pw_ref, ps_ref, win_ref, wout_ref,
                              y_ref, pool_ref, hs_ref, d_ref, *, tb, seq, n_past):
    m_rows = tb * seq
    x = x_ref[...].reshape(m_rows, D_MODEL)
    h = _rms(x, gains_ref[0:1, :])
    n_blk = D_MODEL // LANES
    blk_per_grp = POOL_GROUP_DIM // LANES
    at_time = lambda t: pl.ds(t, tb, stride=seq)
    for cb in range(n_blk):
        lanes = slice(cb * LANES, (cb + 1) * LANES)
        hs_ref[cb] = h[:, lanes]
        win = POOL_WINDOWS[cb // blk_per_grp]
        hist = [buf_ref[s, :, lanes] for s in range(n_past)]
        hist += [hs_ref[cb, at_time(t), :] for t in range(seq)]
        for t in range(seq):
            tot = hist[n_past + t]
            for s in range(1, win):
                tot = tot + hist[n_past + t - s]
            d_ref[cb, at_time(t), :] = tot * (1.0 / min(win, t + 1 + n_past)) - hist[n_past + t]
        for s in range(n_past):
            pool_ref[s, :, lanes] = hist[seq + s]
    diffs = [jnp.concatenate([d_ref[cb] for cb in range(gi * blk_per_grp, (gi + 1) * blk_per_grp)],
                             axis=1) for gi in range(len(POOL_WINDOWS))]
    m = _pool_project(diffs, pw_ref, ps_ref[...])
    y = _residual_ffn(x, m, gains_ref[1:2, :], gains_ref[2:3, :], gains_ref[3:4, :],
                      win_ref, wout_ref)
    y_ref[...] = y.reshape(tb, seq, D_MODEL)


def _pool_layer_prompt(x, gains, pw, ps, win, wout, layer):
    b, l, d = x.shape
    tm = PROMPT_TILE
    nj = l // tm
    n_tile = b * nj
    mixed = lambda t: jnp.minimum(t, n_tile - 1)
    done = lambda t: jnp.maximum(t - 1, 0)
    return pl.pallas_call(
        functools.partial(_pool_layer_prompt_kernel, tm=tm, nj=nj, n_tile=n_tile),
        grid=(n_tile + 1,),
        in_specs=[
            pl.BlockSpec((1, tm, d), lambda t: (mixed(t) // nj, mixed(t) % nj, 0)),
            _const_spec(gains.shape), _const_spec(pw.shape), _const_spec(ps.shape),
            _layer_spec(win, layer), _layer_spec(wout, layer),
        ],
        out_specs=[
            pl.BlockSpec((1, tm, d), lambda t: (done(t) // nj, done(t) % nj, 0)),
            pl.BlockSpec((1, POOL_BUF, d), lambda t: (mixed(t) // nj, 0, 0)),
        ],
        out_shape=[jax.ShapeDtypeStruct((b, l, d), F32),
                   jax.ShapeDtypeStruct((b, POOL_BUF, d), F32)],
        scratch_shapes=[pltpu.VMEM((d // LANES, POOL_HALO + tm, LANES), F32),
                        pltpu.VMEM((d // LANES, tm, LANES), F32),
                        pltpu.VMEM((2, tm, d), F32), pltpu.VMEM((2, tm, d), BF16)],
        compiler_params=_params("arbitrary"),
        name="pool_layer_prompt",
    )(x, gains, pw, ps, win, wout)


def _pool_layer_sample(x, buf, gains, pw, ps, win, wout, layer):
    b, l, d = x.shape
    tb = SAMPLE_BTILE
    n_past = buf.shape[0]
    assert n_past == POOL_BUF
    return pl.pallas_call(
        functools.partial(_pool_layer_sample_kernel, tb=tb, seq=l, n_past=n_past),
        grid=(b // tb,),
        in_specs=[
            pl.BlockSpec((tb, l, d), lambda i: (i, 0, 0)),
            pl.BlockSpec((POOL_BUF, tb, d), lambda i: (0, i, 0)),
            _const_spec(gains.shape), _const_spec(pw.shape), _const_spec(ps.shape),
            _layer_spec(win, layer), _layer_spec(wout, layer),
        ],
        out_specs=[
            pl.BlockSpec((tb, l, d), lambda i: (i, 0, 0)),
            pl.BlockSpec((POOL_BUF, tb, d), lambda i: (0, i, 0)),
        ],
        out_shape=[jax.ShapeDtypeStruct((b, l, d), F32),
                   jax.ShapeDtypeStruct((POOL_BUF, b, d), F32)],
        scratch_shapes=[pltpu.VMEM((d // LANES, tb * l, LANES), F32),
                        pltpu.VMEM((d // LANES, tb * l, LANES), F32)],
        compiler_params=_params("arbitrary"),
        name="pool_layer_sample",
    )(x, buf, gains, pw, ps, win, wout)


def _gdn_qkv_slab(sl, conv, q_ref, k_ref, v_ref, rows):
    if sl >= 2 * K_HEADS:
        v_ref[sl - 2 * K_HEADS, rows, :] = conv
        return
    unit = conv * lax.rsqrt(jnp.sum(conv * conv, axis=-1, keepdims=True) + EPS)
    if sl < K_HEADS:
        q_ref[sl, rows, :] = unit * (HEAD_DIM ** -0.5)
    else:
        k_ref[sl - K_HEADS, rows, :] = unit


def _gdn_gates(ba, alog, dtb, beta_ref, g_ref):
    beta_ref[...] = _sigmoid(ba[:, :LANES])
    g_ref[...] = -jnp.exp(alog) * _softplus(ba[:, LANES:] + dtb)


def _gdn_pre_prompt_kernel(x_ref, gain_ref, wqkvz_ref, wba_ref, cw_ref, alog_ref, dtb_ref,
                           q_ref, k_ref, v_ref, z_ref, beta_ref, g_ref, conv_ref, up_ref, *, tm):
    j = pl.program_id(1)
    h = _rms(x_ref[0], gain_ref[...]).astype(BF16)
    n_slab = CONV_DIM // HEAD_DIM
    n_grp = n_slab // PRE_SLABS

    @pl.when(j == 0)
    def _():
        up_ref[:, 0:CONV_HALO, :] = jnp.zeros((n_slab, CONV_HALO, HEAD_DIM), F32)

    def project(grp):
        c0 = grp * PRE_SLABS * HEAD_DIM
        p = jnp.dot(h, wqkvz_ref[:, c0:c0 + PRE_SLABS * HEAD_DIM], preferred_element_type=F32)
        for t in range(PRE_SLABS):
            up_ref[grp * PRE_SLABS + t, CONV_HALO:CONV_HALO + tm, :] = (
                p[:, t * HEAD_DIM:(t + 1) * HEAD_DIM])

    def project_z(half):
        c0 = CONV_DIM + half * (V_DIM // 2)
        p = jnp.dot(h, wqkvz_ref[:, c0:c0 + V_DIM // 2], preferred_element_type=F32)
        for t in range(V_HEADS // 2):
            z_ref[half * (V_HEADS // 2) + t] = p[:, t * HEAD_DIM:(t + 1) * HEAD_DIM]

    base = CONV_HALO - (CONV_WIDTH - 1)

    def convolve(grp):
        for sl in range(grp * PRE_SLABS, (grp + 1) * PRE_SLABS):
            lanes = slice(sl * HEAD_DIM, (sl + 1) * HEAD_DIM)
            taps = [cw_ref[tap:tap + 1, lanes] for tap in range(CONV_WIDTH)]
            for row0 in range(0, tm, ROW_GROUP):
                tiles = [up_ref[sl, _strided_rows(row0 + base + s), :]
                         for s in range(ROW_STRIDE + CONV_WIDTH - 1)]
                for r in range(ROW_STRIDE):
                    acc = tiles[r] * taps[0]
                    for tap in range(1, CONV_WIDTH):
                        acc = acc + tiles[r + tap] * taps[tap]
                    _gdn_qkv_slab(sl, _silu(acc), q_ref, k_ref, v_ref, _strided_rows(row0 + r))

    project(0)
    for grp in range(n_grp):
        if grp + 1 < n_grp:
            project(grp + 1)
        else:
            project_z(0)
        convolve(grp)
    project_z(1)
    ba = jnp.dot(h, wba_ref[...], preferred_element_type=F32)
    _gdn_gates(ba, alog_ref[...], dtb_ref[...], beta_ref, g_ref)

    @pl.when(j == pl.num_programs(1) - 1)
    def _():
        for sl in range(n_slab):
            conv_ref[0, :, sl * HEAD_DIM:(sl + 1) * HEAD_DIM] = up_ref[
                sl, tm + CONV_HALO - (CONV_WIDTH - 1):tm + CONV_HALO, :]

    up_ref[:, 0:CONV_HALO, :] = up_ref[:, tm:tm + CONV_HALO, :]


def _gdn_pre_sample_kernel(x_ref, buf_ref, gain_ref, wqkvz_ref, wba_ref, cw_ref, alog_ref,
                           dtb_ref, q_ref, k_ref, v_ref, z_ref, beta_ref, g_ref, conv_ref, up_ref,
                           *, tb, seq):
    m = tb * seq
    n_buf = CONV_WIDTH - 1
    h = _rms(x_ref[...].reshape(m, D_MODEL), gain_ref[...]).astype(BF16)
    proj = jnp.dot(h, wqkvz_ref[:, :CONV_DIM + V_DIM], preferred_element_type=F32)
    ba = jnp.dot(h, wba_ref[...], preferred_element_type=F32)
    for hh in range(V_HEADS):
        z_ref[hh] = proj[:, CONV_DIM + hh * HEAD_DIM:CONV_DIM + (hh + 1) * HEAD_DIM]
    _gdn_gates(ba, alog_ref[...], dtb_ref[...], beta_ref, g_ref)
    at_time = lambda t: pl.ds(t, tb, stride=seq)
    for sl in range(CONV_DIM // HEAD_DIM):
        lanes = slice(sl * HEAD_DIM, (sl + 1) * HEAD_DIM)
        up_ref[sl] = proj[:, lanes]
        ups = [buf_ref[s, :, lanes] for s in range(n_buf)]
        ups += [up_ref[sl, at_time(t), :] for t in range(seq)]
        for t in range(seq):
            acc = ups[t] * cw_ref[0:1, lanes]
            for tap in range(1, CONV_WIDTH):
                acc = acc + ups[t + tap] * cw_ref[tap:tap + 1, lanes]
            _gdn_qkv_slab(sl, _silu(acc), q_ref, k_ref, v_ref, at_time(t))
        for s in range(n_buf):
            conv_ref[s, :, lanes] = ups[seq + s]


def _gdn_pre_out_shapes(n):
    return [jax.ShapeDtypeStruct((K_HEADS, n, HEAD_DIM), F32),
            jax.ShapeDtypeStruct((K_HEADS, n, HEAD_DIM), F32),
            jax.ShapeDtypeStruct((V_HEADS, n, HEAD_DIM), F32),
            jax.ShapeDtypeStruct((V_HEADS, n, HEAD_DIM), F32),
            jax.ShapeDtypeStruct((n, LANES), F32), jax.ShapeDtypeStruct((n, LANES), F32)]


def _gdn_pre_out_specs(rows, index):
    heads = lambda n: pl.BlockSpec((n, rows, HEAD_DIM), lambda *g: (0, index(*g), 0))
    lane = pl.BlockSpec((rows, LANES), lambda *g: (index(*g), 0))
    return [heads(K_HEADS), heads(K_HEADS), heads(V_HEADS), heads(V_HEADS), lane, lane]


def _gdn_pre_prompt(x, gain, wqkvz, wba, cw, alog, dtb):
    b, l, d = x.shape
    tm = GDN_PRE_TILE
    nj = l // tm
    return pl.pallas_call(
        functools.partial(_gdn_pre_prompt_kernel, tm=tm),
        grid=(b, nj),
        in_specs=[pl.BlockSpec((1, tm, d), lambda i, j: (i, j, 0))]
        + [_const_spec(a.shape) for a in (gain, wqkvz, wba, cw, alog, dtb)],
        out_specs=_gdn_pre_out_specs(tm, lambda i, j: i * nj + j)
        + [pl.BlockSpec((1, CONV_WIDTH - 1, CONV_DIM), lambda i, j: (i, 0, 0))],
        out_shape=_gdn_pre_out_shapes(b * l)
        + [jax.ShapeDtypeStruct((b, CONV_WIDTH - 1, CONV_DIM), F32)],
        scratch_shapes=[pltpu.VMEM((CONV_DIM // HEAD_DIM, CONV_HALO + tm, HEAD_DIM), F32)],
        compiler_params=_params("arbitrary", "arbitrary"),
        name="gdn_pre_prompt",
    )(x, gain, wqkvz, wba, cw, alog, dtb)


def _gdn_pre_sample(x, buf, gain, wqkvz, wba, cw, alog, dtb):
    b, l, d = x.shape
    tb = SAMPLE_BTILE
    state = pl.BlockSpec((CONV_WIDTH - 1, tb, CONV_DIM), lambda i: (0, i, 0))
    return pl.pallas_call(
        functools.partial(_gdn_pre_sample_kernel, tb=tb, seq=l),
        grid=(b // tb,),
        in_specs=[pl.BlockSpec((tb, l, d), lambda i: (i, 0, 0)), state]
        + [_const_spec(a.shape) for a in (gain, wqkvz, wba, cw, alog, dtb)],
        out_specs=_gdn_pre_out_specs(tb * l, lambda i: i) + [state],
        out_shape=_gdn_pre_out_shapes(b * l)
        + [jax.ShapeDtypeStruct((CONV_WIDTH - 1, b, CONV_DIM), F32)],
        scratch_shapes=[pltpu.VMEM((CONV_DIM // HEAD_DIM, tb * l, HEAD_DIM), F32)],
        compiler_params=_params("arbitrary"),
        name="gdn_pre_sample",
    )(x, buf, gain, wqkvz, wba, cw, alog, dtb)


def _unit_lower_inverses(mats, c):
    ri = lax.broadcasted_iota(jnp.int32, (c, c), 0)
    ci = lax.broadcasted_iota(jnp.int32, (c, c), 1)
    eye = (ri == ci).astype(F32)
    pair = ((ri // 2) == (ci // 2)) & (ri > ci)
    xs = [eye - jnp.where(pair, a, 0.0) for a in mats]
    mats = [a.astype(BF16) for a in mats]
    blk = 2
    while blk < c:
        off = ((ri // (2 * blk)) == (ci // (2 * blk))) & ((ri // blk) > (ci // blk))
        xbs = [x.astype(BF16) for x in xs]
        ys = [_dot(jnp.where(off, a, jnp.zeros_like(a)), xb) for a, xb in zip(mats, xbs)]
        xs = [x - _dot(xb, y) for x, xb, y in zip(xs, xbs, ys)]
        blk *= 2
    return xs


def _delta_chunks(q_ref, k_ref, v_ref, z_ref, beta_ref, g_ref, og_ref, onorm, rows,
                  state_load, state_store, c):
    n = len(rows)
    ri = lax.broadcasted_iota(jnp.int32, (c, c), 0)
    ci = lax.broadcasted_iota(jnp.int32, (c, c), 1)
    causal = ri >= ci
    strict = ri > ci
    rep = V_HEADS // K_HEADS
    units = [(i, h) for i in range(n) for h in range(V_HEADS)]
    kunits = [(i, j) for i in range(n) for j in range(K_HEADS)]

    gcum = [_cumsum_rows(g_ref[rows[i], :], c) for i in range(n)]
    gcum_t = [x.T for x in gcum]
    egcum = [jnp.exp(x) for x in gcum]
    etail_t = [x[:, c - 1:c] - x for x in gcum_t]
    etail_t = [jnp.exp(x) for x in etail_t]
    beta = [beta_ref[rows[i], :] for i in range(n)]
    ks = {(i, j): k_ref[j, rows[i], :] for i, j in kunits}
    kts = {u: ks[u].T for u in kunits}
    kq = {(i, j): _dot(jnp.concatenate([ks[i, j], q_ref[j, rows[i], :]], axis=0), kts[i, j])
          for i, j in kunits}
    gcol = {(i, h): gcum[i][:, h:h + 1] for i, h in units}
    bcol = {(i, h): beta[i][:, h:h + 1] for i, h in units}
    egc = {(i, h): egcum[i][:, h:h + 1] for i, h in units}
    decay = {(i, h): jnp.where(
        causal, jnp.exp(jnp.minimum(gcol[i, h] - gcum_t[i][h:h + 1, :], 0.0)), 0.0)
        for i, h in units}
    a_mats = [jnp.where(strict, kq[i, h // rep][:c] * bcol[i, h] * decay[i, h], 0.0)
              for i, h in units]
    t_inv = dict(zip(units, _unit_lower_inverses(a_mats, c)))
    uw = {(i, h): _dot(t_inv[i, h], jnp.concatenate(
        [v_ref[h, rows[i], :] * bcol[i, h], ks[i, h // rep] * (bcol[i, h] * egc[i, h])], axis=1))
        for i, h in units}
    wq = {(i, h): jnp.concatenate(
        [uw[i, h][:, HEAD_DIM:], q_ref[h // rep, rows[i], :] * egc[i, h]], axis=0).astype(BF16)
        for i, h in units}
    qkd = {(i, h): kq[i, h // rep][c:] * decay[i, h] for i, h in units}
    stacked = c % (2 * SUBLANES) == 0
    if stacked:
        qkd = {(i, h): jnp.concatenate(
            [qkd[i, h], kts[i, h // rep] * etail_t[i][h:h + 1, :]], axis=0).astype(BF16)
            for i, h in units}
    else:
        qkd = {u: x.astype(BF16) for u, x in qkd.items()}

    heads = range(V_HEADS)
    for i in range(n):
        s_old = [state_load(i, h) for h in heads]
        ws = [_dot(wq[i, h], s_old[h]) for h in heads]
        v_new = [uw[i, h][:, :HEAD_DIM] - ws[h][:c] for h in heads]
        ov = [_dot(qkd[i, h], v_new[h]) for h in heads]
        for h in heads:
            if stacked:
                kv = ov[h][c:]
            else:
                kv = _dot(kts[i, h // rep] * etail_t[i][h:h + 1, :], v_new[h])
            state_store(i, h, s_old[h] * egcum[i][c - 1:c, h:h + 1] + kv)
        for h in heads:
            og_ref[rows[i], h * HEAD_DIM:(h + 1) * HEAD_DIM] = (
                _rms(ws[h][c:] + ov[h][:c], onorm) * _silu(z_ref[h, rows[i], :])).astype(BF16)


def _cumsum_rows(g, c):
    ri = lax.broadcasted_iota(jnp.int32, (c, c), 0)
    ci = lax.broadcasted_iota(jnp.int32, (c, c), 1)
    tri = (ri >= ci).astype(BF16)
    hi = g.astype(BF16)
    rest = g - hi.astype(F32)
    mid = rest.astype(BF16)
    lo = (rest - mid.astype(F32)).astype(BF16)
    return (jnp.dot(tri, hi, preferred_element_type=F32)
            + jnp.dot(tri, mid, preferred_element_type=F32)
            + jnp.dot(tri, lo, preferred_element_type=F32))


def _gdn_scan_prompt_kernel(q_ref, k_ref, v_ref, z_ref, beta_ref, g_ref, onorm_ref,
                            og_ref, s_ref, *, c, n_chunk):
    @pl.when(pl.program_id(1) == 0)
    def _():
        s_ref[...] = jnp.zeros(s_ref.shape, F32)

    def load(i, hh):
        return s_ref[0, hh]

    def store(i, hh, val):
        s_ref[0, hh] = val

    first = 0
    for size in SCAN_GROUPS:
        rows = [pl.ds((first + i) * c, c) for i in range(size)]
        _delta_chunks(q_ref, k_ref, v_ref, z_ref, beta_ref, g_ref, og_ref, onorm_ref[...], rows,
                      load, store, c)
        first += size
    assert first == n_chunk


def _gdn_scan_sample_kernel(q_ref, k_ref, v_ref, z_ref, beta_ref, g_ref, onorm_ref, s0_ref,
                            og_ref, s_ref, *, tb, c):
    for b0 in range(0, tb, SAMPLE_SCAN_GROUP):
        def load(i, hh, b0=b0):
            return s0_ref[b0 + i, hh]

        def store(i, hh, val, b0=b0):
            s_ref[b0 + i, hh] = val

        rows = [pl.ds((b0 + i) * c, c) for i in range(SAMPLE_SCAN_GROUP)]
        _delta_chunks(q_ref, k_ref, v_ref, z_ref, beta_ref, g_ref, og_ref, onorm_ref[...], rows,
                      load, store, c)


def _gdn_scan_prompt(q, k, v, z, beta, g, onorm, b, l):
    c = PROMPT_CHUNK
    tm = SCAN_TILE
    nc = l // tm
    tok = lambda w: pl.BlockSpec((tm, w), lambda i, j: (i * nc + j, 0))
    return pl.pallas_call(
        functools.partial(_gdn_scan_prompt_kernel, c=c, n_chunk=tm // c),
        grid=(b, nc),
        in_specs=_gdn_pre_out_specs(tm, lambda i, j: i * nc + j) + [_const_spec(onorm.shape)],
        out_specs=[tok(V_DIM),
                   pl.BlockSpec((1, V_HEADS, HEAD_DIM, HEAD_DIM), lambda i, j: (i, 0, 0, 0))],
        out_shape=[jax.ShapeDtypeStruct((b * l, V_DIM), BF16),
                   jax.ShapeDtypeStruct((b, V_HEADS, HEAD_DIM, HEAD_DIM), F32)],
        compiler_params=_params("arbitrary", "arbitrary"),
        name="gdn_scan_prompt",
    )(q, k, v, z, beta, g, onorm)


def _gdn_scan_sample(q, k, v, z, beta, g, onorm, s0, b, l):
    tb = SAMPLE_SCAN_BTILE
    tok = lambda w: pl.BlockSpec((tb * l, w), lambda i: (i, 0))
    st = pl.BlockSpec((tb, V_HEADS, HEAD_DIM, HEAD_DIM), lambda i: (i, 0, 0, 0))
    return pl.pallas_call(
        functools.partial(_gdn_scan_sample_kernel, tb=tb, c=l),
        grid=(b // tb,),
        in_specs=_gdn_pre_out_specs(tb * l, lambda i: i) + [_const_spec(onorm.shape), st],
        out_specs=[tok(V_DIM), st],
        out_shape=[jax.ShapeDtypeStruct((b * l, V_DIM), BF16),
                   jax.ShapeDtypeStruct((b, V_HEADS, HEAD_DIM, HEAD_DIM), F32)],
        compiler_params=_params("arbitrary"),
        name="gdn_scan_sample",
    )(q, k, v, z, beta, g, onorm, s0)


def _gdn_post_kernel(og_ref, x_ref, gains_ref, wo_ref, win_ref, wout_ref, y_ref):
    m = jnp.dot(og_ref[...], wo_ref[...], preferred_element_type=F32)
    y_ref[...] = _residual_ffn(x_ref[...], m, gains_ref[0:1, :], gains_ref[1:2, :],
                               gains_ref[2:3, :], win_ref, wout_ref)


def _gdn_post(og, x, gains, wo, win, wout, layer):
    n, d = x.shape
    tm = PROMPT_TILE
    return pl.pallas_call(
        _gdn_post_kernel,
        grid=(n // tm,),
        in_specs=[pl.BlockSpec((tm, V_DIM), lambda i: (i, 0)),
                  pl.BlockSpec((tm, d), lambda i: (i, 0))]
        + [_const_spec(gains.shape), _const_spec(wo.shape), _layer_spec(win, layer),
           _layer_spec(wout, layer)],
        out_specs=pl.BlockSpec((tm, d), lambda i: (i, 0)),
        out_shape=jax.ShapeDtypeStruct((n, d), F32),
        compiler_params=_params("arbitrary"),
        name="gdn_post",
    )(og, x, gains, wo, win, wout)


def _head_lanes(vec):
    return jnp.pad(vec.astype(F32), (0, LANES - V_HEADS)).reshape(1, LANES)


def kernel(x_prompt, x_sample, state_pool, state_gdn_conv, state_gdn_rec, norm_mix_pre,
           norm_mix_post, norm_ffn_pre, norm_ffn_post, pool_w, pool_scale, gdn_w_in,
           gdn_conv_w, gdn_a_log, gdn_dt_bias, gdn_o_norm, gdn_w_out, ffn_w_in, ffn_w_out):
    bp, lp, d = x_prompt.shape
    bs, ls, _ = x_sample.shape

    gains0 = jnp.stack([norm_mix_pre[0], norm_mix_post[0], norm_ffn_pre[0], norm_ffn_post[0]])
    gains1 = jnp.stack([norm_mix_post[1], norm_ffn_pre[1], norm_ffn_post[1]])
    gain1_pre = norm_mix_pre[1].reshape(1, d)
    pw = pool_w[0].astype(BF16)
    ps = pool_scale[0].reshape(1, d)
    win, wout = ffn_w_in.astype(BF16), ffn_w_out.astype(BF16)
    w_in = gdn_w_in[0]
    wqkvz = w_in.astype(BF16)
    w_b = w_in[:, CONV_DIM + V_DIM:CONV_DIM + V_DIM + V_HEADS]
    w_a = w_in[:, CONV_DIM + V_DIM + V_HEADS:]
    lane_pad = ((0, 0), (0, LANES - V_HEADS))
    wba = jnp.concatenate([jnp.pad(w_b, lane_pad), jnp.pad(w_a, lane_pad)], axis=1).astype(BF16)
    cw = gdn_conv_w[0]
    alog, dtb = _head_lanes(gdn_a_log[0]), _head_lanes(gdn_dt_bias[0])
    onorm = gdn_o_norm[0].reshape(1, HEAD_DIM)
    wo = gdn_w_out[0].astype(BF16)

    xp1, pool_p = _pool_layer_prompt(x_prompt, gains0, pw, ps, win, wout, 0)
    time_major = lambda a: jnp.transpose(a, (1, 0, 2))
    xs1, pool_s = _pool_layer_sample(x_sample, time_major(state_pool[0]), gains0, pw, ps, win,
                                     wout, 0)

    qp, kp, vp, zp, betap, gp, conv_p = _gdn_pre_prompt(xp1, gain1_pre, wqkvz, wba, cw, alog, dtb)
    qs, ks, vs, zs, betas, gs, conv_s = _gdn_pre_sample(xs1, time_major(state_gdn_conv[0]),
                                                        gain1_pre, wqkvz, wba, cw, alog, dtb)

    ogp, rec_p = _gdn_scan_prompt(qp, kp, vp, zp, betap, gp, onorm, bp, lp)
    ogs, rec_s = _gdn_scan_sample(qs, ks, vs, zs, betas, gs, onorm, state_gdn_rec[0], bs, ls)

    yp = _gdn_post(ogp, xp1.reshape(bp * lp, d), gains1, wo, win, wout, 1).reshape(bp, lp, d)
    ys = _gdn_post(ogs, xs1.reshape(bs * ls, d), gains1, wo, win, wout, 1).reshape(bs, ls, d)

    return (yp, ys, pool_p[None], time_major(pool_s)[None], conv_p[None],
            time_major(conv_s)[None], rec_p[None], rec_s[None])
```

```python
import functools

import jax
import jax.numpy as jnp
from jax import lax
from jax.experimental import pallas as pl
from jax.experimental.pallas import tpu as pltpu

D_MODEL = 1024
POOL_WINDOWS = (2, 4, 8, 16)
POOL_GROUP_DIM = D_MODEL // len(POOL_WINDOWS)
POOL_BUF = max(POOL_WINDOWS) - 1
K_HEADS = 8
V_HEADS = 16
HEAD_DIM = 128
QK_DIM = K_HEADS * HEAD_DIM
V_DIM = V_HEADS * HEAD_DIM
CONV_DIM = 2 * QK_DIM + V_DIM
CONV_WIDTH = 4
D_FF = 2816
EPS = 1e-6

F32 = jnp.float32
BF16 = jnp.bfloat16

SUBLANES = 8
LANES = 128
POOL_HALO = 16
CONV_HALO = SUBLANES
VMEM_LIMIT = 56 * 1024 * 1024
ROW_STRIDE = 4
ROW_GROUP = SUBLANES * ROW_STRIDE

PROMPT_TILE = 512
GDN_PRE_TILE = 256
PRE_SLABS = 8
SAMPLE_BTILE = 32
PROMPT_CHUNK = 64
SCAN_TILE = 512
SCAN_GROUPS = (2, 2, 2, 2)
SAMPLE_SCAN_BTILE = 8
SAMPLE_SCAN_GROUP = 4
SCAN_INTERLEAVE = 3
STATE_STAGES = "state stages"


def _rms(x, gain):
    ms = jnp.mean(x * x, axis=-1, keepdims=True)
    return x * lax.rsqrt(ms + EPS) * gain


def _sigmoid(x):
    return 1.0 / (1.0 + jnp.exp(-x))


def _silu(x):
    return x * _sigmoid(x)


def _softplus(x):
    return jnp.maximum(x, 0.0) + jnp.log1p(jnp.exp(-jnp.abs(x)))


def _dot(a, b):
    return jnp.dot(a.astype(BF16), b.astype(BF16), preferred_element_type=F32)


def _const_spec(shape):
    nd = len(shape)
    return pl.BlockSpec(shape, lambda *_: (0,) * nd, pipeline_mode=pl.Buffered(1))


def _layer_spec(stacked, layer):
    nd = stacked.ndim - 1
    return pl.BlockSpec((None,) + stacked.shape[1:], lambda *_: (layer,) + (0,) * nd,
                        pipeline_mode=pl.Buffered(1))


def _params(*sem):
    return pltpu.CompilerParams(dimension_semantics=sem, vmem_limit_bytes=VMEM_LIMIT)


def _residual_ffn(x, m, g_post, g_fpre, g_fpost, win_ref, wout_ref):
    x1 = x + _rms(m, g_post)
    h = _rms(x1, g_fpre).astype(BF16)
    gate = jnp.dot(h, win_ref[:, :D_FF], preferred_element_type=F32)
    up = jnp.dot(h, win_ref[:, D_FF:], preferred_element_type=F32)
    act = (_silu(gate) * up).astype(BF16)
    f = jnp.dot(act, wout_ref[...], preferred_element_type=F32)
    return x1 + _rms(f, g_fpost)


def _pool_project(diffs, pw_ref, scale):
    parts = [_dot(d, pw_ref[gi]) for gi, d in enumerate(diffs)]
    return jnp.concatenate(parts, axis=-1) * scale


def _strided_rows(first):
    return pl.ds(first, SUBLANES, stride=ROW_STRIDE)


def _pool_layer_prompt_kernel(x_ref, gains_ref, pw_ref, ps_ref, win_ref, wout_ref,
                              y_ref, pool_ref, hp_ref, d_ref, x1_s, h2_s, *, tm, nj, n_tile):
    t = pl.program_id(0)
    wr = lax.rem(t, 2)
    rd = 1 - wr
    jp = lax.rem(jnp.minimum(t, n_tile - 1), nj)
    n_blk = D_MODEL // LANES
    blk_per_grp = POOL_GROUP_DIM // LANES

    @pl.when(t == 0)
    def _():
        x1_s[1] = jnp.zeros(x1_s.shape[1:], F32)
        h2_s[1] = jnp.zeros(h2_s.shape[1:], BF16)

    @pl.when(jp == 0)
    def _():
        hp_ref[:, 0:POOL_HALO, :] = jnp.zeros((n_blk, POOL_HALO, LANES), F32)

    h2 = h2_s[rd]
    gate = jnp.dot(h2, win_ref[:, :D_FF], preferred_element_type=F32)

    x = x_ref[0]
    h = _rms(x, gains_ref[0:1, :])
    for cb in range(n_blk):
        hp_ref[cb, POOL_HALO:POOL_HALO + tm, :] = h[:, cb * LANES:(cb + 1) * LANES]
    t_tile = lax.broadcasted_iota(jnp.int32, (SUBLANES, 1), 0) * ROW_STRIDE + (jp * tm + 1)
    for row0 in range(0, tm, ROW_GROUP):
        for gi, win in enumerate(POOL_WINDOWS):
            invs = [1.0 / jnp.minimum(win, t_tile + (row0 + r)).astype(F32)
                    for r in range(ROW_STRIDE)]
            for cb in range(gi * blk_per_grp, (gi + 1) * blk_per_grp):
                tiles = [hp_ref[cb, _strided_rows(row0 + POOL_HALO - (win - 1) + k), :]
                         for k in range(win + ROW_STRIDE - 1)]
                shared = range(ROW_STRIDE - 1, win)
                common = (functools.reduce(lambda a, b: a + b, [tiles[k] for k in shared])
                          if len(shared) > 1 else None)
                for r in range(ROW_STRIDE):
                    own = [k for k in range(r, r + win) if common is None or k not in shared]
                    tot = tiles[own[0]] if common is None else common + tiles[own[0]]
                    for k in own[1:]:
                        tot = tot + tiles[k]
                    cur = tiles[r + win - 1]
                    d_ref[cb, _strided_rows(row0 + r), :] = tot * invs[r] - cur
    diffs = [jnp.concatenate([d_ref[cb] for cb in range(gi * blk_per_grp, (gi + 1) * blk_per_grp)],
                             axis=1) for gi in range(len(POOL_WINDOWS))]
    m = _pool_project(diffs, pw_ref, ps_ref[...])
    x1_new = x + _rms(m, gains_ref[1:2, :])
    x1_s[wr] = x1_new
    h2_s[wr] = _rms(x1_new, gains_ref[2:3, :]).astype(BF16)

    up = jnp.dot(h2, win_ref[:, D_FF:], preferred_element_type=F32)
    act = (_silu(gate) * up).astype(BF16)
    f = jnp.dot(act, wout_ref[...], preferred_element_type=F32)
    y_ref[0] = x1_s[rd] + _rms(f, gains_ref[3:4, :])

    @pl.when((jp == nj - 1) & (t < n_tile))
    def _():
        for cb in range(n_blk):
            pool_ref[0, :, cb * LANES:(cb + 1) * LANES] = hp_ref[
                cb, tm + POOL_HALO - POOL_BUF:tm + POOL_HALO, :]

    hp_ref[:, 0:POOL_HALO, :] = hp_ref[:, tm:tm + POOL_HALO, :]


def _pool_layer_sample_kernel(x_ref, buf_ref, gains_ref, pw_ref, ps_ref, win_ref, wout_ref,
                              y_ref, pool_ref, hs_ref, d_ref, *, tb, seq, n_past):
    m_rows = tb * seq
    x = x_ref[...].reshape(m_rows, D_MODEL)
    h = _rms(x, gains_ref[0:1, :])
    n_blk = D_MODEL // LANES
    blk_per_grp = POOL_GROUP_DIM // LANES
    at_time = lambda t: pl.ds(t, tb, stride=seq)
    for cb in range(n_blk):
        lanes = slice(cb * LANES, (cb + 1) * LANES)
        hs_ref[cb] = h[:, lanes]
        win = POOL_WINDOWS[cb // blk_per_grp]
        hist = [buf_ref[s, :, lanes] for s in range(n_past)]
        hist += [hs_ref[cb, at_time(t), :] for t in range(seq)]
        for t in range(seq):
            tot = hist[n_past + t]
            for s in range(1, win):
                tot = tot + hist[n_past + t - s]
            d_ref[cb, at_time(t), :] = tot * (1.0 / min(win, t + 1 + n_past)) - hist[n_past + t]
        for s in range(n_past):
            pool_ref[s, :, lanes] = hist[seq + s]
    diffs = [jnp.concatenate([d_ref[cb] for cb in range(gi * blk_per_grp, (gi + 1) * blk_per_grp)],
                             axis=1) for gi in range(len(POOL_WINDOWS))]
    m = _pool_project(diffs, pw_ref, ps_ref[...])
    y = _residual_ffn(x, m, gains_ref[1:2, :], gains_ref[2:3, :], gains_ref[3:4, :],
                      win_ref, wout_ref)
    y_ref[...] = y.reshape(tb, seq, D_MODEL)


def _pool_layer_prompt(x, gains, pw, ps, win, wout, layer):
    b, l, d = x.shape
    tm = PROMPT_TILE
    nj = l // tm
    n_tile = b * nj
    mixed = lambda t: jnp.minimum(t, n_tile - 1)
    done = lambda t: jnp.maximum(t - 1, 0)
    return pl.pallas_call(
        functools.partial(_pool_layer_prompt_kernel, tm=tm, nj=nj, n_tile=n_tile),
        grid=(n_tile + 1,),
        in_specs=[
            pl.BlockSpec((1, tm, d), lambda t: (mixed(t) // nj, mixed(t) % nj, 0)),
            _const_spec(gains.shape), _const_spec(pw.shape), _const_spec(ps.shape),
            _layer_spec(win, layer), _layer_spec(wout, layer),
        ],
        out_specs=[
            pl.BlockSpec((1, tm, d), lambda t: (done(t) // nj, done(t) % nj, 0)),
            pl.BlockSpec((1, POOL_BUF, d), lambda t: (mixed(t) // nj, 0, 0)),
        ],
        out_shape=[jax.ShapeDtypeStruct((b, l, d), F32),
                   jax.ShapeDtypeStruct((b, POOL_BUF, d), F32)],
        scratch_shapes=[pltpu.VMEM((d // LANES, POOL_HALO + tm, LANES), F32),
                        pltpu.VMEM((d // LANES, tm, LANES), F32),
                        pltpu.VMEM((2, tm, d), F32), pltpu.VMEM((2, tm, d), BF16)],
        compiler_params=_params("arbitrary"),
        name="pool_layer_prompt",
    )(x, gains, pw, ps, win, wout)


def _pool_layer_sample(x, buf, gains, pw, ps, win, wout, layer):
    b, l, d = x.shape
    tb = SAMPLE_BTILE
    n_past = buf.shape[0]
    assert n_past == POOL_BUF
    return pl.pallas_call(
        functools.partial(_pool_layer_sample_kernel, tb=tb, seq=l, n_past=n_past),
        grid=(b // tb,),
        in_specs=[
            pl.BlockSpec((tb, l, d), lambda i: (i, 0, 0)),
            pl.BlockSpec((POOL_BUF, tb, d), lambda i: (0, i, 0)),
            _const_spec(gains.shape), _const_spec(pw.shape), _const_spec(ps.shape),
            _layer_spec(win, layer), _layer_spec(wout, layer),
        ],
        out_specs=[
            pl.BlockSpec((tb, l, d), lambda i: (i, 0, 0)),
            pl.BlockSpec((POOL_BUF, tb, d), lambda i: (0, i, 0)),
        ],
        out_shape=[jax.ShapeDtypeStruct((b, l, d), F32),
                   jax.ShapeDtypeStruct((POOL_BUF, b, d), F32)],
        scratch_shapes=[pltpu.VMEM((d // LANES, tb * l, LANES), F32),
                        pltpu.VMEM((d // LANES, tb * l, LANES), F32)],
        compiler_params=_params("arbitrary"),
        name="pool_layer_sample",
    )(x, buf, gains, pw, ps, win, wout)


def _gdn_qkv_slab(sl, conv, q_ref, k_ref, v_ref, rows):
    if sl >= 2 * K_HEADS:
        v_ref[sl - 2 * K_HEADS, rows, :] = conv
        return
    unit = conv * lax.rsqrt(jnp.sum(conv * conv, axis=-1, keepdims=True) + EPS)
    if sl < K_HEADS:
        q_ref[sl, rows, :] = unit * (HEAD_DIM ** -0.5)
    else:
        k_ref[sl - K_HEADS, rows, :] = unit


def _gdn_gates(ba, alog, dtb, beta_ref, g_ref):
    beta_ref[...] = _sigmoid(ba[:, :LANES])
    g_ref[...] = -jnp.exp(alog) * _softplus(ba[:, LANES:] + dtb)


def _gdn_pre_prompt_kernel(x_ref, gain_ref, wqkvz_ref, wba_ref, cw_ref, alog_ref, dtb_ref,
                           q_ref, k_ref, v_ref, z_ref, beta_ref, g_ref, conv_ref, up_ref, *, tm):
    j = pl.program_id(1)
    h = _rms(x_ref[0], gain_ref[...]).astype(BF16)
    n_slab = CONV_DIM // HEAD_DIM
    n_grp = n_slab // PRE_SLABS

    @pl.when(j == 0)
    def _():
        up_ref[:, 0:CONV_HALO, :] = jnp.zeros((n_slab, CONV_HALO, HEAD_DIM), F32)

    def project(grp):
        c0 = grp * PRE_SLABS * HEAD_DIM
        p = jnp.dot(h, wqkvz_ref[:, c0:c0 + PRE_SLABS * HEAD_DIM], preferred_element_type=F32)
        for t in range(PRE_SLABS):
            up_ref[grp * PRE_SLABS + t, CONV_HALO:CONV_HALO + tm, :] = (
                p[:, t * HEAD_DIM:(t + 1) * HEAD_DIM])

    def project_z(half):
        c0 = CONV_DIM + half * (V_DIM // 2)
        p = jnp.dot(h, wqkvz_ref[:, c0:c0 + V_DIM // 2], preferred_element_type=F32)
        for t in range(V_HEADS // 2):
            z_ref[half * (V_HEADS // 2) + t] = p[:, t * HEAD_DIM:(t + 1) * HEAD_DIM]

    base = CONV_HALO - (CONV_WIDTH - 1)

    def convolve(grp):
        for sl in range(grp * PRE_SLABS, (grp + 1) * PRE_SLABS):
            lanes = slice(sl * HEAD_DIM, (sl + 1) * HEAD_DIM)
            taps = [cw_ref[tap:tap + 1, lanes] for tap in range(CONV_WIDTH)]
            for row0 in range(0, tm, ROW_GROUP):
                tiles = [up_ref[sl, _strided_rows(row0 + base + s), :]
                         for s in range(ROW_STRIDE + CONV_WIDTH - 1)]
                for r in range(ROW_STRIDE):
                    acc = tiles[r] * taps[0]
                    for tap in range(1, CONV_WIDTH):
                        acc = acc + tiles[r + tap] * taps[tap]
                    _gdn_qkv_slab(sl, _silu(acc), q_ref, k_ref, v_ref, _strided_rows(row0 + r))

    project(0)
    for grp in range(n_grp):
        if grp + 1 < n_grp:
            project(grp + 1)
        else:
            project_z(0)
        convolve(grp)
    project_z(1)
    ba = jnp.dot(h, wba_ref[...], preferred_element_type=F32)
    _gdn_gates(ba, alog_ref[...], dtb_ref[...], beta_ref, g_ref)

    @pl.when(j == pl.num_programs(1) - 1)
    def _():
        for sl in range(n_slab):
            conv_ref[0, :, sl * HEAD_DIM:(sl + 1) * HEAD_DIM] = up_ref[
                sl, tm + CONV_HALO - (CONV_WIDTH - 1):tm + CONV_HALO, :]

    up_ref[:, 0:CONV_HALO, :] = up_ref[:, tm:tm + CONV_HALO, :]


def _gdn_pre_sample_kernel(x_ref, buf_ref, gain_ref, wqkvz_ref, wba_ref, cw_ref, alog_ref,
                           dtb_ref, q_ref, k_ref, v_ref, z_ref, beta_ref, g_ref, conv_ref, up_ref,
                           *, tb, seq):
    m = tb * seq
    n_buf = CONV_WIDTH - 1
    h = _rms(x_ref[...].reshape(m, D_MODEL), gain_ref[...]).astype(BF16)
    proj = jnp.dot(h, wqkvz_ref[:, :CONV_DIM + V_DIM], preferred_element_type=F32)
    ba = jnp.dot(h, wba_ref[...], preferred_element_type=F32)
    for hh in range(V_HEADS):
        z_ref[hh] = proj[:, CONV_DIM + hh * HEAD_DIM:CONV_DIM + (hh + 1) * HEAD_DIM]
    _gdn_gates(ba, alog_ref[...], dtb_ref[...], beta_ref, g_ref)
    at_time = lambda t: pl.ds(t, tb, stride=seq)
    for sl in range(CONV_DIM // HEAD_DIM):
        lanes = slice(sl * HEAD_DIM, (sl + 1) * HEAD_DIM)
        up_ref[sl] = proj[:, lanes]
        ups = [buf_ref[s, :, lanes] for s in range(n_buf)]
        ups += [up_ref[sl, at_time(t), :] for t in range(seq)]
        for t in range(seq):
            acc = ups[t] * cw_ref[0:1, lanes]
            for tap in range(1, CONV_WIDTH):
                acc = acc + ups[t + tap] * cw_ref[tap:tap + 1, lanes]
            _gdn_qkv_slab(sl, _silu(acc), q_ref, k_ref, v_ref, at_time(t))
        for s in range(n_buf):
            conv_ref[s, :, lanes] = ups[seq + s]


def _gdn_pre_out_shapes(n):
    return [jax.ShapeDtypeStruct((K_HEADS, n, HEAD_DIM), F32),
            jax.ShapeDtypeStruct((K_HEADS, n, HEAD_DIM), F32),
            jax.ShapeDtypeStruct((V_HEADS, n, HEAD_DIM), F32),
            jax.ShapeDtypeStruct((V_HEADS, n, HEAD_DIM), F32),
            jax.ShapeDtypeStruct((n, LANES), F32), jax.ShapeDtypeStruct((n, LANES), F32)]


def _gdn_pre_out_specs(rows, index):
    heads = lambda n: pl.BlockSpec((n, rows, HEAD_DIM), lambda *g: (0, index(*g), 0))
    lane = pl.BlockSpec((rows, LANES), lambda *g: (index(*g), 0))
    return [heads(K_HEADS), heads(K_HEADS), heads(V_HEADS), heads(V_HEADS), lane, lane]


def _gdn_pre_prompt(x, gain, wqkvz, wba, cw, alog, dtb):
    b, l, d = x.shape
    tm = GDN_PRE_TILE
    nj = l // tm
    return pl.pallas_call(
        functools.partial(_gdn_pre_prompt_kernel, tm=tm),
        grid=(b, nj),
        in_specs=[pl.BlockSpec((1, tm, d), lambda i, j: (i, j, 0))]
        + [_const_spec(a.shape) for a in (gain, wqkvz, wba, cw, alog, dtb)],
        out_specs=_gdn_pre_out_specs(tm, lambda i, j: i * nj + j)
        + [pl.BlockSpec((1, CONV_WIDTH - 1, CONV_DIM), lambda i, j: (i, 0, 0))],
        out_shape=_gdn_pre_out_shapes(b * l)
        + [jax.ShapeDtypeStruct((b, CONV_WIDTH - 1, CONV_DIM), F32)],
        scratch_shapes=[pltpu.VMEM((CONV_DIM // HEAD_DIM, CONV_HALO + tm, HEAD_DIM), F32)],
        compiler_params=_params("arbitrary", "arbitrary"),
        name="gdn_pre_prompt",
    )(x, gain, wqkvz, wba, cw, alog, dtb)


def _gdn_pre_sample(x, buf, gain, wqkvz, wba, cw, alog, dtb):
    b, l, d = x.shape
    tb = SAMPLE_BTILE
    state = pl.BlockSpec((CONV_WIDTH - 1, tb, CONV_DIM), lambda i: (0, i, 0))
    return pl.pallas_call(
        functools.partial(_gdn_pre_sample_kernel, tb=tb, seq=l),
        grid=(b // tb,),
        in_specs=[pl.BlockSpec((tb, l, d), lambda i: (i, 0, 0)), state]
        + [_const_spec(a.shape) for a in (gain, wqkvz, wba, cw, alog, dtb)],
        out_specs=_gdn_pre_out_specs(tb * l, lambda i: i) + [state],
        out_shape=_gdn_pre_out_shapes(b * l)
        + [jax.ShapeDtypeStruct((CONV_WIDTH - 1, b, CONV_DIM), F32)],
        scratch_shapes=[pltpu.VMEM((CONV_DIM // HEAD_DIM, tb * l, HEAD_DIM), F32)],
        compiler_params=_params("arbitrary"),
        name="gdn_pre_sample",
    )(x, buf, gain, wqkvz, wba, cw, alog, dtb)


def _unit_lower_inverses(mats, c):
    ri = lax.broadcasted_iota(jnp.int32, (c, c), 0)
    ci = lax.broadcasted_iota(jnp.int32, (c, c), 1)
    eye = (ri == ci).astype(F32)
    pair = ((ri // 2) == (ci // 2)) & (ri > ci)
    xs = [eye - jnp.where(pair, a, 0.0) for a in mats]
    mats = [a.astype(BF16) for a in mats]
    blk = 2
    while blk < c:
        off = ((ri // (2 * blk)) == (ci // (2 * blk))) & ((ri // blk) > (ci // blk))
        xbs = [x.astype(BF16) for x in xs]
        ys = [_dot(jnp.where(off, a, jnp.zeros_like(a)), xb) for a, xb in zip(mats, xbs)]
        yield
        xs = [x - _dot(xb, y) for x, xb, y in zip(xs, xbs, ys)]
        yield
        blk *= 2
    return xs


def _delta_chunks(q_ref, k_ref, v_ref, z_ref, beta_ref, g_ref, og_ref, onorm, rows,
                  state_load, state_store, c):
    n = len(rows)
    ri = lax.broadcasted_iota(jnp.int32, (c, c), 0)
    ci = lax.broadcasted_iota(jnp.int32, (c, c), 1)
    causal = ri >= ci
    strict = ri > ci
    rep = V_HEADS // K_HEADS
    units = [(i, h) for i in range(n) for h in range(V_HEADS)]
    kunits = [(i, j) for i in range(n) for j in range(K_HEADS)]

    gcum = [_cumsum_rows(g_ref[rows[i], :], c) for i in range(n)]
    gcum_t = [x.T for x in gcum]
    egcum = [jnp.exp(x) for x in gcum]
    etail_t = [x[:, c - 1:c] - x for x in gcum_t]
    etail_t = [jnp.exp(x) for x in etail_t]
    beta = [beta_ref[rows[i], :] for i in range(n)]
    ks = {(i, j): k_ref[j, rows[i], :] for i, j in kunits}
    kts = {u: ks[u].T for u in kunits}
    kq = {(i, j): _dot(jnp.concatenate([ks[i, j], q_ref[j, rows[i], :]], axis=0), kts[i, j])
          for i, j in kunits}
    yield
    gcol = {(i, h): gcum[i][:, h:h + 1] for i, h in units}
    bcol = {(i, h): beta[i][:, h:h + 1] for i, h in units}
    egc = {(i, h): egcum[i][:, h:h + 1] for i, h in units}
    decay = {(i, h): jnp.where(
        causal, jnp.exp(jnp.minimum(gcol[i, h] - gcum_t[i][h:h + 1, :], 0.0)), 0.0)
        for i, h in units}
    a_mats = [jnp.where(strict, kq[i, h // rep][:c] * bcol[i, h] * decay[i, h], 0.0)
              for i, h in units]
    t_inv = dict(zip(units, (yield from _unit_lower_inverses(a_mats, c))))
    uw = {(i, h): _dot(t_inv[i, h], jnp.concatenate(
        [v_ref[h, rows[i], :] * bcol[i, h], ks[i, h // rep] * (bcol[i, h] * egc[i, h])], axis=1))
        for i, h in units}
    yield
    wq = {(i, h): jnp.concatenate(
        [uw[i, h][:, HEAD_DIM:], q_ref[h // rep, rows[i], :] * egc[i, h]], axis=0).astype(BF16)
        for i, h in units}
    qkd = {(i, h): kq[i, h // rep][c:] * decay[i, h] for i, h in units}
    stacked = c % (2 * SUBLANES) == 0
    if stacked:
        qkd = {(i, h): jnp.concatenate(
            [qkd[i, h], kts[i, h // rep] * etail_t[i][h:h + 1, :]], axis=0).astype(BF16)
            for i, h in units}
    else:
        qkd = {u: x.astype(BF16) for u, x in qkd.items()}

    yield STATE_STAGES
    heads = range(V_HEADS)
    for i in range(n):
        s_old = [state_load(i, h) for h in heads]
        ws = [_dot(wq[i, h], s_old[h]) for h in heads]
        yield
        v_new = [uw[i, h][:, :HEAD_DIM] - ws[h][:c] for h in heads]
        ov = [_dot(qkd[i, h], v_new[h]) for h in heads]
        yield
        for h in heads:
            if stacked:
                kv = ov[h][c:]
            else:
                kv = _dot(kts[i, h // rep] * etail_t[i][h:h + 1, :], v_new[h])
            state_store(i, h, s_old[h] * egcum[i][c - 1:c, h:h + 1] + kv)
        for h in heads:
            og_ref[rows[i], h * HEAD_DIM:(h + 1) * HEAD_DIM] = (
                _rms(ws[h][c:] + ov[h][:c], onorm) * _silu(z_ref[h, rows[i], :])).astype(BF16)


def _pipeline_groups(groups):
    def to_state_stages(group, filler=None):
        for n, tag in enumerate(group, 1):
            if tag == STATE_STAGES:
                return
            if filler is not None and n % SCAN_INTERLEAVE == 0:
                next(filler, None)

    to_state_stages(groups[0])
    for g, group in enumerate(groups):
        if g + 1 < len(groups):
            to_state_stages(groups[g + 1], filler=group)
        for _ in group:
            pass


def _cumsum_rows(g, c):
    ri = lax.broadcasted_iota(jnp.int32, (c, c), 0)
    ci = lax.broadcasted_iota(jnp.int32, (c, c), 1)
    tri = (ri >= ci).astype(BF16)
    hi = g.astype(BF16)
    rest = g - hi.astype(F32)
    mid = rest.astype(BF16)
    lo = (rest - mid.astype(F32)).astype(BF16)
    return (jnp.dot(tri, hi, preferred_element_type=F32)
            + jnp.dot(tri, mid, preferred_element_type=F32)
            + jnp.dot(tri, lo, preferred_element_type=F32))


def _gdn_scan_prompt_kernel(q_ref, k_ref, v_ref, z_ref, beta_ref, g_ref, onorm_ref,
                            og_ref, s_ref, *, c, n_chunk):
    @pl.when(pl.program_id(1) == 0)
    def _():
        s_ref[...] = jnp.zeros(s_ref.shape, F32)

    def load(i, hh):
        return s_ref[0, hh]

    def store(i, hh, val):
        s_ref[0, hh] = val

    groups = []
    first = 0
    for size in SCAN_GROUPS:
        rows = [pl.ds((first + i) * c, c) for i in range(size)]
        groups.append(_delta_chunks(q_ref, k_ref, v_ref, z_ref, beta_ref, g_ref, og_ref,
                                    onorm_ref[...], rows, load, store, c))
        first += size
    assert first == n_chunk
    _pipeline_groups(groups)


def _gdn_scan_sample_kernel(q_ref, k_ref, v_ref, z_ref, beta_ref, g_ref, onorm_ref, s0_ref,
                            og_ref, s_ref, *, tb, c):
    groups = []
    for b0 in range(0, tb, SAMPLE_SCAN_GROUP):
        def load(i, hh, b0=b0):
            return s0_ref[b0 + i, hh]

        def store(i, hh, val, b0=b0):
            s_ref[b0 + i, hh] = val

        rows = [pl.ds((b0 + i) * c, c) for i in range(SAMPLE_SCAN_GROUP)]
        groups.append(_delta_chunks(q_ref, k_ref, v_ref, z_ref, beta_ref, g_ref, og_ref,
                                    onorm_ref[...], rows, load, store, c))
    _pipeline_groups(groups)


def _gdn_scan_prompt(q, k, v, z, beta, g, onorm, b, l):
    c = PROMPT_CHUNK
    tm = SCAN_TILE
    nc = l // tm
    tok = lambda w: pl.BlockSpec((tm, w), lambda i, j: (i * nc + j, 0))
    return pl.pallas_call(
        functools.partial(_gdn_scan_prompt_kernel, c=c, n_chunk=tm // c),
        grid=(b, nc),
        in_specs=_gdn_pre_out_specs(tm, lambda i, j: i * nc + j) + [_const_spec(onorm.shape)],
        out_specs=[tok(V_DIM),
                   pl.BlockSpec((1, V_HEADS, HEAD_DIM, HEAD_DIM), lambda i, j: (i, 0, 0, 0))],
        out_shape=[jax.ShapeDtypeStruct((b * l, V_DIM), BF16),
                   jax.ShapeDtypeStruct((b, V_HEADS, HEAD_DIM, HEAD_DIM), F32)],
        compiler_params=_params("arbitrary", "arbitrary"),
        name="gdn_scan_prompt",
    )(q, k, v, z, beta, g, onorm)


def _gdn_scan_sample(q, k, v, z, beta, g, onorm, s0, b, l):
    tb = SAMPLE_SCAN_BTILE
    tok = lambda w: pl.BlockSpec((tb * l, w), lambda i: (i, 0))
    st = pl.BlockSpec((tb, V_HEADS, HEAD_DIM, HEAD_DIM), lambda i: (i, 0, 0, 0))
    return pl.pallas_call(
        functools.partial(_gdn_scan_sample_kernel, tb=tb, c=l),
        grid=(b // tb,),
        in_specs=_gdn_pre_out_specs(tb * l, lambda i: i) + [_const_spec(onorm.shape), st],
        out_specs=[tok(V_DIM), st],
        out_shape=[jax.ShapeDtypeStruct((b * l, V_DIM), BF16),
                   jax.ShapeDtypeStruct((b, V_HEADS, HEAD_DIM, HEAD_DIM), F32)],
        compiler_params=_params("arbitrary"),
        name="gdn_scan_sample",
    )(q, k, v, z, beta, g, onorm, s0)


def _gdn_post_kernel(og_ref, x_ref, gains_ref, wo_ref, win_ref, wout_ref, y_ref):
    m = jnp.dot(og_ref[...], wo_ref[...], preferred_element_type=F32)
    y_ref[...] = _residual_ffn(x_ref[...], m, gains_ref[0:1, :], gains_ref[1:2, :],
                               gains_ref[2:3, :], win_ref, wout_ref)


def _gdn_post(og, x, gains, wo, win, wout, layer):
    n, d = x.shape
    tm = PROMPT_TILE
    return pl.pallas_call(
        _gdn_post_kernel,
        grid=(n // tm,),
        in_specs=[pl.BlockSpec((tm, V_DIM), lambda i: (i, 0)),
                  pl.BlockSpec((tm, d), lambda i: (i, 0))]
        + [_const_spec(gains.shape), _const_spec(wo.shape), _layer_spec(win, layer),
           _layer_spec(wout, layer)],
        out_specs=pl.BlockSpec((tm, d), lambda i: (i, 0)),
        out_shape=jax.ShapeDtypeStruct((n, d), F32),
        compiler_params=_params("arbitrary"),
        name="gdn_post",
    )(og, x, gains, wo, win, wout)


def _head_lanes(vec):
    return jnp.pad(vec.astype(F32), (0, LANES - V_HEADS)).reshape(1, LANES)


def kernel(x_prompt, x_sample, state_pool, state_gdn_conv, state_gdn_rec, norm_mix_pre,
           norm_mix_post, norm_ffn_pre, norm_ffn_post, pool_w, pool_scale, gdn_w_in,
           gdn_conv_w, gdn_a_log, gdn_dt_bias, gdn_o_norm, gdn_w_out, ffn_w_in, ffn_w_out):
    bp, lp, d = x_prompt.shape
    bs, ls, _ = x_sample.shape

    gains0 = jnp.stack([norm_mix_pre[0], norm_mix_post[0], norm_ffn_pre[0], norm_ffn_post[0]])
    gains1 = jnp.stack([norm_mix_post[1], norm_ffn_pre[1], norm_ffn_post[1]])
    gain1_pre = norm_mix_pre[1].reshape(1, d)
    pw = pool_w[0].astype(BF16)
    ps = pool_scale[0].reshape(1, d)
    win, wout = ffn_w_in.astype(BF16), ffn_w_out.astype(BF16)
    w_in = gdn_w_in[0]
    wqkvz = w_in.astype(BF16)
    w_b = w_in[:, CONV_DIM + V_DIM:CONV_DIM + V_DIM + V_HEADS]
    w_a = w_in[:, CONV_DIM + V_DIM + V_HEADS:]
    lane_pad = ((0, 0), (0, LANES - V_HEADS))
    wba = jnp.concatenate([jnp.pad(w_b, lane_pad), jnp.pad(w_a, lane_pad)], axis=1).astype(BF16)
    cw = gdn_conv_w[0]
    alog, dtb = _head_lanes(gdn_a_log[0]), _head_lanes(gdn_dt_bias[0])
    onorm = gdn_o_norm[0].reshape(1, HEAD_DIM)
    wo = gdn_w_out[0].astype(BF16)

    xp1, pool_p = _pool_layer_prompt(x_prompt, gains0, pw, ps, win, wout, 0)
    time_major = lambda a: jnp.transpose(a, (1, 0, 2))
    xs1, pool_s = _pool_layer_sample(x_sample, time_major(state_pool[0]), gains0, pw, ps, win,
                                     wout, 0)

    qp, kp, vp, zp, betap, gp, conv_p = _gdn_pre_prompt(xp1, gain1_pre, wqkvz, wba, cw, alog, dtb)
    qs, ks, vs, zs, betas, gs, conv_s = _gdn_pre_sample(xs1, time_major(state_gdn_conv[0]),
                                                        gain1_pre, wqkvz, wba, cw, alog, dtb)

    ogp, rec_p = _gdn_scan_prompt(qp, kp, vp, zp, betap, gp, onorm, bp, lp)
    ogs, rec_s = _gdn_scan_sample(qs, ks, vs, zs, betas, gs, onorm, state_gdn_rec[0], bs, ls)

    yp = _gdn_post(ogp, xp1.reshape(bp * lp, d), gains1, wo, win, wout, 1).reshape(bp, lp, d)
    ys = _gdn_post(ogs, xs1.reshape(bs * ls, d), gains1, wo, win, wout, 1).reshape(bs, ls, d)

    return (yp, ys, pool_p[None], time_major(pool_s)[None], conv_p[None],
            time_major(conv_s)[None], rec_p[None], rec_s[None])
```

```python
import functools

import jax
import jax.numpy as jnp
from jax import lax
from jax.experimental import pallas as pl
from jax.experimental.pallas import tpu as pltpu

D_MODEL = 1024
POOL_WINDOWS = (2, 4, 8, 16)
POOL_GROUP_DIM = D_MODEL // len(POOL_WINDOWS)
POOL_BUF = max(POOL_WINDOWS) - 1
K_HEADS = 8
V_HEADS = 16
HEAD_DIM = 128
QK_DIM = K_HEADS * HEAD_DIM
V_DIM = V_HEADS * HEAD_DIM
CONV_DIM = 2 * QK_DIM + V_DIM
CONV_WIDTH = 4
D_FF = 2816
EPS = 1e-6

F32 = jnp.float32
BF16 = jnp.bfloat16

SUBLANES = 8
LANES = 128
POOL_HALO = 16
CONV_HALO = SUBLANES
VMEM_LIMIT = 56 * 1024 * 1024
ROW_STRIDE = 4
ROW_GROUP = SUBLANES * ROW_STRIDE

PROMPT_TILE = 512
GDN_PRE_TILE = 256
PRE_SLABS = 8
SAMPLE_BTILE = 32
PROMPT_CHUNK = 64
SCAN_TILE = 512
SCAN_GROUPS = (2, 2, 2, 2)
SAMPLE_SCAN_BTILE = 8
SAMPLE_SCAN_GROUP = 4
SCAN_INTERLEAVE = 3
STATE_STAGES = "state stages"


def _rms(x, gain):
    ms = jnp.mean(x * x, axis=-1, keepdims=True)
    return x * lax.rsqrt(ms + EPS) * gain


def _sigmoid(x):
    return 1.0 / (1.0 + jnp.exp(-x))


def _silu(x):
    return x * _sigmoid(x)


def _softplus(x):
    return jnp.maximum(x, 0.0) + jnp.log1p(jnp.exp(-jnp.abs(x)))


def _dot(a, b):
    return jnp.dot(a.astype(BF16), b.astype(BF16), preferred_element_type=F32)


def _const_spec(shape):
    nd = len(shape)
    return pl.BlockSpec(shape, lambda *_: (0,) * nd, pipeline_mode=pl.Buffered(1))


def _layer_spec(stacked, layer):
    nd = stacked.ndim - 1
    return pl.BlockSpec((None,) + stacked.shape[1:], lambda *_: (layer,) + (0,) * nd,
                        pipeline_mode=pl.Buffered(1))


def _params(*sem):
    return pltpu.CompilerParams(dimension_semantics=sem, vmem_limit_bytes=VMEM_LIMIT)


def _residual_ffn(x, m, g_post, g_fpre, g_fpost, win_ref, wout_ref):
    x1 = x + _rms(m, g_post)
    h = _rms(x1, g_fpre).astype(BF16)
    gate = jnp.dot(h, win_ref[:, :D_FF], preferred_element_type=F32)
    up = jnp.dot(h, win_ref[:, D_FF:], preferred_element_type=F32)
    act = (_silu(gate) * up).astype(BF16)
    f = jnp.dot(act, wout_ref[...], preferred_element_type=F32)
    return x1 + _rms(f, g_fpost)


def _pool_project(diffs, pw_ref, scale):
    parts = [_dot(d, pw_ref[gi]) for gi, d in enumerate(diffs)]
    return jnp.concatenate(parts, axis=-1) * scale


def _strided_rows(first):
    return pl.ds(first, SUBLANES, stride=ROW_STRIDE)


def _pool_layer_prompt_kernel(x_ref, gains_ref, pw_ref, ps_ref, win_ref, wout_ref,
                              y_ref, pool_ref, hp_ref, d_ref, x1_s, h2_s, *, tm, nj, n_tile):
    t = pl.program_id(0)
    wr = lax.rem(t, 2)
    rd = 1 - wr
    jp = lax.rem(jnp.minimum(t, n_tile - 1), nj)
    n_blk = D_MODEL // LANES
    blk_per_grp = POOL_GROUP_DIM // LANES

    @pl.when(t == 0)
    def _():
        x1_s[1] = jnp.zeros(x1_s.shape[1:], F32)
        h2_s[1] = jnp.zeros(h2_s.shape[1:], BF16)

    @pl.when(jp == 0)
    def _():
        hp_ref[:, 0:POOL_HALO, :] = jnp.zeros((n_blk, POOL_HALO, LANES), F32)

    h2 = h2_s[rd]
    gate = jnp.dot(h2, win_ref[:, :D_FF], preferred_element_type=F32)

    x = x_ref[0]
    h = _rms(x, gains_ref[0:1, :])
    for cb in range(n_blk):
        hp_ref[cb, POOL_HALO:POOL_HALO + tm, :] = h[:, cb * LANES:(cb + 1) * LANES]
    t_tile = lax.broadcasted_iota(jnp.int32, (SUBLANES, 1), 0) * ROW_STRIDE + (jp * tm + 1)
    for row0 in range(0, tm, ROW_GROUP):
        for gi, win in enumerate(POOL_WINDOWS):
            invs = [1.0 / jnp.minimum(win, t_tile + (row0 + r)).astype(F32)
                    for r in range(ROW_STRIDE)]
            for cb in range(gi * blk_per_grp, (gi + 1) * blk_per_grp):
                tiles = [hp_ref[cb, _strided_rows(row0 + POOL_HALO - (win - 1) + k), :]
                         for k in range(win + ROW_STRIDE - 1)]
                shared = range(ROW_STRIDE - 1, win)
                common = (functools.reduce(lambda a, b: a + b, [tiles[k] for k in shared])
                          if len(shared) > 1 else None)
                for r in range(ROW_STRIDE):
                    own = [k for k in range(r, r + win) if common is None or k not in shared]
                    tot = tiles[own[0]] if common is None else common + tiles[own[0]]
                    for k in own[1:]:
                        tot = tot + tiles[k]
                    cur = tiles[r + win - 1]
                    d_ref[cb, _strided_rows(row0 + r), :] = tot * invs[r] - cur
    diffs = [jnp.concatenate([d_ref[cb] for cb in range(gi * blk_per_grp, (gi + 1) * blk_per_grp)],
                             axis=1) for gi in range(len(POOL_WINDOWS))]
    m = _pool_project(diffs, pw_ref, ps_ref[...])
    x1_new = x + _rms(m, gains_ref[1:2, :])
    x1_s[wr] = x1_new
    h2_s[wr] = _rms(x1_new, gains_ref[2:3, :]).astype(BF16)

    up = jnp.dot(h2, win_ref[:, D_FF:], preferred_element_type=F32)
    act = (_silu(gate) * up).astype(BF16)
    f = jnp.dot(act, wout_ref[...], preferred_element_type=F32)
    y_ref[0] = x1_s[rd] + _rms(f, gains_ref[3:4, :])

    @pl.when((jp == nj - 1) & (t < n_tile))
    def _():
        for cb in range(n_blk):
            pool_ref[0, :, cb * LANES:(cb + 1) * LANES] = hp_ref[
                cb, tm + POOL_HALO - POOL_BUF:tm + POOL_HALO, :]

    hp_ref[:, 0:POOL_HALO, :] = hp_ref[:, tm:tm + POOL_HALO, :]


def _pool_layer_sample_kernel(x_ref, buf_ref, gains_ref, pw_ref, ps_ref, win_ref, wout_ref,
                              y_ref, pool_ref, hs_ref, d_ref, *, tb, seq, n_past):
    m_rows = tb * seq
    x = x_ref[...].reshape(m_rows, D_MODEL)
    h = _rms(x, gains_ref[0:1, :])
    n_blk = D_MODEL // LANES
    blk_per_grp = POOL_GROUP_DIM // LANES
    at_time = lambda t: pl.ds(t, tb, stride=seq)
    for cb in range(n_blk):
        lanes = slice(cb * LANES, (cb + 1) * LANES)
        hs_ref[cb] = h[:, lanes]
        win = POOL_WINDOWS[cb // blk_per_grp]
        hist = [buf_ref[s, :, lanes] for s in range(n_past)]
        hist += [hs_ref[cb, at_time(t), :] for t in range(seq)]
        for t in range(seq):
            tot = hist[n_past + t]
            for s in range(1, win):
                tot = tot + hist[n_past + t - s]
            d_ref[cb, at_time(t), :] = tot * (1.0 / min(win, t + 1 + n_past)) - hist[n_past + t]
        for s in range(n_past):
            pool_ref[s, :, lanes] = hist[seq + s]
    diffs = [jnp.concatenate([d_ref[cb] for cb in range(gi * blk_per_grp, (gi + 1) * blk_per_grp)],
                             axis=1) for gi in range(len(POOL_WINDOWS))]
    m = _pool_project(diffs, pw_ref, ps_ref[...])
    y = _residual_ffn(x, m, gains_ref[1:2, :], gains_ref[2:3, :], gains_ref[3:4, :],
                      win_ref, wout_ref)
    y_ref[...] = y.reshape(tb, seq, D_MODEL)


def _pool_layer_prompt(x, gains, pw, ps, win, wout, layer):
    b, l, d = x.shape
    tm = PROMPT_TILE
    nj = l // tm
    n_tile = b * nj
    mixed = lambda t: jnp.minimum(t, n_tile - 1)
    done = lambda t: jnp.maximum(t - 1, 0)
    return pl.pallas_call(
        functools.partial(_pool_layer_prompt_kernel, tm=tm, nj=nj, n_tile=n_tile),
        grid=(n_tile + 1,),
        in_specs=[
            pl.BlockSpec((1, tm, d), lambda t: (mixed(t) // nj, mixed(t) % nj, 0)),
            _const_spec(gains.shape), _const_spec(pw.shape), _const_spec(ps.shape),
            _layer_spec(win, layer), _layer_spec(wout, layer),
        ],
        out_specs=[
            pl.BlockSpec((1, tm, d), lambda t: (done(t) // nj, done(t) % nj, 0)),
            pl.BlockSpec((1, POOL_BUF, d), lambda t: (mixed(t) // nj, 0, 0)),
        ],
        out_shape=[jax.ShapeDtypeStruct((b, l, d), F32),
                   jax.ShapeDtypeStruct((b, POOL_BUF, d), F32)],
        scratch_shapes=[pltpu.VMEM((d // LANES, POOL_HALO + tm, LANES), F32),
                        pltpu.VMEM((d // LANES, tm, LANES), F32),
                        pltpu.VMEM((2, tm, d), F32), pltpu.VMEM((2, tm, d), BF16)],
        compiler_params=_params("arbitrary"),
        name="pool_layer_prompt",
    )(x, gains, pw, ps, win, wout)


def _pool_layer_sample(x, buf, gains, pw, ps, win, wout, layer):
    b, l, d = x.shape
    tb = SAMPLE_BTILE
    n_past = buf.shape[0]
    assert n_past == POOL_BUF
    return pl.pallas_call(
        functools.partial(_pool_layer_sample_kernel, tb=tb, seq=l, n_past=n_past),
        grid=(b // tb,),
        in_specs=[
            pl.BlockSpec((tb, l, d), lambda i: (i, 0, 0)),
            pl.BlockSpec((POOL_BUF, tb, d), lambda i: (0, i, 0)),
            _const_spec(gains.shape), _const_spec(pw.shape), _const_spec(ps.shape),
            _layer_spec(win, layer), _layer_spec(wout, layer),
        ],
        out_specs=[
            pl.BlockSpec((tb, l, d), lambda i: (i, 0, 0)),
            pl.BlockSpec((POOL_BUF, tb, d), lambda i: (0, i, 0)),
        ],
        out_shape=[jax.ShapeDtypeStruct((b, l, d), F32),
                   jax.ShapeDtypeStruct((POOL_BUF, b, d), F32)],
        scratch_shapes=[pltpu.VMEM((d // LANES, tb * l, LANES), F32),
                        pltpu.VMEM((d // LANES, tb * l, LANES), F32)],
        compiler_params=_params("arbitrary"),
        name="pool_layer_sample",
    )(x, buf, gains, pw, ps, win, wout)


def _gdn_qkv_slab(sl, conv, q_ref, k_ref, v_ref, rows):
    if sl >= 2 * K_HEADS:
        v_ref[sl - 2 * K_HEADS, rows, :] = conv
        return
    unit = conv * lax.rsqrt(jnp.sum(conv * conv, axis=-1, keepdims=True) + EPS)
    if sl < K_HEADS:
        q_ref[sl, rows, :] = unit * (HEAD_DIM ** -0.5)
    else:
        k_ref[sl - K_HEADS, rows, :] = unit


def _gdn_gates(ba, alog, dtb, beta_ref, g_ref):
    beta_ref[...] = _sigmoid(ba[:, :LANES])
    g_ref[...] = -jnp.exp(alog) * _softplus(ba[:, LANES:] + dtb)


def _gdn_pre_prompt_kernel(x_ref, gain_ref, wqkvz_ref, wba_ref, cw_ref, alog_ref, dtb_ref,
                           q_ref, k_ref, v_ref, z_ref, beta_ref, g_ref, conv_ref, up_ref, *, tm):
    j = pl.program_id(1)
    h = _rms(x_ref[0], gain_ref[...]).astype(BF16)
    n_slab = CONV_DIM // HEAD_DIM
    n_grp = n_slab // PRE_SLABS

    @pl.when(j == 0)
    def _():
        up_ref[:, 0:CONV_HALO, :] = jnp.zeros((n_slab, CONV_HALO, HEAD_DIM), F32)

    def project(grp):
        c0 = grp * PRE_SLABS * HEAD_DIM
        p = jnp.dot(h, wqkvz_ref[:, c0:c0 + PRE_SLABS * HEAD_DIM], preferred_element_type=F32)
        for t in range(PRE_SLABS):
            up_ref[grp * PRE_SLABS + t, CONV_HALO:CONV_HALO + tm, :] = (
                p[:, t * HEAD_DIM:(t + 1) * HEAD_DIM])

    def project_z(half):
        c0 = CONV_DIM + half * (V_DIM // 2)
        p = jnp.dot(h, wqkvz_ref[:, c0:c0 + V_DIM // 2], preferred_element_type=F32)
        for t in range(V_HEADS // 2):
            z_ref[half * (V_HEADS // 2) + t] = p[:, t * HEAD_DIM:(t + 1) * HEAD_DIM]

    base = CONV_HALO - (CONV_WIDTH - 1)

    def convolve(grp):
        for sl in range(grp * PRE_SLABS, (grp + 1) * PRE_SLABS):
            lanes = slice(sl * HEAD_DIM, (sl + 1) * HEAD_DIM)
            taps = [cw_ref[tap:tap + 1, lanes] for tap in range(CONV_WIDTH)]
            for row0 in range(0, tm, ROW_GROUP):
                tiles = [up_ref[sl, _strided_rows(row0 + base + s), :]
                         for s in range(ROW_STRIDE + CONV_WIDTH - 1)]
                for r in range(ROW_STRIDE):
                    acc = tiles[r] * taps[0]
                    for tap in range(1, CONV_WIDTH):
                        acc = acc + tiles[r + tap] * taps[tap]
                    _gdn_qkv_slab(sl, _silu(acc), q_ref, k_ref, v_ref, _strided_rows(row0 + r))

    project(0)
    for grp in range(n_grp):
        if grp + 1 < n_grp:
            project(grp + 1)
        else:
            project_z(0)
        convolve(grp)
    project_z(1)
    ba = jnp.dot(h, wba_ref[...], preferred_element_type=F32)
    _gdn_gates(ba, alog_ref[...], dtb_ref[...], beta_ref, g_ref)

    @pl.when(j == pl.num_programs(1) - 1)
    def _():
        for sl in range(n_slab):
            conv_ref[0, :, sl * HEAD_DIM:(sl + 1) * HEAD_DIM] = up_ref[
                sl, tm + CONV_HALO - (CONV_WIDTH - 1):tm + CONV_HALO, :]

    up_ref[:, 0:CONV_HALO, :] = up_ref[:, tm:tm + CONV_HALO, :]


def _gdn_pre_sample_kernel(x_ref, buf_ref, gain_ref, wqkvz_ref, wba_ref, cw_ref, alog_ref,
                           dtb_ref, q_ref, k_ref, v_ref, z_ref, beta_ref, g_ref, conv_ref, up_ref,
                           *, tb, seq):
    m = tb * seq
    n_buf = CONV_WIDTH - 1
    h = _rms(x_ref[...].reshape(m, D_MODEL), gain_ref[...]).astype(BF16)
    proj = jnp.dot(h, wqkvz_ref[:, :CONV_DIM + V_DIM], preferred_element_type=F32)
    ba = jnp.dot(h, wba_ref[...], preferred_element_type=F32)
    for hh in range(V_HEADS):
        z_ref[hh] = proj[:, CONV_DIM + hh * HEAD_DIM:CONV_DIM + (hh + 1) * HEAD_DIM]
    _gdn_gates(ba, alog_ref[...], dtb_ref[...], beta_ref, g_ref)
    at_time = lambda t: pl.ds(t, tb, stride=seq)
    for sl in range(CONV_DIM // HEAD_DIM):
        lanes = slice(sl * HEAD_DIM, (sl + 1) * HEAD_DIM)
        up_ref[sl] = proj[:, lanes]
        ups = [buf_ref[s, :, lanes] for s in range(n_buf)]
        ups += [up_ref[sl, at_time(t), :] for t in range(seq)]
        for t in range(seq):
            acc = ups[t] * cw_ref[0:1, lanes]
            for tap in range(1, CONV_WIDTH):
                acc = acc + ups[t + tap] * cw_ref[tap:tap + 1, lanes]
            _gdn_qkv_slab(sl, _silu(acc), q_ref, k_ref, v_ref, at_time(t))
        for s in range(n_buf):
            conv_ref[s, :, lanes] = ups[seq + s]


def _gdn_pre_out_shapes(n):
    return [jax.ShapeDtypeStruct((K_HEADS, n, HEAD_DIM), F32),
            jax.ShapeDtypeStruct((K_HEADS, n, HEAD_DIM), F32),
            jax.ShapeDtypeStruct((V_HEADS, n, HEAD_DIM), F32),
            jax.ShapeDtypeStruct((V_HEADS, n, HEAD_DIM), F32),
            jax.ShapeDtypeStruct((n, LANES), F32), jax.ShapeDtypeStruct((n, LANES), F32)]


def _gdn_pre_out_specs(rows, index):
    heads = lambda n: pl.BlockSpec((n, rows, HEAD_DIM), lambda *g: (0, index(*g), 0))
    lane = pl.BlockSpec((rows, LANES), lambda *g: (index(*g), 0))
    return [heads(K_HEADS), heads(K_HEADS), heads(V_HEADS), heads(V_HEADS), lane, lane]


def _gdn_pre_prompt(x, gain, wqkvz, wba, cw, alog, dtb):
    b, l, d = x.shape
    tm = GDN_PRE_TILE
    nj = l // tm
    return pl.pallas_call(
        functools.partial(_gdn_pre_prompt_kernel, tm=tm),
        grid=(b, nj),
        in_specs=[pl.BlockSpec((1, tm, d), lambda i, j: (i, j, 0))]
        + [_const_spec(a.shape) for a in (gain, wqkvz, wba, cw, alog, dtb)],
        out_specs=_gdn_pre_out_specs(tm, lambda i, j: i * nj + j)
        + [pl.BlockSpec((1, CONV_WIDTH - 1, CONV_DIM), lambda i, j: (i, 0, 0))],
        out_shape=_gdn_pre_out_shapes(b * l)
        + [jax.ShapeDtypeStruct((b, CONV_WIDTH - 1, CONV_DIM), F32)],
        scratch_shapes=[pltpu.VMEM((CONV_DIM // HEAD_DIM, CONV_HALO + tm, HEAD_DIM), F32)],
        compiler_params=_params("arbitrary", "arbitrary"),
        name="gdn_pre_prompt",
    )(x, gain, wqkvz, wba, cw, alog, dtb)


def _gdn_pre_sample(x, buf, gain, wqkvz, wba, cw, alog, dtb):
    b, l, d = x.shape
    tb = SAMPLE_BTILE
    state = pl.BlockSpec((CONV_WIDTH - 1, tb, CONV_DIM), lambda i: (0, i, 0))
    return pl.pallas_call(
        functools.partial(_gdn_pre_sample_kernel, tb=tb, seq=l),
        grid=(b // tb,),
        in_specs=[pl.BlockSpec((tb, l, d), lambda i: (i, 0, 0)), state]
        + [_const_spec(a.shape) for a in (gain, wqkvz, wba, cw, alog, dtb)],
        out_specs=_gdn_pre_out_specs(tb * l, lambda i: i) + [state],
        out_shape=_gdn_pre_out_shapes(b * l)
        + [jax.ShapeDtypeStruct((CONV_WIDTH - 1, b, CONV_DIM), F32)],
        scratch_shapes=[pltpu.VMEM((CONV_DIM // HEAD_DIM, tb * l, HEAD_DIM), F32)],
        compiler_params=_params("arbitrary"),
        name="gdn_pre_sample",
    )(x, buf, gain, wqkvz, wba, cw, alog, dtb)


def _unit_lower_inverses(mats, c):
    ri = lax.broadcasted_iota(jnp.int32, (c, c), 0)
    ci = lax.broadcasted_iota(jnp.int32, (c, c), 1)
    eye = (ri == ci).astype(F32)
    pair = ((ri // 2) == (ci // 2)) & (ri > ci)
    xs = [eye - jnp.where(pair, a, 0.0) for a in mats]
    mats = [a.astype(BF16) for a in mats]
    blk = 2
    while blk < c:
        off = ((ri // (2 * blk)) == (ci // (2 * blk))) & ((ri // blk) > (ci // blk))
        xbs = [x.astype(BF16) for x in xs]
        ys = [_dot(jnp.where(off, a, jnp.zeros_like(a)), xb) for a, xb in zip(mats, xbs)]
        yield
        xs = [x - _dot(xb, y) for x, xb, y in zip(xs, xbs, ys)]
        yield
        blk *= 2
    return xs


def _delta_chunks(q_ref, k_ref, v_ref, z_ref, beta_ref, g_ref, og_ref, onorm, rows,
                  state_load, state_store, c):
    n = len(rows)
    ri = lax.broadcasted_iota(jnp.int32, (c, c), 0)
    ci = lax.broadcasted_iota(jnp.int32, (c, c), 1)
    causal = ri >= ci
    strict = ri > ci
    rep = V_HEADS // K_HEADS
    units = [(i, h) for i in range(n) for h in range(V_HEADS)]
    kunits = [(i, j) for i in range(n) for j in range(K_HEADS)]

    gcum = [_cumsum_rows(g_ref[rows[i], :], c) for i in range(n)]
    gcum_t = [x.T for x in gcum]
    egcum = [jnp.exp(x) for x in gcum]
    etail_t = [x[:, c - 1:c] - x for x in gcum_t]
    etail_t = [jnp.exp(x) for x in etail_t]
    beta = [beta_ref[rows[i], :] for i in range(n)]
    ks = {(i, j): k_ref[j, rows[i], :] for i, j in kunits}
    kts = {u: ks[u].T for u in kunits}
    kq = {(i, j): _dot(jnp.concatenate([ks[i, j], q_ref[j, rows[i], :]], axis=0), kts[i, j])
          for i, j in kunits}
    yield
    gcol = {(i, h): gcum[i][:, h:h + 1] for i, h in units}
    bcol = {(i, h): beta[i][:, h:h + 1] for i, h in units}
    egc = {(i, h): egcum[i][:, h:h + 1] for i, h in units}
    decay = {(i, h): jnp.where(
        causal, jnp.exp(jnp.minimum(gcol[i, h] - gcum_t[i][h:h + 1, :], 0.0)), 0.0)
        for i, h in units}
    a_mats = [jnp.where(strict, kq[i, h // rep][:c] * bcol[i, h] * decay[i, h], 0.0)
              for i, h in units]
    t_inv = dict(zip(units, (yield from _unit_lower_inverses(a_mats, c))))
    uw = {(i, h): _dot(t_inv[i, h], jnp.concatenate(
        [v_ref[h, rows[i], :] * bcol[i, h], ks[i, h // rep] * (bcol[i, h] * egc[i, h])], axis=1))
        for i, h in units}
    yield
    wq = {(i, h): jnp.concatenate(
        [uw[i, h][:, HEAD_DIM:], q_ref[h // rep, rows[i], :] * egc[i, h]], axis=0).astype(BF16)
        for i, h in units}
    qkd = {(i, h): kq[i, h // rep][c:] * decay[i, h] for i, h in units}
    stacked = c % (2 * SUBLANES) == 0
    if stacked:
        qkd = {(i, h): jnp.concatenate(
            [qkd[i, h], kts[i, h // rep] * etail_t[i][h:h + 1, :]], axis=0).astype(BF16)
            for i, h in units}
    else:
        qkd = {u: x.astype(BF16) for u, x in qkd.items()}

    yield STATE_STAGES
    heads = range(V_HEADS)
    for i in range(n):
        s_old = [state_load(i, h) for h in heads]
        ws = [_dot(wq[i, h], s_old[h]) for h in heads]
        yield
        v_new = [uw[i, h][:, :HEAD_DIM] - ws[h][:c] for h in heads]
        ov = [_dot(qkd[i, h], v_new[h]) for h in heads]
        yield
        for h in heads:
            if stacked:
                kv = ov[h][c:]
            else:
                kv = _dot(kts[i, h // rep] * etail_t[i][h:h + 1, :], v_new[h])
            state_store(i, h, s_old[h] * egcum[i][c - 1:c, h:h + 1] + kv)
        for h in heads:
            og_ref[rows[i], h * HEAD_DIM:(h + 1) * HEAD_DIM] = (
                _rms(ws[h][c:] + ov[h][:c], onorm) * _silu(z_ref[h, rows[i], :])).astype(BF16)


def _pipeline_groups(groups):
    def to_state_stages(group, filler=None):
        for n, tag in enumerate(group, 1):
            if tag == STATE_STAGES:
                return
            if filler is not None and n % SCAN_INTERLEAVE == 0:
                next(filler, None)

    to_state_stages(groups[0])
    for g, group in enumerate(groups):
        if g + 1 < len(groups):
            to_state_stages(groups[g + 1], filler=group)
        for _ in group:
            pass


def _cumsum_rows(g, c):
    ri = lax.broadcasted_iota(jnp.int32, (c, c), 0)
    ci = lax.broadcasted_iota(jnp.int32, (c, c), 1)
    tri = (ri >= ci).astype(BF16)
    hi = g.astype(BF16)
    rest = g - hi.astype(F32)
    mid = rest.astype(BF16)
    lo = (rest - mid.astype(F32)).astype(BF16)
    return (jnp.dot(tri, hi, preferred_element_type=F32)
            + jnp.dot(tri, mid, preferred_element_type=F32)
            + jnp.dot(tri, lo, preferred_element_type=F32))


def _gdn_scan_prompt_kernel(q_ref, k_ref, v_ref, z_ref, beta_ref, g_ref, onorm_ref,
                            og_ref, s_ref, *, c, n_chunk):
    @pl.when(pl.program_id(1) == 0)
    def _():
        s_ref[...] = jnp.zeros(s_ref.shape, F32)

    def load(i, hh):
        return s_ref[0, hh]

    def store(i, hh, val):
        s_ref[0, hh] = val

    groups = []
    first = 0
    for size in SCAN_GROUPS:
        rows = [pl.ds((first + i) * c, c) for i in range(size)]
        groups.append(_delta_chunks(q_ref, k_ref, v_ref, z_ref, beta_ref, g_ref, og_ref,
                                    onorm_ref[...], rows, load, store, c))
        first += size
    assert first == n_chunk
    _pipeline_groups(groups)


def _gdn_scan_sample_kernel(q_ref, k_ref, v_ref, z_ref, beta_ref, g_ref, onorm_ref, s0_ref,
                            og_ref, s_ref, *, tb, c):
    groups = []
    for b0 in range(0, tb, SAMPLE_SCAN_GROUP):
        def load(i, hh, b0=b0):
            return s0_ref[b0 + i, hh]

        def store(i, hh, val, b0=b0):
            s_ref[b0 + i, hh] = val

        rows = [pl.ds((b0 + i) * c, c) for i in range(SAMPLE_SCAN_GROUP)]
        groups.append(_delta_chunks(q_ref, k_ref, v_ref, z_ref, beta_ref, g_ref, og_ref,
                                    onorm_ref[...], rows, load, store, c))
    _pipeline_groups(groups)


def _gdn_scan_prompt(q, k, v, z, beta, g, onorm, b, l):
    c = PROMPT_CHUNK
    tm = SCAN_TILE
    nc = l // tm
    tok = lambda w: pl.BlockSpec((tm, w), lambda i, j: (i * nc + j, 0))
    return pl.pallas_call(
        functools.partial(_gdn_scan_prompt_kernel, c=c, n_chunk=tm // c),
        grid=(b, nc),
        in_specs=_gdn_pre_out_specs(tm, lambda i, j: i * nc + j) + [_const_spec(onorm.shape)],
        out_specs=[tok(V_DIM),
                   pl.BlockSpec((1, V_HEADS, HEAD_DIM, HEAD_DIM), lambda i, j: (i, 0, 0, 0))],
        out_shape=[jax.ShapeDtypeStruct((b * l, V_DIM), BF16),
                   jax.ShapeDtypeStruct((b, V_HEADS, HEAD_DIM, HEAD_DIM), F32)],
        compiler_params=_params("arbitrary", "arbitrary"),
        name="gdn_scan_prompt",
    )(q, k, v, z, beta, g, onorm)


def _gdn_scan_sample(q, k, v, z, beta, g, onorm, s0, b, l):
    tb = SAMPLE_SCAN_BTILE
    tok = lambda w: pl.BlockSpec((tb * l, w), lambda i: (i, 0))
    st = pl.BlockSpec((tb, V_HEADS, HEAD_DIM, HEAD_DIM), lambda i: (i, 0, 0, 0))
    return pl.pallas_call(
        functools.partial(_gdn_scan_sample_kernel, tb=tb, c=l),
        grid=(b // tb,),
        in_specs=_gdn_pre_out_specs(tb * l, lambda i: i) + [_const_spec(onorm.shape), st],
        out_specs=[tok(V_DIM), st],
        out_shape=[jax.ShapeDtypeStruct((b * l, V_DIM), BF16),
                   jax.ShapeDtypeStruct((b, V_HEADS, HEAD_DIM, HEAD_DIM), F32)],
        compiler_params=_params("arbitrary"),
        name="gdn_scan_sample",
    )(q, k, v, z, beta, g, onorm, s0)


def _gdn_post_kernel(oga_ref, xa_ref, ogb_ref, xb_ref, gains_ref, wo_ref, win_ref, wout_ref,
                     ya_ref, yb_ref, *, n_a):
    def run(og_ref, x_ref, y_ref):
        m = jnp.dot(og_ref[...], wo_ref[...], preferred_element_type=F32)
        y_ref[...] = _residual_ffn(x_ref[...], m, gains_ref[0:1, :], gains_ref[1:2, :],
                                   gains_ref[2:3, :], win_ref, wout_ref)

    @pl.when(pl.program_id(0) < n_a)
    def _():
        run(oga_ref, xa_ref, ya_ref)

    @pl.when(pl.program_id(0) >= n_a)
    def _():
        run(ogb_ref, xb_ref, yb_ref)


def _gdn_post(og_a, x_a, og_b, x_b, gains, wo, win, wout, layer):
    d = x_a.shape[1]
    tm = PROMPT_TILE
    n_a, n_b = x_a.shape[0] // tm, x_b.shape[0] // tm
    a_idx = lambda i: (jnp.minimum(i, n_a - 1), 0)
    b_idx = lambda i: (jnp.maximum(i - n_a, 0), 0)
    return pl.pallas_call(
        functools.partial(_gdn_post_kernel, n_a=n_a),
        grid=(n_a + n_b,),
        in_specs=[pl.BlockSpec((tm, V_DIM), a_idx), pl.BlockSpec((tm, d), a_idx),
                  pl.BlockSpec((tm, V_DIM), b_idx), pl.BlockSpec((tm, d), b_idx)]
        + [_const_spec(gains.shape), _const_spec(wo.shape), _layer_spec(win, layer),
           _layer_spec(wout, layer)],
        out_specs=[pl.BlockSpec((tm, d), a_idx), pl.BlockSpec((tm, d), b_idx)],
        out_shape=[jax.ShapeDtypeStruct(x_a.shape, F32), jax.ShapeDtypeStruct(x_b.shape, F32)],
        compiler_params=_params("arbitrary"),
        name="gdn_post",
    )(og_a, x_a, og_b, x_b, gains, wo, win, wout)


def _head_lanes(vec):
    return jnp.pad(vec.astype(F32), (0, LANES - V_HEADS)).reshape(1, LANES)


def kernel(x_prompt, x_sample, state_pool, state_gdn_conv, state_gdn_rec, norm_mix_pre,
           norm_mix_post, norm_ffn_pre, norm_ffn_post, pool_w, pool_scale, gdn_w_in,
           gdn_conv_w, gdn_a_log, gdn_dt_bias, gdn_o_norm, gdn_w_out, ffn_w_in, ffn_w_out):
    bp, lp, d = x_prompt.shape
    bs, ls, _ = x_sample.shape

    gains0 = jnp.stack([norm_mix_pre[0], norm_mix_post[0], norm_ffn_pre[0], norm_ffn_post[0]])
    gains1 = jnp.stack([norm_mix_post[1], norm_ffn_pre[1], norm_ffn_post[1]])
    gain1_pre = norm_mix_pre[1].reshape(1, d)
    pw = pool_w[0].astype(BF16)
    ps = pool_scale[0].reshape(1, d)
    win, wout = ffn_w_in.astype(BF16), ffn_w_out.astype(BF16)
    w_in = gdn_w_in[0]
    wqkvz = w_in.astype(BF16)
    w_b = w_in[:, CONV_DIM + V_DIM:CONV_DIM + V_DIM + V_HEADS]
    w_a = w_in[:, CONV_DIM + V_DIM + V_HEADS:]
    lane_pad = ((0, 0), (0, LANES - V_HEADS))
    wba = jnp.concatenate([jnp.pad(w_b, lane_pad), jnp.pad(w_a, lane_pad)], axis=1).astype(BF16)
    cw = gdn_conv_w[0]
    alog, dtb = _head_lanes(gdn_a_log[0]), _head_lanes(gdn_dt_bias[0])
    onorm = gdn_o_norm[0].reshape(1, HEAD_DIM)
    wo = gdn_w_out[0].astype(BF16)

    xp1, pool_p = _pool_layer_prompt(x_prompt, gains0, pw, ps, win, wout, 0)
    time_major = lambda a: jnp.transpose(a, (1, 0, 2))
    xs1, pool_s = _pool_layer_sample(x_sample, time_major(state_pool[0]), gains0, pw, ps, win,
                                     wout, 0)

    qp, kp, vp, zp, betap, gp, conv_p = _gdn_pre_prompt(xp1, gain1_pre, wqkvz, wba, cw, alog, dtb)
    qs, ks, vs, zs, betas, gs, conv_s = _gdn_pre_sample(xs1, time_major(state_gdn_conv[0]),
                                                        gain1_pre, wqkvz, wba, cw, alog, dtb)

    ogp, rec_p = _gdn_scan_prompt(qp, kp, vp, zp, betap, gp, onorm, bp, lp)
    ogs, rec_s = _gdn_scan_sample(qs, ks, vs, zs, betas, gs, onorm, state_gdn_rec[0], bs, ls)

    yp, ys = _gdn_post(ogp, xp1.reshape(bp * lp, d), ogs, xs1.reshape(bs * ls, d), gains1, wo,
                       win, wout, 1)
    yp, ys = yp.reshape(bp, lp, d), ys.reshape(bs, ls, d)

    return (yp, ys, pool_p[None], time_major(pool_s)[None], conv_p[None],
            time_major(conv_s)[None], rec_p[None], rec_s[None])
```

```python
import functools

import jax
import jax.numpy as jnp
from jax import lax
from jax.experimental import pallas as pl
from jax.experimental.pallas import tpu as pltpu

D_MODEL = 1024
POOL_WINDOWS = (2, 4, 8, 16)
POOL_GROUP_DIM = D_MODEL // len(POOL_WINDOWS)
POOL_BUF = max(POOL_WINDOWS) - 1
K_HEADS = 8
V_HEADS = 16
HEAD_DIM = 128
QK_DIM = K_HEADS * HEAD_DIM
V_DIM = V_HEADS * HEAD_DIM
CONV_DIM = 2 * QK_DIM + V_DIM
CONV_WIDTH = 4
D_FF = 2816
EPS = 1e-6

F32 = jnp.float32
BF16 = jnp.bfloat16

SUBLANES = 8
LANES = 128
POOL_HALO = 16
CONV_HALO = SUBLANES
VMEM_LIMIT = 56 * 1024 * 1024
ROW_STRIDE = 4
ROW_GROUP = SUBLANES * ROW_STRIDE

PROMPT_TILE = 512
GDN_PRE_TILE = 256
PRE_SLABS = 8
SAMPLE_BTILE = 32
PROMPT_CHUNK = 64
SCAN_TILE = 512
SCAN_GROUPS = (2, 2, 2, 2)
SAMPLE_SCAN_BTILE = 8
SAMPLE_SCAN_GROUP = 4
SCAN_INTERLEAVE = 3
STATE_STAGES = "state stages"


def _rms(x, gain):
    ms = jnp.mean(x * x, axis=-1, keepdims=True)
    return x * lax.rsqrt(ms + EPS) * gain


def _sigmoid(x):
    return 1.0 / (1.0 + jnp.exp(-x))


def _silu(x):
    return x * _sigmoid(x)


def _softplus(x):
    return jnp.maximum(x, 0.0) + jnp.log1p(jnp.exp(-jnp.abs(x)))


def _dot(a, b):
    return jnp.dot(a.astype(BF16), b.astype(BF16), preferred_element_type=F32)


def _const_spec(shape):
    nd = len(shape)
    return pl.BlockSpec(shape, lambda *_: (0,) * nd, pipeline_mode=pl.Buffered(1))


def _layer_spec(stacked, layer):
    nd = stacked.ndim - 1
    return pl.BlockSpec((None,) + stacked.shape[1:], lambda *_: (layer,) + (0,) * nd,
                        pipeline_mode=pl.Buffered(1))


def _params(*sem):
    return pltpu.CompilerParams(dimension_semantics=sem, vmem_limit_bytes=VMEM_LIMIT)


def _residual_ffn(x, m, g_post, g_fpre, g_fpost, win_ref, wout_ref):
    x1 = x + _rms(m, g_post)
    h = _rms(x1, g_fpre).astype(BF16)
    gate = jnp.dot(h, win_ref[:, :D_FF], preferred_element_type=F32)
    up = jnp.dot(h, win_ref[:, D_FF:], preferred_element_type=F32)
    act = (_silu(gate) * up).astype(BF16)
    f = jnp.dot(act, wout_ref[...], preferred_element_type=F32)
    return x1 + _rms(f, g_fpost)


def _pool_project(diffs, pw_ref, scale):
    parts = [_dot(d, pw_ref[gi]) for gi, d in enumerate(diffs)]
    return jnp.concatenate(parts, axis=-1) * scale


def _strided_rows(first):
    return pl.ds(first, SUBLANES, stride=ROW_STRIDE)


def _pool_layer_prompt_kernel(x_ref, gains_ref, pw_ref, ps_ref, win_ref, wout_ref,
                              y_ref, pool_ref, hp_ref, d_ref, x1_s, h2_s, *, tm, nj, n_tile):
    t = pl.program_id(0)
    wr = lax.rem(t, 2)
    rd = 1 - wr
    jp = lax.rem(jnp.minimum(t, n_tile - 1), nj)
    n_blk = D_MODEL // LANES
    blk_per_grp = POOL_GROUP_DIM // LANES

    @pl.when(t == 0)
    def _():
        x1_s[1] = jnp.zeros(x1_s.shape[1:], F32)
        h2_s[1] = jnp.zeros(h2_s.shape[1:], BF16)

    @pl.when(jp == 0)
    def _():
        hp_ref[:, 0:POOL_HALO, :] = jnp.zeros((n_blk, POOL_HALO, LANES), F32)

    h2 = h2_s[rd]
    gate = jnp.dot(h2, win_ref[:, :D_FF], preferred_element_type=F32)

    x = x_ref[0]
    h = _rms(x, gains_ref[0:1, :])
    for cb in range(n_blk):
        hp_ref[cb, POOL_HALO:POOL_HALO + tm, :] = h[:, cb * LANES:(cb + 1) * LANES]
    t_tile = lax.broadcasted_iota(jnp.int32, (SUBLANES, 1), 0) * ROW_STRIDE + (jp * tm + 1)
    for row0 in range(0, tm, ROW_GROUP):
        for gi, win in enumerate(POOL_WINDOWS):
            invs = [1.0 / jnp.minimum(win, t_tile + (row0 + r)).astype(F32)
                    for r in range(ROW_STRIDE)]
            for cb in range(gi * blk_per_grp, (gi + 1) * blk_per_grp):
                tiles = [hp_ref[cb, _strided_rows(row0 + POOL_HALO - (win - 1) + k), :]
                         for k in range(win + ROW_STRIDE - 1)]
                shared = range(ROW_STRIDE - 1, win)
                common = (functools.reduce(lambda a, b: a + b, [tiles[k] for k in shared])
                          if len(shared) > 1 else None)
                for r in range(ROW_STRIDE):
                    own = [k for k in range(r, r + win) if common is None or k not in shared]
                    tot = tiles[own[0]] if common is None else common + tiles[own[0]]
                    for k in own[1:]:
                        tot = tot + tiles[k]
                    cur = tiles[r + win - 1]
                    d_ref[cb, _strided_rows(row0 + r), :] = tot * invs[r] - cur
    diffs = [jnp.concatenate([d_ref[cb] for cb in range(gi * blk_per_grp, (gi + 1) * blk_per_grp)],
                             axis=1) for gi in range(len(POOL_WINDOWS))]
    m = _pool_project(diffs, pw_ref, ps_ref[...])
    x1_new = x + _rms(m, gains_ref[1:2, :])
    x1_s[wr] = x1_new
    h2_s[wr] = _rms(x1_new, gains_ref[2:3, :]).astype(BF16)

    up = jnp.dot(h2, win_ref[:, D_FF:], preferred_element_type=F32)
    act = (_silu(gate) * up).astype(BF16)
    f = jnp.dot(act, wout_ref[...], preferred_element_type=F32)
    y_ref[0] = x1_s[rd] + _rms(f, gains_ref[3:4, :])

    @pl.when((jp == nj - 1) & (t < n_tile))
    def _():
        for cb in range(n_blk):
            pool_ref[0, :, cb * LANES:(cb + 1) * LANES] = hp_ref[
                cb, tm + POOL_HALO - POOL_BUF:tm + POOL_HALO, :]

    hp_ref[:, 0:POOL_HALO, :] = hp_ref[:, tm:tm + POOL_HALO, :]


def _pool_layer_sample_kernel(x_ref, buf_ref, gains_ref, pw_ref, ps_ref, win_ref, wout_ref,
                              y_ref, pool_ref, hs_ref, d_ref, *, tb, seq, n_past):
    m_rows = tb * seq
    x = x_ref[...].reshape(m_rows, D_MODEL)
    h = _rms(x, gains_ref[0:1, :])
    n_blk = D_MODEL // LANES
    blk_per_grp = POOL_GROUP_DIM // LANES
    at_time = lambda t: pl.ds(t, tb, stride=seq)
    for cb in range(n_blk):
        lanes = slice(cb * LANES, (cb + 1) * LANES)
        hs_ref[cb] = h[:, lanes]
        win = POOL_WINDOWS[cb // blk_per_grp]
        hist = [buf_ref[s, :, lanes] for s in range(n_past)]
        hist += [hs_ref[cb, at_time(t), :] for t in range(seq)]
        for t in range(seq):
            tot = hist[n_past + t]
            for s in range(1, win):
                tot = tot + hist[n_past + t - s]
            d_ref[cb, at_time(t), :] = tot * (1.0 / min(win, t + 1 + n_past)) - hist[n_past + t]
        for s in range(n_past):
            pool_ref[s, :, lanes] = hist[seq + s]
    diffs = [jnp.concatenate([d_ref[cb] for cb in range(gi * blk_per_grp, (gi + 1) * blk_per_grp)],
                             axis=1) for gi in range(len(POOL_WINDOWS))]
    m = _pool_project(diffs, pw_ref, ps_ref[...])
    y = _residual_ffn(x, m, gains_ref[1:2, :], gains_ref[2:3, :], gains_ref[3:4, :],
                      win_ref, wout_ref)
    y_ref[...] = y.reshape(tb, seq, D_MODEL)


def _pool_layer_prompt(x, gains, pw, ps, win, wout, layer):
    b, l, d = x.shape
    tm = PROMPT_TILE
    nj = l // tm
    n_tile = b * nj
    mixed = lambda t: jnp.minimum(t, n_tile - 1)
    done = lambda t: jnp.maximum(t - 1, 0)
    return pl.pallas_call(
        functools.partial(_pool_layer_prompt_kernel, tm=tm, nj=nj, n_tile=n_tile),
        grid=(n_tile + 1,),
        in_specs=[
            pl.BlockSpec((1, tm, d), lambda t: (mixed(t) // nj, mixed(t) % nj, 0)),
            _const_spec(gains.shape), _const_spec(pw.shape), _const_spec(ps.shape),
            _layer_spec(win, layer), _layer_spec(wout, layer),
        ],
        out_specs=[
            pl.BlockSpec((1, tm, d), lambda t: (done(t) // nj, done(t) % nj, 0)),
            pl.BlockSpec((1, POOL_BUF, d), lambda t: (mixed(t) // nj, 0, 0)),
        ],
        out_shape=[jax.ShapeDtypeStruct((b, l, d), F32),
                   jax.ShapeDtypeStruct((b, POOL_BUF, d), F32)],
        scratch_shapes=[pltpu.VMEM((d // LANES, POOL_HALO + tm, LANES), F32),
                        pltpu.VMEM((d // LANES, tm, LANES), F32),
                        pltpu.VMEM((2, tm, d), F32), pltpu.VMEM((2, tm, d), BF16)],
        compiler_params=_params("arbitrary"),
        name="pool_layer_prompt",
    )(x, gains, pw, ps, win, wout)


def _pool_layer_sample(x, buf, gains, pw, ps, win, wout, layer):
    b, l, d = x.shape
    tb = SAMPLE_BTILE
    n_past = buf.shape[0]
    assert n_past == POOL_BUF
    return pl.pallas_call(
        functools.partial(_pool_layer_sample_kernel, tb=tb, seq=l, n_past=n_past),
        grid=(b // tb,),
        in_specs=[
            pl.BlockSpec((tb, l, d), lambda i: (i, 0, 0)),
            pl.BlockSpec((POOL_BUF, tb, d), lambda i: (0, i, 0)),
            _const_spec(gains.shape), _const_spec(pw.shape), _const_spec(ps.shape),
            _layer_spec(win, layer), _layer_spec(wout, layer),
        ],
        out_specs=[
            pl.BlockSpec((tb, l, d), lambda i: (i, 0, 0)),
            pl.BlockSpec((POOL_BUF, tb, d), lambda i: (0, i, 0)),
        ],
        out_shape=[jax.ShapeDtypeStruct((b, l, d), F32),
                   jax.ShapeDtypeStruct((POOL_BUF, b, d), F32)],
        scratch_shapes=[pltpu.VMEM((d // LANES, tb * l, LANES), F32),
                        pltpu.VMEM((d // LANES, tb * l, LANES), F32)],
        compiler_params=_params("arbitrary"),
        name="pool_layer_sample",
    )(x, buf, gains, pw, ps, win, wout)


def _gdn_qkv_slab(sl, conv, q_ref, k_ref, v_ref, rows):
    if sl >= 2 * K_HEADS:
        v_ref[sl - 2 * K_HEADS, rows, :] = conv
        return
    unit = conv * lax.rsqrt(jnp.sum(conv * conv, axis=-1, keepdims=True) + EPS)
    if sl < K_HEADS:
        q_ref[sl, rows, :] = unit * (HEAD_DIM ** -0.5)
    else:
        k_ref[sl - K_HEADS, rows, :] = unit


def _gdn_gates(ba, alog, dtb, beta_ref, g_ref):
    beta_ref[...] = _sigmoid(ba[:, :LANES])
    g_ref[...] = -jnp.exp(alog) * _softplus(ba[:, LANES:] + dtb)


def _gdn_pre_prompt_kernel(x_ref, xnext_ref, gain_ref, wqkvz_ref, wba_ref, cw_ref, alog_ref,
                           dtb_ref, q_ref, k_ref, v_ref, z_ref, beta_ref, g_ref, conv_ref,
                           up_ref, h_s, *, tm):
    j = pl.program_id(1)

    @pl.when((pl.program_id(0) == 0) & (j == 0))
    def _():
        h_s[...] = _rms(x_ref[0], gain_ref[...]).astype(BF16)

    h = h_s[...]
    n_slab = CONV_DIM // HEAD_DIM
    n_grp = n_slab // PRE_SLABS

    @pl.when(j == 0)
    def _():
        up_ref[:, 0:CONV_HALO, :] = jnp.zeros((n_slab, CONV_HALO, HEAD_DIM), F32)

    def project(grp):
        c0 = grp * PRE_SLABS * HEAD_DIM
        p = jnp.dot(h, wqkvz_ref[:, c0:c0 + PRE_SLABS * HEAD_DIM], preferred_element_type=F32)
        for t in range(PRE_SLABS):
            up_ref[grp * PRE_SLABS + t, CONV_HALO:CONV_HALO + tm, :] = (
                p[:, t * HEAD_DIM:(t + 1) * HEAD_DIM])

    def project_z(half):
        c0 = CONV_DIM + half * (V_DIM // 2)
        p = jnp.dot(h, wqkvz_ref[:, c0:c0 + V_DIM // 2], preferred_element_type=F32)
        for t in range(V_HEADS // 2):
            z_ref[half * (V_HEADS // 2) + t] = p[:, t * HEAD_DIM:(t + 1) * HEAD_DIM]

    base = CONV_HALO - (CONV_WIDTH - 1)

    def convolve(grp):
        for sl in range(grp * PRE_SLABS, (grp + 1) * PRE_SLABS):
            lanes = slice(sl * HEAD_DIM, (sl + 1) * HEAD_DIM)
            taps = [cw_ref[tap:tap + 1, lanes] for tap in range(CONV_WIDTH)]
            for row0 in range(0, tm, ROW_GROUP):
                tiles = [up_ref[sl, _strided_rows(row0 + base + s), :]
                         for s in range(ROW_STRIDE + CONV_WIDTH - 1)]
                for r in range(ROW_STRIDE):
                    acc = tiles[r] * taps[0]
                    for tap in range(1, CONV_WIDTH):
                        acc = acc + tiles[r + tap] * taps[tap]
                    _gdn_qkv_slab(sl, _silu(acc), q_ref, k_ref, v_ref, _strided_rows(row0 + r))

    project(0)
    for grp in range(n_grp):
        if grp + 1 < n_grp:
            project(grp + 1)
        else:
            project_z(0)
        convolve(grp)
    project_z(1)
    ba = jnp.dot(h, wba_ref[...], preferred_element_type=F32)
    _gdn_gates(ba, alog_ref[...], dtb_ref[...], beta_ref, g_ref)
    h_s[...] = _rms(xnext_ref[0], gain_ref[...]).astype(BF16)

    @pl.when(j == pl.num_programs(1) - 1)
    def _():
        for sl in range(n_slab):
            conv_ref[0, :, sl * HEAD_DIM:(sl + 1) * HEAD_DIM] = up_ref[
                sl, tm + CONV_HALO - (CONV_WIDTH - 1):tm + CONV_HALO, :]

    up_ref[:, 0:CONV_HALO, :] = up_ref[:, tm:tm + CONV_HALO, :]


def _gdn_pre_sample_kernel(x_ref, buf_ref, gain_ref, wqkvz_ref, wba_ref, cw_ref, alog_ref,
                           dtb_ref, q_ref, k_ref, v_ref, z_ref, beta_ref, g_ref, conv_ref, up_ref,
                           *, tb, seq):
    m = tb * seq
    n_buf = CONV_WIDTH - 1
    h = _rms(x_ref[...].reshape(m, D_MODEL), gain_ref[...]).astype(BF16)
    proj = jnp.dot(h, wqkvz_ref[:, :CONV_DIM + V_DIM], preferred_element_type=F32)
    ba = jnp.dot(h, wba_ref[...], preferred_element_type=F32)
    for hh in range(V_HEADS):
        z_ref[hh] = proj[:, CONV_DIM + hh * HEAD_DIM:CONV_DIM + (hh + 1) * HEAD_DIM]
    _gdn_gates(ba, alog_ref[...], dtb_ref[...], beta_ref, g_ref)
    at_time = lambda t: pl.ds(t, tb, stride=seq)
    for sl in range(CONV_DIM // HEAD_DIM):
        lanes = slice(sl * HEAD_DIM, (sl + 1) * HEAD_DIM)
        up_ref[sl] = proj[:, lanes]
        ups = [buf_ref[s, :, lanes] for s in range(n_buf)]
        ups += [up_ref[sl, at_time(t), :] for t in range(seq)]
        for t in range(seq):
            acc = ups[t] * cw_ref[0:1, lanes]
            for tap in range(1, CONV_WIDTH):
                acc = acc + ups[t + tap] * cw_ref[tap:tap + 1, lanes]
            _gdn_qkv_slab(sl, _silu(acc), q_ref, k_ref, v_ref, at_time(t))
        for s in range(n_buf):
            conv_ref[s, :, lanes] = ups[seq + s]


def _gdn_pre_out_shapes(n):
    return [jax.ShapeDtypeStruct((K_HEADS, n, HEAD_DIM), F32),
            jax.ShapeDtypeStruct((K_HEADS, n, HEAD_DIM), F32),
            jax.ShapeDtypeStruct((V_HEADS, n, HEAD_DIM), F32),
            jax.ShapeDtypeStruct((V_HEADS, n, HEAD_DIM), F32),
            jax.ShapeDtypeStruct((n, LANES), F32), jax.ShapeDtypeStruct((n, LANES), F32)]


def _gdn_pre_out_specs(rows, index):
    heads = lambda n: pl.BlockSpec((n, rows, HEAD_DIM), lambda *g: (0, index(*g), 0))
    lane = pl.BlockSpec((rows, LANES), lambda *g: (index(*g), 0))
    return [heads(K_HEADS), heads(K_HEADS), heads(V_HEADS), heads(V_HEADS), lane, lane]


def _gdn_pre_prompt(x, gain, wqkvz, wba, cw, alog, dtb):
    b, l, d = x.shape
    tm = GDN_PRE_TILE
    nj = l // tm
    following = lambda i, j: jnp.minimum(i * nj + j + 1, b * nj - 1)
    return pl.pallas_call(
        functools.partial(_gdn_pre_prompt_kernel, tm=tm),
        grid=(b, nj),
        in_specs=[pl.BlockSpec((1, tm, d), lambda i, j: (i, j, 0)),
                  pl.BlockSpec((1, tm, d),
                               lambda i, j: (following(i, j) // nj, following(i, j) % nj, 0))]
        + [_const_spec(a.shape) for a in (gain, wqkvz, wba, cw, alog, dtb)],
        out_specs=_gdn_pre_out_specs(tm, lambda i, j: i * nj + j)
        + [pl.BlockSpec((1, CONV_WIDTH - 1, CONV_DIM), lambda i, j: (i, 0, 0))],
        out_shape=_gdn_pre_out_shapes(b * l)
        + [jax.ShapeDtypeStruct((b, CONV_WIDTH - 1, CONV_DIM), F32)],
        scratch_shapes=[pltpu.VMEM((CONV_DIM // HEAD_DIM, CONV_HALO + tm, HEAD_DIM), F32),
                        pltpu.VMEM((tm, d), BF16)],
        compiler_params=_params("arbitrary", "arbitrary"),
        name="gdn_pre_prompt",
    )(x, x, gain, wqkvz, wba, cw, alog, dtb)


def _gdn_pre_sample(x, buf, gain, wqkvz, wba, cw, alog, dtb):
    b, l, d = x.shape
    tb = SAMPLE_BTILE
    state = pl.BlockSpec((CONV_WIDTH - 1, tb, CONV_DIM), lambda i: (0, i, 0))
    return pl.pallas_call(
        functools.partial(_gdn_pre_sample_kernel, tb=tb, seq=l),
        grid=(b // tb,),
        in_specs=[pl.BlockSpec((tb, l, d), lambda i: (i, 0, 0)), state]
        + [_const_spec(a.shape) for a in (gain, wqkvz, wba, cw, alog, dtb)],
        out_specs=_gdn_pre_out_specs(tb * l, lambda i: i) + [state],
        out_shape=_gdn_pre_out_shapes(b * l)
        + [jax.ShapeDtypeStruct((CONV_WIDTH - 1, b, CONV_DIM), F32)],
        scratch_shapes=[pltpu.VMEM((CONV_DIM // HEAD_DIM, tb * l, HEAD_DIM), F32)],
        compiler_params=_params("arbitrary"),
        name="gdn_pre_sample",
    )(x, buf, gain, wqkvz, wba, cw, alog, dtb)


def _unit_lower_inverses(mats, c):
    ri = lax.broadcasted_iota(jnp.int32, (c, c), 0)
    ci = lax.broadcasted_iota(jnp.int32, (c, c), 1)
    eye = (ri == ci).astype(F32)
    pair = ((ri // 2) == (ci // 2)) & (ri > ci)
    xs = [eye - jnp.where(pair, a, 0.0) for a in mats]
    mats = [a.astype(BF16) for a in mats]
    blk = 2
    while blk < c:
        off = ((ri // (2 * blk)) == (ci // (2 * blk))) & ((ri // blk) > (ci // blk))
        xbs = [x.astype(BF16) for x in xs]
        ys = [_dot(jnp.where(off, a, jnp.zeros_like(a)), xb) for a, xb in zip(mats, xbs)]
        yield
        xs = [x - _dot(xb, y) for x, xb, y in zip(xs, xbs, ys)]
        yield
        blk *= 2
    return xs


def _delta_chunks(q_ref, k_ref, v_ref, z_ref, beta_ref, g_ref, og_ref, onorm, rows,
                  state_load, state_store, c):
    n = len(rows)
    ri = lax.broadcasted_iota(jnp.int32, (c, c), 0)
    ci = lax.broadcasted_iota(jnp.int32, (c, c), 1)
    causal = ri >= ci
    strict = ri > ci
    rep = V_HEADS // K_HEADS
    units = [(i, h) for i in range(n) for h in range(V_HEADS)]
    kunits = [(i, j) for i in range(n) for j in range(K_HEADS)]

    gcum = [_cumsum_rows(g_ref[rows[i], :], c) for i in range(n)]
    gcum_t = [x.T for x in gcum]
    egcum = [jnp.exp(x) for x in gcum]
    etail_t = [x[:, c - 1:c] - x for x in gcum_t]
    etail_t = [jnp.exp(x) for x in etail_t]
    beta = [beta_ref[rows[i], :] for i in range(n)]
    ks = {(i, j): k_ref[j, rows[i], :] for i, j in kunits}
    kts = {u: ks[u].T for u in kunits}
    kq = {(i, j): _dot(jnp.concatenate([ks[i, j], q_ref[j, rows[i], :]], axis=0), kts[i, j])
          for i, j in kunits}
    yield
    gcol = {(i, h): gcum[i][:, h:h + 1] for i, h in units}
    bcol = {(i, h): beta[i][:, h:h + 1] for i, h in units}
    egc = {(i, h): egcum[i][:, h:h + 1] for i, h in units}
    decay = {(i, h): jnp.where(
        causal, jnp.exp(jnp.minimum(gcol[i, h] - gcum_t[i][h:h + 1, :], 0.0)), 0.0)
        for i, h in units}
    a_mats = [jnp.where(strict, kq[i, h // rep][:c] * bcol[i, h] * decay[i, h], 0.0)
              for i, h in units]
    t_inv = dict(zip(units, (yield from _unit_lower_inverses(a_mats, c))))
    uw = {(i, h): _dot(t_inv[i, h], jnp.concatenate(
        [v_ref[h, rows[i], :] * bcol[i, h], ks[i, h // rep] * (bcol[i, h] * egc[i, h])], axis=1))
        for i, h in units}
    yield
    wq = {(i, h): jnp.concatenate(
        [uw[i, h][:, HEAD_DIM:], q_ref[h // rep, rows[i], :] * egc[i, h]], axis=0).astype(BF16)
        for i, h in units}
    qkd = {(i, h): kq[i, h // rep][c:] * decay[i, h] for i, h in units}
    stacked = c % (2 * SUBLANES) == 0
    if stacked:
        qkd = {(i, h): jnp.concatenate(
            [qkd[i, h], kts[i, h // rep] * etail_t[i][h:h + 1, :]], axis=0).astype(BF16)
            for i, h in units}
    else:
        qkd = {u: x.astype(BF16) for u, x in qkd.items()}

    yield STATE_STAGES
    heads = range(V_HEADS)
    for i in range(n):
        s_old = [state_load(i, h) for h in heads]
        ws = [_dot(wq[i, h], s_old[h]) for h in heads]
        yield
        v_new = [uw[i, h][:, :HEAD_DIM] - ws[h][:c] for h in heads]
        ov = [_dot(qkd[i, h], v_new[h]) for h in heads]
        yield
        for h in heads:
            if stacked:
                kv = ov[h][c:]
            else:
                kv = _dot(kts[i, h // rep] * etail_t[i][h:h + 1, :], v_new[h])
            state_store(i, h, s_old[h] * egcum[i][c - 1:c, h:h + 1] + kv)
        for h in heads:
            og_ref[rows[i], h * HEAD_DIM:(h + 1) * HEAD_DIM] = (
                _rms(ws[h][c:] + ov[h][:c], onorm) * _silu(z_ref[h, rows[i], :])).astype(BF16)


def _pipeline_groups(groups):
    def to_state_stages(group, filler=None):
        for n, tag in enumerate(group, 1):
            if tag == STATE_STAGES:
                return
            if filler is not None and n % SCAN_INTERLEAVE == 0:
                next(filler, None)

    to_state_stages(groups[0])
    for g, group in enumerate(groups):
        if g + 1 < len(groups):
            to_state_stages(groups[g + 1], filler=group)
        for _ in group:
            pass


def _cumsum_rows(g, c):
    ri = lax.broadcasted_iota(jnp.int32, (c, c), 0)
    ci = lax.broadcasted_iota(jnp.int32, (c, c), 1)
    tri = (ri >= ci).astype(BF16)
    hi = g.astype(BF16)
    rest = g - hi.astype(F32)
    mid = rest.astype(BF16)
    lo = (rest - mid.astype(F32)).astype(BF16)
    return (jnp.dot(tri, hi, preferred_element_type=F32)
            + jnp.dot(tri, mid, preferred_element_type=F32)
            + jnp.dot(tri, lo, preferred_element_type=F32))


def _gdn_scan_prompt_kernel(q_ref, k_ref, v_ref, z_ref, beta_ref, g_ref, onorm_ref,
                            og_ref, s_ref, *, c, n_chunk):
    @pl.when(pl.program_id(1) == 0)
    def _():
        s_ref[...] = jnp.zeros(s_ref.shape, F32)

    def load(i, hh):
        return s_ref[0, hh]

    def store(i, hh, val):
        s_ref[0, hh] = val

    groups = []
    first = 0
    for size in SCAN_GROUPS:
        rows = [pl.ds((first + i) * c, c) for i in range(size)]
        groups.append(_delta_chunks(q_ref, k_ref, v_ref, z_ref, beta_ref, g_ref, og_ref,
                                    onorm_ref[...], rows, load, store, c))
        first += size
    assert first == n_chunk
    _pipeline_groups(groups)


def _gdn_scan_sample_kernel(q_ref, k_ref, v_ref, z_ref, beta_ref, g_ref, onorm_ref, s0_ref,
                            og_ref, s_ref, *, tb, c):
    groups = []
    for b0 in range(0, tb, SAMPLE_SCAN_GROUP):
        def load(i, hh, b0=b0):
            return s0_ref[b0 + i, hh]

        def store(i, hh, val, b0=b0):
            s_ref[b0 + i, hh] = val

        rows = [pl.ds((b0 + i) * c, c) for i in range(SAMPLE_SCAN_GROUP)]
        groups.append(_delta_chunks(q_ref, k_ref, v_ref, z_ref, beta_ref, g_ref, og_ref,
                                    onorm_ref[...], rows, load, store, c))
    _pipeline_groups(groups)


def _gdn_scan_prompt(q, k, v, z, beta, g, onorm, b, l):
    c = PROMPT_CHUNK
    tm = SCAN_TILE
    nc = l // tm
    tok = lambda w: pl.BlockSpec((tm, w), lambda i, j: (i * nc + j, 0))
    return pl.pallas_call(
        functools.partial(_gdn_scan_prompt_kernel, c=c, n_chunk=tm // c),
        grid=(b, nc),
        in_specs=_gdn_pre_out_specs(tm, lambda i, j: i * nc + j) + [_const_spec(onorm.shape)],
        out_specs=[tok(V_DIM),
                   pl.BlockSpec((1, V_HEADS, HEAD_DIM, HEAD_DIM), lambda i, j: (i, 0, 0, 0))],
        out_shape=[jax.ShapeDtypeStruct((b * l, V_DIM), BF16),
                   jax.ShapeDtypeStruct((b, V_HEADS, HEAD_DIM, HEAD_DIM), F32)],
        compiler_params=_params("arbitrary", "arbitrary"),
        name="gdn_scan_prompt",
    )(q, k, v, z, beta, g, onorm)


def _gdn_scan_sample(q, k, v, z, beta, g, onorm, s0, b, l):
    tb = SAMPLE_SCAN_BTILE
    tok = lambda w: pl.BlockSpec((tb * l, w), lambda i: (i, 0))
    st = pl.BlockSpec((tb, V_HEADS, HEAD_DIM, HEAD_DIM), lambda i: (i, 0, 0, 0))
    return pl.pallas_call(
        functools.partial(_gdn_scan_sample_kernel, tb=tb, c=l),
        grid=(b // tb,),
        in_specs=_gdn_pre_out_specs(tb * l, lambda i: i) + [_const_spec(onorm.shape), st],
        out_specs=[tok(V_DIM), st],
        out_shape=[jax.ShapeDtypeStruct((b * l, V_DIM), BF16),
                   jax.ShapeDtypeStruct((b, V_HEADS, HEAD_DIM, HEAD_DIM), F32)],
        compiler_params=_params("arbitrary"),
        name="gdn_scan_sample",
    )(q, k, v, z, beta, g, onorm, s0)


def _gdn_post_kernel(og_ref, x_ref, gains_ref, wo_ref, win_ref, wout_ref, y_ref):
    m = jnp.dot(og_ref[...], wo_ref[...], preferred_element_type=F32)
    y_ref[...] = _residual_ffn(x_ref[...], m, gains_ref[0:1, :], gains_ref[1:2, :],
                               gains_ref[2:3, :], win_ref, wout_ref)


def _gdn_post(og, x, gains, wo, win, wout, layer):
    n, d = x.shape
    tm = PROMPT_TILE
    return pl.pallas_call(
        _gdn_post_kernel,
        grid=(n // tm,),
        in_specs=[pl.BlockSpec((tm, V_DIM), lambda i: (i, 0)),
                  pl.BlockSpec((tm, d), lambda i: (i, 0))]
        + [_const_spec(gains.shape), _const_spec(wo.shape), _layer_spec(win, layer),
           _layer_spec(wout, layer)],
        out_specs=pl.BlockSpec((tm, d), lambda i: (i, 0)),
        out_shape=jax.ShapeDtypeStruct((n, d), F32),
        compiler_params=_params("arbitrary"),
        name="gdn_post",
    )(og, x, gains, wo, win, wout)


def _head_lanes(vec):
    return jnp.pad(vec.astype(F32), (0, LANES - V_HEADS)).reshape(1, LANES)


def kernel(x_prompt, x_sample, state_pool, state_gdn_conv, state_gdn_rec, norm_mix_pre,
           norm_mix_post, norm_ffn_pre, norm_ffn_post, pool_w, pool_scale, gdn_w_in,
           gdn_conv_w, gdn_a_log, gdn_dt_bias, gdn_o_norm, gdn_w_out, ffn_w_in, ffn_w_out):
    bp, lp, d = x_prompt.shape
    bs, ls, _ = x_sample.shape

    gains0 = jnp.stack([norm_mix_pre[0], norm_mix_post[0], norm_ffn_pre[0], norm_ffn_post[0]])
    gains1 = jnp.stack([norm_mix_post[1], norm_ffn_pre[1], norm_ffn_post[1]])
    gain1_pre = norm_mix_pre[1].reshape(1, d)
    pw = pool_w[0].astype(BF16)
    ps = pool_scale[0].reshape(1, d)
    win, wout = ffn_w_in.astype(BF16), ffn_w_out.astype(BF16)
    w_in = gdn_w_in[0]
    wqkvz = w_in.astype(BF16)
    w_b = w_in[:, CONV_DIM + V_DIM:CONV_DIM + V_DIM + V_HEADS]
    w_a = w_in[:, CONV_DIM + V_DIM + V_HEADS:]
    lane_pad = ((0, 0), (0, LANES - V_HEADS))
    wba = jnp.concatenate([jnp.pad(w_b, lane_pad), jnp.pad(w_a, lane_pad)], axis=1).astype(BF16)
    cw = gdn_conv_w[0]
    alog, dtb = _head_lanes(gdn_a_log[0]), _head_lanes(gdn_dt_bias[0])
    onorm = gdn_o_norm[0].reshape(1, HEAD_DIM)
    wo = gdn_w_out[0].astype(BF16)

    xp1, pool_p = _pool_layer_prompt(x_prompt, gains0, pw, ps, win, wout, 0)
    time_major = lambda a: jnp.transpose(a, (1, 0, 2))
    xs1, pool_s = _pool_layer_sample(x_sample, time_major(state_pool[0]), gains0, pw, ps, win,
                                     wout, 0)

    qp, kp, vp, zp, betap, gp, conv_p = _gdn_pre_prompt(xp1, gain1_pre, wqkvz, wba, cw, alog, dtb)
    qs, ks, vs, zs, betas, gs, conv_s = _gdn_pre_sample(xs1, time_major(state_gdn_conv[0]),
                                                        gain1_pre, wqkvz, wba, cw, alog, dtb)

    ogp, rec_p = _gdn_scan_prompt(qp, kp, vp, zp, betap, gp, onorm, bp, lp)
    ogs, rec_s = _gdn_scan_sample(qs, ks, vs, zs, betas, gs, onorm, state_gdn_rec[0], bs, ls)

    yp = _gdn_post(ogp, xp1.reshape(bp * lp, d), gains1, wo, win, wout, 1).reshape(bp, lp, d)
    ys = _gdn_post(ogs, xs1.reshape(bs * ls, d), gains1, wo, win, wout, 1).reshape(bs, ls, d)

    return (yp, ys, pool_p[None], time_major(pool_s)[None], conv_p[None],
            time_major(conv_s)[None], rec_p[None], rec_s[None])
```
